```python
import math
import jax
import jax.numpy as jnp
from jax import lax
import numpy as np

D_MODEL = 1024
BATCH = 4
SEQ = 8192
DEPTH = 4

N_MIXERS = 3
N_CONV_LAYERS = len(range(0, DEPTH, N_MIXERS))
N_NSA_LAYERS = len(range(1, DEPTH, N_MIXERS))
N_MLA_LAYERS = len(range(2, DEPTH, N_MIXERS))

EPS = 1e-6
NEG = -1e30
BIG = 1e30
ROPE_THETA = 500000.0

CONV_WIDTH = 31

NSA_HEADS = 16
NSA_KV_GROUPS = 4
NSA_HEAD_DIM = D_MODEL // NSA_HEADS
NSA_ROT_DIM = NSA_HEAD_DIM // 4
CMP_BLOCK = 32
CMP_STRIDE = 16
SLC_BLOCK = 64
SLC_TOP_N = 16
WINDOW = 512
NSA_Q_BLOCK = 64
NSA_Q_DIM = NSA_HEADS * NSA_HEAD_DIM
NSA_KV_DIM = NSA_KV_GROUPS * NSA_HEAD_DIM
NSA_SIZES = [NSA_Q_DIM] + [NSA_KV_DIM] * 6 + [3 * NSA_HEADS]
NSA_IN = sum(NSA_SIZES)
NSA_SPLITS = [int(v) for v in np.cumsum(NSA_SIZES)[:-1]]

MLA_HEADS = 16
MLA_Q_LORA = 384
MLA_KV_LORA = 256
MLA_NOPE = 64
MLA_ROPE = 32
MLA_V = 64
MLA_QK = MLA_NOPE + MLA_ROPE
MLA_IN = MLA_Q_LORA + MLA_KV_LORA + MLA_ROPE
MLA_Q_BLOCK = 128

N_EXPERTS = 16
N_GROUPS = 4
EXPERTS_PER_GROUP = N_EXPERTS // N_GROUPS
TOP_K = 2
D_EXPERT = 512

kernel_name = "hybrid_conv_nsa_mla_grouped_moe"


def rms_norm(x, g):
    xf = x.astype(jnp.float32)
    y = xf * lax.rsqrt(jnp.mean(xf * xf, axis=-1, keepdims=True) + EPS)
    return (y * g.astype(jnp.float32)).astype(x.dtype)


def layer_norm(x, g, b):
    xf = x.astype(jnp.float32)
    mu = jnp.mean(xf, axis=-1, keepdims=True)
    var = jnp.mean(jnp.square(xf - mu), axis=-1, keepdims=True)
    y = (xf - mu) * lax.rsqrt(var + EPS) * g.astype(jnp.float32) + b.astype(jnp.float32)
    return y.astype(x.dtype)


def modulate(h, shift, scale):
    return h * (1.0 + scale[:, None, :]) + shift[:, None, :]


def rope_tables(pos, dim):
    inv_freq = ROPE_THETA ** (-jnp.arange(0, dim, 2, dtype=jnp.float32) / dim)
    ang = pos.astype(jnp.float32)[:, None] * inv_freq[None, :]
    return jnp.cos(ang)[:, None, :], jnp.sin(ang)[:, None, :]


def apply_rope(x, cos, sin):
    x1, x2 = jnp.split(x.astype(jnp.float32), 2, axis=-1)
    return jnp.concatenate([x1 * cos - x2 * sin, x2 * cos + x1 * sin], axis=-1).astype(x.dtype)


def partial_rope(x, cos, sin, rot):
    return jnp.concatenate([apply_rope(x[..., :rot], cos, sin), x[..., rot:]], axis=-1)


def conv_mixer(h, w_pw1, b_pw1, w_dw, b_dw, ln_g, ln_b, w_pw2, b_pw2):
    a, g = jnp.split(h @ w_pw1 + b_pw1, 2, axis=-1)
    u = a * jax.nn.sigmoid(g)
    u = jnp.pad(u, ((0, 0), (CONV_WIDTH - 1, 0), (0, 0)))
    u = lax.conv_general_dilated(u, w_dw[:, None, :], window_strides=(1,), padding='VALID',
                                 dimension_numbers=('NWC', 'WIO', 'NWC'),
                                 feature_group_count=D_MODEL) + b_dw
    u = layer_norm(u, ln_g, ln_b)
    return jax.nn.silu(u) @ w_pw2 + b_pw2


def nsa_mixer(h, w_in, w_cmp1, w_cmp2, cmp_pos, q_gain, k_gain, w_out):
    B, S, _ = h.shape
    H, G, dh = NSA_HEADS, NSA_KV_GROUPS, NSA_HEAD_DIM
    R = H // G
    n_cmp = S // CMP_STRIDE - 1
    n_slc = S // SLC_BLOCK
    n_top = min(SLC_TOP_N, n_slc)
    QB = NSA_Q_BLOCK
    scale = dh ** -0.5
    cos, sin = rope_tables(jnp.arange(S), NSA_ROT_DIM)
    cmp_end = jnp.arange(n_cmp) * CMP_STRIDE + (CMP_BLOCK - 1)
    cos_c, sin_c = rope_tables(cmp_end, NSA_ROT_DIM)

    q, kc, vc, ks, vs, kw, vw, gl = jnp.split(h @ w_in, NSA_SPLITS, axis=-1)
    q = partial_rope(rms_norm(q.reshape(B, S, H, dh), q_gain), cos, sin, NSA_ROT_DIM)

    def kv(t):
        return t.reshape(B, S, G, dh)

    def compress(t, w1, w2, pe):
        tr = t.reshape(B, S // CMP_STRIDE, CMP_STRIDE, G, dh)
        blk = jnp.concatenate([tr[:, :-1], tr[:, 1:]], axis=2) + pe[:, None, :]
        z = jax.nn.silu(jnp.einsum('bnlgd,lde->bnge', blk, w1))
        return jnp.einsum('bnge,ef->bngf', z, w2)

    kc = compress(kv(kc), w_cmp1[0], w_cmp2[0], cmp_pos[0])
    vc = compress(kv(vc), w_cmp1[1], w_cmp2[1], cmp_pos[1])
    kc = partial_rope(rms_norm(kc, k_gain[0]), cos_c, sin_c, NSA_ROT_DIM)
    ks = partial_rope(rms_norm(kv(ks), k_gain[1]), cos, sin, NSA_ROT_DIM)
    kw = partial_rope(rms_norm(kv(kw), k_gain[2]), cos, sin, NSA_ROT_DIM)

    def hf(t):
        return t.transpose(0, 2, 1, 3)

    qg = hf(q).reshape(B, G, R, S, dh)
    kc, vc = hf(kc), hf(vc)
    ks_blk = hf(ks).reshape(B, G, n_slc, SLC_BLOCK, dh)
    vs_blk = hf(kv(vs)).reshape(B, G, n_slc, SLC_BLOCK, dh)
    pad = ((0, 0), (0, 0), (WINDOW, 0), (0, 0))
    kw_pad = jnp.pad(hf(kw), pad)
    vw_pad = jnp.pad(hf(kv(vw)), pad)
    gates = jax.nn.sigmoid(gl.astype(jnp.float32)).reshape(B, S, H, 3)
    gates = gates.transpose(0, 2, 1, 3).reshape(B, G, R, S, 3)

    ii = jnp.arange(n_cmp)[:, None]
    jj = jnp.arange(n_slc)[None, :]
    overlap = ((ii * CMP_STRIDE < (jj + 1) * SLC_BLOCK) &
               (ii * CMP_STRIDE + CMP_BLOCK > jj * SLC_BLOCK)).astype(jnp.float32)
    bi = jnp.arange(B)[:, None, None, None]
    gi = jnp.arange(G)[None, :, None, None]
    slot = jnp.arange(SLC_BLOCK)
    blk_id = jnp.arange(n_slc)

    def q_block(qi):
        t0 = qi * QB
        tq = t0 + jnp.arange(QB)
        qb = lax.dynamic_slice_in_dim(qg, t0, QB, axis=3)
        gb = lax.dynamic_slice_in_dim(gates, t0, QB, axis=3)
        s = jnp.einsum('bgrqd,bgnd->bgrqn', qb, kc, preferred_element_type=jnp.float32) * scale
        m = cmp_end[None, :] <= tq[:, None]
        p = jnp.where(m, jax.nn.softmax(jnp.where(m, s, NEG), axis=-1), 0.0)
        o_c = jnp.einsum('bgrqn,bgnd->bgrqd', p.astype(vc.dtype), vc)
        imp = jnp.einsum('bgrqn,nj->bgqj', p, overlap)
        imp = jnp.where(blk_id[None, :] * SLC_BLOCK <= tq[:, None], imp, -BIG)
        forced = (blk_id[None, :] == 0) | (blk_id[None, :] == tq[:, None] // SLC_BLOCK)
        imp = jnp.where(forced, BIG, imp)
        _, sel = lax.top_k(imp, n_top)
        k_sel = ks_blk[bi, gi, sel].reshape(B, G, QB, n_top * SLC_BLOCK, dh)
        v_sel = vs_blk[bi, gi, sel].reshape(B, G, QB, n_top * SLC_BLOCK, dh)
        kpos = (sel[..., None] * SLC_BLOCK + slot).reshape(B, G, QB, n_top * SLC_BLOCK)
        s = jnp.einsum('bgrqd,bgqkd->bgrqk', qb, k_sel, preferred_element_type=jnp.float32) * scale
        m = (kpos <= tq[:, None])[:, :, None]
        p = jax.nn.softmax(jnp.where(m, s, NEG), axis=-1)
        o_s = jnp.einsum('bgrqk,bgqkd->bgrqd', p.astype(v_sel.dtype), v_sel)
        k_w = lax.dynamic_slice_in_dim(kw_pad, t0, QB + WINDOW, axis=2)
        v_w = lax.dynamic_slice_in_dim(vw_pad, t0, QB + WINDOW, axis=2)
        kpos_w = t0 - WINDOW + jnp.arange(QB + WINDOW)
        m = ((kpos_w[None, :] >= 0) & (kpos_w[None, :] <= tq[:, None]) &
             (kpos_w[None, :] > tq[:, None] - WINDOW))
        s = jnp.einsum('bgrqd,bgkd->bgrqk', qb, k_w, preferred_element_type=jnp.float32) * scale
        p = jax.nn.softmax(jnp.where(m, s, NEG), axis=-1)
        o_w = jnp.einsum('bgrqk,bgkd->bgrqd', p.astype(v_w.dtype), v_w)
        o = gb[..., 0:1] * o_c + gb[..., 1:2] * o_s + gb[..., 2:3] * o_w
        return o.astype(h.dtype)

    o = lax.map(q_block, jnp.arange(S // QB))
    o = o.transpose(1, 0, 4, 2, 3, 5).reshape(B, S, H * dh)
    return o @ w_out


def mla_mixer(h, w_in, q_lat_g, kv_lat_g, w_uq, w_ukv, q_gain, k_gain, w_out):
    B, S, _ = h.shape
    H = MLA_HEADS
    QB = MLA_Q_BLOCK
    scale = MLA_QK ** -0.5
    cos, sin = rope_tables(jnp.arange(S), MLA_ROPE)
    q_lat, kv_lat, k_pe = jnp.split(h @ w_in, [MLA_Q_LORA, MLA_Q_LORA + MLA_KV_LORA], axis=-1)
    q = (rms_norm(q_lat, q_lat_g) @ w_uq).reshape(B, S, H, MLA_QK)
    kvu = (rms_norm(kv_lat, kv_lat_g) @ w_ukv).reshape(B, S, H, MLA_NOPE + MLA_V)
    k_nope, v = jnp.split(kvu, [MLA_NOPE], axis=-1)
    k = jnp.concatenate([k_nope, jnp.broadcast_to(k_pe[:, :, None, :], (B, S, H, MLA_ROPE))], axis=-1)

    def norm_rot(t, g):
        t = rms_norm(t, g)
        return jnp.concatenate([t[..., :MLA_NOPE], apply_rope(t[..., MLA_NOPE:], cos, sin)], axis=-1)

    q = norm_rot(q, q_gain).transpose(0, 2, 1, 3)
    k = norm_rot(k, k_gain).transpose(0, 2, 1, 3)
    v = v.transpose(0, 2, 1, 3)
    kpos = jnp.arange(S)

    def q_block(qi):
        t0 = qi * QB
        qb = lax.dynamic_slice_in_dim(q, t0, QB, axis=2)
        s = jnp.einsum('bhqd,bhkd->bhqk', qb, k, preferred_element_type=jnp.float32) * scale
        m = kpos[None, :] <= (t0 + jnp.arange(QB))[:, None]
        p = jax.nn.softmax(jnp.where(m, s, NEG), axis=-1)
        return jnp.einsum('bhqk,bhkd->bhqd', p.astype(v.dtype), v)

    o = lax.map(q_block, jnp.arange(S // QB))
    o = o.transpose(1, 0, 3, 2, 4).reshape(B, S, H * MLA_V)
    return o @ w_out


def grouped_moe(h, router_w, router_bias, w_in, w_out):
    B, S, D = h.shape
    xt = h.reshape(-1, D)
    scores = jax.nn.sigmoid((xt @ router_w).astype(jnp.float32))
    biased = scores + router_bias.astype(jnp.float32)
    grp_score = lax.top_k(biased.reshape(-1, N_GROUPS, EXPERTS_PER_GROUP), 2)[0].sum(-1)
    g_sel = jnp.argmax(grp_score, axis=-1)
    expert_group = jnp.arange(N_EXPERTS) // EXPERTS_PER_GROUP
    masked = jnp.where(expert_group[None, :] == g_sel[:, None], biased, -BIG)
    _, expert = lax.top_k(masked, TOP_K)
    w = jnp.take_along_axis(scores, expert, axis=1)
    w = w / jnp.sum(w, axis=-1, keepdims=True)
    flat_e = expert.reshape(-1)
    order = jnp.argsort(flat_e)
    tok = order // TOP_K
    sizes = jnp.bincount(flat_e, length=N_EXPERTS).astype(jnp.int32)
    xs = xt[tok]
    gate, up = jnp.split(lax.ragged_dot(xs, w_in, sizes), 2, axis=-1)
    y = lax.ragged_dot(jax.nn.silu(gate) * up, w_out, sizes)
    y = y * w.reshape(-1)[order][:, None].astype(y.dtype)
    return jnp.zeros_like(xt).at[tok].add(y).reshape(B, S, D)


def setup_inputs(seed: int = 0) -> dict:
    key = jax.random.key(seed)
    keys = iter(jax.random.split(key, 48))

    def nrm(shape, scale):
        return jax.random.normal(next(keys), shape, jnp.float32) * scale

    def gain(shape):
        return 1.0 + nrm(shape, 0.02)

    D = D_MODEL
    NA, NB, NC = N_CONV_LAYERS, N_NSA_LAYERS, N_MLA_LAYERS
    dh = NSA_HEAD_DIM
    return {
        "x": nrm((BATCH, SEQ, D), 1.0),
        "c": nrm((BATCH, D), 1.0),
        "norm_mix_g": gain((DEPTH, D)),
        "norm_ffn_g": gain((DEPTH, D)),
        "w_ada": nrm((DEPTH, D, 6 * D), 0.5 * D ** -0.5),
        "b_ada": nrm((DEPTH, 6 * D), 0.02),
        "conv_w_pw1": nrm((NA, D, 2 * D), D ** -0.5),
        "conv_b_pw1": nrm((NA, 2 * D), 0.02),
        "conv_w_dw": nrm((NA, CONV_WIDTH, D), CONV_WIDTH ** -0.5),
        "conv_b_dw": nrm((NA, D), 0.02),
        "conv_ln_g": gain((NA, D)),
        "conv_ln_b": nrm((NA, D), 0.02),
        "conv_w_pw2": nrm((NA, D, D), D ** -0.5),
        "conv_b_pw2": nrm((NA, D), 0.02),
        "nsa_w_in": nrm((NB, D, NSA_IN), D ** -0.5),
        "nsa_w_cmp1": nrm((NB, 2, CMP_BLOCK, dh, dh), (CMP_BLOCK * dh) ** -0.5),
        "nsa_w_cmp2": nrm((NB, 2, dh, dh), dh ** -0.5),
        "nsa_cmp_pos": nrm((NB, 2, CMP_BLOCK, dh), 0.1),
        "nsa_q_gain": gain((NB, dh)),
        "nsa_k_gain": gain((NB, 3, dh)),
        "nsa_w_out": nrm((NB, NSA_Q_DIM, D), NSA_Q_DIM ** -0.5),
        "mla_w_in": nrm((NC, D, MLA_IN), D ** -0.5),
        "mla_q_lat_g": gain((NC, MLA_Q_LORA)),
        "mla_kv_lat_g": gain((NC, MLA_KV_LORA)),
        "mla_w_uq": nrm((NC, MLA_Q_LORA, MLA_HEADS * MLA_QK), MLA_Q_LORA ** -0.5),
        "mla_w_ukv": nrm((NC, MLA_KV_LORA, MLA_HEADS * (MLA_NOPE + MLA_V)), MLA_KV_LORA ** -0.5),
        "mla_q_gain": gain((NC, MLA_QK)),
        "mla_k_gain": gain((NC, MLA_QK)),
        "mla_w_out": nrm((NC, MLA_HEADS * MLA_V, D), (MLA_HEADS * MLA_V) ** -0.5),
        "router_w": nrm((D, N_EXPERTS), D ** -0.5),
        "router_bias": nrm((N_EXPERTS,), 0.01),
        "moe_w_in": nrm((DEPTH, N_EXPERTS, D, 2 * D_EXPERT), D ** -0.5),
        "moe_w_out": nrm((DEPTH, N_EXPERTS, D_EXPERT, D), D_EXPERT ** -0.5),
    }


def reference(x, c, norm_mix_g, norm_ffn_g, w_ada, b_ada,
              conv_w_pw1, conv_b_pw1, conv_w_dw, conv_b_dw, conv_ln_g, conv_ln_b, conv_w_pw2, conv_b_pw2,
              nsa_w_in, nsa_w_cmp1, nsa_w_cmp2, nsa_cmp_pos, nsa_q_gain, nsa_k_gain, nsa_w_out,
              mla_w_in, mla_q_lat_g, mla_kv_lat_g, mla_w_uq, mla_w_ukv, mla_q_gain, mla_k_gain, mla_w_out,
              router_w, router_bias, moe_w_in, moe_w_out):
    mods = jnp.einsum('bd,lde->lbe', jax.nn.silu(c), w_ada) + b_ada[:, None, :]
    for i in range(DEPTH):
        sh1, sc1, g1, sh2, sc2, g2 = jnp.split(mods[i], 6, axis=-1)
        h = modulate(rms_norm(x, norm_mix_g[i]), sh1, sc1)
        kind, j = i % N_MIXERS, i // N_MIXERS
        if kind == 0:
            y = conv_mixer(h, conv_w_pw1[j], conv_b_pw1[j], conv_w_dw[j], conv_b_dw[j],
                           conv_ln_g[j], conv_ln_b[j], conv_w_pw2[j], conv_b_pw2[j])
        elif kind == 1:
            y = nsa_mixer(h, nsa_w_in[j], nsa_w_cmp1[j], nsa_w_cmp2[j], nsa_cmp_pos[j],
                          nsa_q_gain[j], nsa_k_gain[j], nsa_w_out[j])
        else:
            y = mla_mixer(h, mla_w_in[j], mla_q_lat_g[j], mla_kv_lat_g[j], mla_w_uq[j],
                          mla_w_ukv[j], mla_q_gain[j], mla_k_gain[j], mla_w_out[j])
        x = x + g1[:, None, :] * y
        h = modulate(rms_norm(x, norm_ffn_g[i]), sh2, sc2)
        x = x + g2[:, None, :] * grouped_moe(h, router_w, router_bias, moe_w_in[i], moe_w_out[i])
    return x
```

```python
import functools
import math

import numpy as np
import jax
import jax.numpy as jnp
from jax import lax
from jax.experimental import pallas as pl
from jax.experimental.pallas import tpu as pltpu

F32 = jnp.float32
BF16 = jnp.bfloat16
I32 = jnp.int32
HIGHEST = lax.Precision.HIGHEST

EPS = 1e-6
NEG = -1e30
BIG = 1e30
ROPE_THETA = 500000.0
LANE = 128
VMEM_LIMIT = 56 * 1024 * 1024

D_MODEL = 1024
CONV_WIDTH = 31
CONV_HALO = 32

NSA_HEADS = 16
NSA_GROUPS = 4
NSA_REP = NSA_HEADS // NSA_GROUPS
NSA_DH = 64
NSA_ROT = 16
CMP_BLOCK = 32
CMP_STRIDE = 16
SLC_BLOCK = 64
SLC_TOP_N = 16
WINDOW = 512
MAX_SLC = 128

MLA_HEADS = 16
MLA_Q_LORA = 384
MLA_KV_LORA = 256
MLA_NOPE = 64
MLA_ROPE = 32
MLA_V = 64
MLA_QK = MLA_NOPE + MLA_ROPE

N_EXPERTS = 16
N_GROUPS = 4
EXPERTS_PER_GROUP = 4
D_EXPERT = 512
MOE_TILE = 512

_NT = (((1,), (1,)), ((), ()))


def _cparams(*sem):
    return pltpu.CompilerParams(dimension_semantics=sem, vmem_limit_bytes=VMEM_LIMIT)


def _sigmoid(x):
    return 1.0 / (1.0 + jnp.exp(-x))


def _silu(x):
    return x * _sigmoid(x)


def _normmod(x, a, b):
    ms = jnp.mean(x * x, axis=-1, keepdims=True)
    return x * lax.rsqrt(ms + EPS) * a + b


def _rope(x, c, sa, sb, half):
    n = x.shape[-1]
    return x * c + pltpu.roll(x, n - half, 1) * sa + pltpu.roll(x, half, 1) * sb


def _ada_body(c_ref, w_ref, b_ref, o_ref):
    c = c_ref[...]
    o_ref[0] = jnp.dot(_silu(c), w_ref[0], preferred_element_type=F32, precision=HIGHEST) + b_ref[0]


def _ada(c, w_ada, b_ada):
    B = c.shape[0]
    L, Dm, N = w_ada.shape
    Bp = -(-B // 8) * 8
    cp = jnp.pad(c, ((0, Bp - B), (0, 0)))
    tn = 1536
    out = pl.pallas_call(
        _ada_body,
        grid=(L, N // tn),
        in_specs=[pl.BlockSpec((Bp, Dm), lambda l, j: (0, 0)),
                  pl.BlockSpec((1, Dm, tn), lambda l, j: (l, 0, j)),
                  pl.BlockSpec((1, 1, tn), lambda l, j: (l, 0, j))],
        out_specs=pl.BlockSpec((1, Bp, tn), lambda l, j: (l, 0, j)),
        out_shape=jax.ShapeDtypeStruct((L, Bp, N), F32),
        compiler_params=_cparams("arbitrary", "arbitrary"),
        name="adaln",
    )(cp, w_ada, b_ada.reshape(L, 1, N))
    return out[:, :B]


def _conv_pw1_body(x_ref, a_ref, b_ref, w_ref, bias_ref, u_ref):
    h = _normmod(x_ref[...], a_ref[0], b_ref[0]).astype(BF16)
    r = jnp.dot(h, w_ref[...], preferred_element_type=F32) + bias_ref[...]
    d = u_ref.shape[-1]
    u_ref[...] = r[:, :d] * _sigmoid(r[:, d:])


def _conv_dw_body(u_ref, uh_ref, wdw_ref, bdw_ref, lng_ref, lnb_ref, w2_ref, b2_ref, x_ref, g_ref,
                  o_ref, ext_ref, *, tm, seq):
    i = pl.program_id(0)
    at_seq_start = (i * tm) % seq == 0
    ext_ref[0:CONV_HALO, :] = jnp.where(at_seq_start, 0.0, uh_ref[...])
    ext_ref[CONV_HALO:, :] = u_ref[...]
    base = CONV_HALO - (CONV_WIDTH - 1)
    acc = jnp.zeros((tm, u_ref.shape[-1]), F32) + bdw_ref[...]
    for k in range(CONV_WIDTH):
        acc = acc + wdw_ref[k:k + 1, :] * ext_ref[base + k:base + k + tm, :]
    mu = jnp.mean(acc, axis=-1, keepdims=True)
    dlt = acc - mu
    var = jnp.mean(dlt * dlt, axis=-1, keepdims=True)
    y = dlt * lax.rsqrt(var + EPS) * lng_ref[...] + lnb_ref[...]
    z = _silu(y).astype(BF16)
    out = jnp.dot(z, w2_ref[...], preferred_element_type=F32) + b2_ref[...]
    o_ref[...] = x_ref[...] + g_ref[0] * out


def _conv_mixer(x2, a, b, g, w_pw1, b_pw1, w_dw, b_dw, ln_g, ln_b, w_pw2, b_pw2, seq):
    T, Dm = x2.shape
    tm = 512
    nb = seq // tm
    row = lambda i: (i, 0)
    per_b = lambda i: (i // nb, 0, 0)
    full = lambda i: (0, 0)
    u = pl.pallas_call(
        _conv_pw1_body,
        grid=(T // tm,),
        in_specs=[pl.BlockSpec((tm, Dm), row),
                  pl.BlockSpec((1, 1, Dm), per_b),
                  pl.BlockSpec((1, 1, Dm), per_b),
                  pl.BlockSpec((Dm, 2 * Dm), full),
                  pl.BlockSpec((1, 2 * Dm), full)],
        out_specs=pl.BlockSpec((tm, Dm), row),
        out_shape=jax.ShapeDtypeStruct((T, Dm), F32),
        compiler_params=_cparams("parallel"),
        name="conv_pw1_glu",
    )(x2, a, b, w_pw1.astype(BF16), b_pw1.reshape(1, -1))
    hb = tm // CONV_HALO
    wdw = jnp.pad(w_dw, ((0, CONV_HALO - CONV_WIDTH), (0, 0)))
    vec = lambda v: v.reshape(1, -1)
    return pl.pallas_call(
        functools.partial(_conv_dw_body, tm=tm, seq=seq),
        grid=(T // tm,),
        in_specs=[pl.BlockSpec((tm, Dm), row),
                  pl.BlockSpec((CONV_HALO, Dm), lambda i: (jnp.maximum(i * hb - 1, 0), 0)),
                  pl.BlockSpec((CONV_HALO, Dm), full),
                  pl.BlockSpec((1, Dm), full),
                  pl.BlockSpec((1, Dm), full),
                  pl.BlockSpec((1, Dm), full),
                  pl.BlockSpec((Dm, Dm), full),
                  pl.BlockSpec((1, Dm), full),
                  pl.BlockSpec((tm, Dm), row),
                  pl.BlockSpec((1, 1, Dm), per_b)],
        out_specs=pl.BlockSpec((tm, Dm), row),
        out_shape=jax.ShapeDtypeStruct((T, Dm), F32),
        scratch_shapes=[pltpu.VMEM((tm + CONV_HALO, Dm), F32)],
        compiler_params=_cparams("parallel"),
        name="conv_dw_ln_pw2",
    )(u, u, wdw, vec(b_dw), vec(ln_g), vec(ln_b), w_pw2.astype(BF16), vec(b_pw2), x2, g)


def _flash_body(qi_ref, kj_ref, first_ref, last_ref, *refs, mode, rep, tq, tk):
    if mode == "select":
        q_ref, k_ref, v_ref, sel_ref, o_ref, m_sc, l_sc, acc_sc = refs
    else:
        q_ref, k_ref, v_ref, o_ref, m_sc, l_sc, acc_sc = refs
    s = pl.program_id(2)
    qi = qi_ref[s]
    kj = kj_ref[s]

    @pl.when(first_ref[s] == 1)
    def _():
        m_sc[...] = jnp.full(m_sc.shape, NEG, F32)
        l_sc[...] = jnp.zeros(l_sc.shape, F32)
        acc_sc[...] = jnp.zeros(acc_sc.shape, F32)

    @pl.when(kj >= 0)
    def _():
        q = q_ref[...].reshape(rep * tq, LANE)
        sc = lax.dot_general(q, k_ref[0], _NT, preferred_element_type=F32)
        qpos = qi * tq + lax.broadcasted_iota(I32, (tq, tk), 0)
        kpos = kj * tk + lax.broadcasted_iota(I32, (tq, tk), 1)
        mask = kpos <= qpos
        if mode == "window":
            mask = mask & (kpos > qpos - WINDOW)
        if mode == "select":
            blk = lax.broadcasted_iota(I32, (MAX_SLC, tk), 0)
            col = lax.broadcasted_iota(I32, (MAX_SLC, tk), 1)
            expand = jnp.where(blk == kj * (tk // SLC_BLOCK) + jnp.right_shift(col, 6), 1.0, 0.0).astype(BF16)
            picked = jnp.dot(sel_ref[0], expand, preferred_element_type=F32)
            mask = mask & (picked > 0.5)
        sc = jnp.where(mask[None], sc.reshape(rep, tq, tk), NEG).reshape(rep * tq, tk)
        m_prev = m_sc[...]
        m_new = jnp.maximum(m_prev, jnp.max(sc, axis=1, keepdims=True))
        alpha = jnp.exp(m_prev - m_new)
        p = jnp.exp(sc - m_new)
        l_sc[...] = alpha * l_sc[...] + jnp.sum(p, axis=1, keepdims=True)
        acc_sc[...] = alpha * acc_sc[...] + jnp.dot(p.astype(BF16), v_ref[0], preferred_element_type=F32)
        m_sc[...] = m_new

    @pl.when(last_ref[s] == 1)
    def _():
        o_ref[...] = (acc_sc[...] / l_sc[...]).reshape(rep, tq, LANE).astype(o_ref.dtype)


def _flash_schedule(seq, tq, tk, mode):
    qi, kj, first, last = [], [], [], []
    for i in range(seq // tq):
        hi = (i * tq + tq - 1) // tk
        lo = 0 if mode != "window" else (i * tq - (WINDOW - 1)) // tk
        js = [j for j in range(lo, hi + 1)]
        for n, j in enumerate(js):
            qi.append(i)
            kj.append(j if j >= 0 else -1)
            first.append(1 if n == 0 else 0)
            last.append(1 if n == len(js) - 1 else 0)
    as_i32 = lambda v: jnp.asarray(np.asarray(v, np.int32))
    return as_i32(qi), as_i32(kj), as_i32(first), as_i32(last), len(qi)


def _flash(q, k, v, sel, *, batch, seq, mode, tq, tk):
    hq, T, _ = q.shape
    hkv = k.shape[0]
    rep = hq // hkv
    qi, kj, first, last, nsteps = _flash_schedule(seq, tq, tk, mode)
    nq, nk = seq // tq, seq // tk
    q_map = lambda g, b, s, qi, kj, fi, la: (g, b * nq + qi[s], 0)
    k_map = lambda g, b, s, qi, kj, fi, la: (g, b * nk + jnp.maximum(kj[s], 0), 0)
    in_specs = [pl.BlockSpec((rep, tq, LANE), q_map),
                pl.BlockSpec((1, tk, LANE), k_map),
                pl.BlockSpec((1, tk, LANE), k_map)]
    args = [q, k, v]
    if mode == "select":
        in_specs.append(pl.BlockSpec((1, tq, LANE), q_map))
        args.append(sel)
    rows = rep * tq
    return pl.pallas_call(
        functools.partial(_flash_body, mode=mode, rep=rep, tq=tq, tk=tk),
        grid_spec=pltpu.PrefetchScalarGridSpec(
            num_scalar_prefetch=4,
            grid=(hkv, batch, nsteps),
            in_specs=in_specs,
            out_specs=pl.BlockSpec((rep, tq, LANE), q_map),
            scratch_shapes=[pltpu.VMEM((rows, 1), F32), pltpu.VMEM((rows, 1), F32),
                            pltpu.VMEM((rows, LANE), F32)]),
        out_shape=jax.ShapeDtypeStruct((hq, T, LANE), BF16),
        compiler_params=_cparams("parallel", "parallel", "arbitrary"),
        name="flash_" + mode,
    )(qi, kj, first, last, *args)


def _attn_out_body(*refs, n_branch, heads):
    o_refs = refs[:n_branch]
    if n_branch > 1:
        gl_ref, w_ref, x_ref, g_ref, out_ref = refs[n_branch:]
        gates = _sigmoid(gl_ref[...])
    else:
        w_ref, x_ref, g_ref, out_ref = refs[n_branch:]
    acc = jnp.zeros(x_ref.shape, F32)
    for h in range(heads):
        if n_branch > 1:
            o = jnp.zeros(o_refs[0].shape[1:], F32)
            for c in range(n_branch):
                col = n_branch * h + c
                o = o + gates[:, col:col + 1] * o_refs[c][h].astype(F32)
            o = o.astype(BF16)
        else:
            o = o_refs[0][h]
        acc = acc + jnp.dot(o, w_ref[h], preferred_element_type=F32)
    out_ref[...] = x_ref[...] + g_ref[0] * acc


def _attn_out(os, gl, w_heads, x2, g, seq):
    T, Dm = x2.shape
    heads = w_heads.shape[0]
    tm = 256
    nb = seq // tm
    row = lambda i: (i, 0)
    o_spec = pl.BlockSpec((heads, tm, LANE), lambda i: (0, i, 0))
    in_specs = [o_spec] * len(os)
    args = list(os)
    if len(os) > 1:
        in_specs.append(pl.BlockSpec((tm, LANE), row))
        args.append(gl)
    in_specs += [pl.BlockSpec((heads, LANE, Dm), lambda i: (0, 0, 0)),
                 pl.BlockSpec((tm, Dm), row),
                 pl.BlockSpec((1, 1, Dm), lambda i: (i // nb, 0, 0))]
    args += [w_heads, x2, g]
    return pl.pallas_call(
        functools.partial(_attn_out_body, n_branch=len(os), heads=heads),
        grid=(T // tm,),
        in_specs=in_specs,
        out_specs=pl.BlockSpec((tm, Dm), row),
        out_shape=jax.ShapeDtypeStruct((T, Dm), F32),
        compiler_params=_cparams("parallel"),
        name="attn_out_%d" % len(os),
    )(*args)


def _pad_heads_rows(w, heads, dh):
    w = w.reshape(heads, dh, -1)
    return jnp.pad(w, ((0, 0), (0, LANE - dh), (0, 0))).astype(BF16)


def _pad_heads_cols(w, heads, dh):
    k = w.shape[0]
    w = w.reshape(k, heads, dh)
    return jnp.pad(w, ((0, 0), (0, 0), (0, LANE - dh))).reshape(k, heads * LANE)


def _rope_tables(pos, rot, offset):
    half = rot // 2
    inv_freq = ROPE_THETA ** (-jnp.arange(0, rot, 2, dtype=F32) / rot)
    ang = pos.astype(F32)[:, None] * inv_freq[None, :]
    cos, sin = jnp.cos(ang), jnp.sin(ang)
    n = pos.shape[0]
    c = jnp.ones((n, LANE), F32).at[:, offset:offset + rot].set(jnp.concatenate([cos, cos], axis=1))
    sa = jnp.zeros((n, LANE), F32).at[:, offset:offset + half].set(-sin)
    sb = jnp.zeros((n, LANE), F32).at[:, offset + half:offset + rot].set(sin)
    return c, sa, sb


def _mla_proj_body(x_ref, a_ref, b_ref, win_ref, qlg_ref, kvlg_ref, wuq_ref, wuk_ref, wuv_ref,
                   qg_ref, kg_ref, c_ref, sa_ref, sb_ref, q_ref, k_ref, v_ref):
    h = _normmod(x_ref[...], a_ref[0], b_ref[0]).astype(BF16)
    r = jnp.dot(h, win_ref[...], preferred_element_type=F32)
    q_lat = r[:, :MLA_Q_LORA]
    kv_lat = r[:, MLA_Q_LORA:MLA_Q_LORA + MLA_KV_LORA]
    kpe = r[:, MLA_Q_LORA + MLA_KV_LORA:]
    ql = q_lat * lax.rsqrt(jnp.mean(q_lat * q_lat, axis=-1, keepdims=True) + EPS) * qlg_ref[...]
    kvl = kv_lat * lax.rsqrt(jnp.mean(kv_lat * kv_lat, axis=-1, keepdims=True) + EPS) * kvlg_ref[...]
    kvl = kvl.astype(BF16)
    q = jnp.dot(ql.astype(BF16), wuq_ref[...], preferred_element_type=F32)
    kn = jnp.dot(kvl, wuk_ref[...], preferred_element_type=F32)
    v = jnp.dot(kvl, wuv_ref[...], preferred_element_type=F32)
    c, sa, sb = c_ref[...], sa_ref[...], sb_ref[...]

    def norm_rot(t, gain):
        t = t * lax.rsqrt(jnp.sum(t * t, axis=-1, keepdims=True) * (1.0 / MLA_QK) + EPS) * gain
        return _rope(t, c, sa, sb, MLA_ROPE // 2)

    for hd in range(MLA_HEADS):
        sl = slice(hd * LANE, (hd + 1) * LANE)
        q_ref[hd] = norm_rot(q[:, sl], qg_ref[...]).astype(BF16)
        k_ref[hd] = norm_rot(kn[:, sl] + kpe, kg_ref[...]).astype(BF16)
        v_ref[hd] = v[:, sl].astype(BF16)


def _mla_mixer(x2, a, b, g, w_in, q_lat_g, kv_lat_g, w_uq, w_ukv, q_gain, k_gain, w_out, batch, seq):
    T, Dm = x2.shape
    H = MLA_HEADS
    tm = 256
    nb = seq // tm
    scale = MLA_QK ** -0.5
    lat = MLA_Q_LORA + MLA_KV_LORA
    kpe_cols = jnp.zeros((Dm, LANE), F32).at[:, MLA_NOPE:MLA_QK].set(w_in[:, lat:])
    win_p = jnp.concatenate([w_in[:, :lat], kpe_cols], axis=1).astype(BF16)
    wuq_p = _pad_heads_cols(w_uq, H, MLA_QK).astype(BF16)
    wukv = w_ukv.reshape(MLA_KV_LORA, H, MLA_NOPE + MLA_V)
    wuk_p = _pad_heads_cols(wukv[:, :, :MLA_NOPE].reshape(MLA_KV_LORA, -1), H, MLA_NOPE).astype(BF16)
    wuv_p = _pad_heads_cols(wukv[:, :, MLA_NOPE:].reshape(MLA_KV_LORA, -1), H, MLA_V).astype(BF16)
    pad_gain = lambda v: jnp.pad(v, (0, LANE - MLA_QK)).reshape(1, LANE)
    c, sa, sb = _rope_tables(jnp.arange(seq), MLA_ROPE, MLA_NOPE)
    row = lambda i: (i, 0)
    full = lambda i: (0, 0)
    per_b = lambda i: (i // nb, 0, 0)
    pos = lambda i: (i % nb, 0)
    head_out = pl.BlockSpec((H, tm, LANE), lambda i: (0, i, 0))
    hshape = jax.ShapeDtypeStruct((H, T, LANE), BF16)
    q, k, v = pl.pallas_call(
        _mla_proj_body,
        grid=(T // tm,),
        in_specs=[pl.BlockSpec((tm, Dm), row),
                  pl.BlockSpec((1, 1, Dm), per_b),
                  pl.BlockSpec((1, 1, Dm), per_b),
                  pl.BlockSpec(win_p.shape, full),
                  pl.BlockSpec((1, MLA_Q_LORA), full),
                  pl.BlockSpec((1, MLA_KV_LORA), full),
                  pl.BlockSpec(wuq_p.shape, full),
                  pl.BlockSpec(wuk_p.shape, full),
                  pl.BlockSpec(wuv_p.shape, full),
                  pl.BlockSpec((1, LANE), full),
                  pl.BlockSpec((1, LANE), full),
                  pl.BlockSpec((tm, LANE), pos),
                  pl.BlockSpec((tm, LANE), pos),
                  pl.BlockSpec((tm, LANE), pos)],
        out_specs=[head_out, head_out, head_out],
        out_shape=[hshape, hshape, hshape],
        compiler_params=_cparams("parallel"),
        name="mla_proj",
    )(x2, a, b, win_p, q_lat_g.reshape(1, -1), kv_lat_g.reshape(1, -1), wuq_p, wuk_p, wuv_p,
      pad_gain(q_gain * scale), pad_gain(k_gain), c, sa, sb)
    blk = min(1024, seq)
    o = _flash(q, k, v, None, batch=batch, seq=seq, mode="causal", tq=blk, tk=blk)
    return _attn_out([o], None, _pad_heads_rows(w_out, H, MLA_V), x2, g, seq)


N_KV_STREAMS = 6


def _nsa_proj_body(x_ref, a_ref, b_ref, w_ref, qg_ref, kg_ref, c_ref, sa_ref, sb_ref,
                   q_ref, kv_ref, gl_ref):
    h = _normmod(x_ref[...], a_ref[0], b_ref[0]).astype(BF16)
    r = jnp.dot(h, w_ref[...], preferred_element_type=F32)
    c, sa, sb = c_ref[...], sa_ref[...], sb_ref[...]

    def norm_rot(t, gain):
        t = t * lax.rsqrt(jnp.sum(t * t, axis=-1, keepdims=True) * (1.0 / NSA_DH) + EPS) * gain
        return _rope(t, c, sa, sb, NSA_ROT // 2)

    for hd in range(NSA_HEADS):
        q_ref[hd] = norm_rot(r[:, hd * LANE:(hd + 1) * LANE], qg_ref[...]).astype(BF16)
    base = NSA_HEADS * LANE
    for st in range(N_KV_STREAMS):
        for gi in range(NSA_GROUPS):
            off = base + (st * NSA_GROUPS + gi) * LANE
            t = r[:, off:off + LANE]
            if st == 2:
                t = norm_rot(t, kg_ref[1:2, :])
            elif st == 4:
                t = norm_rot(t, kg_ref[2:3, :])
            kv_ref[st * NSA_GROUPS + gi] = t.astype(BF16)
    gl_ref[...] = r[:, base + N_KV_STREAMS * NSA_GROUPS * LANE:]


def _nsa_compress_body(x_ref, pea_ref, peb_ref, w1a_ref, w1b_ref, w2_ref, kg_ref, c_ref, sa_ref, sb_ref,
                       o_ref, *, is_key, n_cmp):
    x = x_ref[0].astype(F32)
    xa = (x + pea_ref[...]).astype(BF16)
    xb = (x + peb_ref[...]).astype(BF16)
    za = jnp.dot(xa, w1a_ref[...], preferred_element_type=F32)
    zb = jnp.dot(xb, w1b_ref[...], preferred_element_type=F32)
    rows = za.shape[0]
    z = _silu(za + pltpu.roll(zb, rows - 1, 0))
    t = jnp.dot(z.astype(BF16), w2_ref[...], preferred_element_type=F32)
    if is_key:
        t = t * lax.rsqrt(jnp.sum(t * t, axis=-1, keepdims=True) * (1.0 / NSA_DH) + EPS) * kg_ref[...]
        t = _rope(t, c_ref[...], sa_ref[...], sb_ref[...], NSA_ROT // 2)
    valid = lax.broadcasted_iota(I32, t.shape, 0) < n_cmp
    o_ref[0] = jnp.where(valid, t, 0.0).astype(BF16)


def _nsa_cmp_select_body(q_ref, kc_ref, vc_ref, oc_ref, sel_ref, p_sc, imp_sc, *, tq, ncp, n_cmp, n_top):
    i = pl.program_id(2)
    t0 = i * tq
    kc = kc_ref[0]
    vc = vc_ref[0]
    rep = q_ref.shape[0]
    q = q_ref[...].reshape(rep * tq, LANE)
    sc = lax.dot_general(q, kc, _NT, preferred_element_type=F32)
    qpos = t0 + lax.broadcasted_iota(I32, (tq, ncp), 0)
    blk = lax.broadcasted_iota(I32, (tq, ncp), 1)
    mask = ((blk * CMP_STRIDE + (CMP_BLOCK - 1)) <= qpos) & (blk < n_cmp)
    sc = jnp.where(mask[None], sc.reshape(rep, tq, ncp), NEG)
    p = jnp.exp(sc - jnp.max(sc, axis=-1, keepdims=True))
    p = jnp.where(mask[None], p / jnp.sum(p, axis=-1, keepdims=True), 0.0)
    oc = jnp.dot(p.reshape(rep * tq, ncp).astype(BF16), vc, preferred_element_type=F32)
    oc_ref[...] = oc.reshape(rep, tq, LANE).astype(BF16)
    qpos_t = t0 + lax.broadcasted_iota(I32, (ncp, tq), 1)
    blk_t = lax.broadcasted_iota(I32, (ncp, tq), 0)
    mask_t = ((blk_t * CMP_STRIDE + (CMP_BLOCK - 1)) <= qpos_t) & (blk_t < n_cmp)
    psum = jnp.zeros((ncp, tq), F32)
    for r in range(rep):
        st = lax.dot_general(kc, q_ref[r], _NT, preferred_element_type=F32)
        st = jnp.where(mask_t, st, NEG)
        pt = jnp.exp(st - jnp.max(st, axis=0, keepdims=True))
        psum = psum + jnp.where(mask_t, pt / jnp.sum(pt, axis=0, keepdims=True), 0.0)
    p_sc[...] = jnp.zeros(p_sc.shape, F32)
    p_sc[8:8 + ncp, :] = psum
    per = SLC_BLOCK // CMP_STRIDE
    imp = p_sc[pl.ds(7, MAX_SLC, stride=per), :]
    for k in range(1, per + 1):
        imp = imp + p_sc[pl.ds(7 + k, MAX_SLC, stride=per), :]
    jb = lax.broadcasted_iota(I32, (MAX_SLC, tq), 0)
    qp = t0 + lax.broadcasted_iota(I32, (MAX_SLC, tq), 1)
    imp = jnp.where(jb * SLC_BLOCK <= qp, imp, -BIG)
    imp = jnp.where((jb == 0) | (jb == jnp.right_shift(qp, 6)), BIG, imp)
    imp_sc[...] = imp

    def count(ii, cnt):
        row = imp_sc[pl.ds(ii, 1), :]
        ahead = (row > imp) | ((row == imp) & (ii < jb))
        return cnt + jnp.where(ahead, 1.0, 0.0)

    rank = lax.fori_loop(0, MAX_SLC, count, jnp.zeros((MAX_SLC, tq), F32))
    sel_t = jnp.where(rank < n_top, 1.0, 0.0)
    sel_ref[0] = sel_t.T.astype(BF16)


def _nsa_mixer(x2, a, b, g, w_in, w_cmp1, w_cmp2, cmp_pos, q_gain, k_gain, w_out, batch, seq):
    T, Dm = x2.shape
    H, G, dh = NSA_HEADS, NSA_GROUPS, NSA_DH
    scale = dh ** -0.5
    n_cmp = seq // CMP_STRIDE - 1
    ncp = seq // CMP_STRIDE
    n_slc = seq // SLC_BLOCK
    assert n_slc <= MAX_SLC and ncp <= MAX_SLC * (SLC_BLOCK // CMP_STRIDE)
    n_top = min(SLC_TOP_N, n_slc)
    tm = 256
    nb = seq // tm
    q_cols = _pad_heads_cols(w_in[:, :H * dh], H, dh)
    kv_cols = _pad_heads_cols(w_in[:, H * dh:H * dh + N_KV_STREAMS * G * dh], N_KV_STREAMS * G, dh)
    gl_cols = jnp.pad(w_in[:, H * dh + N_KV_STREAMS * G * dh:], ((0, 0), (0, LANE - 3 * H)))
    w_p = jnp.concatenate([q_cols, kv_cols, gl_cols], axis=1).astype(BF16)
    pad_gain = lambda v: jnp.pad(v, ((0, 0), (0, LANE - dh)))
    c, sa, sb = _rope_tables(jnp.arange(seq), NSA_ROT, 0)
    row = lambda i: (i, 0)
    full = lambda i: (0, 0)
    per_b = lambda i: (i // nb, 0, 0)
    pos = lambda i: (i % nb, 0)
    q, kv, gl = pl.pallas_call(
        _nsa_proj_body,
        grid=(T // tm,),
        in_specs=[pl.BlockSpec((tm, Dm), row),
                  pl.BlockSpec((1, 1, Dm), per_b),
                  pl.BlockSpec((1, 1, Dm), per_b),
                  pl.BlockSpec(w_p.shape, full),
                  pl.BlockSpec((1, LANE), full),
                  pl.BlockSpec((3, LANE), full),
                  pl.BlockSpec((tm, LANE), pos),
                  pl.BlockSpec((tm, LANE), pos),
                  pl.BlockSpec((tm, LANE), pos)],
        out_specs=[pl.BlockSpec((H, tm, LANE), lambda i: (0, i, 0)),
                   pl.BlockSpec((N_KV_STREAMS * G, tm, LANE), lambda i: (0, i, 0)),
                   pl.BlockSpec((tm, LANE), row)],
        out_shape=[jax.ShapeDtypeStruct((H, T, LANE), BF16),
                   jax.ShapeDtypeStruct((N_KV_STREAMS * G, T, LANE), BF16),
                   jax.ShapeDtypeStruct((T, LANE), F32)],
        compiler_params=_cparams("parallel"),
        name="nsa_proj",
    )(x2, a, b, w_p, pad_gain(q_gain.reshape(1, dh) * scale), pad_gain(k_gain), c, sa, sb)
    kc_raw, vc_raw, ks, vs, kw, vw = [kv[s * G:(s + 1) * G] for s in range(N_KV_STREAMS)]

    cmp_end = jnp.arange(ncp) * CMP_STRIDE + (CMP_BLOCK - 1)
    cc, csa, csb = _rope_tables(cmp_end, NSA_ROT, 0)
    kdim = CMP_STRIDE * LANE

    def compress(raw, w1, w2, pe, is_key):
        x16 = raw.reshape(G, T // CMP_STRIDE, kdim)
        w1p = jnp.pad(w1, ((0, 0), (0, LANE - dh), (0, LANE - dh)))
        w1a = w1p[:CMP_STRIDE].reshape(kdim, LANE).astype(BF16)
        w1b = w1p[CMP_STRIDE:].reshape(kdim, LANE).astype(BF16)
        pep = jnp.pad(pe, ((0, 0), (0, LANE - dh)))
        pea = pep[:CMP_STRIDE].reshape(1, kdim)
        peb = pep[CMP_STRIDE:].reshape(1, kdim)
        w2p = jnp.pad(w2, ((0, LANE - dh), (0, LANE - dh))).astype(BF16)
        const = lambda gi, bi: (0, 0)
        return pl.pallas_call(
            functools.partial(_nsa_compress_body, is_key=is_key, n_cmp=n_cmp),
            grid=(G, batch),
            in_specs=[pl.BlockSpec((1, ncp, kdim), lambda gi, bi: (gi, bi, 0)),
                      pl.BlockSpec((1, kdim), const),
                      pl.BlockSpec((1, kdim), const),
                      pl.BlockSpec((kdim, LANE), const),
                      pl.BlockSpec((kdim, LANE), const),
                      pl.BlockSpec((LANE, LANE), const),
                      pl.BlockSpec((1, LANE), const),
                      pl.BlockSpec((ncp, LANE), const),
                      pl.BlockSpec((ncp, LANE), const),
                      pl.BlockSpec((ncp, LANE), const)],
            out_specs=pl.BlockSpec((1, ncp, LANE), lambda gi, bi: (gi, bi, 0)),
            out_shape=jax.ShapeDtypeStruct((G, batch * ncp, LANE), BF16),
            compiler_params=_cparams("parallel", "parallel"),
            name="nsa_compress_" + ("k" if is_key else "v"),
        )(x16, pea, peb, w1a, w1b, w2p, pad_gain(k_gain)[0:1], cc, csa, csb)

    kc = compress(kc_raw, w_cmp1[0], w_cmp2[0], cmp_pos[0], True)
    vc = compress(vc_raw, w_cmp1[1], w_cmp2[1], cmp_pos[1], False)

    tq = 128
    nq = seq // tq
    q_map = lambda gi, bi, i: (gi, bi * nq + i, 0)
    c_map = lambda gi, bi, i: (gi, bi, 0)
    o_c, sel = pl.pallas_call(
        functools.partial(_nsa_cmp_select_body, tq=tq, ncp=ncp, n_cmp=n_cmp, n_top=n_top),
        grid=(G, batch, nq),
        in_specs=[pl.BlockSpec((NSA_REP, tq, LANE), q_map),
                  pl.BlockSpec((1, ncp, LANE), c_map),
                  pl.BlockSpec((1, ncp, LANE), c_map)],
        out_specs=[pl.BlockSpec((NSA_REP, tq, LANE), q_map),
                   pl.BlockSpec((1, tq, LANE), q_map)],
        out_shape=[jax.ShapeDtypeStruct((H, T, LANE), BF16),
                   jax.ShapeDtypeStruct((G, T, LANE), BF16)],
        scratch_shapes=[pltpu.VMEM((8 + MAX_SLC * (SLC_BLOCK // CMP_STRIDE), tq), F32),
                        pltpu.VMEM((MAX_SLC, tq), F32)],
        compiler_params=_cparams("parallel", "parallel", "parallel"),
        name="nsa_cmp_select",
    )(q, kc, vc)

    o_s = _flash(q, ks, vs, sel, batch=batch, seq=seq, mode="select", tq=256, tk=min(1024, seq))
    o_w = _flash(q, kw, vw, None, batch=batch, seq=seq, mode="window", tq=256, tk=WINDOW)
    return _attn_out([o_c, o_s, o_w], gl, _pad_heads_rows(w_out, H, dh), x2, g, seq)


def _router_body(x_ref, a_ref, b_ref, rwt_ref, rb_ref, tri_ref, h_ref, e_ref, w_ref, rank_ref, cnt_ref,
                 carry_sc, *, tm):
    i = pl.program_id(0)

    @pl.when(i == 0)
    def _():
        carry_sc[...] = jnp.zeros(carry_sc.shape, F32)

    h = _normmod(x_ref[...], a_ref[0], b_ref[0])
    h_ref[...] = h
    logits = lax.dot_general(rwt_ref[...], h, _NT, preferred_element_type=F32, precision=HIGHEST)
    scores = _sigmoid(logits)
    biased = scores + rb_ref[...]
    ng, per = N_GROUPS, EXPERTS_PER_GROUP
    row = lambda arr, r: arr[r:r + 1, :]
    gsel = jnp.zeros((1, tm), I32)
    best = None
    for gi in range(ng):
        v = [row(biased, gi * per + k) for k in range(per)]
        top2 = None
        for p in range(per):
            for q in range(p + 1, per):
                s = v[p] + v[q]
                top2 = s if top2 is None else jnp.maximum(top2, s)
        if best is None:
            best = top2
        else:
            better = top2 > best
            gsel = jnp.where(better, gi, gsel)
            best = jnp.where(better, top2, best)
    cb, cs = [], []
    for k in range(per):
        b_k = row(biased, k)
        s_k = row(scores, k)
        for gi in range(1, ng):
            hit = gsel == gi
            b_k = jnp.where(hit, row(biased, gi * per + k), b_k)
            s_k = jnp.where(hit, row(scores, gi * per + k), s_k)
        cb.append(b_k)
        cs.append(s_k)

    def argmax_first(vals):
        idx = jnp.zeros((1, tm), I32)
        top = vals[0]
        for k in range(1, per):
            better = vals[k] > top
            idx = jnp.where(better, k, idx)
            top = jnp.where(better, vals[k], top)
        return idx

    def pick(vals, idx):
        out = vals[0]
        for k in range(1, per):
            out = jnp.where(idx == k, vals[k], out)
        return out

    i1 = argmax_first(cb)
    i2 = argmax_first([jnp.where(i1 == k, -jnp.inf, cb[k]) for k in range(per)])
    w1 = pick(cs, i1)
    w2 = pick(cs, i2)
    tot = w1 + w2
    e1 = gsel * per + i1
    e2 = gsel * per + i2
    eid = lax.broadcasted_iota(I32, (N_EXPERTS, tm), 0)
    hot = (eid == e1) | (eid == e2)
    onehot = jnp.where(hot, 1.0, 0.0)
    before = carry_sc[...] + jnp.dot(onehot.astype(BF16), tri_ref[...], preferred_element_type=F32)
    r1 = jnp.sum(jnp.where(eid == e1, before, 0.0), axis=0, keepdims=True)
    r2 = jnp.sum(jnp.where(eid == e2, before, 0.0), axis=0, keepdims=True)
    carry = carry_sc[...] + jnp.sum(onehot, axis=1, keepdims=True)
    carry_sc[...] = carry
    cnt_ref[...] = jnp.broadcast_to(carry, cnt_ref.shape)
    zi = jnp.zeros((6, tm), I32)
    e_ref[...] = jnp.concatenate([e1, e2, zi], axis=0)
    rank_ref[...] = jnp.concatenate([r1.astype(I32), r2.astype(I32), zi], axis=0)
    w_ref[...] = jnp.concatenate([w1 / tot, w2 / tot, jnp.zeros((6, tm), F32)], axis=0)


def _moe_expert_body(te_ref, nv_ref, src_ref, h_hbm, win_ref, wout_ref, y_ref, xbuf, sem, *, tg):
    i = pl.program_id(0)

    def row_copy(tok, r):
        return pltpu.make_async_copy(h_hbm.at[pl.ds(tok, 1)], xbuf.at[pl.ds(r, 1)], sem)

    @pl.when(i < nv_ref[0])
    def _():
        def issue(r, carry):
            row_copy(src_ref[0, 0, r], r).start()
            return carry

        lax.fori_loop(0, tg, issue, 0)

        def drain(r, carry):
            row_copy(0, r).wait()
            return carry

        lax.fori_loop(0, tg, drain, 0)
        x = xbuf[...].astype(BF16)
        gu = jnp.dot(x, win_ref[0], preferred_element_type=F32)
        act = (_silu(gu[:, :D_EXPERT]) * gu[:, D_EXPERT:]).astype(BF16)
        y_ref[...] = jnp.dot(act, wout_ref[0], preferred_element_type=F32)

    @pl.when(i >= nv_ref[0])
    def _():
        y_ref[...] = jnp.zeros(y_ref.shape, F32)


def _moe_combine_body(dst_ref, y_hbm, x_ref, w_ref, g_ref, o_ref, ybuf, sem, *, tm):
    def row_copy(slot, k, r):
        return pltpu.make_async_copy(y_hbm.at[pl.ds(slot, 1)], ybuf.at[k, pl.ds(r, 1)], sem)

    def issue(r, carry):
        row_copy(dst_ref[0, 0, r], 0, r).start()
        row_copy(dst_ref[0, 1, r], 1, r).start()
        return carry

    lax.fori_loop(0, tm, issue, 0)

    def drain(r, carry):
        row_copy(0, 0, r).wait()
        row_copy(0, 1, r).wait()
        return carry

    lax.fori_loop(0, tm, drain, 0)
    w = w_ref[...]
    y = w[:, 0:1] * ybuf[0] + w[:, 1:2] * ybuf[1]
    o_ref[...] = x_ref[...] + g_ref[0] * y


def _grouped_moe(x2, a, b, g, router_w, router_bias, w_in, w_out, seq):
    T, Dm = x2.shape
    E = N_EXPERTS
    tm = 512
    nb = seq // tm
    nt = T // tm
    row = lambda i: (i, 0)
    full = lambda i: (0, 0)
    per_b = lambda i: (i // nb, 0, 0)
    lanes = lambda i: (0, i)
    tri = jnp.asarray(np.triu(np.ones((tm, tm), np.float32), 1)).astype(BF16)
    h, e, w, rank, cnt = pl.pallas_call(
        functools.partial(_router_body, tm=tm),
        grid=(nt,),
        in_specs=[pl.BlockSpec((tm, Dm), row),
                  pl.BlockSpec((1, 1, Dm), per_b),
                  pl.BlockSpec((1, 1, Dm), per_b),
                  pl.BlockSpec((E, Dm), full),
                  pl.BlockSpec((E, 1), full),
                  pl.BlockSpec((tm, tm), full)],
        out_specs=[pl.BlockSpec((tm, Dm), row),
                   pl.BlockSpec((8, tm), lanes),
                   pl.BlockSpec((8, tm), lanes),
                   pl.BlockSpec((8, tm), lanes),
                   pl.BlockSpec((E, LANE), full)],
        out_shape=[jax.ShapeDtypeStruct((T, Dm), F32),
                   jax.ShapeDtypeStruct((8, T), I32),
                   jax.ShapeDtypeStruct((8, T), F32),
                   jax.ShapeDtypeStruct((8, T), I32),
                   jax.ShapeDtypeStruct((E, LANE), F32)],
        scratch_shapes=[pltpu.VMEM((E, 1), F32)],
        compiler_params=_cparams("arbitrary"),
        name="moe_router",
    )(x2, a, b, router_w.T, router_bias.reshape(E, 1), tri)

    tg = MOE_TILE
    n_tiles = (2 * T) // tg + E
    n_slots = n_tiles * tg
    counts = cnt[:, 0].astype(I32)
    padded = ((counts + tg - 1) // tg) * tg
    ends = jnp.cumsum(padded)
    starts = ends - padded
    dest = starts[e[:2]] + rank[:2]
    tok = jnp.broadcast_to(jnp.arange(T, dtype=I32)[None], (2, T))
    src = jnp.zeros((n_slots,), I32).at[dest.reshape(-1)].set(tok.reshape(-1))
    tile_start = jnp.arange(n_tiles, dtype=I32) * tg
    tile_expert = jnp.minimum(jnp.searchsorted(ends, tile_start, side="right"), E - 1).astype(I32)
    n_valid = (ends[-1] // tg).astype(I32).reshape(1)

    y = pl.pallas_call(
        functools.partial(_moe_expert_body, tg=tg),
        grid_spec=pltpu.PrefetchScalarGridSpec(
            num_scalar_prefetch=2,
            grid=(n_tiles,),
            in_specs=[pl.BlockSpec((1, 1, tg), lambda i, te, nv: (i, 0, 0), memory_space=pltpu.SMEM),
                      pl.BlockSpec(memory_space=pl.ANY),
                      pl.BlockSpec((1, Dm, 2 * D_EXPERT), lambda i, te, nv: (te[i], 0, 0)),
                      pl.BlockSpec((1, D_EXPERT, Dm), lambda i, te, nv: (te[i], 0, 0))],
            out_specs=pl.BlockSpec((tg, Dm), lambda i, te, nv: (i, 0)),
            scratch_shapes=[pltpu.VMEM((tg, Dm), F32), pltpu.SemaphoreType.DMA(())]),
        out_shape=jax.ShapeDtypeStruct((n_slots, Dm), F32),
        compiler_params=_cparams("arbitrary"),
        name="moe_experts",
    )(tile_expert, n_valid, src.reshape(n_tiles, 1, tg), h, w_in.astype(BF16), w_out.astype(BF16))

    tc = 256
    nbc = seq // tc
    dst3 = dest.reshape(2, T // tc, tc).transpose(1, 0, 2)
    return pl.pallas_call(
        functools.partial(_moe_combine_body, tm=tc),
        grid=(T // tc,),
        in_specs=[pl.BlockSpec((1, 2, tc), lambda i: (i, 0, 0), memory_space=pltpu.SMEM),
                  pl.BlockSpec(memory_space=pl.ANY),
                  pl.BlockSpec((tc, Dm), row),
                  pl.BlockSpec((tc, 2), row),
                  pl.BlockSpec((1, 1, Dm), lambda i: (i // nbc, 0, 0))],
        out_specs=pl.BlockSpec((tc, Dm), row),
        out_shape=jax.ShapeDtypeStruct((T, Dm), F32),
        scratch_shapes=[pltpu.VMEM((2, tc, Dm), F32), pltpu.SemaphoreType.DMA(())],
        compiler_params=_cparams("arbitrary"),
        name="moe_combine",
    )(dst3, y, x2, w[:2].T, g)


def kernel(x, c, norm_mix_g, norm_ffn_g, w_ada, b_ada, conv_w_pw1, conv_b_pw1, conv_w_dw, conv_b_dw, conv_ln_g, conv_ln_b, conv_w_pw2, conv_b_pw2, nsa_w_in, nsa_w_cmp1, nsa_w_cmp2, nsa_cmp_pos, nsa_q_gain, nsa_k_gain, nsa_w_out, mla_w_in, mla_q_lat_g, mla_kv_lat_g, mla_w_uq, mla_w_ukv, mla_q_gain, mla_k_gain, mla_w_out, router_w, router_bias, moe_w_in, moe_w_out):
    B, S, Dm = x.shape
    depth = w_ada.shape[0]
    mods = _ada(c, w_ada, b_ada)
    x2 = x.reshape(B * S, Dm)
    for i in range(depth):
        sh1, sc1, g1, sh2, sc2, g2 = [m.reshape(B, 1, Dm) for m in jnp.split(mods[i], 6, axis=-1)]
        a1 = norm_mix_g[i] * (1.0 + sc1)
        kind, j = i % 3, i // 3
        if kind == 0:
            x2 = _conv_mixer(x2, a1, sh1, g1, conv_w_pw1[j], conv_b_pw1[j], conv_w_dw[j], conv_b_dw[j],
                             conv_ln_g[j], conv_ln_b[j], conv_w_pw2[j], conv_b_pw2[j], S)
        elif kind == 1:
            x2 = _nsa_mixer(x2, a1, sh1, g1, nsa_w_in[j], nsa_w_cmp1[j], nsa_w_cmp2[j], nsa_cmp_pos[j],
                            nsa_q_gain[j], nsa_k_gain[j], nsa_w_out[j], B, S)
        else:
            x2 = _mla_mixer(x2, a1, sh1, g1, mla_w_in[j], mla_q_lat_g[j], mla_kv_lat_g[j], mla_w_uq[j],
                            mla_w_ukv[j], mla_q_gain[j], mla_k_gain[j], mla_w_out[j], B, S)
        a2 = norm_ffn_g[i] * (1.0 + sc2)
        x2 = _grouped_moe(x2, a2, sh2, g2, router_w, router_bias, moe_w_in[i], moe_w_out[i], S)
    return x2.reshape(B, S, Dm)
```

```python
import functools
import math

import numpy as np
import jax
import jax.numpy as jnp
from jax import lax
from jax.experimental import pallas as pl
from jax.experimental.pallas import tpu as pltpu

F32 = jnp.float32
BF16 = jnp.bfloat16
I32 = jnp.int32
HIGHEST = lax.Precision.HIGHEST

EPS = 1e-6
NEG = -1e30
BIG = 1e30
ROPE_THETA = 500000.0
LANE = 128
VMEM_LIMIT = 56 * 1024 * 1024

D_MODEL = 1024
CONV_WIDTH = 31
CONV_HALO = 32

NSA_HEADS = 16
NSA_GROUPS = 4
NSA_REP = NSA_HEADS // NSA_GROUPS
NSA_DH = 64
NSA_ROT = 16
CMP_BLOCK = 32
CMP_STRIDE = 16
SLC_BLOCK = 64
SLC_TOP_N = 16
WINDOW = 512
MAX_SLC = 128

MLA_HEADS = 16
MLA_Q_LORA = 384
MLA_KV_LORA = 256
MLA_NOPE = 64
MLA_ROPE = 32
MLA_V = 64
MLA_QK = MLA_NOPE + MLA_ROPE

N_EXPERTS = 16
N_GROUPS = 4
EXPERTS_PER_GROUP = 4
D_EXPERT = 512
MOE_TILE = 512

_NT = (((1,), (1,)), ((), ()))


def _cparams(*sem):
    return pltpu.CompilerParams(dimension_semantics=sem, vmem_limit_bytes=VMEM_LIMIT)


def _sigmoid(x):
    return 1.0 / (1.0 + jnp.exp(-x))


def _silu(x):
    return x * _sigmoid(x)


def _normmod(x, a, b):
    ms = jnp.mean(x * x, axis=-1, keepdims=True)
    return x * lax.rsqrt(ms + EPS) * a + b


def _rope(x, c, sa, sb, half):
    n = x.shape[-1]
    return x * c + pltpu.roll(x, n - half, 1) * sa + pltpu.roll(x, half, 1) * sb


def _ada_body(c_ref, w_ref, b_ref, o_ref):
    c = c_ref[...]
    o_ref[0] = jnp.dot(_silu(c), w_ref[0], preferred_element_type=F32, precision=HIGHEST) + b_ref[0]


def _ada(c, w_ada, b_ada):
    B = c.shape[0]
    L, Dm, N = w_ada.shape
    Bp = -(-B // 8) * 8
    cp = jnp.pad(c, ((0, Bp - B), (0, 0)))
    tn = 1536
    out = pl.pallas_call(
        _ada_body,
        grid=(L, N // tn),
        in_specs=[pl.BlockSpec((Bp, Dm), lambda l, j: (0, 0)),
                  pl.BlockSpec((1, Dm, tn), lambda l, j: (l, 0, j)),
                  pl.BlockSpec((1, 1, tn), lambda l, j: (l, 0, j))],
        out_specs=pl.BlockSpec((1, Bp, tn), lambda l, j: (l, 0, j)),
        out_shape=jax.ShapeDtypeStruct((L, Bp, N), F32),
        compiler_params=_cparams("arbitrary", "arbitrary"),
        name="adaln",
    )(cp, w_ada, b_ada.reshape(L, 1, N))
    return out[:, :B]


def _conv_pw1_body(x_ref, a_ref, b_ref, w_ref, bias_ref, u_ref):
    h = _normmod(x_ref[...], a_ref[0], b_ref[0]).astype(BF16)
    r = jnp.dot(h, w_ref[...], preferred_element_type=F32) + bias_ref[...]
    d = u_ref.shape[-1]
    u_ref[...] = r[:, :d] * _sigmoid(r[:, d:])


def _conv_dw_body(u_ref, uh_ref, wdw_ref, bdw_ref, lng_ref, lnb_ref, w2_ref, b2_ref, x_ref, g_ref,
                  o_ref, ext_ref, *, tm, seq):
    i = pl.program_id(0)
    at_seq_start = (i * tm) % seq == 0
    ext_ref[0:CONV_HALO, :] = jnp.where(at_seq_start, 0.0, uh_ref[...])
    ext_ref[CONV_HALO:, :] = u_ref[...]
    base = CONV_HALO - (CONV_WIDTH - 1)
    acc = jnp.zeros((tm, u_ref.shape[-1]), F32) + bdw_ref[...]
    for k in range(CONV_WIDTH):
        acc = acc + wdw_ref[k:k + 1, :] * ext_ref[base + k:base + k + tm, :]
    mu = jnp.mean(acc, axis=-1, keepdims=True)
    dlt = acc - mu
    var = jnp.mean(dlt * dlt, axis=-1, keepdims=True)
    y = dlt * lax.rsqrt(var + EPS) * lng_ref[...] + lnb_ref[...]
    z = _silu(y).astype(BF16)
    out = jnp.dot(z, w2_ref[...], preferred_element_type=F32) + b2_ref[...]
    o_ref[...] = x_ref[...] + g_ref[0] * out


def _conv_mixer(x2, a, b, g, w_pw1, b_pw1, w_dw, b_dw, ln_g, ln_b, w_pw2, b_pw2, seq):
    T, Dm = x2.shape
    tm = 512
    nb = seq // tm
    row = lambda i: (i, 0)
    per_b = lambda i: (i // nb, 0, 0)
    full = lambda i: (0, 0)
    u = pl.pallas_call(
        _conv_pw1_body,
        grid=(T // tm,),
        in_specs=[pl.BlockSpec((tm, Dm), row),
                  pl.BlockSpec((1, 1, Dm), per_b),
                  pl.BlockSpec((1, 1, Dm), per_b),
                  pl.BlockSpec((Dm, 2 * Dm), full),
                  pl.BlockSpec((1, 2 * Dm), full)],
        out_specs=pl.BlockSpec((tm, Dm), row),
        out_shape=jax.ShapeDtypeStruct((T, Dm), F32),
        compiler_params=_cparams("parallel"),
        name="conv_pw1_glu",
    )(x2, a, b, w_pw1.astype(BF16), b_pw1.reshape(1, -1))
    hb = tm // CONV_HALO
    wdw = jnp.pad(w_dw, ((0, CONV_HALO - CONV_WIDTH), (0, 0)))
    vec = lambda v: v.reshape(1, -1)
    return pl.pallas_call(
        functools.partial(_conv_dw_body, tm=tm, seq=seq),
        grid=(T // tm,),
        in_specs=[pl.BlockSpec((tm, Dm), row),
                  pl.BlockSpec((CONV_HALO, Dm), lambda i: (jnp.maximum(i * hb - 1, 0), 0)),
                  pl.BlockSpec((CONV_HALO, Dm), full),
                  pl.BlockSpec((1, Dm), full),
                  pl.BlockSpec((1, Dm), full),
                  pl.BlockSpec((1, Dm), full),
                  pl.BlockSpec((Dm, Dm), full),
                  pl.BlockSpec((1, Dm), full),
                  pl.BlockSpec((tm, Dm), row),
                  pl.BlockSpec((1, 1, Dm), per_b)],
        out_specs=pl.BlockSpec((tm, Dm), row),
        out_shape=jax.ShapeDtypeStruct((T, Dm), F32),
        scratch_shapes=[pltpu.VMEM((tm + CONV_HALO, Dm), F32)],
        compiler_params=_cparams("parallel"),
        name="conv_dw_ln_pw2",
    )(u, u, wdw, vec(b_dw), vec(ln_g), vec(ln_b), w_pw2.astype(BF16), vec(b_pw2), x2, g)


FLASH_FIRST, FLASH_LAST, FLASH_MASKED = 1, 2, 4
FLASH_CHUNK = 16
SEL_OFF = -(2.0 ** 100)
ONE_LANE = 64
LOG2E = 1.4426950408889634
FLASH_NPROB = 2


def _one_lane():
    return jnp.where(lax.broadcasted_iota(I32, (1, LANE), 1) == ONE_LANE, 1.0, 0.0)


def _flash_body(qi_ref, kj_ref, flag_ref, *refs, mode, rep, tq, tk, nprob):
    if mode == "select":
        q_ref, k_ref, v_ref, selb_ref, o_ref, qa_sc, s_sc, p_sc, mb_sc, al_sc, acc_sc = refs
    else:
        q_ref, k_ref, v_ref, o_ref, s_sc, p_sc, mb_sc, al_sc, acc_sc = refs
    step = pl.program_id(2)
    qi = qi_ref[step]
    kj = kj_ref[step]
    flag = flag_ref[step]
    rows = rep * tq
    ch = FLASH_CHUNK
    nl = tk // LANE
    lane_fold = lambda t, op: functools.reduce(op, [t[:, i * LANE:(i + 1) * LANE] for i in range(nl)])

    @pl.when((flag & FLASH_FIRST) != 0)
    def _():
        mb_sc[...] = jnp.full(mb_sc.shape, NEG, F32)
        acc_sc[...] = jnp.zeros(acc_sc.shape, F32)
        if mode == "select":
            for pr in range(nprob):
                qa_sc[pr, :, :LANE] = q_ref[pr * rep:(pr + 1) * rep].reshape(rows, LANE)
                qa_sc[pr, :, LANE:] = jnp.concatenate([selb_ref[pr]] * rep, axis=0)

    def process(masked):
        thr = qi * tq - kj * tk
        for pr in range(nprob):
            q = qa_sc[pr] if mode == "select" else q_ref[pr * rep:(pr + 1) * rep].reshape(rows, LANE)
            s_sc[pr] = lax.dot_general(q, k_ref[pr], _NT, preferred_element_type=F32)
        if masked:
            diff = lax.broadcasted_iota(I32, (ch, tk), 1) - lax.broadcasted_iota(I32, (ch, tk), 0)
        for pr in range(nprob):
            for r0 in range(0, rows, ch):
                sc = s_sc[pr, r0:r0 + ch, :]
                if masked:
                    lim = thr + (r0 % tq)
                    ok = diff <= lim
                    if mode == "window":
                        ok = ok & (diff > lim - WINDOW)
                    sc = jnp.where(ok, sc, NEG)
                    s_sc[pr, r0:r0 + ch, :] = sc
                m_prev = mb_sc[pr, r0:r0 + ch, :]
                m_new = jnp.maximum(m_prev, jnp.max(lane_fold(sc, jnp.maximum), axis=1, keepdims=True))
                al_sc[pr, r0:r0 + ch, :] = jnp.exp2(m_prev - m_new)
                mb_sc[pr, r0:r0 + ch, :] = m_new
            for r0 in range(0, rows, ch):
                mb = mb_sc[pr, r0:r0 + ch, :]
                p = jnp.exp2(s_sc[pr, r0:r0 + ch, :] - jnp.concatenate([mb] * nl, axis=1))
                p_sc[pr, r0:r0 + ch, :] = p.astype(BF16)
            acc_sc[pr] = al_sc[pr] * acc_sc[pr] + jnp.dot(p_sc[pr], v_ref[pr], preferred_element_type=F32)

    if mode == "window":
        pl.when(kj >= 0)(lambda: process(True))
    else:
        pl.when((kj >= 0) & ((flag & FLASH_MASKED) != 0))(lambda: process(True))
        pl.when((kj >= 0) & ((flag & FLASH_MASKED) == 0))(lambda: process(False))

    @pl.when((flag & FLASH_LAST) != 0)
    def _():
        for pr in range(nprob):
            acc = acc_sc[pr]
            o = acc / acc[:, ONE_LANE:ONE_LANE + 1]
            o_ref[pr * rep:(pr + 1) * rep] = o.reshape(rep, tq, LANE).astype(o_ref.dtype)


def _flash_schedule(seq, tq, tk, mode):
    qi, kj, flags = [], [], []
    for i in range(seq // tq):
        hi = (i * tq + tq - 1) // tk
        lo = 0 if mode != "window" else (i * tq - (WINDOW - 1)) // tk
        js = list(range(lo, hi + 1))
        for n, j in enumerate(js):
            crosses_diagonal = (j + 1) * tk - 1 > i * tq
            qi.append(i)
            kj.append(j if j >= 0 else -1)
            flags.append((FLASH_FIRST if n == 0 else 0) | (FLASH_LAST if n == len(js) - 1 else 0)
                         | (FLASH_MASKED if crosses_diagonal else 0))
    as_i32 = lambda v: jnp.asarray(np.asarray(v, np.int32))
    return as_i32(qi), as_i32(kj), as_i32(flags), len(qi)


def _flash(q, k, v, selb, *, batch, seq, mode, tq, tk):
    hq, T, _ = q.shape
    hkv = k.shape[0]
    kw = k.shape[-1]
    rep = hq // hkv
    npb = FLASH_NPROB
    assert tq & (tq - 1) == 0 and tq % FLASH_CHUNK == 0 and hkv % npb == 0
    qi, kj, flags, nsteps = _flash_schedule(seq, tq, tk, mode)
    nq, nk = seq // tq, seq // tk
    q_map = lambda g, b, s, qi, kj, fl: (g, b * nq + qi[s], 0)
    k_map = lambda g, b, s, qi, kj, fl: (g, b * nk + jnp.maximum(kj[s], 0), 0)
    in_specs = [pl.BlockSpec((npb * rep, tq, LANE), q_map),
                pl.BlockSpec((npb, tk, kw), k_map),
                pl.BlockSpec((npb, tk, LANE), k_map)]
    args = [q, k, v]
    rows = rep * tq
    scratch = []
    if mode == "select":
        in_specs.append(pl.BlockSpec((npb, tq, LANE), q_map))
        args.append(selb)
        scratch.append(pltpu.VMEM((npb, rows, kw), BF16))
    scratch += [pltpu.VMEM((npb, rows, tk), F32), pltpu.VMEM((npb, rows, tk), BF16),
                pltpu.VMEM((npb, rows, LANE), F32), pltpu.VMEM((npb, rows, LANE), F32),
                pltpu.VMEM((npb, rows, LANE), F32)]
    return pl.pallas_call(
        functools.partial(_flash_body, mode=mode, rep=rep, tq=tq, tk=tk, nprob=npb),
        grid_spec=pltpu.PrefetchScalarGridSpec(
            num_scalar_prefetch=3,
            grid=(hkv // npb, batch, nsteps),
            in_specs=in_specs,
            out_specs=pl.BlockSpec((npb * rep, tq, LANE), q_map),
            scratch_shapes=scratch),
        out_shape=jax.ShapeDtypeStruct((hq, T, LANE), BF16),
        compiler_params=_cparams("parallel", "parallel", "arbitrary"),
        name="flash_" + mode,
    )(qi, kj, flags, *args)


def _attn_out_body(*refs, n_branch, heads):
    o_refs = refs[:n_branch]
    if n_branch > 1:
        gl_ref, w_ref, x_ref, g_ref, out_ref = refs[n_branch:]
        gates = _sigmoid(gl_ref[...])
    else:
        w_ref, x_ref, g_ref, out_ref = refs[n_branch:]
    acc = jnp.zeros(x_ref.shape, F32)
    for h in range(heads):
        if n_branch > 1:
            o = jnp.zeros(o_refs[0].shape[1:], F32)
            for c in range(n_branch):
                col = n_branch * h + c
                o = o + gates[:, col:col + 1] * o_refs[c][h].astype(F32)
            o = o.astype(BF16)
        else:
            o = o_refs[0][h]
        acc = acc + jnp.dot(o, w_ref[h], preferred_element_type=F32)
    out_ref[...] = x_ref[...] + g_ref[0] * acc


def _attn_out(os, gl, w_heads, x2, g, seq):
    T, Dm = x2.shape
    heads = w_heads.shape[0]
    tm = 256
    nb = seq // tm
    row = lambda i: (i, 0)
    o_spec = pl.BlockSpec((heads, tm, LANE), lambda i: (0, i, 0))
    in_specs = [o_spec] * len(os)
    args = list(os)
    if len(os) > 1:
        in_specs.append(pl.BlockSpec((tm, LANE), row))
        args.append(gl)
    in_specs += [pl.BlockSpec((heads, LANE, Dm), lambda i: (0, 0, 0)),
                 pl.BlockSpec((tm, Dm), row),
                 pl.BlockSpec((1, 1, Dm), lambda i: (i // nb, 0, 0))]
    args += [w_heads, x2, g]
    return pl.pallas_call(
        functools.partial(_attn_out_body, n_branch=len(os), heads=heads),
        grid=(T // tm,),
        in_specs=in_specs,
        out_specs=pl.BlockSpec((tm, Dm), row),
        out_shape=jax.ShapeDtypeStruct((T, Dm), F32),
        compiler_params=_cparams("parallel"),
        name="attn_out_%d" % len(os),
    )(*args)


def _pad_heads_rows(w, heads, dh):
    w = w.reshape(heads, dh, -1)
    return jnp.pad(w, ((0, 0), (0, LANE - dh), (0, 0))).astype(BF16)


def _pad_heads_cols(w, heads, dh):
    k = w.shape[0]
    w = w.reshape(k, heads, dh)
    return jnp.pad(w, ((0, 0), (0, 0), (0, LANE - dh))).reshape(k, heads * LANE)


def _rope_tables(pos, rot, offset):
    half = rot // 2
    inv_freq = ROPE_THETA ** (-jnp.arange(0, rot, 2, dtype=F32) / rot)
    ang = pos.astype(F32)[:, None] * inv_freq[None, :]
    cos, sin = jnp.cos(ang), jnp.sin(ang)
    n = pos.shape[0]
    c = jnp.ones((n, LANE), F32).at[:, offset:offset + rot].set(jnp.concatenate([cos, cos], axis=1))
    sa = jnp.zeros((n, LANE), F32).at[:, offset:offset + half].set(-sin)
    sb = jnp.zeros((n, LANE), F32).at[:, offset + half:offset + rot].set(sin)
    return c, sa, sb


def _mla_proj_body(x_ref, a_ref, b_ref, win_ref, qlg_ref, kvlg_ref, wuq_ref, wuk_ref, wuv_ref,
                   qg_ref, kg_ref, c_ref, sa_ref, sb_ref, q_ref, k_ref, v_ref):
    h = _normmod(x_ref[...], a_ref[0], b_ref[0]).astype(BF16)
    r = jnp.dot(h, win_ref[...], preferred_element_type=F32)
    q_lat = r[:, :MLA_Q_LORA]
    kv_lat = r[:, MLA_Q_LORA:MLA_Q_LORA + MLA_KV_LORA]
    kpe = r[:, MLA_Q_LORA + MLA_KV_LORA:]
    ql = q_lat * lax.rsqrt(jnp.mean(q_lat * q_lat, axis=-1, keepdims=True) + EPS) * qlg_ref[...]
    kvl = kv_lat * lax.rsqrt(jnp.mean(kv_lat * kv_lat, axis=-1, keepdims=True) + EPS) * kvlg_ref[...]
    kvl = kvl.astype(BF16)
    q = jnp.dot(ql.astype(BF16), wuq_ref[...], preferred_element_type=F32)
    kn = jnp.dot(kvl, wuk_ref[...], preferred_element_type=F32)
    v = jnp.dot(kvl, wuv_ref[...], preferred_element_type=F32)
    c, sa, sb = c_ref[...], sa_ref[...], sb_ref[...]

    def norm_rot(t, gain):
        t = t * lax.rsqrt(jnp.sum(t * t, axis=-1, keepdims=True) * (1.0 / MLA_QK) + EPS) * gain
        return _rope(t, c, sa, sb, MLA_ROPE // 2)

    for hd in range(MLA_HEADS):
        sl = slice(hd * LANE, (hd + 1) * LANE)
        q_ref[hd] = norm_rot(q[:, sl], qg_ref[...]).astype(BF16)
        k_ref[hd] = norm_rot(kn[:, sl] + kpe, kg_ref[...]).astype(BF16)
        v_ref[hd] = (v[:, sl] + _one_lane()).astype(BF16)


def _mla_mixer(x2, a, b, g, w_in, q_lat_g, kv_lat_g, w_uq, w_ukv, q_gain, k_gain, w_out, batch, seq):
    T, Dm = x2.shape
    H = MLA_HEADS
    tm = 256
    nb = seq // tm
    scale = MLA_QK ** -0.5 * LOG2E
    lat = MLA_Q_LORA + MLA_KV_LORA
    kpe_cols = jnp.zeros((Dm, LANE), F32).at[:, MLA_NOPE:MLA_QK].set(w_in[:, lat:])
    win_p = jnp.concatenate([w_in[:, :lat], kpe_cols], axis=1).astype(BF16)
    wuq_p = _pad_heads_cols(w_uq, H, MLA_QK).astype(BF16)
    wukv = w_ukv.reshape(MLA_KV_LORA, H, MLA_NOPE + MLA_V)
    wuk_p = _pad_heads_cols(wukv[:, :, :MLA_NOPE].reshape(MLA_KV_LORA, -1), H, MLA_NOPE).astype(BF16)
    wuv_p = _pad_heads_cols(wukv[:, :, MLA_NOPE:].reshape(MLA_KV_LORA, -1), H, MLA_V).astype(BF16)
    pad_gain = lambda v: jnp.pad(v, (0, LANE - MLA_QK)).reshape(1, LANE)
    c, sa, sb = _rope_tables(jnp.arange(seq), MLA_ROPE, MLA_NOPE)
    row = lambda i: (i, 0)
    full = lambda i: (0, 0)
    per_b = lambda i: (i // nb, 0, 0)
    pos = lambda i: (i % nb, 0)
    head_out = pl.BlockSpec((H, tm, LANE), lambda i: (0, i, 0))
    hshape = jax.ShapeDtypeStruct((H, T, LANE), BF16)
    q, k, v = pl.pallas_call(
        _mla_proj_body,
        grid=(T // tm,),
        in_specs=[pl.BlockSpec((tm, Dm), row),
                  pl.BlockSpec((1, 1, Dm), per_b),
                  pl.BlockSpec((1, 1, Dm), per_b),
                  pl.BlockSpec(win_p.shape, full),
                  pl.BlockSpec((1, MLA_Q_LORA), full),
                  pl.BlockSpec((1, MLA_KV_LORA), full),
                  pl.BlockSpec(wuq_p.shape, full),
                  pl.BlockSpec(wuk_p.shape, full),
                  pl.BlockSpec(wuv_p.shape, full),
                  pl.BlockSpec((1, LANE), full),
                  pl.BlockSpec((1, LANE), full),
                  pl.BlockSpec((tm, LANE), pos),
                  pl.BlockSpec((tm, LANE), pos),
                  pl.BlockSpec((tm, LANE), pos)],
        out_specs=[head_out, head_out, head_out],
        out_shape=[hshape, hshape, hshape],
        compiler_params=_cparams("parallel"),
        name="mla_proj",
    )(x2, a, b, win_p, q_lat_g.reshape(1, -1), kv_lat_g.reshape(1, -1), wuq_p, wuk_p, wuv_p,
      pad_gain(q_gain * scale), pad_gain(k_gain), c, sa, sb)
    blk = min(1024, seq)
    o = _flash(q, k, v, None, batch=batch, seq=seq, mode="causal", tq=blk, tk=blk)
    return _attn_out([o], None, _pad_heads_rows(w_out, H, MLA_V), x2, g, seq)


N_KV_STREAMS = 6


def _nsa_proj_body(x_ref, a_ref, b_ref, w_ref, qg_ref, kg_ref, c_ref, sa_ref, sb_ref, blk_ref,
                   q_ref, kv_ref, ks_ref, gl_ref):
    h = _normmod(x_ref[...], a_ref[0], b_ref[0]).astype(BF16)
    r = jnp.dot(h, w_ref[...], preferred_element_type=F32)
    c, sa, sb = c_ref[...], sa_ref[...], sb_ref[...]

    def norm_rot(t, gain):
        t = t * lax.rsqrt(jnp.sum(t * t, axis=-1, keepdims=True) * (1.0 / NSA_DH) + EPS) * gain
        return _rope(t, c, sa, sb, NSA_ROT // 2)

    for hd in range(NSA_HEADS):
        q_ref[hd] = norm_rot(r[:, hd * LANE:(hd + 1) * LANE], qg_ref[...]).astype(BF16)
    base = NSA_HEADS * LANE
    for st in range(N_KV_STREAMS):
        for gi in range(NSA_GROUPS):
            off = base + (st * NSA_GROUPS + gi) * LANE
            t = r[:, off:off + LANE]
            if st == 2:
                t = norm_rot(t, kg_ref[1:2, :])
                ks_ref[gi] = jnp.concatenate([t.astype(BF16), blk_ref[...]], axis=1)
            elif st == 4:
                t = norm_rot(t, kg_ref[2:3, :])
            elif st in (3, 5):
                t = t + _one_lane()
            kv_ref[st * NSA_GROUPS + gi] = t.astype(BF16)
    gl_ref[...] = r[:, base + N_KV_STREAMS * NSA_GROUPS * LANE:]


def _nsa_compress_body(x_ref, pea_ref, peb_ref, w1a_ref, w1b_ref, w2_ref, kg_ref, c_ref, sa_ref, sb_ref,
                       o_ref, *, is_key, n_cmp):
    x = x_ref[0].astype(F32)
    xa = (x + pea_ref[...]).astype(BF16)
    xb = (x + peb_ref[...]).astype(BF16)
    za = jnp.dot(xa, w1a_ref[...], preferred_element_type=F32)
    zb = jnp.dot(xb, w1b_ref[...], preferred_element_type=F32)
    rows = za.shape[0]
    z = _silu(za + pltpu.roll(zb, rows - 1, 0))
    t = jnp.dot(z.astype(BF16), w2_ref[...], preferred_element_type=F32)
    if is_key:
        t = t * lax.rsqrt(jnp.sum(t * t, axis=-1, keepdims=True) * (1.0 / NSA_DH) + EPS) * kg_ref[...]
        t = _rope(t, c_ref[...], sa_ref[...], sb_ref[...], NSA_ROT // 2)
    valid = lax.broadcasted_iota(I32, t.shape, 0) < n_cmp
    o_ref[0] = jnp.where(valid, t, 0.0).astype(BF16)


def _nsa_cmp_select_body(q_ref, kc_ref, vc_ref, oc_ref, sel_ref, p_sc, imp_sc, *, tq, ncp, n_cmp, n_top):
    i = pl.program_id(2)
    t0 = i * tq
    kc = kc_ref[0]
    vc = vc_ref[0]
    rep = q_ref.shape[0]
    q = q_ref[...].reshape(rep * tq, LANE)
    sc = lax.dot_general(q, kc, _NT, preferred_element_type=F32)
    qpos = t0 + lax.broadcasted_iota(I32, (tq, ncp), 0)
    blk = lax.broadcasted_iota(I32, (tq, ncp), 1)
    mask = ((blk * CMP_STRIDE + (CMP_BLOCK - 1)) <= qpos) & (blk < n_cmp)
    sc = jnp.where(mask[None], sc.reshape(rep, tq, ncp), NEG)
    p = jnp.exp2(sc - jnp.max(sc, axis=-1, keepdims=True))
    p = jnp.where(mask[None], p / jnp.sum(p, axis=-1, keepdims=True), 0.0)
    oc = jnp.dot(p.reshape(rep * tq, ncp).astype(BF16), vc, preferred_element_type=F32)
    oc_ref[...] = oc.reshape(rep, tq, LANE).astype(BF16)
    qpos_t = t0 + lax.broadcasted_iota(I32, (ncp, tq), 1)
    blk_t = lax.broadcasted_iota(I32, (ncp, tq), 0)
    mask_t = ((blk_t * CMP_STRIDE + (CMP_BLOCK - 1)) <= qpos_t) & (blk_t < n_cmp)
    psum = jnp.zeros((ncp, tq), F32)
    for r in range(rep):
        st = lax.dot_general(kc, q_ref[r], _NT, preferred_element_type=F32)
        st = jnp.where(mask_t, st, NEG)
        pt = jnp.exp2(st - jnp.max(st, axis=0, keepdims=True))
        psum = psum + jnp.where(mask_t, pt / jnp.sum(pt, axis=0, keepdims=True), 0.0)
    p_sc[...] = jnp.zeros(p_sc.shape, F32)
    p_sc[8:8 + ncp, :] = psum
    per = SLC_BLOCK // CMP_STRIDE
    imp = p_sc[pl.ds(7, MAX_SLC, stride=per), :]
    for k in range(1, per + 1):
        imp = imp + p_sc[pl.ds(7 + k, MAX_SLC, stride=per), :]
    jb = lax.broadcasted_iota(I32, (MAX_SLC, tq), 0)
    qp = t0 + lax.broadcasted_iota(I32, (MAX_SLC, tq), 1)
    imp = jnp.where(jb * SLC_BLOCK <= qp, imp, -BIG)
    imp = jnp.where((jb == 0) | (jb == jnp.right_shift(qp, 6)), BIG, imp)
    imp_sc[...] = imp

    def count(ii, cnt):
        row = imp_sc[pl.ds(ii, 1), :]
        ahead = (row > imp) | ((row == imp) & (ii < jb))
        return cnt + jnp.where(ahead, 1.0, 0.0)

    rank = lax.fori_loop(0, MAX_SLC, count, jnp.zeros((MAX_SLC, tq), F32))
    sel_t = jnp.where(rank < n_top, 0.0, SEL_OFF)
    sel_ref[0] = sel_t.T.astype(BF16)


def _nsa_mixer(x2, a, b, g, w_in, w_cmp1, w_cmp2, cmp_pos, q_gain, k_gain, w_out, batch, seq):
    T, Dm = x2.shape
    H, G, dh = NSA_HEADS, NSA_GROUPS, NSA_DH
    scale = dh ** -0.5 * LOG2E
    n_cmp = seq // CMP_STRIDE - 1
    ncp = seq // CMP_STRIDE
    n_slc = seq // SLC_BLOCK
    assert n_slc <= MAX_SLC and ncp <= MAX_SLC * (SLC_BLOCK // CMP_STRIDE)
    n_top = min(SLC_TOP_N, n_slc)
    tm = 256
    nb = seq // tm
    q_cols = _pad_heads_cols(w_in[:, :H * dh], H, dh)
    kv_cols = _pad_heads_cols(w_in[:, H * dh:H * dh + N_KV_STREAMS * G * dh], N_KV_STREAMS * G, dh)
    gl_cols = jnp.pad(w_in[:, H * dh + N_KV_STREAMS * G * dh:], ((0, 0), (0, LANE - 3 * H)))
    w_p = jnp.concatenate([q_cols, kv_cols, gl_cols], axis=1).astype(BF16)
    pad_gain = lambda v: jnp.pad(v, ((0, 0), (0, LANE - dh)))
    c, sa, sb = _rope_tables(jnp.arange(seq), NSA_ROT, 0)
    row = lambda i: (i, 0)
    full = lambda i: (0, 0)
    per_b = lambda i: (i // nb, 0, 0)
    pos = lambda i: (i % nb, 0)
    blk_onehot = (jnp.arange(seq)[:, None] // SLC_BLOCK == jnp.arange(MAX_SLC)[None, :]).astype(BF16)
    q, kv, ks, gl = pl.pallas_call(
        _nsa_proj_body,
        grid=(T // tm,),
        in_specs=[pl.BlockSpec((tm, Dm), row),
                  pl.BlockSpec((1, 1, Dm), per_b),
                  pl.BlockSpec((1, 1, Dm), per_b),
                  pl.BlockSpec(w_p.shape, full),
                  pl.BlockSpec((1, LANE), full),
                  pl.BlockSpec((3, LANE), full),
                  pl.BlockSpec((tm, LANE), pos),
                  pl.BlockSpec((tm, LANE), pos),
                  pl.BlockSpec((tm, LANE), pos),
                  pl.BlockSpec((tm, MAX_SLC), pos)],
        out_specs=[pl.BlockSpec((H, tm, LANE), lambda i: (0, i, 0)),
                   pl.BlockSpec((N_KV_STREAMS * G, tm, LANE), lambda i: (0, i, 0)),
                   pl.BlockSpec((G, tm, LANE + MAX_SLC), lambda i: (0, i, 0)),
                   pl.BlockSpec((tm, LANE), row)],
        out_shape=[jax.ShapeDtypeStruct((H, T, LANE), BF16),
                   jax.ShapeDtypeStruct((N_KV_STREAMS * G, T, LANE), BF16),
                   jax.ShapeDtypeStruct((G, T, LANE + MAX_SLC), BF16),
                   jax.ShapeDtypeStruct((T, LANE), F32)],
        compiler_params=_cparams("parallel"),
        name="nsa_proj",
    )(x2, a, b, w_p, pad_gain(q_gain.reshape(1, dh) * scale), pad_gain(k_gain), c, sa, sb, blk_onehot)
    kc_raw, vc_raw, _, vs, kw, vw = [kv[s * G:(s + 1) * G] for s in range(N_KV_STREAMS)]

    cmp_end = jnp.arange(ncp) * CMP_STRIDE + (CMP_BLOCK - 1)
    cc, csa, csb = _rope_tables(cmp_end, NSA_ROT, 0)
    kdim = CMP_STRIDE * LANE

    def compress(raw, w1, w2, pe, is_key):
        x16 = raw.reshape(G, T // CMP_STRIDE, kdim)
        w1p = jnp.pad(w1, ((0, 0), (0, LANE - dh), (0, LANE - dh)))
        w1a = w1p[:CMP_STRIDE].reshape(kdim, LANE).astype(BF16)
        w1b = w1p[CMP_STRIDE:].reshape(kdim, LANE).astype(BF16)
        pep = jnp.pad(pe, ((0, 0), (0, LANE - dh)))
        pea = pep[:CMP_STRIDE].reshape(1, kdim)
        peb = pep[CMP_STRIDE:].reshape(1, kdim)
        w2p = jnp.pad(w2, ((0, LANE - dh), (0, LANE - dh))).astype(BF16)
        const = lambda gi, bi: (0, 0)
        return pl.pallas_call(
            functools.partial(_nsa_compress_body, is_key=is_key, n_cmp=n_cmp),
            grid=(G, batch),
            in_specs=[pl.BlockSpec((1, ncp, kdim), lambda gi, bi: (gi, bi, 0)),
                      pl.BlockSpec((1, kdim), const),
                      pl.BlockSpec((1, kdim), const),
                      pl.BlockSpec((kdim, LANE), const),
                      pl.BlockSpec((kdim, LANE), const),
                      pl.BlockSpec((LANE, LANE), const),
                      pl.BlockSpec((1, LANE), const),
                      pl.BlockSpec((ncp, LANE), const),
                      pl.BlockSpec((ncp, LANE), const),
                      pl.BlockSpec((ncp, LANE), const)],
            out_specs=pl.BlockSpec((1, ncp, LANE), lambda gi, bi: (gi, bi, 0)),
            out_shape=jax.ShapeDtypeStruct((G, batch * ncp, LANE), BF16),
            compiler_params=_cparams("parallel", "parallel"),
            name="nsa_compress_" + ("k" if is_key else "v"),
        )(x16, pea, peb, w1a, w1b, w2p, pad_gain(k_gain)[0:1], cc, csa, csb)

    kc = compress(kc_raw, w_cmp1[0], w_cmp2[0], cmp_pos[0], True)
    vc = compress(vc_raw, w_cmp1[1], w_cmp2[1], cmp_pos[1], False)

    tq = 128
    nq = seq // tq
    q_map = lambda gi, bi, i: (gi, bi * nq + i, 0)
    c_map = lambda gi, bi, i: (gi, bi, 0)
    o_c, sel = pl.pallas_call(
        functools.partial(_nsa_cmp_select_body, tq=tq, ncp=ncp, n_cmp=n_cmp, n_top=n_top),
        grid=(G, batch, nq),
        in_specs=[pl.BlockSpec((NSA_REP, tq, LANE), q_map),
                  pl.BlockSpec((1, ncp, LANE), c_map),
                  pl.BlockSpec((1, ncp, LANE), c_map)],
        out_specs=[pl.BlockSpec((NSA_REP, tq, LANE), q_map),
                   pl.BlockSpec((1, tq, LANE), q_map)],
        out_shape=[jax.ShapeDtypeStruct((H, T, LANE), BF16),
                   jax.ShapeDtypeStruct((G, T, LANE), BF16)],
        scratch_shapes=[pltpu.VMEM((8 + MAX_SLC * (SLC_BLOCK // CMP_STRIDE), tq), F32),
                        pltpu.VMEM((MAX_SLC, tq), F32)],
        compiler_params=_cparams("parallel", "parallel", "parallel"),
        name="nsa_cmp_select",
    )(q, kc, vc)

    o_s = _flash(q, ks, vs, sel, batch=batch, seq=seq, mode="select", tq=256, tk=min(1024, seq))
    o_w = _flash(q, kw, vw, None, batch=batch, seq=seq, mode="window", tq=256, tk=WINDOW)
    return _attn_out([o_c, o_s, o_w], gl, _pad_heads_rows(w_out, H, dh), x2, g, seq)


def _router_body(x_ref, a_ref, b_ref, rwt_ref, rb_ref, tri_ref, h_ref, e_ref, w_ref, rank_ref, cnt_ref,
                 carry_sc, *, tm):
    i = pl.program_id(0)

    @pl.when(i == 0)
    def _():
        carry_sc[...] = jnp.zeros(carry_sc.shape, F32)

    h = _normmod(x_ref[...], a_ref[0], b_ref[0])
    h_ref[...] = h
    logits = lax.dot_general(rwt_ref[...], h, _NT, preferred_element_type=F32, precision=HIGHEST)
    scores = _sigmoid(logits)
    biased = scores + rb_ref[...]
    ng, per = N_GROUPS, EXPERTS_PER_GROUP
    row = lambda arr, r: arr[r:r + 1, :]
    gsel = jnp.zeros((1, tm), I32)
    best = None
    for gi in range(ng):
        v = [row(biased, gi * per + k) for k in range(per)]
        top2 = None
        for p in range(per):
            for q in range(p + 1, per):
                s = v[p] + v[q]
                top2 = s if top2 is None else jnp.maximum(top2, s)
        if best is None:
            best = top2
        else:
            better = top2 > best
            gsel = jnp.where(better, gi, gsel)
            best = jnp.where(better, top2, best)
    cb, cs = [], []
    for k in range(per):
        b_k = row(biased, k)
        s_k = row(scores, k)
        for gi in range(1, ng):
            hit = gsel == gi
            b_k = jnp.where(hit, row(biased, gi * per + k), b_k)
            s_k = jnp.where(hit, row(scores, gi * per + k), s_k)
        cb.append(b_k)
        cs.append(s_k)

    def argmax_first(vals):
        idx = jnp.zeros((1, tm), I32)
        top = vals[0]
        for k in range(1, per):
            better = vals[k] > top
            idx = jnp.where(better, k, idx)
            top = jnp.where(better, vals[k], top)
        return idx

    def pick(vals, idx):
        out = vals[0]
        for k in range(1, per):
            out = jnp.where(idx == k, vals[k], out)
        return out

    i1 = argmax_first(cb)
    i2 = argmax_first([jnp.where(i1 == k, -jnp.inf, cb[k]) for k in range(per)])
    w1 = pick(cs, i1)
    w2 = pick(cs, i2)
    tot = w1 + w2
    e1 = gsel * per + i1
    e2 = gsel * per + i2
    eid = lax.broadcasted_iota(I32, (N_EXPERTS, tm), 0)
    hot = (eid == e1) | (eid == e2)
    onehot = jnp.where(hot, 1.0, 0.0)
    before = carry_sc[...] + jnp.dot(onehot.astype(BF16), tri_ref[...], preferred_element_type=F32)
    r1 = jnp.sum(jnp.where(eid == e1, before, 0.0), axis=0, keepdims=True)
    r2 = jnp.sum(jnp.where(eid == e2, before, 0.0), axis=0, keepdims=True)
    carry = carry_sc[...] + jnp.sum(onehot, axis=1, keepdims=True)
    carry_sc[...] = carry
    cnt_ref[...] = jnp.broadcast_to(carry, cnt_ref.shape)
    zi = jnp.zeros((6, tm), I32)
    e_ref[...] = jnp.concatenate([e1, e2, zi], axis=0)
    rank_ref[...] = jnp.concatenate([r1.astype(I32), r2.astype(I32), zi], axis=0)
    w_ref[...] = jnp.concatenate([w1 / tot, w2 / tot, jnp.zeros((6, tm), F32)], axis=0)


def _moe_expert_body(te_ref, nv_ref, src_ref, h_hbm, win_ref, wout_ref, y_ref, xbuf, sem, *, tg):
    i = pl.program_id(0)

    def row_copy(tok, r):
        return pltpu.make_async_copy(h_hbm.at[pl.ds(tok, 1)], xbuf.at[pl.ds(r, 1)], sem)

    @pl.when(i < nv_ref[0])
    def _():
        def issue(r, carry):
            row_copy(src_ref[0, 0, r], r).start()
            return carry

        lax.fori_loop(0, tg, issue, 0)

        def drain(r, carry):
            row_copy(0, r).wait()
            return carry

        lax.fori_loop(0, tg, drain, 0)
        x = xbuf[...].astype(BF16)
        gu = jnp.dot(x, win_ref[0], preferred_element_type=F32)
        act = (_silu(gu[:, :D_EXPERT]) * gu[:, D_EXPERT:]).astype(BF16)
        y_ref[...] = jnp.dot(act, wout_ref[0], preferred_element_type=F32)

    @pl.when(i >= nv_ref[0])
    def _():
        y_ref[...] = jnp.zeros(y_ref.shape, F32)


def _moe_combine_body(dst_ref, y_hbm, x_ref, w_ref, g_ref, o_ref, ybuf, sem, *, tm):
    def row_copy(slot, k, r):
        return pltpu.make_async_copy(y_hbm.at[pl.ds(slot, 1)], ybuf.at[k, pl.ds(r, 1)], sem)

    def issue(r, carry):
        row_copy(dst_ref[0, 0, r], 0, r).start()
        row_copy(dst_ref[0, 1, r], 1, r).start()
        return carry

    lax.fori_loop(0, tm, issue, 0)

    def drain(r, carry):
        row_copy(0, 0, r).wait()
        row_copy(0, 1, r).wait()
        return carry

    lax.fori_loop(0, tm, drain, 0)
    w = w_ref[...]
    y = w[:, 0:1] * ybuf[0] + w[:, 1:2] * ybuf[1]
    o_ref[...] = x_ref[...] + g_ref[0] * y


def _grouped_moe(x2, a, b, g, router_w, router_bias, w_in, w_out, seq):
    T, Dm = x2.shape
    E = N_EXPERTS
    tm = 512
    nb = seq // tm
    nt = T // tm
    row = lambda i: (i, 0)
    full = lambda i: (0, 0)
    per_b = lambda i: (i // nb, 0, 0)
    lanes = lambda i: (0, i)
    tri = jnp.asarray(np.triu(np.ones((tm, tm), np.float32), 1)).astype(BF16)
    h, e, w, rank, cnt = pl.pallas_call(
        functools.partial(_router_body, tm=tm),
        grid=(nt,),
        in_specs=[pl.BlockSpec((tm, Dm), row),
                  pl.BlockSpec((1, 1, Dm), per_b),
                  pl.BlockSpec((1, 1, Dm), per_b),
                  pl.BlockSpec((E, Dm), full),
                  pl.BlockSpec((E, 1), full),
                  pl.BlockSpec((tm, tm), full)],
        out_specs=[pl.BlockSpec((tm, Dm), row),
                   pl.BlockSpec((8, tm), lanes),
                   pl.BlockSpec((8, tm), lanes),
                   pl.BlockSpec((8, tm), lanes),
                   pl.BlockSpec((E, LANE), full)],
        out_shape=[jax.ShapeDtypeStruct((T, Dm), F32),
                   jax.ShapeDtypeStruct((8, T), I32),
                   jax.ShapeDtypeStruct((8, T), F32),
                   jax.ShapeDtypeStruct((8, T), I32),
                   jax.ShapeDtypeStruct((E, LANE), F32)],
        scratch_shapes=[pltpu.VMEM((E, 1), F32)],
        compiler_params=_cparams("arbitrary"),
        name="moe_router",
    )(x2, a, b, router_w.T, router_bias.reshape(E, 1), tri)

    tg = MOE_TILE
    n_tiles = (2 * T) // tg + E
    n_slots = n_tiles * tg
    counts = cnt[:, 0].astype(I32)
    padded = ((counts + tg - 1) // tg) * tg
    ends = jnp.cumsum(padded)
    starts = ends - padded
    dest = starts[e[:2]] + rank[:2]
    tok = jnp.broadcast_to(jnp.arange(T, dtype=I32)[None], (2, T))
    src = jnp.zeros((n_slots,), I32).at[dest.reshape(-1)].set(tok.reshape(-1))
    tile_start = jnp.arange(n_tiles, dtype=I32) * tg
    tile_expert = jnp.minimum(jnp.searchsorted(ends, tile_start, side="right"), E - 1).astype(I32)
    n_valid = (ends[-1] // tg).astype(I32).reshape(1)

    y = pl.pallas_call(
        functools.partial(_moe_expert_body, tg=tg),
        grid_spec=pltpu.PrefetchScalarGridSpec(
            num_scalar_prefetch=2,
            grid=(n_tiles,),
            in_specs=[pl.BlockSpec((1, 1, tg), lambda i, te, nv: (i, 0, 0), memory_space=pltpu.SMEM),
                      pl.BlockSpec(memory_space=pl.ANY),
                      pl.BlockSpec((1, Dm, 2 * D_EXPERT), lambda i, te, nv: (te[i], 0, 0)),
                      pl.BlockSpec((1, D_EXPERT, Dm), lambda i, te, nv: (te[i], 0, 0))],
            out_specs=pl.BlockSpec((tg, Dm), lambda i, te, nv: (i, 0)),
            scratch_shapes=[pltpu.VMEM((tg, Dm), F32), pltpu.SemaphoreType.DMA(())]),
        out_shape=jax.ShapeDtypeStruct((n_slots, Dm), F32),
        compiler_params=_cparams("arbitrary"),
        name="moe_experts",
    )(tile_expert, n_valid, src.reshape(n_tiles, 1, tg), h, w_in.astype(BF16), w_out.astype(BF16))

    tc = 256
    nbc = seq // tc
    dst3 = dest.reshape(2, T // tc, tc).transpose(1, 0, 2)
    return pl.pallas_call(
        functools.partial(_moe_combine_body, tm=tc),
        grid=(T // tc,),
        in_specs=[pl.BlockSpec((1, 2, tc), lambda i: (i, 0, 0), memory_space=pltpu.SMEM),
                  pl.BlockSpec(memory_space=pl.ANY),
                  pl.BlockSpec((tc, Dm), row),
                  pl.BlockSpec((tc, 2), row),
                  pl.BlockSpec((1, 1, Dm), lambda i: (i // nbc, 0, 0))],
        out_specs=pl.BlockSpec((tc, Dm), row),
        out_shape=jax.ShapeDtypeStruct((T, Dm), F32),
        scratch_shapes=[pltpu.VMEM((2, tc, Dm), F32), pltpu.SemaphoreType.DMA(())],
        compiler_params=_cparams("arbitrary"),
        name="moe_combine",
    )(dst3, y, x2, w[:2].T, g)


def kernel(x, c, norm_mix_g, norm_ffn_g, w_ada, b_ada, conv_w_pw1, conv_b_pw1, conv_w_dw, conv_b_dw, conv_ln_g, conv_ln_b, conv_w_pw2, conv_b_pw2, nsa_w_in, nsa_w_cmp1, nsa_w_cmp2, nsa_cmp_pos, nsa_q_gain, nsa_k_gain, nsa_w_out, mla_w_in, mla_q_lat_g, mla_kv_lat_g, mla_w_uq, mla_w_ukv, mla_q_gain, mla_k_gain, mla_w_out, router_w, router_bias, moe_w_in, moe_w_out):
    B, S, Dm = x.shape
    depth = w_ada.shape[0]
    mods = _ada(c, w_ada, b_ada)
    x2 = x.reshape(B * S, Dm)
    for i in range(depth):
        sh1, sc1, g1, sh2, sc2, g2 = [m.reshape(B, 1, Dm) for m in jnp.split(mods[i], 6, axis=-1)]
        a1 = norm_mix_g[i] * (1.0 + sc1)
        kind, j = i % 3, i // 3
        if kind == 0:
            x2 = _conv_mixer(x2, a1, sh1, g1, conv_w_pw1[j], conv_b_pw1[j], conv_w_dw[j], conv_b_dw[j],
                             conv_ln_g[j], conv_ln_b[j], conv_w_pw2[j], conv_b_pw2[j], S)
        elif kind == 1:
            x2 = _nsa_mixer(x2, a1, sh1, g1, nsa_w_in[j], nsa_w_cmp1[j], nsa_w_cmp2[j], nsa_cmp_pos[j],
                            nsa_q_gain[j], nsa_k_gain[j], nsa_w_out[j], B, S)
        else:
            x2 = _mla_mixer(x2, a1, sh1, g1, mla_w_in[j], mla_q_lat_g[j], mla_kv_lat_g[j], mla_w_uq[j],
                            mla_w_ukv[j], mla_q_gain[j], mla_k_gain[j], mla_w_out[j], B, S)
        a2 = norm_ffn_g[i] * (1.0 + sc2)
        x2 = _grouped_moe(x2, a2, sh2, g2, router_w, router_bias, moe_w_in[i], moe_w_out[i], S)
    return x2.reshape(B, S, Dm)
```

```python
import functools
import math

import numpy as np
import jax
import jax.numpy as jnp
from jax import lax
from jax.experimental import pallas as pl
from jax.experimental.pallas import tpu as pltpu

F32 = jnp.float32
BF16 = jnp.bfloat16
I32 = jnp.int32
HIGHEST = lax.Precision.HIGHEST

EPS = 1e-6
NEG = -1e30
BIG = 1e30
ROPE_THETA = 500000.0
LANE = 128
VMEM_LIMIT = 56 * 1024 * 1024

D_MODEL = 1024
CONV_WIDTH = 31
CONV_HALO = 32

NSA_HEADS = 16
NSA_GROUPS = 4
NSA_REP = NSA_HEADS // NSA_GROUPS
NSA_DH = 64
NSA_ROT = 16
CMP_BLOCK = 32
CMP_STRIDE = 16
SLC_BLOCK = 64
SLC_TOP_N = 16
WINDOW = 512
MAX_SLC = 128

MLA_HEADS = 16
MLA_Q_LORA = 384
MLA_KV_LORA = 256
MLA_NOPE = 64
MLA_ROPE = 32
MLA_V = 64
MLA_QK = MLA_NOPE + MLA_ROPE

N_EXPERTS = 16
N_GROUPS = 4
EXPERTS_PER_GROUP = 4
D_EXPERT = 512
MOE_TILE = 512

_NT = (((1,), (1,)), ((), ()))


def _cparams(*sem):
    return pltpu.CompilerParams(dimension_semantics=sem, vmem_limit_bytes=VMEM_LIMIT)


def _sigmoid(x):
    return 1.0 / (1.0 + jnp.exp(-x))


def _silu(x):
    return x * _sigmoid(x)


def _normmod(x, a, b):
    ms = jnp.mean(x * x, axis=-1, keepdims=True)
    return x * lax.rsqrt(ms + EPS) * a + b


def _rope(x, c, sa, sb, half):
    n = x.shape[-1]
    return x * c + pltpu.roll(x, n - half, 1) * sa + pltpu.roll(x, half, 1) * sb


def _ada_body(c_ref, w_ref, b_ref, o_ref):
    c = c_ref[...]
    o_ref[0] = jnp.dot(_silu(c), w_ref[0], preferred_element_type=F32, precision=HIGHEST) + b_ref[0]


def _ada(c, w_ada, b_ada):
    B = c.shape[0]
    L, Dm, N = w_ada.shape
    Bp = -(-B // 8) * 8
    cp = jnp.pad(c, ((0, Bp - B), (0, 0)))
    tn = 1536
    out = pl.pallas_call(
        _ada_body,
        grid=(L, N // tn),
        in_specs=[pl.BlockSpec((Bp, Dm), lambda l, j: (0, 0)),
                  pl.BlockSpec((1, Dm, tn), lambda l, j: (l, 0, j)),
                  pl.BlockSpec((1, 1, tn), lambda l, j: (l, 0, j))],
        out_specs=pl.BlockSpec((1, Bp, tn), lambda l, j: (l, 0, j)),
        out_shape=jax.ShapeDtypeStruct((L, Bp, N), F32),
        compiler_params=_cparams("arbitrary", "arbitrary"),
        name="adaln",
    )(cp, w_ada, b_ada.reshape(L, 1, N))
    return out[:, :B]


def _conv_pw1_body(x_ref, a_ref, b_ref, w_ref, bias_ref, u_ref):
    h = _normmod(x_ref[...], a_ref[0], b_ref[0]).astype(BF16)
    r = jnp.dot(h, w_ref[...], preferred_element_type=F32) + bias_ref[...]
    d = u_ref.shape[-1]
    u_ref[...] = r[:, :d] * _sigmoid(r[:, d:])


def _conv_dw_body(u_ref, uh_ref, wdw_ref, bdw_ref, lng_ref, lnb_ref, w2_ref, b2_ref, x_ref, g_ref,
                  o_ref, ext_ref, *, tm, seq):
    i = pl.program_id(0)
    at_seq_start = (i * tm) % seq == 0
    ext_ref[0:CONV_HALO, :] = jnp.where(at_seq_start, 0.0, uh_ref[...])
    ext_ref[CONV_HALO:, :] = u_ref[...]
    base = CONV_HALO - (CONV_WIDTH - 1)
    acc = jnp.zeros((tm, u_ref.shape[-1]), F32) + bdw_ref[...]
    for k in range(CONV_WIDTH):
        acc = acc + wdw_ref[k:k + 1, :] * ext_ref[base + k:base + k + tm, :]
    mu = jnp.mean(acc, axis=-1, keepdims=True)
    dlt = acc - mu
    var = jnp.mean(dlt * dlt, axis=-1, keepdims=True)
    y = dlt * lax.rsqrt(var + EPS) * lng_ref[...] + lnb_ref[...]
    z = _silu(y).astype(BF16)
    out = jnp.dot(z, w2_ref[...], preferred_element_type=F32) + b2_ref[...]
    o_ref[...] = x_ref[...] + g_ref[0] * out


def _conv_mixer(x2, a, b, g, w_pw1, b_pw1, w_dw, b_dw, ln_g, ln_b, w_pw2, b_pw2, seq):
    T, Dm = x2.shape
    tm = 512
    nb = seq // tm
    row = lambda i: (i, 0)
    per_b = lambda i: (i // nb, 0, 0)
    full = lambda i: (0, 0)
    u = pl.pallas_call(
        _conv_pw1_body,
        grid=(T // tm,),
        in_specs=[pl.BlockSpec((tm, Dm), row),
                  pl.BlockSpec((1, 1, Dm), per_b),
                  pl.BlockSpec((1, 1, Dm), per_b),
                  pl.BlockSpec((Dm, 2 * Dm), full),
                  pl.BlockSpec((1, 2 * Dm), full)],
        out_specs=pl.BlockSpec((tm, Dm), row),
        out_shape=jax.ShapeDtypeStruct((T, Dm), F32),
        compiler_params=_cparams("parallel"),
        name="conv_pw1_glu",
    )(x2, a, b, w_pw1.astype(BF16), b_pw1.reshape(1, -1))
    hb = tm // CONV_HALO
    wdw = jnp.pad(w_dw, ((0, CONV_HALO - CONV_WIDTH), (0, 0)))
    vec = lambda v: v.reshape(1, -1)
    return pl.pallas_call(
        functools.partial(_conv_dw_body, tm=tm, seq=seq),
        grid=(T // tm,),
        in_specs=[pl.BlockSpec((tm, Dm), row),
                  pl.BlockSpec((CONV_HALO, Dm), lambda i: (jnp.maximum(i * hb - 1, 0), 0)),
                  pl.BlockSpec((CONV_HALO, Dm), full),
                  pl.BlockSpec((1, Dm), full),
                  pl.BlockSpec((1, Dm), full),
                  pl.BlockSpec((1, Dm), full),
                  pl.BlockSpec((Dm, Dm), full),
                  pl.BlockSpec((1, Dm), full),
                  pl.BlockSpec((tm, Dm), row),
                  pl.BlockSpec((1, 1, Dm), per_b)],
        out_specs=pl.BlockSpec((tm, Dm), row),
        out_shape=jax.ShapeDtypeStruct((T, Dm), F32),
        scratch_shapes=[pltpu.VMEM((tm + CONV_HALO, Dm), F32)],
        compiler_params=_cparams("parallel"),
        name="conv_dw_ln_pw2",
    )(u, u, wdw, vec(b_dw), vec(ln_g), vec(ln_b), w_pw2.astype(BF16), vec(b_pw2), x2, g)


FLASH_FIRST, FLASH_LAST, FLASH_MASKED = 1, 2, 4
FLASH_CHUNK = 16
SEL_OFF = -(2.0 ** 100)
ONE_LANE = 64
LOG2E = 1.4426950408889634
FLASH_NPROB = 2


def _one_lane():
    return jnp.where(lax.broadcasted_iota(I32, (1, LANE), 1) == ONE_LANE, 1.0, 0.0)


def _flash_body(qi_ref, kj_ref, flag_ref, *refs, mode, rep, tq, tk, nprob):
    if mode == "select":
        q_ref, k_ref, v_ref, selb_ref, o_ref, qa_sc, s_sc, p_sc, mb_sc, al_sc, acc_sc = refs
    else:
        q_ref, k_ref, v_ref, o_ref, s_sc, p_sc, mb_sc, al_sc, acc_sc = refs
    step = pl.program_id(2)
    qi = qi_ref[step]
    kj = kj_ref[step]
    flag = flag_ref[step]
    rows = rep * tq
    ch = FLASH_CHUNK
    nl = tk // LANE
    lane_fold = lambda t, op: functools.reduce(op, [t[:, i * LANE:(i + 1) * LANE] for i in range(nl)])

    @pl.when((flag & FLASH_FIRST) != 0)
    def _():
        mb_sc[...] = jnp.full(mb_sc.shape, NEG, F32)
        acc_sc[...] = jnp.zeros(acc_sc.shape, F32)
        if mode == "select":
            for pr in range(nprob):
                qa_sc[pr, :, :LANE] = q_ref[pr * rep:(pr + 1) * rep].reshape(rows, LANE)
                qa_sc[pr, :, LANE:] = jnp.concatenate([selb_ref[pr]] * rep, axis=0)

    def process(masked):
        thr = qi * tq - kj * tk
        for pr in range(nprob):
            q = qa_sc[pr] if mode == "select" else q_ref[pr * rep:(pr + 1) * rep].reshape(rows, LANE)
            s_sc[pr] = lax.dot_general(q, k_ref[pr], _NT, preferred_element_type=F32)
        if masked:
            diff = lax.broadcasted_iota(I32, (ch, tk), 1) - lax.broadcasted_iota(I32, (ch, tk), 0)
        for pr in range(nprob):
            for r0 in range(0, rows, ch):
                sc = s_sc[pr, r0:r0 + ch, :]
                if masked:
                    lim = thr + (r0 % tq)
                    ok = diff <= lim
                    if mode == "window":
                        ok = ok & (diff > lim - WINDOW)
                    sc = jnp.where(ok, sc, NEG)
                    s_sc[pr, r0:r0 + ch, :] = sc
                m_prev = mb_sc[pr, r0:r0 + ch, :]
                m_new = jnp.maximum(m_prev, jnp.max(lane_fold(sc, jnp.maximum), axis=1, keepdims=True))
                al_sc[pr, r0:r0 + ch, :] = jnp.exp2(m_prev - m_new)
                mb_sc[pr, r0:r0 + ch, :] = m_new
            for r0 in range(0, rows, ch):
                mb = mb_sc[pr, r0:r0 + ch, :]
                p = jnp.exp2(s_sc[pr, r0:r0 + ch, :] - jnp.concatenate([mb] * nl, axis=1))
                p_sc[pr, r0:r0 + ch, :] = p.astype(BF16)
            acc_sc[pr] = al_sc[pr] * acc_sc[pr] + jnp.dot(p_sc[pr], v_ref[pr], preferred_element_type=F32)

    if mode == "window":
        pl.when(kj >= 0)(lambda: process(True))
    else:
        pl.when((kj >= 0) & ((flag & FLASH_MASKED) != 0))(lambda: process(True))
        pl.when((kj >= 0) & ((flag & FLASH_MASKED) == 0))(lambda: process(False))

    @pl.when((flag & FLASH_LAST) != 0)
    def _():
        for pr in range(nprob):
            acc = acc_sc[pr]
            o = acc / acc[:, ONE_LANE:ONE_LANE + 1]
            o_ref[pr * rep:(pr + 1) * rep] = o.reshape(rep, tq, LANE).astype(o_ref.dtype)


def _flash_schedule(seq, tq, tk, mode):
    qi, kj, flags = [], [], []
    for i in range(seq // tq):
        hi = (i * tq + tq - 1) // tk
        lo = 0 if mode != "window" else (i * tq - (WINDOW - 1)) // tk
        js = list(range(lo, hi + 1))
        for n, j in enumerate(js):
            crosses_diagonal = (j + 1) * tk - 1 > i * tq
            qi.append(i)
            kj.append(j if j >= 0 else -1)
            flags.append((FLASH_FIRST if n == 0 else 0) | (FLASH_LAST if n == len(js) - 1 else 0)
                         | (FLASH_MASKED if crosses_diagonal else 0))
    as_i32 = lambda v: jnp.asarray(np.asarray(v, np.int32))
    return as_i32(qi), as_i32(kj), as_i32(flags), len(qi)


def _flash(q, k, v, selb, *, batch, seq, mode, tq, tk):
    hq, T, _ = q.shape
    hkv = k.shape[0]
    kw = k.shape[-1]
    rep = hq // hkv
    npb = FLASH_NPROB
    assert tq & (tq - 1) == 0 and tq % FLASH_CHUNK == 0 and hkv % npb == 0
    qi, kj, flags, nsteps = _flash_schedule(seq, tq, tk, mode)
    nq, nk = seq // tq, seq // tk
    q_map = lambda g, b, s, qi, kj, fl: (g, b * nq + qi[s], 0)
    k_map = lambda g, b, s, qi, kj, fl: (g, b * nk + jnp.maximum(kj[s], 0), 0)
    in_specs = [pl.BlockSpec((npb * rep, tq, LANE), q_map),
                pl.BlockSpec((npb, tk, kw), k_map),
                pl.BlockSpec((npb, tk, LANE), k_map)]
    args = [q, k, v]
    rows = rep * tq
    scratch = []
    if mode == "select":
        in_specs.append(pl.BlockSpec((npb, tq, LANE), q_map))
        args.append(selb)
        scratch.append(pltpu.VMEM((npb, rows, kw), BF16))
    scratch += [pltpu.VMEM((npb, rows, tk), F32), pltpu.VMEM((npb, rows, tk), BF16),
                pltpu.VMEM((npb, rows, LANE), F32), pltpu.VMEM((npb, rows, LANE), F32),
                pltpu.VMEM((npb, rows, LANE), F32)]
    return pl.pallas_call(
        functools.partial(_flash_body, mode=mode, rep=rep, tq=tq, tk=tk, nprob=npb),
        grid_spec=pltpu.PrefetchScalarGridSpec(
            num_scalar_prefetch=3,
            grid=(hkv // npb, batch, nsteps),
            in_specs=in_specs,
            out_specs=pl.BlockSpec((npb * rep, tq, LANE), q_map),
            scratch_shapes=scratch),
        out_shape=jax.ShapeDtypeStruct((hq, T, LANE), BF16),
        compiler_params=_cparams("parallel", "parallel", "arbitrary"),
        name="flash_" + mode,
    )(qi, kj, flags, *args)


def _attn_out_body(*refs, n_branch, heads):
    o_refs = refs[:n_branch]
    if n_branch > 1:
        gl_ref, w_ref, x_ref, g_ref, out_ref = refs[n_branch:]
        gates = _sigmoid(gl_ref[...])
    else:
        w_ref, x_ref, g_ref, out_ref = refs[n_branch:]
    acc = jnp.zeros(x_ref.shape, F32)
    for h in range(heads):
        if n_branch > 1:
            o = jnp.zeros(o_refs[0].shape[1:], F32)
            for c in range(n_branch):
                col = n_branch * h + c
                o = o + gates[:, col:col + 1] * o_refs[c][h].astype(F32)
            o = o.astype(BF16)
        else:
            o = o_refs[0][h]
        acc = acc + jnp.dot(o, w_ref[h], preferred_element_type=F32)
    out_ref[...] = x_ref[...] + g_ref[0] * acc


def _attn_out(os, gl, w_heads, x2, g, seq):
    T, Dm = x2.shape
    heads = w_heads.shape[0]
    tm = 256
    nb = seq // tm
    row = lambda i: (i, 0)
    o_spec = pl.BlockSpec((heads, tm, LANE), lambda i: (0, i, 0))
    in_specs = [o_spec] * len(os)
    args = list(os)
    if len(os) > 1:
        in_specs.append(pl.BlockSpec((tm, LANE), row))
        args.append(gl)
    in_specs += [pl.BlockSpec((heads, LANE, Dm), lambda i: (0, 0, 0)),
                 pl.BlockSpec((tm, Dm), row),
                 pl.BlockSpec((1, 1, Dm), lambda i: (i // nb, 0, 0))]
    args += [w_heads, x2, g]
    return pl.pallas_call(
        functools.partial(_attn_out_body, n_branch=len(os), heads=heads),
        grid=(T // tm,),
        in_specs=in_specs,
        out_specs=pl.BlockSpec((tm, Dm), row),
        out_shape=jax.ShapeDtypeStruct((T, Dm), F32),
        compiler_params=_cparams("parallel"),
        name="attn_out_%d" % len(os),
    )(*args)


def _pad_heads_rows(w, heads, dh):
    w = w.reshape(heads, dh, -1)
    return jnp.pad(w, ((0, 0), (0, LANE - dh), (0, 0))).astype(BF16)


def _pad_heads_cols(w, heads, dh):
    k = w.shape[0]
    w = w.reshape(k, heads, dh)
    return jnp.pad(w, ((0, 0), (0, 0), (0, LANE - dh))).reshape(k, heads * LANE)


def _rope_tables(pos, rot, offset):
    half = rot // 2
    inv_freq = ROPE_THETA ** (-jnp.arange(0, rot, 2, dtype=F32) / rot)
    ang = pos.astype(F32)[:, None] * inv_freq[None, :]
    cos, sin = jnp.cos(ang), jnp.sin(ang)
    n = pos.shape[0]
    c = jnp.ones((n, LANE), F32).at[:, offset:offset + rot].set(jnp.concatenate([cos, cos], axis=1))
    sa = jnp.zeros((n, LANE), F32).at[:, offset:offset + half].set(-sin)
    sb = jnp.zeros((n, LANE), F32).at[:, offset + half:offset + rot].set(sin)
    return c, sa, sb


def _mla_proj_body(x_ref, a_ref, b_ref, win_ref, qlg_ref, kvlg_ref, wuq_ref, wuk_ref, wuv_ref,
                   qg_ref, kg_ref, c_ref, sa_ref, sb_ref, q_ref, k_ref, v_ref):
    h = _normmod(x_ref[...], a_ref[0], b_ref[0]).astype(BF16)
    r = jnp.dot(h, win_ref[...], preferred_element_type=F32)
    q_lat = r[:, :MLA_Q_LORA]
    kv_lat = r[:, MLA_Q_LORA:MLA_Q_LORA + MLA_KV_LORA]
    kpe = r[:, MLA_Q_LORA + MLA_KV_LORA:]
    ql = q_lat * lax.rsqrt(jnp.mean(q_lat * q_lat, axis=-1, keepdims=True) + EPS) * qlg_ref[...]
    kvl = kv_lat * lax.rsqrt(jnp.mean(kv_lat * kv_lat, axis=-1, keepdims=True) + EPS) * kvlg_ref[...]
    kvl = kvl.astype(BF16)
    q = jnp.dot(ql.astype(BF16), wuq_ref[...], preferred_element_type=F32)
    kn = jnp.dot(kvl, wuk_ref[...], preferred_element_type=F32)
    v = jnp.dot(kvl, wuv_ref[...], preferred_element_type=F32)
    c, sa, sb = c_ref[...], sa_ref[...], sb_ref[...]

    def norm_rot(t, gain):
        t = t * lax.rsqrt(jnp.sum(t * t, axis=-1, keepdims=True) * (1.0 / MLA_QK) + EPS) * gain
        return _rope(t, c, sa, sb, MLA_ROPE // 2)

    for hd in range(MLA_HEADS):
        sl = slice(hd * LANE, (hd + 1) * LANE)
        q_ref[hd] = norm_rot(q[:, sl], qg_ref[...]).astype(BF16)
        k_ref[hd] = norm_rot(kn[:, sl] + kpe, kg_ref[...]).astype(BF16)
        v_ref[hd] = (v[:, sl] + _one_lane()).astype(BF16)


def _mla_mixer(x2, a, b, g, w_in, q_lat_g, kv_lat_g, w_uq, w_ukv, q_gain, k_gain, w_out, batch, seq):
    T, Dm = x2.shape
    H = MLA_HEADS
    tm = 256
    nb = seq // tm
    scale = MLA_QK ** -0.5 * LOG2E
    lat = MLA_Q_LORA + MLA_KV_LORA
    kpe_cols = jnp.zeros((Dm, LANE), F32).at[:, MLA_NOPE:MLA_QK].set(w_in[:, lat:])
    win_p = jnp.concatenate([w_in[:, :lat], kpe_cols], axis=1).astype(BF16)
    wuq_p = _pad_heads_cols(w_uq, H, MLA_QK).astype(BF16)
    wukv = w_ukv.reshape(MLA_KV_LORA, H, MLA_NOPE + MLA_V)
    wuk_p = _pad_heads_cols(wukv[:, :, :MLA_NOPE].reshape(MLA_KV_LORA, -1), H, MLA_NOPE).astype(BF16)
    wuv_p = _pad_heads_cols(wukv[:, :, MLA_NOPE:].reshape(MLA_KV_LORA, -1), H, MLA_V).astype(BF16)
    pad_gain = lambda v: jnp.pad(v, (0, LANE - MLA_QK)).reshape(1, LANE)
    c, sa, sb = _rope_tables(jnp.arange(seq), MLA_ROPE, MLA_NOPE)
    row = lambda i: (i, 0)
    full = lambda i: (0, 0)
    per_b = lambda i: (i // nb, 0, 0)
    pos = lambda i: (i % nb, 0)
    head_out = pl.BlockSpec((H, tm, LANE), lambda i: (0, i, 0))
    hshape = jax.ShapeDtypeStruct((H, T, LANE), BF16)
    q, k, v = pl.pallas_call(
        _mla_proj_body,
        grid=(T // tm,),
        in_specs=[pl.BlockSpec((tm, Dm), row),
                  pl.BlockSpec((1, 1, Dm), per_b),
                  pl.BlockSpec((1, 1, Dm), per_b),
                  pl.BlockSpec(win_p.shape, full),
                  pl.BlockSpec((1, MLA_Q_LORA), full),
                  pl.BlockSpec((1, MLA_KV_LORA), full),
                  pl.BlockSpec(wuq_p.shape, full),
                  pl.BlockSpec(wuk_p.shape, full),
                  pl.BlockSpec(wuv_p.shape, full),
                  pl.BlockSpec((1, LANE), full),
                  pl.BlockSpec((1, LANE), full),
                  pl.BlockSpec((tm, LANE), pos),
                  pl.BlockSpec((tm, LANE), pos),
                  pl.BlockSpec((tm, LANE), pos)],
        out_specs=[head_out, head_out, head_out],
        out_shape=[hshape, hshape, hshape],
        compiler_params=_cparams("parallel"),
        name="mla_proj",
    )(x2, a, b, win_p, q_lat_g.reshape(1, -1), kv_lat_g.reshape(1, -1), wuq_p, wuk_p, wuv_p,
      pad_gain(q_gain * scale), pad_gain(k_gain), c, sa, sb)
    blk = min(1024, seq)
    o = _flash(q, k, v, None, batch=batch, seq=seq, mode="causal", tq=blk, tk=blk)
    return _attn_out([o], None, _pad_heads_rows(w_out, H, MLA_V), x2, g, seq)


N_KV_STREAMS = 6


def _nsa_proj_body(x_ref, a_ref, b_ref, w_ref, qg_ref, kg_ref, c_ref, sa_ref, sb_ref, blk_ref,
                   q_ref, kv_ref, ks_ref, gl_ref):
    h = _normmod(x_ref[...], a_ref[0], b_ref[0]).astype(BF16)
    r = jnp.dot(h, w_ref[...], preferred_element_type=F32)
    c, sa, sb = c_ref[...], sa_ref[...], sb_ref[...]

    def norm_rot(t, gain):
        t = t * lax.rsqrt(jnp.sum(t * t, axis=-1, keepdims=True) * (1.0 / NSA_DH) + EPS) * gain
        return _rope(t, c, sa, sb, NSA_ROT // 2)

    for hd in range(NSA_HEADS):
        q_ref[hd] = norm_rot(r[:, hd * LANE:(hd + 1) * LANE], qg_ref[...]).astype(BF16)
    base = NSA_HEADS * LANE
    for st in range(N_KV_STREAMS):
        for gi in range(NSA_GROUPS):
            off = base + (st * NSA_GROUPS + gi) * LANE
            t = r[:, off:off + LANE]
            if st == 2:
                t = norm_rot(t, kg_ref[1:2, :])
                ks_ref[gi] = jnp.concatenate([t.astype(BF16), blk_ref[...]], axis=1)
            elif st == 4:
                t = norm_rot(t, kg_ref[2:3, :])
            elif st in (3, 5):
                t = t + _one_lane()
            kv_ref[st * NSA_GROUPS + gi] = t.astype(BF16)
    gl_ref[...] = r[:, base + N_KV_STREAMS * NSA_GROUPS * LANE:]


def _nsa_compress_body(x_ref, pea_ref, peb_ref, w1a_ref, w1b_ref, w2_ref, kg_ref, c_ref, sa_ref, sb_ref,
                       o_ref, *, is_key, n_cmp):
    x = x_ref[0].astype(F32)
    xa = (x + pea_ref[...]).astype(BF16)
    xb = (x + peb_ref[...]).astype(BF16)
    za = jnp.dot(xa, w1a_ref[...], preferred_element_type=F32)
    zb = jnp.dot(xb, w1b_ref[...], preferred_element_type=F32)
    rows = za.shape[0]
    z = _silu(za + pltpu.roll(zb, rows - 1, 0))
    t = jnp.dot(z.astype(BF16), w2_ref[...], preferred_element_type=F32)
    if is_key:
        t = t * lax.rsqrt(jnp.sum(t * t, axis=-1, keepdims=True) * (1.0 / NSA_DH) + EPS) * kg_ref[...]
        t = _rope(t, c_ref[...], sa_ref[...], sb_ref[...], NSA_ROT // 2)
    valid = lax.broadcasted_iota(I32, t.shape, 0) < n_cmp
    o_ref[0] = jnp.where(valid, t, 0.0).astype(BF16)


def _nsa_cmp_select_body(q_ref, kc_ref, vc_ref, oc_ref, sel_ref, p_sc, *, tq, ncp, n_cmp, n_top):
    i = pl.program_id(2)
    t0 = i * tq
    kc = kc_ref[0]
    vc = vc_ref[0]
    rep = q_ref.shape[0]
    q = q_ref[...].reshape(rep * tq, LANE)
    sc = lax.dot_general(q, kc, _NT, preferred_element_type=F32)
    qpos = t0 + lax.broadcasted_iota(I32, (tq, ncp), 0)
    blk = lax.broadcasted_iota(I32, (tq, ncp), 1)
    mask = ((blk * CMP_STRIDE + (CMP_BLOCK - 1)) <= qpos) & (blk < n_cmp)
    sc = jnp.where(mask[None], sc.reshape(rep, tq, ncp), NEG)
    p = jnp.exp2(sc - jnp.max(sc, axis=-1, keepdims=True))
    p = jnp.where(mask[None], p / jnp.sum(p, axis=-1, keepdims=True), 0.0)
    oc = jnp.dot(p.reshape(rep * tq, ncp).astype(BF16), vc, preferred_element_type=F32)
    oc_ref[...] = oc.reshape(rep, tq, LANE).astype(BF16)
    qpos_t = t0 + lax.broadcasted_iota(I32, (ncp, tq), 1)
    blk_t = lax.broadcasted_iota(I32, (ncp, tq), 0)
    mask_t = ((blk_t * CMP_STRIDE + (CMP_BLOCK - 1)) <= qpos_t) & (blk_t < n_cmp)
    psum = jnp.zeros((ncp, tq), F32)
    for r in range(rep):
        st = lax.dot_general(kc, q_ref[r], _NT, preferred_element_type=F32)
        st = jnp.where(mask_t, st, NEG)
        pt = jnp.exp2(st - jnp.max(st, axis=0, keepdims=True))
        psum = psum + jnp.where(mask_t, pt / jnp.sum(pt, axis=0, keepdims=True), 0.0)
    p_sc[...] = jnp.zeros(p_sc.shape, F32)
    p_sc[8:8 + ncp, :] = psum
    per = SLC_BLOCK // CMP_STRIDE
    imp = p_sc[pl.ds(7, MAX_SLC, stride=per), :]
    for k in range(1, per + 1):
        imp = imp + p_sc[pl.ds(7 + k, MAX_SLC, stride=per), :]
    jb = lax.broadcasted_iota(I32, (MAX_SLC, tq), 0)
    qp = t0 + lax.broadcasted_iota(I32, (MAX_SLC, tq), 1)
    imp = jnp.where(jb * SLC_BLOCK <= qp, imp, -BIG)
    imp = jnp.where((jb == 0) | (jb == jnp.right_shift(qp, 6)), BIG, imp)
    jbf = jb.astype(F32)
    sel_t = jnp.full((MAX_SLC, tq), SEL_OFF, F32)
    for _ in range(n_top):
        top = jnp.max(imp, axis=0, keepdims=True)
        first = jnp.min(jnp.where(imp == top, jbf, float(MAX_SLC)), axis=0, keepdims=True)
        hit = jbf == first
        sel_t = jnp.where(hit, 0.0, sel_t)
        imp = jnp.where(hit, -jnp.inf, imp)
    sel_ref[0] = sel_t.T.astype(BF16)


def _nsa_mixer(x2, a, b, g, w_in, w_cmp1, w_cmp2, cmp_pos, q_gain, k_gain, w_out, batch, seq):
    T, Dm = x2.shape
    H, G, dh = NSA_HEADS, NSA_GROUPS, NSA_DH
    scale = dh ** -0.5 * LOG2E
    n_cmp = seq // CMP_STRIDE - 1
    ncp = seq // CMP_STRIDE
    n_slc = seq // SLC_BLOCK
    assert n_slc <= MAX_SLC and ncp <= MAX_SLC * (SLC_BLOCK // CMP_STRIDE)
    n_top = min(SLC_TOP_N, n_slc)
    tm = 256
    nb = seq // tm
    q_cols = _pad_heads_cols(w_in[:, :H * dh], H, dh)
    kv_cols = _pad_heads_cols(w_in[:, H * dh:H * dh + N_KV_STREAMS * G * dh], N_KV_STREAMS * G, dh)
    gl_cols = jnp.pad(w_in[:, H * dh + N_KV_STREAMS * G * dh:], ((0, 0), (0, LANE - 3 * H)))
    w_p = jnp.concatenate([q_cols, kv_cols, gl_cols], axis=1).astype(BF16)
    pad_gain = lambda v: jnp.pad(v, ((0, 0), (0, LANE - dh)))
    c, sa, sb = _rope_tables(jnp.arange(seq), NSA_ROT, 0)
    row = lambda i: (i, 0)
    full = lambda i: (0, 0)
    per_b = lambda i: (i // nb, 0, 0)
    pos = lambda i: (i % nb, 0)
    blk_onehot = (jnp.arange(seq)[:, None] // SLC_BLOCK == jnp.arange(MAX_SLC)[None, :]).astype(BF16)
    q, kv, ks, gl = pl.pallas_call(
        _nsa_proj_body,
        grid=(T // tm,),
        in_specs=[pl.BlockSpec((tm, Dm), row),
                  pl.BlockSpec((1, 1, Dm), per_b),
                  pl.BlockSpec((1, 1, Dm), per_b),
                  pl.BlockSpec(w_p.shape, full),
                  pl.BlockSpec((1, LANE), full),
                  pl.BlockSpec((3, LANE), full),
                  pl.BlockSpec((tm, LANE), pos),
                  pl.BlockSpec((tm, LANE), pos),
                  pl.BlockSpec((tm, LANE), pos),
                  pl.BlockSpec((tm, MAX_SLC), pos)],
        out_specs=[pl.BlockSpec((H, tm, LANE), lambda i: (0, i, 0)),
                   pl.BlockSpec((N_KV_STREAMS * G, tm, LANE), lambda i: (0, i, 0)),
                   pl.BlockSpec((G, tm, LANE + MAX_SLC), lambda i: (0, i, 0)),
                   pl.BlockSpec((tm, LANE), row)],
        out_shape=[jax.ShapeDtypeStruct((H, T, LANE), BF16),
                   jax.ShapeDtypeStruct((N_KV_STREAMS * G, T, LANE), BF16),
                   jax.ShapeDtypeStruct((G, T, LANE + MAX_SLC), BF16),
                   jax.ShapeDtypeStruct((T, LANE), F32)],
        compiler_params=_cparams("parallel"),
        name="nsa_proj",
    )(x2, a, b, w_p, pad_gain(q_gain.reshape(1, dh) * scale), pad_gain(k_gain), c, sa, sb, blk_onehot)
    kc_raw, vc_raw, _, vs, kw, vw = [kv[s * G:(s + 1) * G] for s in range(N_KV_STREAMS)]

    cmp_end = jnp.arange(ncp) * CMP_STRIDE + (CMP_BLOCK - 1)
    cc, csa, csb = _rope_tables(cmp_end, NSA_ROT, 0)
    kdim = CMP_STRIDE * LANE

    def compress(raw, w1, w2, pe, is_key):
        x16 = raw.reshape(G, T // CMP_STRIDE, kdim)
        w1p = jnp.pad(w1, ((0, 0), (0, LANE - dh), (0, LANE - dh)))
        w1a = w1p[:CMP_STRIDE].reshape(kdim, LANE).astype(BF16)
        w1b = w1p[CMP_STRIDE:].reshape(kdim, LANE).astype(BF16)
        pep = jnp.pad(pe, ((0, 0), (0, LANE - dh)))
        pea = pep[:CMP_STRIDE].reshape(1, kdim)
        peb = pep[CMP_STRIDE:].reshape(1, kdim)
        w2p = jnp.pad(w2, ((0, LANE - dh), (0, LANE - dh))).astype(BF16)
        const = lambda gi, bi: (0, 0)
        return pl.pallas_call(
            functools.partial(_nsa_compress_body, is_key=is_key, n_cmp=n_cmp),
            grid=(G, batch),
            in_specs=[pl.BlockSpec((1, ncp, kdim), lambda gi, bi: (gi, bi, 0)),
                      pl.BlockSpec((1, kdim), const),
                      pl.BlockSpec((1, kdim), const),
                      pl.BlockSpec((kdim, LANE), const),
                      pl.BlockSpec((kdim, LANE), const),
                      pl.BlockSpec((LANE, LANE), const),
                      pl.BlockSpec((1, LANE), const),
                      pl.BlockSpec((ncp, LANE), const),
                      pl.BlockSpec((ncp, LANE), const),
                      pl.BlockSpec((ncp, LANE), const)],
            out_specs=pl.BlockSpec((1, ncp, LANE), lambda gi, bi: (gi, bi, 0)),
            out_shape=jax.ShapeDtypeStruct((G, batch * ncp, LANE), BF16),
            compiler_params=_cparams("parallel", "parallel"),
            name="nsa_compress_" + ("k" if is_key else "v"),
        )(x16, pea, peb, w1a, w1b, w2p, pad_gain(k_gain)[0:1], cc, csa, csb)

    kc = compress(kc_raw, w_cmp1[0], w_cmp2[0], cmp_pos[0], True)
    vc = compress(vc_raw, w_cmp1[1], w_cmp2[1], cmp_pos[1], False)

    tq = 128
    nq = seq // tq
    q_map = lambda gi, bi, i: (gi, bi * nq + i, 0)
    c_map = lambda gi, bi, i: (gi, bi, 0)
    o_c, sel = pl.pallas_call(
        functools.partial(_nsa_cmp_select_body, tq=tq, ncp=ncp, n_cmp=n_cmp, n_top=n_top),
        grid=(G, batch, nq),
        in_specs=[pl.BlockSpec((NSA_REP, tq, LANE), q_map),
                  pl.BlockSpec((1, ncp, LANE), c_map),
                  pl.BlockSpec((1, ncp, LANE), c_map)],
        out_specs=[pl.BlockSpec((NSA_REP, tq, LANE), q_map),
                   pl.BlockSpec((1, tq, LANE), q_map)],
        out_shape=[jax.ShapeDtypeStruct((H, T, LANE), BF16),
                   jax.ShapeDtypeStruct((G, T, LANE), BF16)],
        scratch_shapes=[pltpu.VMEM((8 + MAX_SLC * (SLC_BLOCK // CMP_STRIDE), tq), F32)],
        compiler_params=_cparams("parallel", "parallel", "parallel"),
        name="nsa_cmp_select",
    )(q, kc, vc)

    o_s = _flash(q, ks, vs, sel, batch=batch, seq=seq, mode="select", tq=256, tk=min(1024, seq))
    o_w = _flash(q, kw, vw, None, batch=batch, seq=seq, mode="window", tq=256, tk=WINDOW)
    return _attn_out([o_c, o_s, o_w], gl, _pad_heads_rows(w_out, H, dh), x2, g, seq)


def _router_body(x_ref, a_ref, b_ref, rwt_ref, rb_ref, tri_ref, h_ref, e_ref, w_ref, rank_ref, cnt_ref,
                 carry_sc, *, tm):
    i = pl.program_id(0)

    @pl.when(i == 0)
    def _():
        carry_sc[...] = jnp.zeros(carry_sc.shape, F32)

    h = _normmod(x_ref[...], a_ref[0], b_ref[0])
    h_ref[...] = h
    logits = lax.dot_general(rwt_ref[...], h, _NT, preferred_element_type=F32, precision=HIGHEST)
    scores = _sigmoid(logits)
    biased = scores + rb_ref[...]
    ng, per = N_GROUPS, EXPERTS_PER_GROUP
    row = lambda arr, r: arr[r:r + 1, :]
    gsel = jnp.zeros((1, tm), I32)
    best = None
    for gi in range(ng):
        v = [row(biased, gi * per + k) for k in range(per)]
        top2 = None
        for p in range(per):
            for q in range(p + 1, per):
                s = v[p] + v[q]
                top2 = s if top2 is None else jnp.maximum(top2, s)
        if best is None:
            best = top2
        else:
            better = top2 > best
            gsel = jnp.where(better, gi, gsel)
            best = jnp.where(better, top2, best)
    cb, cs = [], []
    for k in range(per):
        b_k = row(biased, k)
        s_k = row(scores, k)
        for gi in range(1, ng):
            hit = gsel == gi
            b_k = jnp.where(hit, row(biased, gi * per + k), b_k)
            s_k = jnp.where(hit, row(scores, gi * per + k), s_k)
        cb.append(b_k)
        cs.append(s_k)

    def argmax_first(vals):
        idx = jnp.zeros((1, tm), I32)
        top = vals[0]
        for k in range(1, per):
            better = vals[k] > top
            idx = jnp.where(better, k, idx)
            top = jnp.where(better, vals[k], top)
        return idx

    def pick(vals, idx):
        out = vals[0]
        for k in range(1, per):
            out = jnp.where(idx == k, vals[k], out)
        return out

    i1 = argmax_first(cb)
    i2 = argmax_first([jnp.where(i1 == k, -jnp.inf, cb[k]) for k in range(per)])
    w1 = pick(cs, i1)
    w2 = pick(cs, i2)
    tot = w1 + w2
    e1 = gsel * per + i1
    e2 = gsel * per + i2
    eid = lax.broadcasted_iota(I32, (N_EXPERTS, tm), 0)
    hot = (eid == e1) | (eid == e2)
    onehot = jnp.where(hot, 1.0, 0.0)
    before = carry_sc[...] + jnp.dot(onehot.astype(BF16), tri_ref[...], preferred_element_type=F32)
    r1 = jnp.sum(jnp.where(eid == e1, before, 0.0), axis=0, keepdims=True)
    r2 = jnp.sum(jnp.where(eid == e2, before, 0.0), axis=0, keepdims=True)
    carry = carry_sc[...] + jnp.sum(onehot, axis=1, keepdims=True)
    carry_sc[...] = carry
    cnt_ref[...] = jnp.broadcast_to(carry, cnt_ref.shape)
    zi = jnp.zeros((6, tm), I32)
    e_ref[...] = jnp.concatenate([e1, e2, zi], axis=0)
    rank_ref[...] = jnp.concatenate([r1.astype(I32), r2.astype(I32), zi], axis=0)
    w_ref[...] = jnp.concatenate([w1 / tot, w2 / tot, jnp.zeros((6, tm), F32)], axis=0)


def _moe_dispatch_body(starts_ref, counts_ref, padded_ref, nv_ref, e_ref, rank_ref, h_ref, xs_hbm, dest_ref,
                       zbuf, sem, *, tm, tg, n_tiles):
    i = pl.program_id(0)

    @pl.when(i == 0)
    def _():
        zbuf[...] = jnp.zeros(zbuf.shape, F32)
        zrow = zbuf.at[pl.ds(0, 1)]
        for ex in range(N_EXPERTS):
            lo = starts_ref[ex] + counts_ref[ex]
            hi = starts_ref[ex] + padded_ref[ex]

            def fill(r, carry):
                pltpu.make_async_copy(zrow, xs_hbm.at[pl.ds(r, 1)], sem).start()
                return carry

            lax.fori_loop(lo, hi, fill, 0)

            def drain_fill(r, carry):
                pltpu.make_async_copy(zrow, xs_hbm.at[pl.ds(0, 1)], sem).wait()
                return carry

            lax.fori_loop(lo, hi, drain_fill, 0)

        def fill_tile(t, carry):
            pltpu.make_async_copy(zbuf, xs_hbm.at[pl.ds(pl.multiple_of(t * tg, tg), tg)], sem).start()
            return carry

        lax.fori_loop(nv_ref[0], n_tiles, fill_tile, 0)

        def drain_tile(t, carry):
            pltpu.make_async_copy(zbuf, xs_hbm.at[pl.ds(0, tg)], sem).wait()
            return carry

        lax.fori_loop(nv_ref[0], n_tiles, drain_tile, 0)

    def row_copy(r, slot):
        return pltpu.make_async_copy(h_ref.at[pl.ds(r, 1)], xs_hbm.at[pl.ds(slot, 1)], sem)

    def issue(r, carry):
        for k in range(2):
            slot = starts_ref[e_ref[k, r]] + rank_ref[k, r]
            dest_ref[0, k, r] = slot
            row_copy(r, slot).start()
        return carry

    lax.fori_loop(0, tm, issue, 0, unroll=4)

    def drain(r, carry):
        row_copy(r, 0).wait()
        row_copy(r, 0).wait()
        return carry

    lax.fori_loop(0, tm, drain, 0, unroll=8)


def _moe_expert_body(te_ref, nv_ref, xs_ref, win_ref, wout_ref, y_ref):
    i = pl.program_id(0)

    @pl.when(i < nv_ref[0])
    def _():
        x = xs_ref[...].astype(BF16)
        gu = jnp.dot(x, win_ref[0], preferred_element_type=F32)
        act = (_silu(gu[:, :D_EXPERT]) * gu[:, D_EXPERT:]).astype(BF16)
        y_ref[...] = jnp.dot(act, wout_ref[0], preferred_element_type=F32)

    @pl.when(i >= nv_ref[0])
    def _():
        y_ref[...] = jnp.zeros(y_ref.shape, F32)


def _moe_combine_body(dcur_ref, dnxt_ref, y_hbm, x_ref, w_ref, g_ref, o_ref, ybuf, sems, *, tm, nt):
    i = pl.program_id(0)
    cur = i % 2

    def row_copy(slot, buf, k, r):
        return pltpu.make_async_copy(y_hbm.at[pl.ds(slot, 1)], ybuf.at[buf, k, pl.ds(r, 1)], sems.at[buf])

    def issue(d_ref, buf):
        def body(r, carry):
            for k in range(2):
                row_copy(d_ref[0, k, r], buf, k, r).start()
            return carry

        lax.fori_loop(0, tm, body, 0, unroll=4)

    @pl.when(i == 0)
    def _():
        issue(dcur_ref, 0)

    @pl.when(i + 1 < nt)
    def _():
        issue(dnxt_ref, 1 - cur)

    def drain(r, carry):
        for k in range(2):
            row_copy(0, cur, k, r).wait()
        return carry

    lax.fori_loop(0, tm, drain, 0, unroll=8)
    w = w_ref[...]
    y = w[:, 0:1] * ybuf[cur, 0] + w[:, 1:2] * ybuf[cur, 1]
    o_ref[...] = x_ref[...] + g_ref[0] * y


def _grouped_moe(x2, a, b, g, router_w, router_bias, w_in, w_out, seq):
    T, Dm = x2.shape
    E = N_EXPERTS
    tm = 512
    nb = seq // tm
    nt = T // tm
    row = lambda i: (i, 0)
    full = lambda i: (0, 0)
    per_b = lambda i: (i // nb, 0, 0)
    lanes = lambda i: (0, i)
    tri = jnp.asarray(np.triu(np.ones((tm, tm), np.float32), 1)).astype(BF16)
    h, e, w, rank, cnt = pl.pallas_call(
        functools.partial(_router_body, tm=tm),
        grid=(nt,),
        in_specs=[pl.BlockSpec((tm, Dm), row),
                  pl.BlockSpec((1, 1, Dm), per_b),
                  pl.BlockSpec((1, 1, Dm), per_b),
                  pl.BlockSpec((E, Dm), full),
                  pl.BlockSpec((E, 1), full),
                  pl.BlockSpec((tm, tm), full)],
        out_specs=[pl.BlockSpec((tm, Dm), row),
                   pl.BlockSpec((8, tm), lanes),
                   pl.BlockSpec((8, tm), lanes),
                   pl.BlockSpec((8, tm), lanes),
                   pl.BlockSpec((E, LANE), full)],
        out_shape=[jax.ShapeDtypeStruct((T, Dm), F32),
                   jax.ShapeDtypeStruct((8, T), I32),
                   jax.ShapeDtypeStruct((8, T), F32),
                   jax.ShapeDtypeStruct((8, T), I32),
                   jax.ShapeDtypeStruct((E, LANE), F32)],
        scratch_shapes=[pltpu.VMEM((E, 1), F32)],
        compiler_params=_cparams("arbitrary"),
        name="moe_router",
    )(x2, a, b, router_w.T, router_bias.reshape(E, 1), tri)

    tg = MOE_TILE
    n_tiles = (2 * T) // tg + E
    n_slots = n_tiles * tg
    counts = cnt[:, 0].astype(I32)
    padded = ((counts + tg - 1) // tg) * tg
    ends = jnp.cumsum(padded)
    starts = ends - padded
    tile_start = jnp.arange(n_tiles, dtype=I32) * tg
    tile_expert = jnp.minimum(jnp.searchsorted(ends, tile_start, side="right"), E - 1).astype(I32)
    n_valid = (ends[-1] // tg).astype(I32).reshape(1)

    tc = 256
    ntc = T // tc
    xs, dest = pl.pallas_call(
        functools.partial(_moe_dispatch_body, tm=tc, tg=tg, n_tiles=n_tiles),
        grid_spec=pltpu.PrefetchScalarGridSpec(
            num_scalar_prefetch=4,
            grid=(ntc,),
            in_specs=[pl.BlockSpec((8, tc), lambda i, *_: (0, i), memory_space=pltpu.SMEM),
                      pl.BlockSpec((8, tc), lambda i, *_: (0, i), memory_space=pltpu.SMEM),
                      pl.BlockSpec((tc, Dm), lambda i, *_: (i, 0))],
            out_specs=[pl.BlockSpec(memory_space=pl.ANY),
                       pl.BlockSpec((1, 2, tc), lambda i, *_: (i, 0, 0), memory_space=pltpu.SMEM)],
            scratch_shapes=[pltpu.VMEM((tg, Dm), F32), pltpu.SemaphoreType.DMA(())]),
        out_shape=[jax.ShapeDtypeStruct((n_slots, Dm), F32),
                   jax.ShapeDtypeStruct((ntc, 2, tc), I32)],
        compiler_params=_cparams("arbitrary"),
        name="moe_dispatch",
    )(starts.astype(I32), counts, padded, n_valid, e, rank, h)

    last_tile = lambda i, te, nv: (jnp.minimum(i, nv[0] - 1), 0)
    y = pl.pallas_call(
        _moe_expert_body,
        grid_spec=pltpu.PrefetchScalarGridSpec(
            num_scalar_prefetch=2,
            grid=(n_tiles,),
            in_specs=[pl.BlockSpec((tg, Dm), last_tile),
                      pl.BlockSpec((1, Dm, 2 * D_EXPERT), lambda i, te, nv: (te[i], 0, 0)),
                      pl.BlockSpec((1, D_EXPERT, Dm), lambda i, te, nv: (te[i], 0, 0))],
            out_specs=pl.BlockSpec((tg, Dm), lambda i, te, nv: (i, 0))),
        out_shape=jax.ShapeDtypeStruct((n_slots, Dm), F32),
        compiler_params=_cparams("arbitrary"),
        name="moe_experts",
    )(tile_expert, n_valid, xs, w_in.astype(BF16), w_out.astype(BF16))

    nbc = seq // tc
    return pl.pallas_call(
        functools.partial(_moe_combine_body, tm=tc, nt=ntc),
        grid=(ntc,),
        in_specs=[pl.BlockSpec((1, 2, tc), lambda i: (i, 0, 0), memory_space=pltpu.SMEM),
                  pl.BlockSpec((1, 2, tc), lambda i: (jnp.minimum(i + 1, ntc - 1), 0, 0),
                               memory_space=pltpu.SMEM),
                  pl.BlockSpec(memory_space=pl.ANY),
                  pl.BlockSpec((tc, Dm), row),
                  pl.BlockSpec((tc, 2), row),
                  pl.BlockSpec((1, 1, Dm), lambda i: (i // nbc, 0, 0))],
        out_specs=pl.BlockSpec((tc, Dm), row),
        out_shape=jax.ShapeDtypeStruct((T, Dm), F32),
        scratch_shapes=[pltpu.VMEM((2, 2, tc, Dm), F32), pltpu.SemaphoreType.DMA((2,))],
        compiler_params=_cparams("arbitrary"),
        name="moe_combine",
    )(dest, dest, y, x2, w[:2].T, g)


def kernel(x, c, norm_mix_g, norm_ffn_g, w_ada, b_ada, conv_w_pw1, conv_b_pw1, conv_w_dw, conv_b_dw, conv_ln_g, conv_ln_b, conv_w_pw2, conv_b_pw2, nsa_w_in, nsa_w_cmp1, nsa_w_cmp2, nsa_cmp_pos, nsa_q_gain, nsa_k_gain, nsa_w_out, mla_w_in, mla_q_lat_g, mla_kv_lat_g, mla_w_uq, mla_w_ukv, mla_q_gain, mla_k_gain, mla_w_out, router_w, router_bias, moe_w_in, moe_w_out):
    B, S, Dm = x.shape
    depth = w_ada.shape[0]
    mods = _ada(c, w_ada, b_ada)
    x2 = x.reshape(B * S, Dm)
    for i in range(depth):
        sh1, sc1, g1, sh2, sc2, g2 = [m.reshape(B, 1, Dm) for m in jnp.split(mods[i], 6, axis=-1)]
        a1 = norm_mix_g[i] * (1.0 + sc1)
        kind, j = i % 3, i // 3
        if kind == 0:
            x2 = _conv_mixer(x2, a1, sh1, g1, conv_w_pw1[j], conv_b_pw1[j], conv_w_dw[j], conv_b_dw[j],
                             conv_ln_g[j], conv_ln_b[j], conv_w_pw2[j], conv_b_pw2[j], S)
        elif kind == 1:
            x2 = _nsa_mixer(x2, a1, sh1, g1, nsa_w_in[j], nsa_w_cmp1[j], nsa_w_cmp2[j], nsa_cmp_pos[j],
                            nsa_q_gain[j], nsa_k_gain[j], nsa_w_out[j], B, S)
        else:
            x2 = _mla_mixer(x2, a1, sh1, g1, mla_w_in[j], mla_q_lat_g[j], mla_kv_lat_g[j], mla_w_uq[j],
                            mla_w_ukv[j], mla_q_gain[j], mla_k_gain[j], mla_w_out[j], B, S)
        a2 = norm_ffn_g[i] * (1.0 + sc2)
        x2 = _grouped_moe(x2, a2, sh2, g2, router_w, router_bias, moe_w_in[i], moe_w_out[i], S)
    return x2.reshape(B, S, Dm)
```

```python
import functools
import math

import numpy as np
import jax
import jax.numpy as jnp
from jax import lax
from jax.experimental import pallas as pl
from jax.experimental.pallas import tpu as pltpu

F32 = jnp.float32
BF16 = jnp.bfloat16
I32 = jnp.int32
HIGHEST = lax.Precision.HIGHEST

EPS = 1e-6
NEG = -1e30
BIG = 1e30
ROPE_THETA = 500000.0
LANE = 128
VMEM_LIMIT = 56 * 1024 * 1024

D_MODEL = 1024
CONV_WIDTH = 31
CONV_HALO = 32

NSA_HEADS = 16
NSA_GROUPS = 4
NSA_REP = NSA_HEADS // NSA_GROUPS
NSA_DH = 64
NSA_ROT = 16
CMP_BLOCK = 32
CMP_STRIDE = 16
SLC_BLOCK = 64
SLC_TOP_N = 16
WINDOW = 512
MAX_SLC = 128

MLA_HEADS = 16
MLA_Q_LORA = 384
MLA_KV_LORA = 256
MLA_NOPE = 64
MLA_ROPE = 32
MLA_V = 64
MLA_QK = MLA_NOPE + MLA_ROPE

N_EXPERTS = 16
N_GROUPS = 4
EXPERTS_PER_GROUP = 4
D_EXPERT = 512
MOE_TILE = 512

_NT = (((1,), (1,)), ((), ()))


def _cparams(*sem):
    return pltpu.CompilerParams(dimension_semantics=sem, vmem_limit_bytes=VMEM_LIMIT)


def _sigmoid(x):
    return 1.0 / (1.0 + jnp.exp(-x))


def _silu(x):
    return x * _sigmoid(x)


def _normmod(x, a, b):
    ms = jnp.mean(x * x, axis=-1, keepdims=True)
    return x * lax.rsqrt(ms + EPS) * a + b


def _rope(x, c, sa, sb, half):
    n = x.shape[-1]
    return x * c + pltpu.roll(x, n - half, 1) * sa + pltpu.roll(x, half, 1) * sb


def _ada_body(c_ref, w_ref, b_ref, o_ref):
    c = c_ref[...]
    o_ref[0] = jnp.dot(_silu(c), w_ref[0], preferred_element_type=F32, precision=HIGHEST) + b_ref[0]


def _ada(c, w_ada, b_ada):
    B = c.shape[0]
    L, Dm, N = w_ada.shape
    Bp = -(-B // 8) * 8
    cp = jnp.pad(c, ((0, Bp - B), (0, 0)))
    tn = 1536
    out = pl.pallas_call(
        _ada_body,
        grid=(L, N // tn),
        in_specs=[pl.BlockSpec((Bp, Dm), lambda l, j: (0, 0)),
                  pl.BlockSpec((1, Dm, tn), lambda l, j: (l, 0, j)),
                  pl.BlockSpec((1, 1, tn), lambda l, j: (l, 0, j))],
        out_specs=pl.BlockSpec((1, Bp, tn), lambda l, j: (l, 0, j)),
        out_shape=jax.ShapeDtypeStruct((L, Bp, N), F32),
        compiler_params=_cparams("arbitrary", "arbitrary"),
        name="adaln",
    )(cp, w_ada, b_ada.reshape(L, 1, N))
    return out[:, :B]


def _conv_pw1_body(x_ref, a_ref, b_ref, w_ref, bias_ref, u_ref):
    h = _normmod(x_ref[...], a_ref[0], b_ref[0]).astype(BF16)
    r = jnp.dot(h, w_ref[...], preferred_element_type=F32) + bias_ref[...]
    d = u_ref.shape[-1]
    u_ref[...] = r[:, :d] * _sigmoid(r[:, d:])


def _conv_dw_body(u_ref, uh_ref, wdw_ref, bdw_ref, lng_ref, lnb_ref, w2_ref, b2_ref, x_ref, g_ref,
                  o_ref, ext_ref, *, tm, seq):
    i = pl.program_id(0)
    at_seq_start = (i * tm) % seq == 0
    ext_ref[0:CONV_HALO, :] = jnp.where(at_seq_start, 0.0, uh_ref[...])
    ext_ref[CONV_HALO:, :] = u_ref[...]
    base = CONV_HALO - (CONV_WIDTH - 1)
    acc = jnp.zeros((tm, u_ref.shape[-1]), F32) + bdw_ref[...]
    for k in range(CONV_WIDTH):
        acc = acc + wdw_ref[k:k + 1, :] * ext_ref[base + k:base + k + tm, :]
    mu = jnp.mean(acc, axis=-1, keepdims=True)
    dlt = acc - mu
    var = jnp.mean(dlt * dlt, axis=-1, keepdims=True)
    y = dlt * lax.rsqrt(var + EPS) * lng_ref[...] + lnb_ref[...]
    z = _silu(y).astype(BF16)
    out = jnp.dot(z, w2_ref[...], preferred_element_type=F32) + b2_ref[...]
    o_ref[...] = x_ref[...] + g_ref[0] * out


def _conv_mixer(x2, a, b, g, w_pw1, b_pw1, w_dw, b_dw, ln_g, ln_b, w_pw2, b_pw2, seq):
    T, Dm = x2.shape
    tm = 512
    nb = seq // tm
    row = lambda i: (i, 0)
    per_b = lambda i: (i // nb, 0, 0)
    full = lambda i: (0, 0)
    u = pl.pallas_call(
        _conv_pw1_body,
        grid=(T // tm,),
        in_specs=[pl.BlockSpec((tm, Dm), row),
                  pl.BlockSpec((1, 1, Dm), per_b),
                  pl.BlockSpec((1, 1, Dm), per_b),
                  pl.BlockSpec((Dm, 2 * Dm), full),
                  pl.BlockSpec((1, 2 * Dm), full)],
        out_specs=pl.BlockSpec((tm, Dm), row),
        out_shape=jax.ShapeDtypeStruct((T, Dm), F32),
        compiler_params=_cparams("parallel"),
        name="conv_pw1_glu",
    )(x2, a, b, w_pw1.astype(BF16), b_pw1.reshape(1, -1))
    hb = tm // CONV_HALO
    wdw = jnp.pad(w_dw, ((0, CONV_HALO - CONV_WIDTH), (0, 0)))
    vec = lambda v: v.reshape(1, -1)
    return pl.pallas_call(
        functools.partial(_conv_dw_body, tm=tm, seq=seq),
        grid=(T // tm,),
        in_specs=[pl.BlockSpec((tm, Dm), row),
                  pl.BlockSpec((CONV_HALO, Dm), lambda i: (jnp.maximum(i * hb - 1, 0), 0)),
                  pl.BlockSpec((CONV_HALO, Dm), full),
                  pl.BlockSpec((1, Dm), full),
                  pl.BlockSpec((1, Dm), full),
                  pl.BlockSpec((1, Dm), full),
                  pl.BlockSpec((Dm, Dm), full),
                  pl.BlockSpec((1, Dm), full),
                  pl.BlockSpec((tm, Dm), row),
                  pl.BlockSpec((1, 1, Dm), per_b)],
        out_specs=pl.BlockSpec((tm, Dm), row),
        out_shape=jax.ShapeDtypeStruct((T, Dm), F32),
        scratch_shapes=[pltpu.VMEM((tm + CONV_HALO, Dm), F32)],
        compiler_params=_cparams("parallel"),
        name="conv_dw_ln_pw2",
    )(u, u, wdw, vec(b_dw), vec(ln_g), vec(ln_b), w_pw2.astype(BF16), vec(b_pw2), x2, g)


FLASH_FIRST, FLASH_LAST, FLASH_MASKED = 1, 2, 4
FLASH_CHUNK = 16
SEL_OFF = -(2.0 ** 100)
ONE_LANE = 64
LOG2E = 1.4426950408889634
FLASH_NPROB = 2


def _one_lane():
    return jnp.where(lax.broadcasted_iota(I32, (1, LANE), 1) == ONE_LANE, 1.0, 0.0)


def _flash_body(qi_ref, kj_ref, flag_ref, *refs, mode, rep, tq, tk, nprob):
    if mode == "select":
        q_ref, k_ref, v_ref, selb_ref, o_ref, qa_sc, s_sc, p_sc, mb_sc, al_sc, acc_sc = refs
    else:
        q_ref, k_ref, v_ref, o_ref, s_sc, p_sc, mb_sc, al_sc, acc_sc = refs
    step = pl.program_id(2)
    qi = qi_ref[step]
    kj = kj_ref[step]
    flag = flag_ref[step]
    rows = rep * tq
    ch = FLASH_CHUNK
    nl = tk // LANE
    lane_fold = lambda t, op: functools.reduce(op, [t[:, i * LANE:(i + 1) * LANE] for i in range(nl)])

    @pl.when((flag & FLASH_FIRST) != 0)
    def _():
        mb_sc[...] = jnp.full(mb_sc.shape, NEG, F32)
        acc_sc[...] = jnp.zeros(acc_sc.shape, F32)
        if mode == "select":
            for pr in range(nprob):
                qa_sc[pr, :, :LANE] = q_ref[pr * rep:(pr + 1) * rep].reshape(rows, LANE)
                qa_sc[pr, :, LANE:] = jnp.concatenate([selb_ref[pr]] * rep, axis=0)

    def process(masked):
        thr = qi * tq - kj * tk
        for pr in range(nprob):
            q = qa_sc[pr] if mode == "select" else q_ref[pr * rep:(pr + 1) * rep].reshape(rows, LANE)
            s_sc[pr] = lax.dot_general(q, k_ref[pr], _NT, preferred_element_type=F32)
        if masked:
            diff = lax.broadcasted_iota(I32, (ch, tk), 1) - lax.broadcasted_iota(I32, (ch, tk), 0)
        for pr in range(nprob):
            for r0 in range(0, rows, ch):
                sc = s_sc[pr, r0:r0 + ch, :]
                if masked:
                    lim = thr + (r0 % tq)
                    ok = diff <= lim
                    if mode == "window":
                        ok = ok & (diff > lim - WINDOW)
                    sc = jnp.where(ok, sc, NEG)
                    s_sc[pr, r0:r0 + ch, :] = sc
                m_prev = mb_sc[pr, r0:r0 + ch, :]
                m_new = jnp.maximum(m_prev, jnp.max(lane_fold(sc, jnp.maximum), axis=1, keepdims=True))
                al_sc[pr, r0:r0 + ch, :] = jnp.exp2(m_prev - m_new)
                mb_sc[pr, r0:r0 + ch, :] = m_new
            for r0 in range(0, rows, ch):
                mb = mb_sc[pr, r0:r0 + ch, :]
                p = jnp.exp2(s_sc[pr, r0:r0 + ch, :] - jnp.concatenate([mb] * nl, axis=1))
                p_sc[pr, r0:r0 + ch, :] = p.astype(BF16)
            acc_sc[pr] = al_sc[pr] * acc_sc[pr] + jnp.dot(p_sc[pr], v_ref[pr], preferred_element_type=F32)

    if mode == "window":
        pl.when(kj >= 0)(lambda: process(True))
    else:
        pl.when((kj >= 0) & ((flag & FLASH_MASKED) != 0))(lambda: process(True))
        pl.when((kj >= 0) & ((flag & FLASH_MASKED) == 0))(lambda: process(False))

    @pl.when((flag & FLASH_LAST) != 0)
    def _():
        for pr in range(nprob):
            acc = acc_sc[pr]
            o = acc / acc[:, ONE_LANE:ONE_LANE + 1]
            o_ref[pr * rep:(pr + 1) * rep] = o.reshape(rep, tq, LANE).astype(o_ref.dtype)


def _flash_schedule(seq, tq, tk, mode):
    qi, kj, flags = [], [], []
    for i in range(seq // tq):
        hi = (i * tq + tq - 1) // tk
        lo = 0 if mode != "window" else (i * tq - (WINDOW - 1)) // tk
        js = list(range(lo, hi + 1))
        for n, j in enumerate(js):
            crosses_diagonal = (j + 1) * tk - 1 > i * tq
            qi.append(i)
            kj.append(j if j >= 0 else -1)
            flags.append((FLASH_FIRST if n == 0 else 0) | (FLASH_LAST if n == len(js) - 1 else 0)
                         | (FLASH_MASKED if crosses_diagonal else 0))
    as_i32 = lambda v: jnp.asarray(np.asarray(v, np.int32))
    return as_i32(qi), as_i32(kj), as_i32(flags), len(qi)


def _flash(q, k, v, selb, *, batch, seq, mode, tq, tk):
    hq, T, _ = q.shape
    hkv = k.shape[0]
    kw = k.shape[-1]
    rep = hq // hkv
    npb = FLASH_NPROB
    assert tq & (tq - 1) == 0 and tq % FLASH_CHUNK == 0 and hkv % npb == 0
    qi, kj, flags, nsteps = _flash_schedule(seq, tq, tk, mode)
    nq, nk = seq // tq, seq // tk
    q_map = lambda g, b, s, qi, kj, fl: (g, b * nq + qi[s], 0)
    k_map = lambda g, b, s, qi, kj, fl: (g, b * nk + jnp.maximum(kj[s], 0), 0)
    in_specs = [pl.BlockSpec((npb * rep, tq, LANE), q_map),
                pl.BlockSpec((npb, tk, kw), k_map),
                pl.BlockSpec((npb, tk, LANE), k_map)]
    args = [q, k, v]
    rows = rep * tq
    scratch = []
    if mode == "select":
        in_specs.append(pl.BlockSpec((npb, tq, LANE), q_map))
        args.append(selb)
        scratch.append(pltpu.VMEM((npb, rows, kw), BF16))
    scratch += [pltpu.VMEM((npb, rows, tk), F32), pltpu.VMEM((npb, rows, tk), BF16),
                pltpu.VMEM((npb, rows, LANE), F32), pltpu.VMEM((npb, rows, LANE), F32),
                pltpu.VMEM((npb, rows, LANE), F32)]
    return pl.pallas_call(
        functools.partial(_flash_body, mode=mode, rep=rep, tq=tq, tk=tk, nprob=npb),
        grid_spec=pltpu.PrefetchScalarGridSpec(
            num_scalar_prefetch=3,
            grid=(hkv // npb, batch, nsteps),
            in_specs=in_specs,
            out_specs=pl.BlockSpec((npb * rep, tq, LANE), q_map),
            scratch_shapes=scratch),
        out_shape=jax.ShapeDtypeStruct((hq, T, LANE), BF16),
        compiler_params=_cparams("parallel", "parallel", "arbitrary"),
        name="flash_" + mode,
    )(qi, kj, flags, *args)


def _attn_out_body(*refs, n_branch, heads):
    o_refs = refs[:n_branch]
    if n_branch > 1:
        gl_ref, w_ref, x_ref, g_ref, out_ref = refs[n_branch:]
        gates = _sigmoid(gl_ref[...])
    else:
        w_ref, x_ref, g_ref, out_ref = refs[n_branch:]
    acc = jnp.zeros(x_ref.shape, F32)
    for h in range(heads):
        if n_branch > 1:
            o = jnp.zeros(o_refs[0].shape[1:], F32)
            for c in range(n_branch):
                col = n_branch * h + c
                o = o + gates[:, col:col + 1] * o_refs[c][h].astype(F32)
            o = o.astype(BF16)
        else:
            o = o_refs[0][h]
        acc = acc + jnp.dot(o, w_ref[h], preferred_element_type=F32)
    out_ref[...] = x_ref[...] + g_ref[0] * acc


def _attn_out(os, gl, w_heads, x2, g, seq):
    T, Dm = x2.shape
    heads = w_heads.shape[0]
    tm = 256
    nb = seq // tm
    row = lambda i: (i, 0)
    o_spec = pl.BlockSpec((heads, tm, LANE), lambda i: (0, i, 0))
    in_specs = [o_spec] * len(os)
    args = list(os)
    if len(os) > 1:
        in_specs.append(pl.BlockSpec((tm, LANE), row))
        args.append(gl)
    in_specs += [pl.BlockSpec((heads, LANE, Dm), lambda i: (0, 0, 0)),
                 pl.BlockSpec((tm, Dm), row),
                 pl.BlockSpec((1, 1, Dm), lambda i: (i // nb, 0, 0))]
    args += [w_heads, x2, g]
    return pl.pallas_call(
        functools.partial(_attn_out_body, n_branch=len(os), heads=heads),
        grid=(T // tm,),
        in_specs=in_specs,
        out_specs=pl.BlockSpec((tm, Dm), row),
        out_shape=jax.ShapeDtypeStruct((T, Dm), F32),
        compiler_params=_cparams("parallel"),
        name="attn_out_%d" % len(os),
    )(*args)


def _pad_heads_rows(w, heads, dh):
    w = w.reshape(heads, dh, -1)
    return jnp.pad(w, ((0, 0), (0, LANE - dh), (0, 0))).astype(BF16)


def _pad_heads_cols(w, heads, dh):
    k = w.shape[0]
    w = w.reshape(k, heads, dh)
    return jnp.pad(w, ((0, 0), (0, 0), (0, LANE - dh))).reshape(k, heads * LANE)


def _rope_tables(pos, rot, offset):
    half = rot // 2
    inv_freq = ROPE_THETA ** (-jnp.arange(0, rot, 2, dtype=F32) / rot)
    ang = pos.astype(F32)[:, None] * inv_freq[None, :]
    cos, sin = jnp.cos(ang), jnp.sin(ang)
    n = pos.shape[0]
    c = jnp.ones((n, LANE), F32).at[:, offset:offset + rot].set(jnp.concatenate([cos, cos], axis=1))
    sa = jnp.zeros((n, LANE), F32).at[:, offset:offset + half].set(-sin)
    sb = jnp.zeros((n, LANE), F32).at[:, offset + half:offset + rot].set(sin)
    return c, sa, sb


def _mla_proj_body(x_ref, a_ref, b_ref, win_ref, qlg_ref, kvlg_ref, wuq_ref, wuk_ref, wuv_ref,
                   qg_ref, kg_ref, c_ref, sa_ref, sb_ref, q_ref, k_ref, v_ref):
    h = _normmod(x_ref[...], a_ref[0], b_ref[0]).astype(BF16)
    r = jnp.dot(h, win_ref[...], preferred_element_type=F32)
    q_lat = r[:, :MLA_Q_LORA]
    kv_lat = r[:, MLA_Q_LORA:MLA_Q_LORA + MLA_KV_LORA]
    kpe = r[:, MLA_Q_LORA + MLA_KV_LORA:]
    ql = q_lat * lax.rsqrt(jnp.mean(q_lat * q_lat, axis=-1, keepdims=True) + EPS) * qlg_ref[...]
    kvl = kv_lat * lax.rsqrt(jnp.mean(kv_lat * kv_lat, axis=-1, keepdims=True) + EPS) * kvlg_ref[...]
    kvl = kvl.astype(BF16)
    q = jnp.dot(ql.astype(BF16), wuq_ref[...], preferred_element_type=F32)
    kn = jnp.dot(kvl, wuk_ref[...], preferred_element_type=F32)
    v = jnp.dot(kvl, wuv_ref[...], preferred_element_type=F32)
    c, sa, sb = c_ref[...], sa_ref[...], sb_ref[...]

    def norm_rot(t, gain):
        t = t * lax.rsqrt(jnp.sum(t * t, axis=-1, keepdims=True) * (1.0 / MLA_QK) + EPS) * gain
        return _rope(t, c, sa, sb, MLA_ROPE // 2)

    for hd in range(MLA_HEADS):
        sl = slice(hd * LANE, (hd + 1) * LANE)
        q_ref[hd] = norm_rot(q[:, sl], qg_ref[...]).astype(BF16)
        k_ref[hd] = norm_rot(kn[:, sl] + kpe, kg_ref[...]).astype(BF16)
        v_ref[hd] = (v[:, sl] + _one_lane()).astype(BF16)


def _mla_mixer(x2, a, b, g, w_in, q_lat_g, kv_lat_g, w_uq, w_ukv, q_gain, k_gain, w_out, batch, seq):
    T, Dm = x2.shape
    H = MLA_HEADS
    tm = 256
    nb = seq // tm
    scale = MLA_QK ** -0.5 * LOG2E
    lat = MLA_Q_LORA + MLA_KV_LORA
    kpe_cols = jnp.zeros((Dm, LANE), F32).at[:, MLA_NOPE:MLA_QK].set(w_in[:, lat:])
    win_p = jnp.concatenate([w_in[:, :lat], kpe_cols], axis=1).astype(BF16)
    wuq_p = _pad_heads_cols(w_uq, H, MLA_QK).astype(BF16)
    wukv = w_ukv.reshape(MLA_KV_LORA, H, MLA_NOPE + MLA_V)
    wuk_p = _pad_heads_cols(wukv[:, :, :MLA_NOPE].reshape(MLA_KV_LORA, -1), H, MLA_NOPE).astype(BF16)
    wuv_p = _pad_heads_cols(wukv[:, :, MLA_NOPE:].reshape(MLA_KV_LORA, -1), H, MLA_V).astype(BF16)
    pad_gain = lambda v: jnp.pad(v, (0, LANE - MLA_QK)).reshape(1, LANE)
    c, sa, sb = _rope_tables(jnp.arange(seq), MLA_ROPE, MLA_NOPE)
    row = lambda i: (i, 0)
    full = lambda i: (0, 0)
    per_b = lambda i: (i // nb, 0, 0)
    pos = lambda i: (i % nb, 0)
    head_out = pl.BlockSpec((H, tm, LANE), lambda i: (0, i, 0))
    hshape = jax.ShapeDtypeStruct((H, T, LANE), BF16)
    q, k, v = pl.pallas_call(
        _mla_proj_body,
        grid=(T // tm,),
        in_specs=[pl.BlockSpec((tm, Dm), row),
                  pl.BlockSpec((1, 1, Dm), per_b),
                  pl.BlockSpec((1, 1, Dm), per_b),
                  pl.BlockSpec(win_p.shape, full),
                  pl.BlockSpec((1, MLA_Q_LORA), full),
                  pl.BlockSpec((1, MLA_KV_LORA), full),
                  pl.BlockSpec(wuq_p.shape, full),
                  pl.BlockSpec(wuk_p.shape, full),
                  pl.BlockSpec(wuv_p.shape, full),
                  pl.BlockSpec((1, LANE), full),
                  pl.BlockSpec((1, LANE), full),
                  pl.BlockSpec((tm, LANE), pos),
                  pl.BlockSpec((tm, LANE), pos),
                  pl.BlockSpec((tm, LANE), pos)],
        out_specs=[head_out, head_out, head_out],
        out_shape=[hshape, hshape, hshape],
        compiler_params=_cparams("parallel"),
        name="mla_proj",
    )(x2, a, b, win_p, q_lat_g.reshape(1, -1), kv_lat_g.reshape(1, -1), wuq_p, wuk_p, wuv_p,
      pad_gain(q_gain * scale), pad_gain(k_gain), c, sa, sb)
    blk = min(1024, seq)
    o = _flash(q, k, v, None, batch=batch, seq=seq, mode="causal", tq=blk, tk=blk)
    return _attn_out([o], None, _pad_heads_rows(w_out, H, MLA_V), x2, g, seq)


N_KV_STREAMS = 6


def _nsa_proj_body(x_ref, a_ref, b_ref, w_ref, qg_ref, kg_ref, c_ref, sa_ref, sb_ref, blk_ref,
                   q_ref, kv_ref, ks_ref, gl_ref):
    h = _normmod(x_ref[...], a_ref[0], b_ref[0]).astype(BF16)
    r = jnp.dot(h, w_ref[...], preferred_element_type=F32)
    c, sa, sb = c_ref[...], sa_ref[...], sb_ref[...]

    def norm_rot(t, gain):
        t = t * lax.rsqrt(jnp.sum(t * t, axis=-1, keepdims=True) * (1.0 / NSA_DH) + EPS) * gain
        return _rope(t, c, sa, sb, NSA_ROT // 2)

    for hd in range(NSA_HEADS):
        q_ref[hd] = norm_rot(r[:, hd * LANE:(hd + 1) * LANE], qg_ref[...]).astype(BF16)
    base = NSA_HEADS * LANE
    for st in range(N_KV_STREAMS):
        for gi in range(NSA_GROUPS):
            off = base + (st * NSA_GROUPS + gi) * LANE
            t = r[:, off:off + LANE]
            if st == 2:
                t = norm_rot(t, kg_ref[1:2, :])
                ks_ref[gi] = jnp.concatenate([t.astype(BF16), blk_ref[...]], axis=1)
            elif st == 4:
                t = norm_rot(t, kg_ref[2:3, :])
            elif st in (3, 5):
                t = t + _one_lane()
            kv_ref[st * NSA_GROUPS + gi] = t.astype(BF16)
    gl_ref[...] = r[:, base + N_KV_STREAMS * NSA_GROUPS * LANE:]


def _nsa_compress_body(x_ref, pea_ref, peb_ref, w1a_ref, w1b_ref, w2_ref, kg_ref, c_ref, sa_ref, sb_ref,
                       o_ref, *, is_key, n_cmp):
    x = x_ref[0].astype(F32)
    xa = (x + pea_ref[...]).astype(BF16)
    xb = (x + peb_ref[...]).astype(BF16)
    za = jnp.dot(xa, w1a_ref[...], preferred_element_type=F32)
    zb = jnp.dot(xb, w1b_ref[...], preferred_element_type=F32)
    rows = za.shape[0]
    z = _silu(za + pltpu.roll(zb, rows - 1, 0))
    t = jnp.dot(z.astype(BF16), w2_ref[...], preferred_element_type=F32)
    if is_key:
        t = t * lax.rsqrt(jnp.sum(t * t, axis=-1, keepdims=True) * (1.0 / NSA_DH) + EPS) * kg_ref[...]
        t = _rope(t, c_ref[...], sa_ref[...], sb_ref[...], NSA_ROT // 2)
    valid = lax.broadcasted_iota(I32, t.shape, 0) < n_cmp
    o_ref[0] = jnp.where(valid, t, 0.0).astype(BF16)


def _nsa_cmp_select_body(q_ref, kc_ref, vc_ref, oc_ref, sel_ref, p_sc, *, tq, ncp, n_cmp, n_top):
    i = pl.program_id(2)
    t0 = i * tq
    kc = kc_ref[0]
    vc = vc_ref[0]
    rep = q_ref.shape[0]
    q = q_ref[...].reshape(rep * tq, LANE)
    sc = lax.dot_general(q, kc, _NT, preferred_element_type=F32)
    qpos = t0 + lax.broadcasted_iota(I32, (tq, ncp), 0)
    blk = lax.broadcasted_iota(I32, (tq, ncp), 1)
    mask = ((blk * CMP_STRIDE + (CMP_BLOCK - 1)) <= qpos) & (blk < n_cmp)
    sc = jnp.where(mask[None], sc.reshape(rep, tq, ncp), NEG)
    p = jnp.exp2(sc - jnp.max(sc, axis=-1, keepdims=True))
    p = jnp.where(mask[None], p / jnp.sum(p, axis=-1, keepdims=True), 0.0)
    oc = jnp.dot(p.reshape(rep * tq, ncp).astype(BF16), vc, preferred_element_type=F32)
    oc_ref[...] = oc.reshape(rep, tq, LANE).astype(BF16)
    qpos_t = t0 + lax.broadcasted_iota(I32, (ncp, tq), 1)
    blk_t = lax.broadcasted_iota(I32, (ncp, tq), 0)
    mask_t = ((blk_t * CMP_STRIDE + (CMP_BLOCK - 1)) <= qpos_t) & (blk_t < n_cmp)
    psum = jnp.zeros((ncp, tq), F32)
    for r in range(rep):
        st = lax.dot_general(kc, q_ref[r], _NT, preferred_element_type=F32)
        st = jnp.where(mask_t, st, NEG)
        pt = jnp.exp2(st - jnp.max(st, axis=0, keepdims=True))
        psum = psum + jnp.where(mask_t, pt / jnp.sum(pt, axis=0, keepdims=True), 0.0)
    p_sc[...] = jnp.zeros(p_sc.shape, F32)
    p_sc[8:8 + ncp, :] = psum
    per = SLC_BLOCK // CMP_STRIDE
    imp = p_sc[pl.ds(7, MAX_SLC, stride=per), :]
    for k in range(1, per + 1):
        imp = imp + p_sc[pl.ds(7 + k, MAX_SLC, stride=per), :]
    jb = lax.broadcasted_iota(I32, (MAX_SLC, tq), 0)
    qp = t0 + lax.broadcasted_iota(I32, (MAX_SLC, tq), 1)
    imp = jnp.where(jb * SLC_BLOCK <= qp, imp, -BIG)
    imp = jnp.where((jb == 0) | (jb == jnp.right_shift(qp, 6)), BIG, imp)
    jbf = jb.astype(F32)
    sel_t = jnp.full((MAX_SLC, tq), SEL_OFF, F32)
    for _ in range(n_top):
        top = jnp.max(imp, axis=0, keepdims=True)
        first = jnp.min(jnp.where(imp == top, jbf, float(MAX_SLC)), axis=0, keepdims=True)
        hit = jbf == first
        sel_t = jnp.where(hit, 0.0, sel_t)
        imp = jnp.where(hit, -jnp.inf, imp)
    sel_ref[0] = sel_t.T.astype(BF16)


def _nsa_mixer(x2, a, b, g, w_in, w_cmp1, w_cmp2, cmp_pos, q_gain, k_gain, w_out, batch, seq):
    T, Dm = x2.shape
    H, G, dh = NSA_HEADS, NSA_GROUPS, NSA_DH
    scale = dh ** -0.5 * LOG2E
    n_cmp = seq // CMP_STRIDE - 1
    ncp = seq // CMP_STRIDE
    n_slc = seq // SLC_BLOCK
    assert n_slc <= MAX_SLC and ncp <= MAX_SLC * (SLC_BLOCK // CMP_STRIDE)
    n_top = min(SLC_TOP_N, n_slc)
    tm = 256
    nb = seq // tm
    q_cols = _pad_heads_cols(w_in[:, :H * dh], H, dh)
    kv_cols = _pad_heads_cols(w_in[:, H * dh:H * dh + N_KV_STREAMS * G * dh], N_KV_STREAMS * G, dh)
    gl_cols = jnp.pad(w_in[:, H * dh + N_KV_STREAMS * G * dh:], ((0, 0), (0, LANE - 3 * H)))
    w_p = jnp.concatenate([q_cols, kv_cols, gl_cols], axis=1).astype(BF16)
    pad_gain = lambda v: jnp.pad(v, ((0, 0), (0, LANE - dh)))
    c, sa, sb = _rope_tables(jnp.arange(seq), NSA_ROT, 0)
    row = lambda i: (i, 0)
    full = lambda i: (0, 0)
    per_b = lambda i: (i // nb, 0, 0)
    pos = lambda i: (i % nb, 0)
    blk_onehot = (jnp.arange(seq)[:, None] // SLC_BLOCK == jnp.arange(MAX_SLC)[None, :]).astype(BF16)
    q, kv, ks, gl = pl.pallas_call(
        _nsa_proj_body,
        grid=(T // tm,),
        in_specs=[pl.BlockSpec((tm, Dm), row),
                  pl.BlockSpec((1, 1, Dm), per_b),
                  pl.BlockSpec((1, 1, Dm), per_b),
                  pl.BlockSpec(w_p.shape, full),
                  pl.BlockSpec((1, LANE), full),
                  pl.BlockSpec((3, LANE), full),
                  pl.BlockSpec((tm, LANE), pos),
                  pl.BlockSpec((tm, LANE), pos),
                  pl.BlockSpec((tm, LANE), pos),
                  pl.BlockSpec((tm, MAX_SLC), pos)],
        out_specs=[pl.BlockSpec((H, tm, LANE), lambda i: (0, i, 0)),
                   pl.BlockSpec((N_KV_STREAMS * G, tm, LANE), lambda i: (0, i, 0)),
                   pl.BlockSpec((G, tm, LANE + MAX_SLC), lambda i: (0, i, 0)),
                   pl.BlockSpec((tm, LANE), row)],
        out_shape=[jax.ShapeDtypeStruct((H, T, LANE), BF16),
                   jax.ShapeDtypeStruct((N_KV_STREAMS * G, T, LANE), BF16),
                   jax.ShapeDtypeStruct((G, T, LANE + MAX_SLC), BF16),
                   jax.ShapeDtypeStruct((T, LANE), F32)],
        compiler_params=_cparams("parallel"),
        name="nsa_proj",
    )(x2, a, b, w_p, pad_gain(q_gain.reshape(1, dh) * scale), pad_gain(k_gain), c, sa, sb, blk_onehot)
    kc_raw, vc_raw, _, vs, kw, vw = [kv[s * G:(s + 1) * G] for s in range(N_KV_STREAMS)]

    cmp_end = jnp.arange(ncp) * CMP_STRIDE + (CMP_BLOCK - 1)
    cc, csa, csb = _rope_tables(cmp_end, NSA_ROT, 0)
    kdim = CMP_STRIDE * LANE

    def compress(raw, w1, w2, pe, is_key):
        x16 = raw.reshape(G, T // CMP_STRIDE, kdim)
        w1p = jnp.pad(w1, ((0, 0), (0, LANE - dh), (0, LANE - dh)))
        w1a = w1p[:CMP_STRIDE].reshape(kdim, LANE).astype(BF16)
        w1b = w1p[CMP_STRIDE:].reshape(kdim, LANE).astype(BF16)
        pep = jnp.pad(pe, ((0, 0), (0, LANE - dh)))
        pea = pep[:CMP_STRIDE].reshape(1, kdim)
        peb = pep[CMP_STRIDE:].reshape(1, kdim)
        w2p = jnp.pad(w2, ((0, LANE - dh), (0, LANE - dh))).astype(BF16)
        const = lambda gi, bi: (0, 0)
        return pl.pallas_call(
            functools.partial(_nsa_compress_body, is_key=is_key, n_cmp=n_cmp),
            grid=(G, batch),
            in_specs=[pl.BlockSpec((1, ncp, kdim), lambda gi, bi: (gi, bi, 0)),
                      pl.BlockSpec((1, kdim), const),
                      pl.BlockSpec((1, kdim), const),
                      pl.BlockSpec((kdim, LANE), const),
                      pl.BlockSpec((kdim, LANE), const),
                      pl.BlockSpec((LANE, LANE), const),
                      pl.BlockSpec((1, LANE), const),
                      pl.BlockSpec((ncp, LANE), const),
                      pl.BlockSpec((ncp, LANE), const),
                      pl.BlockSpec((ncp, LANE), const)],
            out_specs=pl.BlockSpec((1, ncp, LANE), lambda gi, bi: (gi, bi, 0)),
            out_shape=jax.ShapeDtypeStruct((G, batch * ncp, LANE), BF16),
            compiler_params=_cparams("parallel", "parallel"),
            name="nsa_compress_" + ("k" if is_key else "v"),
        )(x16, pea, peb, w1a, w1b, w2p, pad_gain(k_gain)[0:1], cc, csa, csb)

    kc = compress(kc_raw, w_cmp1[0], w_cmp2[0], cmp_pos[0], True)
    vc = compress(vc_raw, w_cmp1[1], w_cmp2[1], cmp_pos[1], False)

    tq = 128
    nq = seq // tq
    q_map = lambda gi, bi, i: (gi, bi * nq + i, 0)
    c_map = lambda gi, bi, i: (gi, bi, 0)
    o_c, sel = pl.pallas_call(
        functools.partial(_nsa_cmp_select_body, tq=tq, ncp=ncp, n_cmp=n_cmp, n_top=n_top),
        grid=(G, batch, nq),
        in_specs=[pl.BlockSpec((NSA_REP, tq, LANE), q_map),
                  pl.BlockSpec((1, ncp, LANE), c_map),
                  pl.BlockSpec((1, ncp, LANE), c_map)],
        out_specs=[pl.BlockSpec((NSA_REP, tq, LANE), q_map),
                   pl.BlockSpec((1, tq, LANE), q_map)],
        out_shape=[jax.ShapeDtypeStruct((H, T, LANE), BF16),
                   jax.ShapeDtypeStruct((G, T, LANE), BF16)],
        scratch_shapes=[pltpu.VMEM((8 + MAX_SLC * (SLC_BLOCK // CMP_STRIDE), tq), F32)],
        compiler_params=_cparams("parallel", "parallel", "parallel"),
        name="nsa_cmp_select",
    )(q, kc, vc)

    o_s = _flash(q, ks, vs, sel, batch=batch, seq=seq, mode="select", tq=256, tk=min(1024, seq))
    o_w = _flash(q, kw, vw, None, batch=batch, seq=seq, mode="window", tq=256, tk=WINDOW)
    return _attn_out([o_c, o_s, o_w], gl, _pad_heads_rows(w_out, H, dh), x2, g, seq)


def _router_body(x_ref, a_ref, b_ref, rwt_ref, rb_ref, tri_ref, e_ref, w_ref, rank_ref, cnt_ref,
                 carry_sc, *, tm):
    i = pl.program_id(0)

    @pl.when(i == 0)
    def _():
        carry_sc[...] = jnp.zeros(carry_sc.shape, F32)

    h = _normmod(x_ref[...], a_ref[0], b_ref[0])
    logits = lax.dot_general(rwt_ref[...], h, _NT, preferred_element_type=F32, precision=HIGHEST)
    scores = _sigmoid(logits)
    biased = scores + rb_ref[...]
    ng, per = N_GROUPS, EXPERTS_PER_GROUP
    row = lambda arr, r: arr[r:r + 1, :]
    gsel = jnp.zeros((1, tm), I32)
    best = None
    for gi in range(ng):
        v = [row(biased, gi * per + k) for k in range(per)]
        top2 = None
        for p in range(per):
            for q in range(p + 1, per):
                s = v[p] + v[q]
                top2 = s if top2 is None else jnp.maximum(top2, s)
        if best is None:
            best = top2
        else:
            better = top2 > best
            gsel = jnp.where(better, gi, gsel)
            best = jnp.where(better, top2, best)
    cb, cs = [], []
    for k in range(per):
        b_k = row(biased, k)
        s_k = row(scores, k)
        for gi in range(1, ng):
            hit = gsel == gi
            b_k = jnp.where(hit, row(biased, gi * per + k), b_k)
            s_k = jnp.where(hit, row(scores, gi * per + k), s_k)
        cb.append(b_k)
        cs.append(s_k)

    def argmax_first(vals):
        idx = jnp.zeros((1, tm), I32)
        top = vals[0]
        for k in range(1, per):
            better = vals[k] > top
            idx = jnp.where(better, k, idx)
            top = jnp.where(better, vals[k], top)
        return idx

    def pick(vals, idx):
        out = vals[0]
        for k in range(1, per):
            out = jnp.where(idx == k, vals[k], out)
        return out

    i1 = argmax_first(cb)
    i2 = argmax_first([jnp.where(i1 == k, -jnp.inf, cb[k]) for k in range(per)])
    w1 = pick(cs, i1)
    w2 = pick(cs, i2)
    tot = w1 + w2
    e1 = gsel * per + i1
    e2 = gsel * per + i2
    eid = lax.broadcasted_iota(I32, (N_EXPERTS, tm), 0)
    hot = (eid == e1) | (eid == e2)
    onehot = jnp.where(hot, 1.0, 0.0)
    before = carry_sc[...] + jnp.dot(onehot.astype(BF16), tri_ref[...], preferred_element_type=F32)
    r1 = jnp.sum(jnp.where(eid == e1, before, 0.0), axis=0, keepdims=True)
    r2 = jnp.sum(jnp.where(eid == e2, before, 0.0), axis=0, keepdims=True)
    carry = carry_sc[...] + jnp.sum(onehot, axis=1, keepdims=True)
    carry_sc[...] = carry
    cnt_ref[...] = jnp.broadcast_to(carry, cnt_ref.shape)
    zi = jnp.zeros((6, tm), I32)
    e_ref[...] = jnp.concatenate([e1, e2, zi], axis=0)
    rank_ref[...] = jnp.concatenate([r1.astype(I32), r2.astype(I32), zi], axis=0)
    w_ref[...] = jnp.concatenate([w1 / tot, w2 / tot, jnp.zeros((6, tm), F32)], axis=0)


ROW_SUB = D_MODEL // LANE


def _row_tile(r):
    return pl.ds(pl.multiple_of(r * ROW_SUB, ROW_SUB), ROW_SUB)


def _to_row_tiles(ref, val):
    n = val.shape[0]
    for s in range(ROW_SUB):
        ref[pl.ds(s, n, stride=ROW_SUB), :] = val[:, s * LANE:(s + 1) * LANE]


def _from_row_tiles(ref, n, s):
    return ref[pl.ds(s, n, stride=ROW_SUB), :]


def _moe_dispatch_body(starts_ref, counts_ref, padded_ref, nv_ref, e0_ref, e1_ref, r0_ref, r1_ref,
                       x_ref, a_ref, b_ref, xs_hbm, d0_ref, d1_ref, hbuf0, hbuf1, zbuf, sems, zsem,
                       *, tm, tg, n_tiles, nt):
    i = pl.program_id(0)

    @pl.when(i == 0)
    def _():
        zbuf[...] = jnp.zeros(zbuf.shape, F32)
        zrow = zbuf.at[pl.ds(0, ROW_SUB)]
        for ex in range(N_EXPERTS):
            lo = starts_ref[ex] + counts_ref[ex]
            hi = starts_ref[ex] + padded_ref[ex]

            def fill(r, carry):
                pltpu.make_async_copy(zrow, xs_hbm.at[_row_tile(r)], zsem).start()
                return carry

            lax.fori_loop(lo, hi, fill, 0)

            def drain_fill(r, carry):
                pltpu.make_async_copy(zrow, xs_hbm.at[_row_tile(0)], zsem).wait()
                return carry

            lax.fori_loop(lo, hi, drain_fill, 0)

        rows_per_tile = tg * ROW_SUB

        def fill_tile(t, carry):
            dst = xs_hbm.at[pl.ds(pl.multiple_of(t * rows_per_tile, rows_per_tile), rows_per_tile)]
            pltpu.make_async_copy(zbuf, dst, zsem).start()
            return carry

        lax.fori_loop(nv_ref[0], n_tiles, fill_tile, 0)

        def drain_tile(t, carry):
            pltpu.make_async_copy(zbuf, xs_hbm.at[pl.ds(0, rows_per_tile)], zsem).wait()
            return carry

        lax.fori_loop(nv_ref[0], n_tiles, drain_tile, 0)

    h = _normmod(x_ref[...], a_ref[0], b_ref[0])

    def scatter_from(hbuf, sem, other_buf, other_sem):
        _to_row_tiles(hbuf, h)

        def row_copy(buf, sm, r, slot):
            return pltpu.make_async_copy(buf.at[_row_tile(r)], xs_hbm.at[_row_tile(slot)], sm)

        def issue(r, carry):
            s0 = starts_ref[e0_ref[r]] + r0_ref[r]
            s1 = starts_ref[e1_ref[r]] + r1_ref[r]
            d0_ref[r] = s0
            d1_ref[r] = s1
            row_copy(hbuf, sem, r, s0).start()
            row_copy(hbuf, sem, r, s1).start()
            return carry

        lax.fori_loop(0, tm, issue, 0, unroll=8)

        def drain(buf, sm):
            def body(r, carry):
                row_copy(buf, sm, r, 0).wait()
                row_copy(buf, sm, r, 0).wait()
                return carry

            lax.fori_loop(0, tm, body, 0, unroll=8)

        pl.when(i > 0)(lambda: drain(other_buf, other_sem))
        pl.when(i == nt - 1)(lambda: drain(hbuf, sem))

    pl.when(i % 2 == 0)(lambda: scatter_from(hbuf0, sems.at[0], hbuf1, sems.at[1]))
    pl.when(i % 2 == 1)(lambda: scatter_from(hbuf1, sems.at[1], hbuf0, sems.at[0]))


def _moe_expert_body(te_ref, nv_ref, xs_ref, win_ref, wout_ref, y_ref, *, tg):
    i = pl.program_id(0)

    @pl.when(i < nv_ref[0])
    def _():
        x = jnp.concatenate([_from_row_tiles(xs_ref, tg, s) for s in range(ROW_SUB)], axis=1).astype(BF16)
        gu = jnp.dot(x, win_ref[0], preferred_element_type=F32)
        act = (_silu(gu[:, :D_EXPERT]) * gu[:, D_EXPERT:]).astype(BF16)
        _to_row_tiles(y_ref, jnp.dot(act, wout_ref[0], preferred_element_type=F32))

    @pl.when(i >= nv_ref[0])
    def _():
        y_ref[...] = jnp.zeros(y_ref.shape, F32)


def _moe_combine_body(d0c_ref, d1c_ref, d0n_ref, d1n_ref, y_hbm, x_ref, w_ref, g_ref, o_ref,
                      ya0, ya1, yb0, yb1, sems, *, tm, nt):
    i = pl.program_id(0)

    def row_copy(slot, buf, sem, r):
        return pltpu.make_async_copy(y_hbm.at[_row_tile(slot)], buf.at[_row_tile(r)], sem)

    def issue(d0_ref, d1_ref, bufs, sem):
        def body(r, carry):
            row_copy(d0_ref[r], bufs[0], sem, r).start()
            row_copy(d1_ref[r], bufs[1], sem, r).start()
            return carry

        lax.fori_loop(0, tm, body, 0, unroll=8)

    def finish(bufs, sem):
        def body(r, carry):
            row_copy(0, bufs[0], sem, r).wait()
            row_copy(0, bufs[1], sem, r).wait()
            return carry

        lax.fori_loop(0, tm, body, 0, unroll=8)
        w = w_ref[...]
        w0, w1 = w[:, 0:1], w[:, 1:2]
        gate = g_ref[0]
        for s in range(ROW_SUB):
            sl = slice(s * LANE, (s + 1) * LANE)
            y = w0 * _from_row_tiles(bufs[0], tm, s) + w1 * _from_row_tiles(bufs[1], tm, s)
            o_ref[:, sl] = x_ref[:, sl] + gate[:, sl] * y

    set_a, set_b = (ya0, ya1), (yb0, yb1)
    pl.when(i == 0)(lambda: issue(d0c_ref, d1c_ref, set_a, sems.at[0]))

    def even():
        pl.when(i + 1 < nt)(lambda: issue(d0n_ref, d1n_ref, set_b, sems.at[1]))
        finish(set_a, sems.at[0])

    def odd():
        pl.when(i + 1 < nt)(lambda: issue(d0n_ref, d1n_ref, set_a, sems.at[0]))
        finish(set_b, sems.at[1])

    pl.when(i % 2 == 0)(even)
    pl.when(i % 2 == 1)(odd)


def _grouped_moe(x2, a, b, g, router_w, router_bias, w_in, w_out, seq):
    T, Dm = x2.shape
    E = N_EXPERTS
    tm = 512
    nb = seq // tm
    nt = T // tm
    row = lambda i: (i, 0)
    full = lambda i: (0, 0)
    per_b = lambda i: (i // nb, 0, 0)
    lanes = lambda i: (0, i)
    tri = jnp.asarray(np.triu(np.ones((tm, tm), np.float32), 1)).astype(BF16)
    e, w, rank, cnt = pl.pallas_call(
        functools.partial(_router_body, tm=tm),
        grid=(nt,),
        in_specs=[pl.BlockSpec((tm, Dm), row),
                  pl.BlockSpec((1, 1, Dm), per_b),
                  pl.BlockSpec((1, 1, Dm), per_b),
                  pl.BlockSpec((E, Dm), full),
                  pl.BlockSpec((E, 1), full),
                  pl.BlockSpec((tm, tm), full)],
        out_specs=[pl.BlockSpec((8, tm), lanes),
                   pl.BlockSpec((8, tm), lanes),
                   pl.BlockSpec((8, tm), lanes),
                   pl.BlockSpec((E, LANE), full)],
        out_shape=[jax.ShapeDtypeStruct((8, T), I32),
                   jax.ShapeDtypeStruct((8, T), F32),
                   jax.ShapeDtypeStruct((8, T), I32),
                   jax.ShapeDtypeStruct((E, LANE), F32)],
        scratch_shapes=[pltpu.VMEM((E, 1), F32)],
        compiler_params=_cparams("arbitrary"),
        name="moe_router",
    )(x2, a, b, router_w.T, router_bias.reshape(E, 1), tri)

    tg = MOE_TILE
    n_tiles = (2 * T) // tg + E
    n_slots = n_tiles * tg
    counts = cnt[:, 0].astype(I32)
    padded = ((counts + tg - 1) // tg) * tg
    ends = jnp.cumsum(padded)
    starts = ends - padded
    tile_start = jnp.arange(n_tiles, dtype=I32) * tg
    tile_expert = jnp.minimum(jnp.sum(tile_start[:, None] >= ends[None, :], axis=1), E - 1).astype(I32)
    n_valid = (ends[-1] // tg).astype(I32).reshape(1)

    tc = 256
    ntc = T // tc
    nbc = seq // tc
    smem_cur = pl.BlockSpec((tc,), lambda i, *_: (i,), memory_space=pltpu.SMEM)
    smem_nxt = pl.BlockSpec((tc,), lambda i, *_: (jnp.minimum(i + 1, ntc - 1),), memory_space=pltpu.SMEM)
    tile_rows = tc * ROW_SUB
    xs, d0, d1 = pl.pallas_call(
        functools.partial(_moe_dispatch_body, tm=tc, tg=tg, n_tiles=n_tiles, nt=ntc),
        grid_spec=pltpu.PrefetchScalarGridSpec(
            num_scalar_prefetch=4,
            grid=(ntc,),
            in_specs=[smem_cur, smem_cur, smem_cur, smem_cur,
                      pl.BlockSpec((tc, Dm), lambda i, *_: (i, 0)),
                      pl.BlockSpec((1, 1, Dm), lambda i, *_: (i // nbc, 0, 0)),
                      pl.BlockSpec((1, 1, Dm), lambda i, *_: (i // nbc, 0, 0))],
            out_specs=[pl.BlockSpec(memory_space=pl.ANY), smem_cur, smem_cur],
            scratch_shapes=[pltpu.VMEM((tile_rows, LANE), F32), pltpu.VMEM((tile_rows, LANE), F32),
                            pltpu.VMEM((tg * ROW_SUB, LANE), F32),
                            pltpu.SemaphoreType.DMA((2,)), pltpu.SemaphoreType.DMA(())]),
        out_shape=[jax.ShapeDtypeStruct((n_slots * ROW_SUB, LANE), F32),
                   jax.ShapeDtypeStruct((T,), I32), jax.ShapeDtypeStruct((T,), I32)],
        compiler_params=_cparams("arbitrary"),
        name="moe_dispatch",
    )(starts.astype(I32), counts, padded, n_valid, e[0], e[1], rank[0], rank[1], x2, a, b)

    last_tile = lambda i, te, nv: (jnp.minimum(i, nv[0] - 1), 0)
    y = pl.pallas_call(
        functools.partial(_moe_expert_body, tg=tg),
        grid_spec=pltpu.PrefetchScalarGridSpec(
            num_scalar_prefetch=2,
            grid=(n_tiles,),
            in_specs=[pl.BlockSpec((tg * ROW_SUB, LANE), last_tile),
                      pl.BlockSpec((1, Dm, 2 * D_EXPERT), lambda i, te, nv: (te[i], 0, 0)),
                      pl.BlockSpec((1, D_EXPERT, Dm), lambda i, te, nv: (te[i], 0, 0))],
            out_specs=pl.BlockSpec((tg * ROW_SUB, LANE), lambda i, te, nv: (i, 0))),
        out_shape=jax.ShapeDtypeStruct((n_slots * ROW_SUB, LANE), F32),
        compiler_params=_cparams("arbitrary"),
        name="moe_experts",
    )(tile_expert, n_valid, xs, w_in.astype(BF16), w_out.astype(BF16))

    return pl.pallas_call(
        functools.partial(_moe_combine_body, tm=tc, nt=ntc),
        grid=(ntc,),
        in_specs=[smem_cur, smem_cur, smem_nxt, smem_nxt,
                  pl.BlockSpec(memory_space=pl.ANY),
                  pl.BlockSpec((tc, Dm), row),
                  pl.BlockSpec((tc, 2), row),
                  pl.BlockSpec((1, 1, Dm), lambda i: (i // nbc, 0, 0))],
        out_specs=pl.BlockSpec((tc, Dm), row),
        out_shape=jax.ShapeDtypeStruct((T, Dm), F32),
        scratch_shapes=[pltpu.VMEM((tile_rows, LANE), F32)] * 4 + [pltpu.SemaphoreType.DMA((2,))],
        compiler_params=_cparams("arbitrary"),
        name="moe_combine",
    )(d0, d1, d0, d1, y, x2, w[:2].T, g)


def kernel(x, c, norm_mix_g, norm_ffn_g, w_ada, b_ada, conv_w_pw1, conv_b_pw1, conv_w_dw, conv_b_dw, conv_ln_g, conv_ln_b, conv_w_pw2, conv_b_pw2, nsa_w_in, nsa_w_cmp1, nsa_w_cmp2, nsa_cmp_pos, nsa_q_gain, nsa_k_gain, nsa_w_out, mla_w_in, mla_q_lat_g, mla_kv_lat_g, mla_w_uq, mla_w_ukv, mla_q_gain, mla_k_gain, mla_w_out, router_w, router_bias, moe_w_in, moe_w_out):
    B, S, Dm = x.shape
    depth = w_ada.shape[0]
    mods = _ada(c, w_ada, b_ada)
    x2 = x.reshape(B * S, Dm)
    for i in range(depth):
        sh1, sc1, g1, sh2, sc2, g2 = [m.reshape(B, 1, Dm) for m in jnp.split(mods[i], 6, axis=-1)]
        a1 = norm_mix_g[i] * (1.0 + sc1)
        kind, j = i % 3, i // 3
        if kind == 0:
            x2 = _conv_mixer(x2, a1, sh1, g1, conv_w_pw1[j], conv_b_pw1[j], conv_w_dw[j], conv_b_dw[j],
                             conv_ln_g[j], conv_ln_b[j], conv_w_pw2[j], conv_b_pw2[j], S)
        elif kind == 1:
            x2 = _nsa_mixer(x2, a1, sh1, g1, nsa_w_in[j], nsa_w_cmp1[j], nsa_w_cmp2[j], nsa_cmp_pos[j],
                            nsa_q_gain[j], nsa_k_gain[j], nsa_w_out[j], B, S)
        else:
            x2 = _mla_mixer(x2, a1, sh1, g1, mla_w_in[j], mla_q_lat_g[j], mla_kv_lat_g[j], mla_w_uq[j],
                            mla_w_ukv[j], mla_q_gain[j], mla_k_gain[j], mla_w_out[j], B, S)
        a2 = norm_ffn_g[i] * (1.0 + sc2)
        x2 = _grouped_moe(x2, a2, sh2, g2, router_w, router_bias, moe_w_in[i], moe_w_out[i], S)
    return x2.reshape(B, S, Dm)
```

```python
import functools
import math

import numpy as np
import jax
import jax.numpy as jnp
from jax import lax
from jax.experimental import pallas as pl
from jax.experimental.pallas import tpu as pltpu

F32 = jnp.float32
BF16 = jnp.bfloat16
I32 = jnp.int32
HIGHEST = lax.Precision.HIGHEST

EPS = 1e-6
NEG = -1e30
BIG = 1e30
ROPE_THETA = 500000.0
LANE = 128
VMEM_LIMIT = 56 * 1024 * 1024

D_MODEL = 1024
CONV_WIDTH = 31
CONV_HALO = 32

NSA_HEADS = 16
NSA_GROUPS = 4
NSA_REP = NSA_HEADS // NSA_GROUPS
NSA_DH = 64
NSA_ROT = 16
CMP_BLOCK = 32
CMP_STRIDE = 16
SLC_BLOCK = 64
SLC_TOP_N = 16
WINDOW = 512
MAX_SLC = 128

MLA_HEADS = 16
MLA_Q_LORA = 384
MLA_KV_LORA = 256
MLA_NOPE = 64
MLA_ROPE = 32
MLA_V = 64
MLA_QK = MLA_NOPE + MLA_ROPE

N_EXPERTS = 16
N_GROUPS = 4
EXPERTS_PER_GROUP = 4
D_EXPERT = 512
MOE_TILE = 512

_NT = (((1,), (1,)), ((), ()))


def _cparams(*sem):
    return pltpu.CompilerParams(dimension_semantics=sem, vmem_limit_bytes=VMEM_LIMIT)


def _sigmoid(x):
    return 1.0 / (1.0 + jnp.exp(-x))


def _silu(x):
    return x * _sigmoid(x)


def _normmod(x, a, b):
    ms = jnp.mean(x * x, axis=-1, keepdims=True)
    return x * lax.rsqrt(ms + EPS) * a + b


def _rope(x, c, sa, sb, half):
    n = x.shape[-1]
    return x * c + pltpu.roll(x, n - half, 1) * sa + pltpu.roll(x, half, 1) * sb


def _ada_body(c_ref, w_ref, b_ref, o_ref):
    c = c_ref[...]
    o_ref[0] = jnp.dot(_silu(c), w_ref[0], preferred_element_type=F32, precision=HIGHEST) + b_ref[0]


def _ada(c, w_ada, b_ada):
    B = c.shape[0]
    L, Dm, N = w_ada.shape
    Bp = -(-B // 8) * 8
    cp = jnp.pad(c, ((0, Bp - B), (0, 0)))
    tn = 1536
    out = pl.pallas_call(
        _ada_body,
        grid=(L, N // tn),
        in_specs=[pl.BlockSpec((Bp, Dm), lambda l, j: (0, 0)),
                  pl.BlockSpec((1, Dm, tn), lambda l, j: (l, 0, j)),
                  pl.BlockSpec((1, 1, tn), lambda l, j: (l, 0, j))],
        out_specs=pl.BlockSpec((1, Bp, tn), lambda l, j: (l, 0, j)),
        out_shape=jax.ShapeDtypeStruct((L, Bp, N), F32),
        compiler_params=_cparams("arbitrary", "arbitrary"),
        name="adaln",
    )(cp, w_ada, b_ada.reshape(L, 1, N))
    return out[:, :B]


def _conv_pw1_body(x_ref, a_ref, b_ref, w_ref, bias_ref, u_ref):
    h = _normmod(x_ref[...], a_ref[0], b_ref[0]).astype(BF16)
    r = jnp.dot(h, w_ref[...], preferred_element_type=F32) + bias_ref[...]
    d = u_ref.shape[-1]
    u_ref[...] = r[:, :d] * _sigmoid(r[:, d:])


def _conv_dw_body(u_ref, uh_ref, wdw_ref, bdw_ref, lng_ref, lnb_ref, w2_ref, b2_ref, x_ref, g_ref,
                  o_ref, ext_ref, *, tm, seq):
    i = pl.program_id(0)
    at_seq_start = (i * tm) % seq == 0
    ext_ref[0:CONV_HALO, :] = jnp.where(at_seq_start, 0.0, uh_ref[...])
    ext_ref[CONV_HALO:, :] = u_ref[...]
    base = CONV_HALO - (CONV_WIDTH - 1)
    acc = jnp.zeros((tm, u_ref.shape[-1]), F32) + bdw_ref[...]
    for k in range(CONV_WIDTH):
        acc = acc + wdw_ref[k:k + 1, :] * ext_ref[base + k:base + k + tm, :]
    mu = jnp.mean(acc, axis=-1, keepdims=True)
    dlt = acc - mu
    var = jnp.mean(dlt * dlt, axis=-1, keepdims=True)
    y = dlt * lax.rsqrt(var + EPS) * lng_ref[...] + lnb_ref[...]
    z = _silu(y).astype(BF16)
    out = jnp.dot(z, w2_ref[...], preferred_element_type=F32) + b2_ref[...]
    o_ref[...] = x_ref[...] + g_ref[0] * out


def _conv_mixer(x2, a, b, g, w_pw1, b_pw1, w_dw, b_dw, ln_g, ln_b, w_pw2, b_pw2, seq):
    T, Dm = x2.shape
    tm = 512
    nb = seq // tm
    row = lambda i: (i, 0)
    per_b = lambda i: (i // nb, 0, 0)
    full = lambda i: (0, 0)
    u = pl.pallas_call(
        _conv_pw1_body,
        grid=(T // tm,),
        in_specs=[pl.BlockSpec((tm, Dm), row),
                  pl.BlockSpec((1, 1, Dm), per_b),
                  pl.BlockSpec((1, 1, Dm), per_b),
                  pl.BlockSpec((Dm, 2 * Dm), full),
                  pl.BlockSpec((1, 2 * Dm), full)],
        out_specs=pl.BlockSpec((tm, Dm), row),
        out_shape=jax.ShapeDtypeStruct((T, Dm), F32),
        compiler_params=_cparams("parallel"),
        name="conv_pw1_glu",
    )(x2, a, b, w_pw1.astype(BF16), b_pw1.reshape(1, -1))
    hb = tm // CONV_HALO
    wdw = jnp.pad(w_dw, ((0, CONV_HALO - CONV_WIDTH), (0, 0)))
    vec = lambda v: v.reshape(1, -1)
    return pl.pallas_call(
        functools.partial(_conv_dw_body, tm=tm, seq=seq),
        grid=(T // tm,),
        in_specs=[pl.BlockSpec((tm, Dm), row),
                  pl.BlockSpec((CONV_HALO, Dm), lambda i: (jnp.maximum(i * hb - 1, 0), 0)),
                  pl.BlockSpec((CONV_HALO, Dm), full),
                  pl.BlockSpec((1, Dm), full),
                  pl.BlockSpec((1, Dm), full),
                  pl.BlockSpec((1, Dm), full),
                  pl.BlockSpec((Dm, Dm), full),
                  pl.BlockSpec((1, Dm), full),
                  pl.BlockSpec((tm, Dm), row),
                  pl.BlockSpec((1, 1, Dm), per_b)],
        out_specs=pl.BlockSpec((tm, Dm), row),
        out_shape=jax.ShapeDtypeStruct((T, Dm), F32),
        scratch_shapes=[pltpu.VMEM((tm + CONV_HALO, Dm), F32)],
        compiler_params=_cparams("parallel"),
        name="conv_dw_ln_pw2",
    )(u, u, wdw, vec(b_dw), vec(ln_g), vec(ln_b), w_pw2.astype(BF16), vec(b_pw2), x2, g)


FLASH_FIRST, FLASH_LAST, FLASH_MASKED = 1, 2, 4
FLASH_CHUNK = 16
SEL_OFF = -(2.0 ** 100)
ONE_LANE = 64
LOG2E = 1.4426950408889634
FLASH_NPROB = 2


def _one_lane():
    return jnp.where(lax.broadcasted_iota(I32, (1, LANE), 1) == ONE_LANE, 1.0, 0.0)


def _flash_body(qi_ref, kj_ref, flag_ref, *refs, mode, rep, tq, tk, nprob):
    if mode == "select":
        q_ref, k_ref, v_ref, selb_ref, o_ref, qa_sc, s_sc, p_sc, mb_sc, al_sc, acc_sc = refs
    else:
        q_ref, k_ref, v_ref, o_ref, s_sc, p_sc, mb_sc, al_sc, acc_sc = refs
    step = pl.program_id(2)
    qi = qi_ref[step]
    kj = kj_ref[step]
    flag = flag_ref[step]
    rows = rep * tq
    ch = FLASH_CHUNK
    nl = tk // LANE
    lane_fold = lambda t, op: functools.reduce(op, [t[:, i * LANE:(i + 1) * LANE] for i in range(nl)])

    @pl.when((flag & FLASH_FIRST) != 0)
    def _():
        mb_sc[...] = jnp.full(mb_sc.shape, NEG, F32)
        acc_sc[...] = jnp.zeros(acc_sc.shape, F32)
        if mode == "select":
            for pr in range(nprob):
                qa_sc[pr, :, :LANE] = q_ref[pr * rep:(pr + 1) * rep].reshape(rows, LANE)
                qa_sc[pr, :, LANE:] = jnp.concatenate([selb_ref[pr]] * rep, axis=0)

    def process(masked):
        thr = qi * tq - kj * tk
        for pr in range(nprob):
            q = qa_sc[pr] if mode == "select" else q_ref[pr * rep:(pr + 1) * rep].reshape(rows, LANE)
            s_sc[pr] = lax.dot_general(q, k_ref[pr], _NT, preferred_element_type=F32)
        if masked:
            diff = lax.broadcasted_iota(I32, (ch, tk), 1) - lax.broadcasted_iota(I32, (ch, tk), 0)
        for pr in range(nprob):
            for r0 in range(0, rows, ch):
                sc = s_sc[pr, r0:r0 + ch, :]
                if masked:
                    lim = thr + (r0 % tq)
                    ok = diff <= lim
                    if mode == "window":
                        ok = ok & (diff > lim - WINDOW)
                    sc = jnp.where(ok, sc, NEG)
                    s_sc[pr, r0:r0 + ch, :] = sc
                m_prev = mb_sc[pr, r0:r0 + ch, :]
                m_new = jnp.maximum(m_prev, jnp.max(lane_fold(sc, jnp.maximum), axis=1, keepdims=True))
                al_sc[pr, r0:r0 + ch, :] = jnp.exp2(m_prev - m_new)
                mb_sc[pr, r0:r0 + ch, :] = m_new
            for r0 in range(0, rows, ch):
                mb = mb_sc[pr, r0:r0 + ch, :]
                p = jnp.exp2(s_sc[pr, r0:r0 + ch, :] - jnp.concatenate([mb] * nl, axis=1))
                p_sc[pr, r0:r0 + ch, :] = p.astype(BF16)
            acc_sc[pr] = al_sc[pr] * acc_sc[pr] + jnp.dot(p_sc[pr], v_ref[pr], preferred_element_type=F32)

    if mode == "window":
        pl.when(kj >= 0)(lambda: process(True))
    else:
        pl.when((kj >= 0) & ((flag & FLASH_MASKED) != 0))(lambda: process(True))
        pl.when((kj >= 0) & ((flag & FLASH_MASKED) == 0))(lambda: process(False))

    @pl.when((flag & FLASH_LAST) != 0)
    def _():
        for pr in range(nprob):
            acc = acc_sc[pr]
            o = acc / acc[:, ONE_LANE:ONE_LANE + 1]
            o_ref[pr * rep:(pr + 1) * rep] = o.reshape(rep, tq, LANE).astype(o_ref.dtype)


def _flash_schedule(seq, tq, tk, mode):
    qi, kj, flags = [], [], []
    for i in range(seq // tq):
        hi = (i * tq + tq - 1) // tk
        lo = 0 if mode != "window" else (i * tq - (WINDOW - 1)) // tk
        js = list(range(lo, hi + 1))
        for n, j in enumerate(js):
            crosses_diagonal = (j + 1) * tk - 1 > i * tq
            qi.append(i)
            kj.append(j if j >= 0 else -1)
            flags.append((FLASH_FIRST if n == 0 else 0) | (FLASH_LAST if n == len(js) - 1 else 0)
                         | (FLASH_MASKED if crosses_diagonal else 0))
    as_i32 = lambda v: jnp.asarray(np.asarray(v, np.int32))
    return as_i32(qi), as_i32(kj), as_i32(flags), len(qi)


def _flash(q, k, v, selb, *, batch, seq, mode, tq, tk):
    hq, T, _ = q.shape
    hkv = k.shape[0]
    kw = k.shape[-1]
    rep = hq // hkv
    npb = FLASH_NPROB
    assert tq & (tq - 1) == 0 and tq % FLASH_CHUNK == 0 and hkv % npb == 0
    qi, kj, flags, nsteps = _flash_schedule(seq, tq, tk, mode)
    nq, nk = seq // tq, seq // tk
    q_map = lambda g, b, s, qi, kj, fl: (g, b * nq + qi[s], 0)
    k_map = lambda g, b, s, qi, kj, fl: (g, b * nk + jnp.maximum(kj[s], 0), 0)
    in_specs = [pl.BlockSpec((npb * rep, tq, LANE), q_map),
                pl.BlockSpec((npb, tk, kw), k_map),
                pl.BlockSpec((npb, tk, LANE), k_map)]
    args = [q, k, v]
    rows = rep * tq
    scratch = []
    if mode == "select":
        in_specs.append(pl.BlockSpec((npb, tq, LANE), q_map))
        args.append(selb)
        scratch.append(pltpu.VMEM((npb, rows, kw), BF16))
    scratch += [pltpu.VMEM((npb, rows, tk), F32), pltpu.VMEM((npb, rows, tk), BF16),
                pltpu.VMEM((npb, rows, LANE), F32), pltpu.VMEM((npb, rows, LANE), F32),
                pltpu.VMEM((npb, rows, LANE), F32)]
    return pl.pallas_call(
        functools.partial(_flash_body, mode=mode, rep=rep, tq=tq, tk=tk, nprob=npb),
        grid_spec=pltpu.PrefetchScalarGridSpec(
            num_scalar_prefetch=3,
            grid=(hkv // npb, batch, nsteps),
            in_specs=in_specs,
            out_specs=pl.BlockSpec((npb * rep, tq, LANE), q_map),
            scratch_shapes=scratch),
        out_shape=jax.ShapeDtypeStruct((hq, T, LANE), BF16),
        compiler_params=_cparams("parallel", "parallel", "arbitrary"),
        name="flash_" + mode,
    )(qi, kj, flags, *args)


def _attn_out_body(*refs, n_branch, heads):
    o_refs = refs[:n_branch]
    if n_branch > 1:
        gl_ref, w_ref, x_ref, g_ref, out_ref = refs[n_branch:]
        gates = _sigmoid(gl_ref[...])
    else:
        w_ref, x_ref, g_ref, out_ref = refs[n_branch:]
    per_head = []
    for h in range(heads):
        if n_branch > 1:
            o = jnp.zeros(o_refs[0].shape[1:], F32)
            for c in range(n_branch):
                col = n_branch * h + c
                o = o + gates[:, col:col + 1] * o_refs[c][h].astype(F32)
            o = o.astype(BF16)
        else:
            o = o_refs[0][h]
        per_head.append(o)
    acc = jnp.dot(jnp.concatenate(per_head, axis=1), w_ref[...], preferred_element_type=F32)
    out_ref[...] = x_ref[...] + g_ref[0] * acc


def _attn_out(os, gl, w_heads, x2, g, seq):
    T, Dm = x2.shape
    heads = w_heads.shape[0]
    tm = 256
    nb = seq // tm
    row = lambda i: (i, 0)
    o_spec = pl.BlockSpec((heads, tm, LANE), lambda i: (0, i, 0))
    in_specs = [o_spec] * len(os)
    args = list(os)
    if len(os) > 1:
        in_specs.append(pl.BlockSpec((tm, LANE), row))
        args.append(gl)
    in_specs += [pl.BlockSpec((heads * LANE, Dm), lambda i: (0, 0)),
                 pl.BlockSpec((tm, Dm), row),
                 pl.BlockSpec((1, 1, Dm), lambda i: (i // nb, 0, 0))]
    args += [w_heads.reshape(heads * LANE, Dm), x2, g]
    return pl.pallas_call(
        functools.partial(_attn_out_body, n_branch=len(os), heads=heads),
        grid=(T // tm,),
        in_specs=in_specs,
        out_specs=pl.BlockSpec((tm, Dm), row),
        out_shape=jax.ShapeDtypeStruct((T, Dm), F32),
        compiler_params=_cparams("parallel"),
        name="attn_out_%d" % len(os),
    )(*args)


def _pad_heads_rows(w, heads, dh):
    w = w.reshape(heads, dh, -1)
    return jnp.pad(w, ((0, 0), (0, LANE - dh), (0, 0))).astype(BF16)


def _pad_heads_cols(w, heads, dh):
    k = w.shape[0]
    w = w.reshape(k, heads, dh)
    return jnp.pad(w, ((0, 0), (0, 0), (0, LANE - dh))).reshape(k, heads * LANE)


def _rope_tables(pos, rot, offset):
    half = rot // 2
    inv_freq = ROPE_THETA ** (-jnp.arange(0, rot, 2, dtype=F32) / rot)
    ang = pos.astype(F32)[:, None] * inv_freq[None, :]
    cos, sin = jnp.cos(ang), jnp.sin(ang)
    n = pos.shape[0]
    c = jnp.ones((n, LANE), F32).at[:, offset:offset + rot].set(jnp.concatenate([cos, cos], axis=1))
    sa = jnp.zeros((n, LANE), F32).at[:, offset:offset + half].set(-sin)
    sb = jnp.zeros((n, LANE), F32).at[:, offset + half:offset + rot].set(sin)
    return c, sa, sb


def _mla_proj_body(x_ref, a_ref, b_ref, win_ref, qlg_ref, kvlg_ref, wuq_ref, wuqs_ref, wuk_ref, wuv_ref,
                   ones_ref, q1_ref, q2_ref, k1_ref, k2_ref, q_ref, k_ref, v_ref):
    h = _normmod(x_ref[...], a_ref[0], b_ref[0]).astype(BF16)
    r = jnp.dot(h, win_ref[...], preferred_element_type=F32)
    lat = MLA_Q_LORA + MLA_KV_LORA
    q_lat = r[:, :MLA_Q_LORA]
    kv_lat = r[:, MLA_Q_LORA:lat]
    kpe = r[:, lat:lat + LANE]
    kpe_swap = r[:, lat + LANE:]
    ql = q_lat * lax.rsqrt(jnp.mean(q_lat * q_lat, axis=-1, keepdims=True) + EPS) * qlg_ref[...]
    kvl = kv_lat * lax.rsqrt(jnp.mean(kv_lat * kv_lat, axis=-1, keepdims=True) + EPS) * kvlg_ref[...]
    ql = ql.astype(BF16)
    kvl = kvl.astype(BF16)
    q = jnp.dot(ql, wuq_ref[...], preferred_element_type=F32)
    q_swap = jnp.dot(ql, wuqs_ref[...], preferred_element_type=F32)
    kn = jnp.dot(kvl, wuk_ref[...], preferred_element_type=F32)
    v = jnp.dot(kvl, wuv_ref[...], preferred_element_type=F32)
    ones = ones_ref[...]
    q1, q2, k1, k2 = q1_ref[...], q2_ref[...], k1_ref[...], k2_ref[...]
    k_rot = kpe_swap * k2

    def inv_rms(t):
        ss = jnp.dot((t * t).astype(BF16), ones, preferred_element_type=F32)
        return lax.rsqrt(ss * (1.0 / MLA_QK) + EPS)

    for hd in range(MLA_HEADS):
        sl = slice(hd * LANE, (hd + 1) * LANE)
        xq = q[:, sl]
        q_ref[hd] = ((xq * q1 + q_swap[:, sl] * q2) * inv_rms(xq)).astype(BF16)
        xk = kn[:, sl] + kpe
        k_ref[hd] = ((xk * k1 + k_rot) * inv_rms(xk)).astype(BF16)
        v_ref[hd] = (v[:, sl] + _one_lane()).astype(BF16)


def _mla_mixer(x2, a, b, g, w_in, q_lat_g, kv_lat_g, w_uq, w_ukv, q_gain, k_gain, w_out, batch, seq):
    T, Dm = x2.shape
    H = MLA_HEADS
    tm = 256
    nb = seq // tm
    scale = MLA_QK ** -0.5 * LOG2E
    lat = MLA_Q_LORA + MLA_KV_LORA
    half = MLA_ROPE // 2

    def swap_rope(t):
        lo, hi = t[..., MLA_NOPE:MLA_NOPE + half], t[..., MLA_NOPE + half:MLA_QK]
        return jnp.concatenate([jnp.zeros_like(t[..., :MLA_NOPE]), hi, lo], axis=-1)

    to_slot = lambda t: jnp.pad(t, [(0, 0)] * (t.ndim - 1) + [(0, LANE - MLA_QK)])
    kpe_w = jnp.concatenate([jnp.zeros((Dm, MLA_NOPE), F32), w_in[:, lat:]], axis=1)
    win_p = jnp.concatenate([w_in[:, :lat], to_slot(kpe_w), to_slot(swap_rope(kpe_w))], axis=1).astype(BF16)
    wuq3 = w_uq.reshape(MLA_Q_LORA, H, MLA_QK)
    wuq_p = to_slot(wuq3).reshape(MLA_Q_LORA, H * LANE).astype(BF16)
    wuqs_p = to_slot(swap_rope(wuq3)).reshape(MLA_Q_LORA, H * LANE).astype(BF16)
    wukv = w_ukv.reshape(MLA_KV_LORA, H, MLA_NOPE + MLA_V)
    wuk_p = _pad_heads_cols(wukv[:, :, :MLA_NOPE].reshape(MLA_KV_LORA, -1), H, MLA_NOPE).astype(BF16)
    wuv_p = _pad_heads_cols(wukv[:, :, MLA_NOPE:].reshape(MLA_KV_LORA, -1), H, MLA_V).astype(BF16)
    c, sa, sb = _rope_tables(jnp.arange(seq), MLA_ROPE, MLA_NOPE)
    qg = q_gain * scale
    q1, q2 = c * to_slot(qg)[None], (sa + sb) * to_slot(swap_rope(qg))[None]
    k1, k2 = c * to_slot(k_gain)[None], (sa + sb) * to_slot(swap_rope(k_gain))[None]
    ones = jnp.ones((LANE, LANE), BF16)
    row = lambda i: (i, 0)
    full = lambda i: (0, 0)
    per_b = lambda i: (i // nb, 0, 0)
    pos = lambda i: (i % nb, 0)
    head_out = pl.BlockSpec((H, tm, LANE), lambda i: (0, i, 0))
    hshape = jax.ShapeDtypeStruct((H, T, LANE), BF16)
    q, k, v = pl.pallas_call(
        _mla_proj_body,
        grid=(T // tm,),
        in_specs=[pl.BlockSpec((tm, Dm), row),
                  pl.BlockSpec((1, 1, Dm), per_b),
                  pl.BlockSpec((1, 1, Dm), per_b),
                  pl.BlockSpec(win_p.shape, full),
                  pl.BlockSpec((1, MLA_Q_LORA), full),
                  pl.BlockSpec((1, MLA_KV_LORA), full),
                  pl.BlockSpec(wuq_p.shape, full),
                  pl.BlockSpec(wuqs_p.shape, full),
                  pl.BlockSpec(wuk_p.shape, full),
                  pl.BlockSpec(wuv_p.shape, full),
                  pl.BlockSpec((LANE, LANE), full),
                  pl.BlockSpec((tm, LANE), pos),
                  pl.BlockSpec((tm, LANE), pos),
                  pl.BlockSpec((tm, LANE), pos),
                  pl.BlockSpec((tm, LANE), pos)],
        out_specs=[head_out, head_out, head_out],
        out_shape=[hshape, hshape, hshape],
        compiler_params=_cparams("parallel"),
        name="mla_proj",
    )(x2, a, b, win_p, q_lat_g.reshape(1, -1), kv_lat_g.reshape(1, -1), wuq_p, wuqs_p, wuk_p, wuv_p,
      ones, q1, q2, k1, k2)
    blk = min(1024, seq)
    o = _flash(q, k, v, None, batch=batch, seq=seq, mode="causal", tq=blk, tk=blk)
    return _attn_out([o], None, _pad_heads_rows(w_out, H, MLA_V), x2, g, seq)


N_KV_STREAMS = 6


def _nsa_proj_body(x_ref, a_ref, b_ref, w_ref, qg_ref, kg_ref, c_ref, sa_ref, sb_ref, blk_ref,
                   q_ref, kv_ref, ks_ref, gl_ref):
    h = _normmod(x_ref[...], a_ref[0], b_ref[0]).astype(BF16)
    r = jnp.dot(h, w_ref[...], preferred_element_type=F32)
    c, sa, sb = c_ref[...], sa_ref[...], sb_ref[...]

    def norm_rot(t, gain):
        t = t * lax.rsqrt(jnp.sum(t * t, axis=-1, keepdims=True) * (1.0 / NSA_DH) + EPS) * gain
        return _rope(t, c, sa, sb, NSA_ROT // 2)

    for hd in range(NSA_HEADS):
        q_ref[hd] = norm_rot(r[:, hd * LANE:(hd + 1) * LANE], qg_ref[...]).astype(BF16)
    base = NSA_HEADS * LANE
    for st in range(N_KV_STREAMS):
        for gi in range(NSA_GROUPS):
            off = base + (st * NSA_GROUPS + gi) * LANE
            t = r[:, off:off + LANE]
            if st == 2:
                t = norm_rot(t, kg_ref[1:2, :])
                ks_ref[gi] = jnp.concatenate([t.astype(BF16), blk_ref[...]], axis=1)
            elif st == 4:
                t = norm_rot(t, kg_ref[2:3, :])
            elif st in (3, 5):
                t = t + _one_lane()
            kv_ref[st * NSA_GROUPS + gi] = t.astype(BF16)
    gl_ref[...] = r[:, base + N_KV_STREAMS * NSA_GROUPS * LANE:]


def _nsa_compress_body(x_ref, pea_ref, peb_ref, w1a_ref, w1b_ref, w2_ref, kg_ref, c_ref, sa_ref, sb_ref,
                       o_ref, *, is_key, n_cmp):
    x = x_ref[0].astype(F32)
    xa = (x + pea_ref[...]).astype(BF16)
    xb = (x + peb_ref[...]).astype(BF16)
    za = jnp.dot(xa, w1a_ref[...], preferred_element_type=F32)
    zb = jnp.dot(xb, w1b_ref[...], preferred_element_type=F32)
    rows = za.shape[0]
    z = _silu(za + pltpu.roll(zb, rows - 1, 0))
    t = jnp.dot(z.astype(BF16), w2_ref[...], preferred_element_type=F32)
    if is_key:
        t = t * lax.rsqrt(jnp.sum(t * t, axis=-1, keepdims=True) * (1.0 / NSA_DH) + EPS) * kg_ref[...]
        t = _rope(t, c_ref[...], sa_ref[...], sb_ref[...], NSA_ROT // 2)
    valid = lax.broadcasted_iota(I32, t.shape, 0) < n_cmp
    o_ref[0] = jnp.where(valid, t, 0.0).astype(BF16)


def _nsa_cmp_select_body(q_ref, kc_ref, vc_ref, oc_ref, sel_ref, p_sc, *, tq, ncp, n_cmp, n_top):
    i = pl.program_id(2)
    t0 = i * tq
    kc = kc_ref[0]
    vc = vc_ref[0]
    rep = q_ref.shape[0]
    q = q_ref[...].reshape(rep * tq, LANE)
    sc = lax.dot_general(q, kc, _NT, preferred_element_type=F32)
    qpos = t0 + lax.broadcasted_iota(I32, (tq, ncp), 0)
    blk = lax.broadcasted_iota(I32, (tq, ncp), 1)
    mask = ((blk * CMP_STRIDE + (CMP_BLOCK - 1)) <= qpos) & (blk < n_cmp)
    sc = jnp.where(mask[None], sc.reshape(rep, tq, ncp), NEG)
    p = jnp.exp2(sc - jnp.max(sc, axis=-1, keepdims=True))
    p = jnp.where(mask[None], p / jnp.sum(p, axis=-1, keepdims=True), 0.0)
    oc = jnp.dot(p.reshape(rep * tq, ncp).astype(BF16), vc, preferred_element_type=F32)
    oc_ref[...] = oc.reshape(rep, tq, LANE).astype(BF16)
    qpos_t = t0 + lax.broadcasted_iota(I32, (ncp, tq), 1)
    blk_t = lax.broadcasted_iota(I32, (ncp, tq), 0)
    mask_t = ((blk_t * CMP_STRIDE + (CMP_BLOCK - 1)) <= qpos_t) & (blk_t < n_cmp)
    psum = jnp.zeros((ncp, tq), F32)
    for r in range(rep):
        st = lax.dot_general(kc, q_ref[r], _NT, preferred_element_type=F32)
        st = jnp.where(mask_t, st, NEG)
        pt = jnp.exp2(st - jnp.max(st, axis=0, keepdims=True))
        psum = psum + jnp.where(mask_t, pt / jnp.sum(pt, axis=0, keepdims=True), 0.0)
    p_sc[...] = jnp.zeros(p_sc.shape, F32)
    per = SLC_BLOCK // CMP_STRIDE
    slabs = []
    for sb in range(tq // LANE):
        p_sc[sb, 8:8 + ncp, :] = psum[:, sb * LANE:(sb + 1) * LANE]
        part = p_sc[sb, pl.ds(7, MAX_SLC, stride=per), :]
        for k in range(1, per + 1):
            part = part + p_sc[sb, pl.ds(7 + k, MAX_SLC, stride=per), :]
        slabs.append(part)
    imp = jnp.concatenate(slabs, axis=1)
    jb = lax.broadcasted_iota(I32, (MAX_SLC, tq), 0)
    qp = t0 + lax.broadcasted_iota(I32, (MAX_SLC, tq), 1)
    imp = jnp.where(jb * SLC_BLOCK <= qp, imp, -BIG)
    imp = jnp.where((jb == 0) | (jb == jnp.right_shift(qp, 6)), BIG, imp)
    jbf = jb.astype(F32)
    sel_t = jnp.full((MAX_SLC, tq), SEL_OFF, F32)
    for _ in range(n_top):
        top = jnp.max(imp, axis=0, keepdims=True)
        first = jnp.min(jnp.where(imp == top, jbf, float(MAX_SLC)), axis=0, keepdims=True)
        hit = jbf == first
        sel_t = jnp.where(hit, 0.0, sel_t)
        imp = jnp.where(hit, -jnp.inf, imp)
    sel_ref[0] = sel_t.T.astype(BF16)


def _nsa_mixer(x2, a, b, g, w_in, w_cmp1, w_cmp2, cmp_pos, q_gain, k_gain, w_out, batch, seq):
    T, Dm = x2.shape
    H, G, dh = NSA_HEADS, NSA_GROUPS, NSA_DH
    scale = dh ** -0.5 * LOG2E
    n_cmp = seq // CMP_STRIDE - 1
    ncp = seq // CMP_STRIDE
    n_slc = seq // SLC_BLOCK
    assert n_slc <= MAX_SLC and ncp <= MAX_SLC * (SLC_BLOCK // CMP_STRIDE)
    n_top = min(SLC_TOP_N, n_slc)
    tm = 256
    nb = seq // tm
    q_cols = _pad_heads_cols(w_in[:, :H * dh], H, dh)
    kv_cols = _pad_heads_cols(w_in[:, H * dh:H * dh + N_KV_STREAMS * G * dh], N_KV_STREAMS * G, dh)
    gl_cols = jnp.pad(w_in[:, H * dh + N_KV_STREAMS * G * dh:], ((0, 0), (0, LANE - 3 * H)))
    w_p = jnp.concatenate([q_cols, kv_cols, gl_cols], axis=1).astype(BF16)
    pad_gain = lambda v: jnp.pad(v, ((0, 0), (0, LANE - dh)))
    c, sa, sb = _rope_tables(jnp.arange(seq), NSA_ROT, 0)
    row = lambda i: (i, 0)
    full = lambda i: (0, 0)
    per_b = lambda i: (i // nb, 0, 0)
    pos = lambda i: (i % nb, 0)
    blk_onehot = (jnp.arange(seq)[:, None] // SLC_BLOCK == jnp.arange(MAX_SLC)[None, :]).astype(BF16)
    q, kv, ks, gl = pl.pallas_call(
        _nsa_proj_body,
        grid=(T // tm,),
        in_specs=[pl.BlockSpec((tm, Dm), row),
                  pl.BlockSpec((1, 1, Dm), per_b),
                  pl.BlockSpec((1, 1, Dm), per_b),
                  pl.BlockSpec(w_p.shape, full),
                  pl.BlockSpec((1, LANE), full),
                  pl.BlockSpec((3, LANE), full),
                  pl.BlockSpec((tm, LANE), pos),
                  pl.BlockSpec((tm, LANE), pos),
                  pl.BlockSpec((tm, LANE), pos),
                  pl.BlockSpec((tm, MAX_SLC), pos)],
        out_specs=[pl.BlockSpec((H, tm, LANE), lambda i: (0, i, 0)),
                   pl.BlockSpec((N_KV_STREAMS * G, tm, LANE), lambda i: (0, i, 0)),
                   pl.BlockSpec((G, tm, LANE + MAX_SLC), lambda i: (0, i, 0)),
                   pl.BlockSpec((tm, LANE), row)],
        out_shape=[jax.ShapeDtypeStruct((H, T, LANE), BF16),
                   jax.ShapeDtypeStruct((N_KV_STREAMS * G, T, LANE), BF16),
                   jax.ShapeDtypeStruct((G, T, LANE + MAX_SLC), BF16),
                   jax.ShapeDtypeStruct((T, LANE), F32)],
        compiler_params=_cparams("parallel"),
        name="nsa_proj",
    )(x2, a, b, w_p, pad_gain(q_gain.reshape(1, dh) * scale), pad_gain(k_gain), c, sa, sb, blk_onehot)
    kc_raw, vc_raw, _, vs, kw, vw = [kv[s * G:(s + 1) * G] for s in range(N_KV_STREAMS)]

    cmp_end = jnp.arange(ncp) * CMP_STRIDE + (CMP_BLOCK - 1)
    cc, csa, csb = _rope_tables(cmp_end, NSA_ROT, 0)
    kdim = CMP_STRIDE * LANE

    def compress(raw, w1, w2, pe, is_key):
        x16 = raw.reshape(G, T // CMP_STRIDE, kdim)
        w1p = jnp.pad(w1, ((0, 0), (0, LANE - dh), (0, LANE - dh)))
        w1a = w1p[:CMP_STRIDE].reshape(kdim, LANE).astype(BF16)
        w1b = w1p[CMP_STRIDE:].reshape(kdim, LANE).astype(BF16)
        pep = jnp.pad(pe, ((0, 0), (0, LANE - dh)))
        pea = pep[:CMP_STRIDE].reshape(1, kdim)
        peb = pep[CMP_STRIDE:].reshape(1, kdim)
        w2p = jnp.pad(w2, ((0, LANE - dh), (0, LANE - dh))).astype(BF16)
        const = lambda gi, bi: (0, 0)
        return pl.pallas_call(
            functools.partial(_nsa_compress_body, is_key=is_key, n_cmp=n_cmp),
            grid=(G, batch),
            in_specs=[pl.BlockSpec((1, ncp, kdim), lambda gi, bi: (gi, bi, 0)),
                      pl.BlockSpec((1, kdim), const),
                      pl.BlockSpec((1, kdim), const),
                      pl.BlockSpec((kdim, LANE), const),
                      pl.BlockSpec((kdim, LANE), const),
                      pl.BlockSpec((LANE, LANE), const),
                      pl.BlockSpec((1, LANE), const),
                      pl.BlockSpec((ncp, LANE), const),
                      pl.BlockSpec((ncp, LANE), const),
                      pl.BlockSpec((ncp, LANE), const)],
            out_specs=pl.BlockSpec((1, ncp, LANE), lambda gi, bi: (gi, bi, 0)),
            out_shape=jax.ShapeDtypeStruct((G, batch * ncp, LANE), BF16),
            compiler_params=_cparams("parallel", "parallel"),
            name="nsa_compress_" + ("k" if is_key else "v"),
        )(x16, pea, peb, w1a, w1b, w2p, pad_gain(k_gain)[0:1], cc, csa, csb)

    kc = compress(kc_raw, w_cmp1[0], w_cmp2[0], cmp_pos[0], True)
    vc = compress(vc_raw, w_cmp1[1], w_cmp2[1], cmp_pos[1], False)

    tq = 256
    nq = seq // tq
    q_map = lambda gi, bi, i: (gi, bi * nq + i, 0)
    c_map = lambda gi, bi, i: (gi, bi, 0)
    o_c, sel = pl.pallas_call(
        functools.partial(_nsa_cmp_select_body, tq=tq, ncp=ncp, n_cmp=n_cmp, n_top=n_top),
        grid=(G, batch, nq),
        in_specs=[pl.BlockSpec((NSA_REP, tq, LANE), q_map),
                  pl.BlockSpec((1, ncp, LANE), c_map),
                  pl.BlockSpec((1, ncp, LANE), c_map)],
        out_specs=[pl.BlockSpec((NSA_REP, tq, LANE), q_map),
                   pl.BlockSpec((1, tq, LANE), q_map)],
        out_shape=[jax.ShapeDtypeStruct((H, T, LANE), BF16),
                   jax.ShapeDtypeStruct((G, T, LANE), BF16)],
        scratch_shapes=[pltpu.VMEM((tq // LANE, 8 + MAX_SLC * (SLC_BLOCK // CMP_STRIDE), LANE), F32)],
        compiler_params=_cparams("parallel", "parallel", "parallel"),
        name="nsa_cmp_select",
    )(q, kc, vc)

    o_s = _flash(q, ks, vs, sel, batch=batch, seq=seq, mode="select", tq=256, tk=min(1024, seq))
    o_w = _flash(q, kw, vw, None, batch=batch, seq=seq, mode="window", tq=256, tk=WINDOW)
    return _attn_out([o_c, o_s, o_w], gl, _pad_heads_rows(w_out, H, dh), x2, g, seq)


def _router_body(x_ref, a_ref, b_ref, rwt_ref, rb_ref, tri_ref, e_ref, w_ref, rank_ref, cnt_ref,
                 carry_sc, *, tm):
    i = pl.program_id(0)

    @pl.when(i == 0)
    def _():
        carry_sc[...] = jnp.zeros(carry_sc.shape, F32)

    h = _normmod(x_ref[...], a_ref[0], b_ref[0])
    logits = lax.dot_general(rwt_ref[...], h, _NT, preferred_element_type=F32, precision=HIGHEST)
    scores = _sigmoid(logits)
    biased = scores + rb_ref[...]
    ng, per = N_GROUPS, EXPERTS_PER_GROUP
    row = lambda arr, r: arr[r:r + 1, :]
    gsel = jnp.zeros((1, tm), I32)
    best = None
    for gi in range(ng):
        v = [row(biased, gi * per + k) for k in range(per)]
        top2 = None
        for p in range(per):
            for q in range(p + 1, per):
                s = v[p] + v[q]
                top2 = s if top2 is None else jnp.maximum(top2, s)
        if best is None:
            best = top2
        else:
            better = top2 > best
            gsel = jnp.where(better, gi, gsel)
            best = jnp.where(better, top2, best)
    cb, cs = [], []
    for k in range(per):
        b_k = row(biased, k)
        s_k = row(scores, k)
        for gi in range(1, ng):
            hit = gsel == gi
            b_k = jnp.where(hit, row(biased, gi * per + k), b_k)
            s_k = jnp.where(hit, row(scores, gi * per + k), s_k)
        cb.append(b_k)
        cs.append(s_k)

    def argmax_first(vals):
        idx = jnp.zeros((1, tm), I32)
        top = vals[0]
        for k in range(1, per):
            better = vals[k] > top
            idx = jnp.where(better, k, idx)
            top = jnp.where(better, vals[k], top)
        return idx

    def pick(vals, idx):
        out = vals[0]
        for k in range(1, per):
            out = jnp.where(idx == k, vals[k], out)
        return out

    i1 = argmax_first(cb)
    i2 = argmax_first([jnp.where(i1 == k, -jnp.inf, cb[k]) for k in range(per)])
    w1 = pick(cs, i1)
    w2 = pick(cs, i2)
    tot = w1 + w2
    e1 = gsel * per + i1
    e2 = gsel * per + i2
    eid = lax.broadcasted_iota(I32, (N_EXPERTS, tm), 0)
    hot = (eid == e1) | (eid == e2)
    onehot = jnp.where(hot, 1.0, 0.0)
    before = carry_sc[...] + jnp.dot(onehot.astype(BF16), tri_ref[...], preferred_element_type=F32)
    r1 = jnp.sum(jnp.where(eid == e1, before, 0.0), axis=0, keepdims=True)
    r2 = jnp.sum(jnp.where(eid == e2, before, 0.0), axis=0, keepdims=True)
    carry = carry_sc[...] + jnp.sum(onehot, axis=1, keepdims=True)
    carry_sc[...] = carry
    cnt_ref[...] = jnp.broadcast_to(carry, cnt_ref.shape)
    zi = jnp.zeros((6, tm), I32)
    e_ref[...] = jnp.concatenate([e1, e2, zi], axis=0)
    rank_ref[...] = jnp.concatenate([r1.astype(I32), r2.astype(I32), zi], axis=0)
    w_ref[...] = jnp.concatenate([w1 / tot, w2 / tot, jnp.zeros((6, tm), F32)], axis=0)


ROW_SUB = D_MODEL // LANE


def _row_tile(r):
    return pl.ds(pl.multiple_of(r * ROW_SUB, ROW_SUB), ROW_SUB)


def _to_row_tiles(ref, val):
    n = val.shape[0]
    for s in range(ROW_SUB):
        ref[pl.ds(s, n, stride=ROW_SUB), :] = val[:, s * LANE:(s + 1) * LANE]


def _from_row_tiles(ref, n, s):
    return ref[pl.ds(s, n, stride=ROW_SUB), :]


def _moe_dispatch_body(starts_ref, counts_ref, padded_ref, nv_ref, e0_ref, e1_ref, r0_ref, r1_ref,
                       x_ref, a_ref, b_ref, xs_hbm, d0_ref, d1_ref, hbuf0, hbuf1, zbuf, sems, zsem,
                       *, tm, tg, n_tiles, nt):
    i = pl.program_id(0)

    @pl.when(i == 0)
    def _():
        zbuf[...] = jnp.zeros(zbuf.shape, F32)
        zrow = zbuf.at[pl.ds(0, ROW_SUB)]
        for ex in range(N_EXPERTS):
            lo = starts_ref[ex] + counts_ref[ex]
            hi = starts_ref[ex] + padded_ref[ex]

            def fill(r, carry):
                pltpu.make_async_copy(zrow, xs_hbm.at[_row_tile(r)], zsem).start()
                return carry

            lax.fori_loop(lo, hi, fill, 0)

            def drain_fill(r, carry):
                pltpu.make_async_copy(zrow, xs_hbm.at[_row_tile(0)], zsem).wait()
                return carry

            lax.fori_loop(lo, hi, drain_fill, 0)

        rows_per_tile = tg * ROW_SUB

        def fill_tile(t, carry):
            dst = xs_hbm.at[pl.ds(pl.multiple_of(t * rows_per_tile, rows_per_tile), rows_per_tile)]
            pltpu.make_async_copy(zbuf, dst, zsem).start()
            return carry

        lax.fori_loop(nv_ref[0], n_tiles, fill_tile, 0)

        def drain_tile(t, carry):
            pltpu.make_async_copy(zbuf, xs_hbm.at[pl.ds(0, rows_per_tile)], zsem).wait()
            return carry

        lax.fori_loop(nv_ref[0], n_tiles, drain_tile, 0)

    h = _normmod(x_ref[...], a_ref[0], b_ref[0])

    def scatter_from(hbuf, sem, other_buf, other_sem):
        _to_row_tiles(hbuf, h)

        def row_copy(buf, sm, r, slot):
            return pltpu.make_async_copy(buf.at[_row_tile(r)], xs_hbm.at[_row_tile(slot)], sm)

        def issue(r, carry):
            s0 = starts_ref[e0_ref[r]] + r0_ref[r]
            s1 = starts_ref[e1_ref[r]] + r1_ref[r]
            d0_ref[r] = s0
            d1_ref[r] = s1
            row_copy(hbuf, sem, r, s0).start()
            row_copy(hbuf, sem, r, s1).start()
            return carry

        lax.fori_loop(0, tm, issue, 0, unroll=8)

        def drain(buf, sm):
            def body(r, carry):
                row_copy(buf, sm, r, 0).wait()
                row_copy(buf, sm, r, 0).wait()
                return carry

            lax.fori_loop(0, tm, body, 0, unroll=8)

        pl.when(i > 0)(lambda: drain(other_buf, other_sem))
        pl.when(i == nt - 1)(lambda: drain(hbuf, sem))

    pl.when(i % 2 == 0)(lambda: scatter_from(hbuf0, sems.at[0], hbuf1, sems.at[1]))
    pl.when(i % 2 == 1)(lambda: scatter_from(hbuf1, sems.at[1], hbuf0, sems.at[0]))


def _moe_expert_body(te_ref, nv_ref, xs_ref, win_ref, wout_ref, y_ref, *, tg):
    i = pl.program_id(0)

    @pl.when(i < nv_ref[0])
    def _():
        x = jnp.concatenate([_from_row_tiles(xs_ref, tg, s) for s in range(ROW_SUB)], axis=1).astype(BF16)
        gu = jnp.dot(x, win_ref[0], preferred_element_type=F32)
        act = (_silu(gu[:, :D_EXPERT]) * gu[:, D_EXPERT:]).astype(BF16)
        _to_row_tiles(y_ref, jnp.dot(act, wout_ref[0], preferred_element_type=F32))

    @pl.when(i >= nv_ref[0])
    def _():
        y_ref[...] = jnp.zeros(y_ref.shape, F32)


def _moe_combine_body(d0c_ref, d1c_ref, d0n_ref, d1n_ref, y_hbm, x_ref, w_ref, g_ref, o_ref,
                      ya0, ya1, yb0, yb1, sems, *, tm, nt):
    i = pl.program_id(0)

    def row_copy(slot, buf, sem, r):
        return pltpu.make_async_copy(y_hbm.at[_row_tile(slot)], buf.at[_row_tile(r)], sem)

    def issue(d0_ref, d1_ref, bufs, sem):
        def body(r, carry):
            row_copy(d0_ref[r], bufs[0], sem, r).start()
            row_copy(d1_ref[r], bufs[1], sem, r).start()
            return carry

        lax.fori_loop(0, tm, body, 0, unroll=8)

    def finish(bufs, sem):
        def body(r, carry):
            row_copy(0, bufs[0], sem, r).wait()
            row_copy(0, bufs[1], sem, r).wait()
            return carry

        lax.fori_loop(0, tm, body, 0, unroll=8)
        w = w_ref[...]
        w0, w1 = w[:, 0:1], w[:, 1:2]
        gate = g_ref[0]
        for s in range(ROW_SUB):
            sl = slice(s * LANE, (s + 1) * LANE)
            y = w0 * _from_row_tiles(bufs[0], tm, s) + w1 * _from_row_tiles(bufs[1], tm, s)
            o_ref[:, sl] = x_ref[:, sl] + gate[:, sl] * y

    set_a, set_b = (ya0, ya1), (yb0, yb1)
    pl.when(i == 0)(lambda: issue(d0c_ref, d1c_ref, set_a, sems.at[0]))

    def even():
        pl.when(i + 1 < nt)(lambda: issue(d0n_ref, d1n_ref, set_b, sems.at[1]))
        finish(set_a, sems.at[0])

    def odd():
        pl.when(i + 1 < nt)(lambda: issue(d0n_ref, d1n_ref, set_a, sems.at[0]))
        finish(set_b, sems.at[1])

    pl.when(i % 2 == 0)(even)
    pl.when(i % 2 == 1)(odd)


def _grouped_moe(x2, a, b, g, router_w, router_bias, w_in, w_out, seq):
    T, Dm = x2.shape
    E = N_EXPERTS
    tm = 512
    nb = seq // tm
    nt = T // tm
    row = lambda i: (i, 0)
    full = lambda i: (0, 0)
    per_b = lambda i: (i // nb, 0, 0)
    lanes = lambda i: (0, i)
    tri = jnp.asarray(np.triu(np.ones((tm, tm), np.float32), 1)).astype(BF16)
    e, w, rank, cnt = pl.pallas_call(
        functools.partial(_router_body, tm=tm),
        grid=(nt,),
        in_specs=[pl.BlockSpec((tm, Dm), row),
                  pl.BlockSpec((1, 1, Dm), per_b),
                  pl.BlockSpec((1, 1, Dm), per_b),
                  pl.BlockSpec((E, Dm), full),
                  pl.BlockSpec((E, 1), full),
                  pl.BlockSpec((tm, tm), full)],
        out_specs=[pl.BlockSpec((8, tm), lanes),
                   pl.BlockSpec((8, tm), lanes),
                   pl.BlockSpec((8, tm), lanes),
                   pl.BlockSpec((E, LANE), full)],
        out_shape=[jax.ShapeDtypeStruct((8, T), I32),
                   jax.ShapeDtypeStruct((8, T), F32),
                   jax.ShapeDtypeStruct((8, T), I32),
                   jax.ShapeDtypeStruct((E, LANE), F32)],
        scratch_shapes=[pltpu.VMEM((E, 1), F32)],
        compiler_params=_cparams("arbitrary"),
        name="moe_router",
    )(x2, a, b, router_w.T, router_bias.reshape(E, 1), tri)

    tg = MOE_TILE
    n_tiles = (2 * T) // tg + E
    n_slots = n_tiles * tg
    counts = cnt[:, 0].astype(I32)
    padded = ((counts + tg - 1) // tg) * tg
    ends = jnp.cumsum(padded)
    starts = ends - padded
    tile_start = jnp.arange(n_tiles, dtype=I32) * tg
    tile_expert = jnp.minimum(jnp.sum(tile_start[:, None] >= ends[None, :], axis=1), E - 1).astype(I32)
    n_valid = (ends[-1] // tg).astype(I32).reshape(1)

    tc = 256
    ntc = T // tc
    nbc = seq // tc
    smem_cur = pl.BlockSpec((tc,), lambda i, *_: (i,), memory_space=pltpu.SMEM)
    smem_nxt = pl.BlockSpec((tc,), lambda i, *_: (jnp.minimum(i + 1, ntc - 1),), memory_space=pltpu.SMEM)
    tile_rows = tc * ROW_SUB
    xs, d0, d1 = pl.pallas_call(
        functools.partial(_moe_dispatch_body, tm=tc, tg=tg, n_tiles=n_tiles, nt=ntc),
        grid_spec=pltpu.PrefetchScalarGridSpec(
            num_scalar_prefetch=4,
            grid=(ntc,),
            in_specs=[smem_cur, smem_cur, smem_cur, smem_cur,
                      pl.BlockSpec((tc, Dm), lambda i, *_: (i, 0)),
                      pl.BlockSpec((1, 1, Dm), lambda i, *_: (i // nbc, 0, 0)),
                      pl.BlockSpec((1, 1, Dm), lambda i, *_: (i // nbc, 0, 0))],
            out_specs=[pl.BlockSpec(memory_space=pl.ANY), smem_cur, smem_cur],
            scratch_shapes=[pltpu.VMEM((tile_rows, LANE), F32), pltpu.VMEM((tile_rows, LANE), F32),
                            pltpu.VMEM((tg * ROW_SUB, LANE), F32),
                            pltpu.SemaphoreType.DMA((2,)), pltpu.SemaphoreType.DMA(())]),
        out_shape=[jax.ShapeDtypeStruct((n_slots * ROW_SUB, LANE), F32),
                   jax.ShapeDtypeStruct((T,), I32), jax.ShapeDtypeStruct((T,), I32)],
        compiler_params=_cparams("arbitrary"),
        name="moe_dispatch",
    )(starts.astype(I32), counts, padded, n_valid, e[0], e[1], rank[0], rank[1], x2, a, b)

    last_tile = lambda i, te, nv: (jnp.minimum(i, nv[0] - 1), 0)
    y = pl.pallas_call(
        functools.partial(_moe_expert_body, tg=tg),
        grid_spec=pltpu.PrefetchScalarGridSpec(
            num_scalar_prefetch=2,
            grid=(n_tiles,),
            in_specs=[pl.BlockSpec((tg * ROW_SUB, LANE), last_tile),
                      pl.BlockSpec((1, Dm, 2 * D_EXPERT), lambda i, te, nv: (te[i], 0, 0)),
                      pl.BlockSpec((1, D_EXPERT, Dm), lambda i, te, nv: (te[i], 0, 0))],
            out_specs=pl.BlockSpec((tg * ROW_SUB, LANE), lambda i, te, nv: (i, 0))),
        out_shape=jax.ShapeDtypeStruct((n_slots * ROW_SUB, LANE), F32),
        compiler_params=_cparams("arbitrary"),
        name="moe_experts",
    )(tile_expert, n_valid, xs, w_in.astype(BF16), w_out.astype(BF16))

    return pl.pallas_call(
        functools.partial(_moe_combine_body, tm=tc, nt=ntc),
        grid=(ntc,),
        in_specs=[smem_cur, smem_cur, smem_nxt, smem_nxt,
                  pl.BlockSpec(memory_space=pl.ANY),
                  pl.BlockSpec((tc, Dm), row),
                  pl.BlockSpec((tc, 2), row),
                  pl.BlockSpec((1, 1, Dm), lambda i: (i // nbc, 0, 0))],
        out_specs=pl.BlockSpec((tc, Dm), row),
        out_shape=jax.ShapeDtypeStruct((T, Dm), F32),
        scratch_shapes=[pltpu.VMEM((tile_rows, LANE), F32)] * 4 + [pltpu.SemaphoreType.DMA((2,))],
        compiler_params=_cparams("arbitrary"),
        name="moe_combine",
    )(d0, d1, d0, d1, y, x2, w[:2].T, g)


def kernel(x, c, norm_mix_g, norm_ffn_g, w_ada, b_ada, conv_w_pw1, conv_b_pw1, conv_w_dw, conv_b_dw, conv_ln_g, conv_ln_b, conv_w_pw2, conv_b_pw2, nsa_w_in, nsa_w_cmp1, nsa_w_cmp2, nsa_cmp_pos, nsa_q_gain, nsa_k_gain, nsa_w_out, mla_w_in, mla_q_lat_g, mla_kv_lat_g, mla_w_uq, mla_w_ukv, mla_q_gain, mla_k_gain, mla_w_out, router_w, router_bias, moe_w_in, moe_w_out):
    B, S, Dm = x.shape
    depth = w_ada.shape[0]
    mods = _ada(c, w_ada, b_ada)
    x2 = x.reshape(B * S, Dm)
    for i in range(depth):
        sh1, sc1, g1, sh2, sc2, g2 = [m.reshape(B, 1, Dm) for m in jnp.split(mods[i], 6, axis=-1)]
        a1 = norm_mix_g[i] * (1.0 + sc1)
        kind, j = i % 3, i // 3
        if kind == 0:
            x2 = _conv_mixer(x2, a1, sh1, g1, conv_w_pw1[j], conv_b_pw1[j], conv_w_dw[j], conv_b_dw[j],
                             conv_ln_g[j], conv_ln_b[j], conv_w_pw2[j], conv_b_pw2[j], S)
        elif kind == 1:
            x2 = _nsa_mixer(x2, a1, sh1, g1, nsa_w_in[j], nsa_w_cmp1[j], nsa_w_cmp2[j], nsa_cmp_pos[j],
                            nsa_q_gain[j], nsa_k_gain[j], nsa_w_out[j], B, S)
        else:
            x2 = _mla_mixer(x2, a1, sh1, g1, mla_w_in[j], mla_q_lat_g[j], mla_kv_lat_g[j], mla_w_uq[j],
                            mla_w_ukv[j], mla_q_gain[j], mla_k_gain[j], mla_w_out[j], B, S)
        a2 = norm_ffn_g[i] * (1.0 + sc2)
        x2 = _grouped_moe(x2, a2, sh2, g2, router_w, router_bias, moe_w_in[i], moe_w_out[i], S)
    return x2.reshape(B, S, Dm)
```

```python
import functools
import math

import numpy as np
import jax
import jax.numpy as jnp
from jax import lax
from jax.experimental import pallas as pl
from jax.experimental.pallas import tpu as pltpu

F32 = jnp.float32
BF16 = jnp.bfloat16
I32 = jnp.int32
HIGHEST = lax.Precision.HIGHEST

EPS = 1e-6
NEG = -1e30
BIG = 1e30
ROPE_THETA = 500000.0
LANE = 128
VMEM_LIMIT = 56 * 1024 * 1024

D_MODEL = 1024
CONV_WIDTH = 31
CONV_HALO = 32

NSA_HEADS = 16
NSA_GROUPS = 4
NSA_REP = NSA_HEADS // NSA_GROUPS
NSA_DH = 64
NSA_ROT = 16
CMP_BLOCK = 32
CMP_STRIDE = 16
SLC_BLOCK = 64
SLC_TOP_N = 16
WINDOW = 512
MAX_SLC = 128

MLA_HEADS = 16
MLA_Q_LORA = 384
MLA_KV_LORA = 256
MLA_NOPE = 64
MLA_ROPE = 32
MLA_V = 64
MLA_QK = MLA_NOPE + MLA_ROPE

N_EXPERTS = 16
N_GROUPS = 4
EXPERTS_PER_GROUP = 4
D_EXPERT = 512
MOE_TILE = 512

_NT = (((1,), (1,)), ((), ()))


def _cparams(*sem):
    return pltpu.CompilerParams(dimension_semantics=sem, vmem_limit_bytes=VMEM_LIMIT)


def _sigmoid(x):
    return 1.0 / (1.0 + jnp.exp(-x))


def _silu(x):
    return x * _sigmoid(x)


def _normmod(x, a, b):
    ms = jnp.mean(x * x, axis=-1, keepdims=True)
    return x * lax.rsqrt(ms + EPS) * a + b


def _rope(x, c, sa, sb, half):
    n = x.shape[-1]
    return x * c + pltpu.roll(x, n - half, 1) * sa + pltpu.roll(x, half, 1) * sb


def _ada_body(c_ref, w_ref, b_ref, o_ref):
    c = c_ref[...]
    o_ref[0] = jnp.dot(_silu(c), w_ref[0], preferred_element_type=F32, precision=HIGHEST) + b_ref[0]


def _ada(c, w_ada, b_ada):
    B = c.shape[0]
    L, Dm, N = w_ada.shape
    Bp = -(-B // 8) * 8
    cp = jnp.pad(c, ((0, Bp - B), (0, 0)))
    tn = 1536
    out = pl.pallas_call(
        _ada_body,
        grid=(L, N // tn),
        in_specs=[pl.BlockSpec((Bp, Dm), lambda l, j: (0, 0)),
                  pl.BlockSpec((1, Dm, tn), lambda l, j: (l, 0, j)),
                  pl.BlockSpec((1, 1, tn), lambda l, j: (l, 0, j))],
        out_specs=pl.BlockSpec((1, Bp, tn), lambda l, j: (l, 0, j)),
        out_shape=jax.ShapeDtypeStruct((L, Bp, N), F32),
        compiler_params=_cparams("arbitrary", "arbitrary"),
        name="adaln",
    )(cp, w_ada, b_ada.reshape(L, 1, N))
    return out[:, :B]


def _conv_pw1_body(x_ref, a_ref, b_ref, w_ref, bias_ref, u_ref):
    h = _normmod(x_ref[...], a_ref[0], b_ref[0]).astype(BF16)
    r = jnp.dot(h, w_ref[...], preferred_element_type=F32) + bias_ref[...]
    d = u_ref.shape[-1]
    u_ref[...] = r[:, :d] * _sigmoid(r[:, d:])


def _conv_dw_body(u_ref, uh_ref, wdw_ref, bdw_ref, lng_ref, lnb_ref, w2_ref, b2_ref, x_ref, g_ref,
                  o_ref, ext_ref, acc_ref, *, tm, seq):
    i = pl.program_id(0)
    at_seq_start = (i * tm) % seq == 0
    ext_ref[0:CONV_HALO, :] = jnp.where(at_seq_start, 0.0, uh_ref[...])
    ext_ref[CONV_HALO:, :] = u_ref[...]
    base = CONV_HALO - (CONV_WIDTH - 1)
    rc = 64
    for lc in range(u_ref.shape[-1] // LANE):
        ls = slice(lc * LANE, (lc + 1) * LANE)
        wl = wdw_ref[:, ls]
        bl = bdw_ref[:, ls]
        for r0 in range(0, tm, rc):
            e = ext_ref[r0:r0 + rc + CONV_HALO, ls]
            acc = bl
            for s in range(8):
                rows = rc if s == 0 else rc + 8
                p = None
                for a in range((base + CONV_WIDTH - 1) // 8 + 1):
                    k = 8 * a + s - base
                    if 0 <= k < CONV_WIDTH and 8 * a + rows <= rc + CONV_HALO:
                        term = wl[k:k + 1, :] * e[8 * a:8 * a + rows, :]
                        p = term if p is None else p + term
                acc = acc + (p if s == 0 else p[s:s + rc, :])
            acc_ref[r0:r0 + rc, ls] = acc
    acc = acc_ref[...]
    mu = jnp.mean(acc, axis=-1, keepdims=True)
    dlt = acc - mu
    var = jnp.mean(dlt * dlt, axis=-1, keepdims=True)
    y = dlt * lax.rsqrt(var + EPS) * lng_ref[...] + lnb_ref[...]
    z = _silu(y).astype(BF16)
    out = jnp.dot(z, w2_ref[...], preferred_element_type=F32) + b2_ref[...]
    o_ref[...] = x_ref[...] + g_ref[0] * out


def _conv_mixer(x2, a, b, g, w_pw1, b_pw1, w_dw, b_dw, ln_g, ln_b, w_pw2, b_pw2, seq):
    T, Dm = x2.shape
    tm = 512
    nb = seq // tm
    row = lambda i: (i, 0)
    per_b = lambda i: (i // nb, 0, 0)
    full = lambda i: (0, 0)
    u = pl.pallas_call(
        _conv_pw1_body,
        grid=(T // tm,),
        in_specs=[pl.BlockSpec((tm, Dm), row),
                  pl.BlockSpec((1, 1, Dm), per_b),
                  pl.BlockSpec((1, 1, Dm), per_b),
                  pl.BlockSpec((Dm, 2 * Dm), full),
                  pl.BlockSpec((1, 2 * Dm), full)],
        out_specs=pl.BlockSpec((tm, Dm), row),
        out_shape=jax.ShapeDtypeStruct((T, Dm), F32),
        compiler_params=_cparams("parallel"),
        name="conv_pw1_glu",
    )(x2, a, b, w_pw1.astype(BF16), b_pw1.reshape(1, -1))
    hb = tm // CONV_HALO
    wdw = jnp.pad(w_dw, ((0, CONV_HALO - CONV_WIDTH), (0, 0)))
    vec = lambda v: v.reshape(1, -1)
    return pl.pallas_call(
        functools.partial(_conv_dw_body, tm=tm, seq=seq),
        grid=(T // tm,),
        in_specs=[pl.BlockSpec((tm, Dm), row),
                  pl.BlockSpec((CONV_HALO, Dm), lambda i: (jnp.maximum(i * hb - 1, 0), 0)),
                  pl.BlockSpec((CONV_HALO, Dm), full),
                  pl.BlockSpec((1, Dm), full),
                  pl.BlockSpec((1, Dm), full),
                  pl.BlockSpec((1, Dm), full),
                  pl.BlockSpec((Dm, Dm), full),
                  pl.BlockSpec((1, Dm), full),
                  pl.BlockSpec((tm, Dm), row),
                  pl.BlockSpec((1, 1, Dm), per_b)],
        out_specs=pl.BlockSpec((tm, Dm), row),
        out_shape=jax.ShapeDtypeStruct((T, Dm), F32),
        scratch_shapes=[pltpu.VMEM((tm + CONV_HALO, Dm), F32), pltpu.VMEM((tm, Dm), F32)],
        compiler_params=_cparams("parallel"),
        name="conv_dw_ln_pw2",
    )(u, u, wdw, vec(b_dw), vec(ln_g), vec(ln_b), w_pw2.astype(BF16), vec(b_pw2), x2, g)


FLASH_FIRST, FLASH_LAST, FLASH_MASKED = 1, 2, 4
FLASH_CHUNK = 16
SEL_OFF = -(2.0 ** 100)
ONE_LANE = 64
LOG2E = 1.4426950408889634
FLASH_NPROB = 2


def _one_lane():
    return jnp.where(lax.broadcasted_iota(I32, (1, LANE), 1) == ONE_LANE, 1.0, 0.0)


def _flash_body(qi_ref, kj_ref, flag_ref, *refs, mode, rep, tq, tk, nprob):
    if mode == "select":
        q_ref, k_ref, v_ref, selb_ref, o_ref, qa_sc, s_sc, p_sc, mb_sc, al_sc, acc_sc = refs
    else:
        q_ref, k_ref, v_ref, o_ref, s_sc, p_sc, mb_sc, al_sc, acc_sc = refs
    step = pl.program_id(2)
    qi = qi_ref[step]
    kj = kj_ref[step]
    flag = flag_ref[step]
    rows = rep * tq
    ch = FLASH_CHUNK
    nl = tk // LANE
    lane_fold = lambda t, op: functools.reduce(op, [t[:, i * LANE:(i + 1) * LANE] for i in range(nl)])

    @pl.when((flag & FLASH_FIRST) != 0)
    def _():
        mb_sc[...] = jnp.full(mb_sc.shape, NEG, F32)
        acc_sc[...] = jnp.zeros(acc_sc.shape, F32)
        if mode == "select":
            for pr in range(nprob):
                qa_sc[pr, :, :LANE] = q_ref[pr * rep:(pr + 1) * rep].reshape(rows, LANE)
                qa_sc[pr, :, LANE:] = jnp.concatenate([selb_ref[pr]] * rep, axis=0)

    def process(masked):
        thr = qi * tq - kj * tk
        for pr in range(nprob):
            q = qa_sc[pr] if mode == "select" else q_ref[pr * rep:(pr + 1) * rep].reshape(rows, LANE)
            s_sc[pr] = lax.dot_general(q, k_ref[pr], _NT, preferred_element_type=F32)
        if masked:
            diff = lax.broadcasted_iota(I32, (ch, tk), 1) - lax.broadcasted_iota(I32, (ch, tk), 0)
        for pr in range(nprob):
            for r0 in range(0, rows, ch):
                sc = s_sc[pr, r0:r0 + ch, :]
                if masked:
                    lim = thr + (r0 % tq)
                    ok = diff <= lim
                    if mode == "window":
                        ok = ok & (diff > lim - WINDOW)
                    sc = jnp.where(ok, sc, NEG)
                    s_sc[pr, r0:r0 + ch, :] = sc
                m_prev = mb_sc[pr, r0:r0 + ch, :]
                m_new = jnp.maximum(m_prev, jnp.max(lane_fold(sc, jnp.maximum), axis=1, keepdims=True))
                al_sc[pr, r0:r0 + ch, :] = jnp.exp2(m_prev - m_new)
                mb_sc[pr, r0:r0 + ch, :] = m_new
            for r0 in range(0, rows, ch):
                mb = mb_sc[pr, r0:r0 + ch, :]
                p = jnp.exp2(s_sc[pr, r0:r0 + ch, :] - jnp.concatenate([mb] * nl, axis=1))
                p_sc[pr, r0:r0 + ch, :] = p.astype(BF16)
            acc_sc[pr] = al_sc[pr] * acc_sc[pr] + jnp.dot(p_sc[pr], v_ref[pr], preferred_element_type=F32)

    if mode == "window":
        pl.when(kj >= 0)(lambda: process(True))
    else:
        pl.when((kj >= 0) & ((flag & FLASH_MASKED) != 0))(lambda: process(True))
        pl.when((kj >= 0) & ((flag & FLASH_MASKED) == 0))(lambda: process(False))

    @pl.when((flag & FLASH_LAST) != 0)
    def _():
        for pr in range(nprob):
            acc = acc_sc[pr]
            o = acc / acc[:, ONE_LANE:ONE_LANE + 1]
            o_ref[pr * rep:(pr + 1) * rep] = o.reshape(rep, tq, LANE).astype(o_ref.dtype)


def _flash_schedule(seq, tq, tk, mode):
    qi, kj, flags = [], [], []
    for i in range(seq // tq):
        hi = (i * tq + tq - 1) // tk
        lo = 0 if mode != "window" else (i * tq - (WINDOW - 1)) // tk
        js = list(range(lo, hi + 1))
        for n, j in enumerate(js):
            crosses_diagonal = (j + 1) * tk - 1 > i * tq
            qi.append(i)
            kj.append(j if j >= 0 else -1)
            flags.append((FLASH_FIRST if n == 0 else 0) | (FLASH_LAST if n == len(js) - 1 else 0)
                         | (FLASH_MASKED if crosses_diagonal else 0))
    as_i32 = lambda v: jnp.asarray(np.asarray(v, np.int32))
    return as_i32(qi), as_i32(kj), as_i32(flags), len(qi)


def _flash(q, k, v, selb, *, batch, seq, mode, tq, tk):
    hq, T, _ = q.shape
    hkv = k.shape[0]
    kw = k.shape[-1]
    rep = hq // hkv
    npb = FLASH_NPROB
    assert tq & (tq - 1) == 0 and tq % FLASH_CHUNK == 0 and hkv % npb == 0
    qi, kj, flags, nsteps = _flash_schedule(seq, tq, tk, mode)
    nq, nk = seq // tq, seq // tk
    q_map = lambda g, b, s, qi, kj, fl: (g, b * nq + qi[s], 0)
    k_map = lambda g, b, s, qi, kj, fl: (g, b * nk + jnp.maximum(kj[s], 0), 0)
    in_specs = [pl.BlockSpec((npb * rep, tq, LANE), q_map),
                pl.BlockSpec((npb, tk, kw), k_map),
                pl.BlockSpec((npb, tk, LANE), k_map)]
    args = [q, k, v]
    rows = rep * tq
    scratch = []
    if mode == "select":
        in_specs.append(pl.BlockSpec((npb, tq, LANE), q_map))
        args.append(selb)
        scratch.append(pltpu.VMEM((npb, rows, kw), BF16))
    scratch += [pltpu.VMEM((npb, rows, tk), F32), pltpu.VMEM((npb, rows, tk), BF16),
                pltpu.VMEM((npb, rows, LANE), F32), pltpu.VMEM((npb, rows, LANE), F32),
                pltpu.VMEM((npb, rows, LANE), F32)]
    return pl.pallas_call(
        functools.partial(_flash_body, mode=mode, rep=rep, tq=tq, tk=tk, nprob=npb),
        grid_spec=pltpu.PrefetchScalarGridSpec(
            num_scalar_prefetch=3,
            grid=(hkv // npb, batch, nsteps),
            in_specs=in_specs,
            out_specs=pl.BlockSpec((npb * rep, tq, LANE), q_map),
            scratch_shapes=scratch),
        out_shape=jax.ShapeDtypeStruct((hq, T, LANE), BF16),
        compiler_params=_cparams("parallel", "parallel", "arbitrary"),
        name="flash_" + mode,
    )(qi, kj, flags, *args)


def _attn_out_body(*refs, n_branch, heads):
    o_refs = refs[:n_branch]
    if n_branch > 1:
        gl_ref, w_ref, x_ref, g_ref, out_ref = refs[n_branch:]
        gates = _sigmoid(gl_ref[...])
    else:
        w_ref, x_ref, g_ref, out_ref = refs[n_branch:]
    per_head = []
    for h in range(heads):
        if n_branch > 1:
            o = jnp.zeros(o_refs[0].shape[1:], F32)
            for c in range(n_branch):
                col = n_branch * h + c
                o = o + gates[:, col:col + 1] * o_refs[c][h].astype(F32)
            o = o.astype(BF16)
        else:
            o = o_refs[0][h]
        per_head.append(o)
    acc = jnp.dot(jnp.concatenate(per_head, axis=1), w_ref[...], preferred_element_type=F32)
    out_ref[...] = x_ref[...] + g_ref[0] * acc


def _attn_out(os, gl, w_heads, x2, g, seq):
    T, Dm = x2.shape
    heads = w_heads.shape[0]
    tm = 256
    nb = seq // tm
    row = lambda i: (i, 0)
    o_spec = pl.BlockSpec((heads, tm, LANE), lambda i: (0, i, 0))
    in_specs = [o_spec] * len(os)
    args = list(os)
    if len(os) > 1:
        in_specs.append(pl.BlockSpec((tm, LANE), row))
        args.append(gl)
    in_specs += [pl.BlockSpec((heads * LANE, Dm), lambda i: (0, 0)),
                 pl.BlockSpec((tm, Dm), row),
                 pl.BlockSpec((1, 1, Dm), lambda i: (i // nb, 0, 0))]
    args += [w_heads.reshape(heads * LANE, Dm), x2, g]
    return pl.pallas_call(
        functools.partial(_attn_out_body, n_branch=len(os), heads=heads),
        grid=(T // tm,),
        in_specs=in_specs,
        out_specs=pl.BlockSpec((tm, Dm), row),
        out_shape=jax.ShapeDtypeStruct((T, Dm), F32),
        compiler_params=_cparams("parallel"),
        name="attn_out_%d" % len(os),
    )(*args)


def _pad_heads_rows(w, heads, dh):
    w = w.reshape(heads, dh, -1)
    return jnp.pad(w, ((0, 0), (0, LANE - dh), (0, 0))).astype(BF16)


def _pad_heads_cols(w, heads, dh):
    k = w.shape[0]
    w = w.reshape(k, heads, dh)
    return jnp.pad(w, ((0, 0), (0, 0), (0, LANE - dh))).reshape(k, heads * LANE)


def _rope_tables(pos, rot, offset):
    half = rot // 2
    inv_freq = ROPE_THETA ** (-jnp.arange(0, rot, 2, dtype=F32) / rot)
    ang = pos.astype(F32)[:, None] * inv_freq[None, :]
    cos, sin = jnp.cos(ang), jnp.sin(ang)
    n = pos.shape[0]
    c = jnp.ones((n, LANE), F32).at[:, offset:offset + rot].set(jnp.concatenate([cos, cos], axis=1))
    sa = jnp.zeros((n, LANE), F32).at[:, offset:offset + half].set(-sin)
    sb = jnp.zeros((n, LANE), F32).at[:, offset + half:offset + rot].set(sin)
    return c, sa, sb


def _mla_proj_body(x_ref, a_ref, b_ref, win_ref, qlg_ref, kvlg_ref, wuq_ref, wuqs_ref, wuk_ref, wuv_ref,
                   ones_ref, q1_ref, q2_ref, k1_ref, k2_ref, q_ref, k_ref, v_ref):
    h = _normmod(x_ref[...], a_ref[0], b_ref[0]).astype(BF16)
    r = jnp.dot(h, win_ref[...], preferred_element_type=F32)
    lat = MLA_Q_LORA + MLA_KV_LORA
    q_lat = r[:, :MLA_Q_LORA]
    kv_lat = r[:, MLA_Q_LORA:lat]
    kpe = r[:, lat:lat + LANE]
    kpe_swap = r[:, lat + LANE:]
    ql = q_lat * lax.rsqrt(jnp.mean(q_lat * q_lat, axis=-1, keepdims=True) + EPS) * qlg_ref[...]
    kvl = kv_lat * lax.rsqrt(jnp.mean(kv_lat * kv_lat, axis=-1, keepdims=True) + EPS) * kvlg_ref[...]
    ql = ql.astype(BF16)
    kvl = kvl.astype(BF16)
    q = jnp.dot(ql, wuq_ref[...], preferred_element_type=F32)
    q_swap = jnp.dot(ql, wuqs_ref[...], preferred_element_type=F32)
    kn = jnp.dot(kvl, wuk_ref[...], preferred_element_type=F32)
    v = jnp.dot(kvl, wuv_ref[...], preferred_element_type=F32)
    ones = ones_ref[...]
    q1, q2, k1, k2 = q1_ref[...], q2_ref[...], k1_ref[...], k2_ref[...]
    k_rot = kpe_swap * k2

    def inv_rms(t):
        ss = jnp.dot((t * t).astype(BF16), ones, preferred_element_type=F32)
        return lax.rsqrt(ss * (1.0 / MLA_QK) + EPS)

    for hd in range(MLA_HEADS):
        sl = slice(hd * LANE, (hd + 1) * LANE)
        xq = q[:, sl]
        q_ref[hd] = ((xq * q1 + q_swap[:, sl] * q2) * inv_rms(xq)).astype(BF16)
        xk = kn[:, sl] + kpe
        k_ref[hd] = ((xk * k1 + k_rot) * inv_rms(xk)).astype(BF16)
        v_ref[hd] = (v[:, sl] + _one_lane()).astype(BF16)


def _mla_mixer(x2, a, b, g, w_in, q_lat_g, kv_lat_g, w_uq, w_ukv, q_gain, k_gain, w_out, batch, seq):
    T, Dm = x2.shape
    H = MLA_HEADS
    tm = 256
    nb = seq // tm
    scale = MLA_QK ** -0.5 * LOG2E
    lat = MLA_Q_LORA + MLA_KV_LORA
    half = MLA_ROPE // 2

    def swap_rope(t):
        lo, hi = t[..., MLA_NOPE:MLA_NOPE + half], t[..., MLA_NOPE + half:MLA_QK]
        return jnp.concatenate([jnp.zeros_like(t[..., :MLA_NOPE]), hi, lo], axis=-1)

    to_slot = lambda t: jnp.pad(t, [(0, 0)] * (t.ndim - 1) + [(0, LANE - MLA_QK)])
    kpe_w = jnp.concatenate([jnp.zeros((Dm, MLA_NOPE), F32), w_in[:, lat:]], axis=1)
    win_p = jnp.concatenate([w_in[:, :lat], to_slot(kpe_w), to_slot(swap_rope(kpe_w))], axis=1).astype(BF16)
    wuq3 = w_uq.reshape(MLA_Q_LORA, H, MLA_QK)
    wuq_p = to_slot(wuq3).reshape(MLA_Q_LORA, H * LANE).astype(BF16)
    wuqs_p = to_slot(swap_rope(wuq3)).reshape(MLA_Q_LORA, H * LANE).astype(BF16)
    wukv = w_ukv.reshape(MLA_KV_LORA, H, MLA_NOPE + MLA_V)
    wuk_p = _pad_heads_cols(wukv[:, :, :MLA_NOPE].reshape(MLA_KV_LORA, -1), H, MLA_NOPE).astype(BF16)
    wuv_p = _pad_heads_cols(wukv[:, :, MLA_NOPE:].reshape(MLA_KV_LORA, -1), H, MLA_V).astype(BF16)
    c, sa, sb = _rope_tables(jnp.arange(seq), MLA_ROPE, MLA_NOPE)
    qg = q_gain * scale
    q1, q2 = c * to_slot(qg)[None], (sa + sb) * to_slot(swap_rope(qg))[None]
    k1, k2 = c * to_slot(k_gain)[None], (sa + sb) * to_slot(swap_rope(k_gain))[None]
    ones = jnp.ones((LANE, LANE), BF16)
    row = lambda i: (i, 0)
    full = lambda i: (0, 0)
    per_b = lambda i: (i // nb, 0, 0)
    pos = lambda i: (i % nb, 0)
    head_out = pl.BlockSpec((H, tm, LANE), lambda i: (0, i, 0))
    hshape = jax.ShapeDtypeStruct((H, T, LANE), BF16)
    q, k, v = pl.pallas_call(
        _mla_proj_body,
        grid=(T // tm,),
        in_specs=[pl.BlockSpec((tm, Dm), row),
                  pl.BlockSpec((1, 1, Dm), per_b),
                  pl.BlockSpec((1, 1, Dm), per_b),
                  pl.BlockSpec(win_p.shape, full),
                  pl.BlockSpec((1, MLA_Q_LORA), full),
                  pl.BlockSpec((1, MLA_KV_LORA), full),
                  pl.BlockSpec(wuq_p.shape, full),
                  pl.BlockSpec(wuqs_p.shape, full),
                  pl.BlockSpec(wuk_p.shape, full),
                  pl.BlockSpec(wuv_p.shape, full),
                  pl.BlockSpec((LANE, LANE), full),
                  pl.BlockSpec((tm, LANE), pos),
                  pl.BlockSpec((tm, LANE), pos),
                  pl.BlockSpec((tm, LANE), pos),
                  pl.BlockSpec((tm, LANE), pos)],
        out_specs=[head_out, head_out, head_out],
        out_shape=[hshape, hshape, hshape],
        compiler_params=_cparams("parallel"),
        name="mla_proj",
    )(x2, a, b, win_p, q_lat_g.reshape(1, -1), kv_lat_g.reshape(1, -1), wuq_p, wuqs_p, wuk_p, wuv_p,
      ones, q1, q2, k1, k2)
    blk = min(1024, seq)
    o = _flash(q, k, v, None, batch=batch, seq=seq, mode="causal", tq=blk, tk=blk)
    return _attn_out([o], None, _pad_heads_rows(w_out, H, MLA_V), x2, g, seq)


N_KV_STREAMS = 6


def _nsa_proj_body(x_ref, a_ref, b_ref, w_ref, qg_ref, kg_ref, c_ref, sa_ref, sb_ref, blk_ref,
                   q_ref, kv_ref, ks_ref, gl_ref):
    h = _normmod(x_ref[...], a_ref[0], b_ref[0]).astype(BF16)
    r = jnp.dot(h, w_ref[...], preferred_element_type=F32)
    c, sa, sb = c_ref[...], sa_ref[...], sb_ref[...]

    def norm_rot(t, gain):
        t = t * lax.rsqrt(jnp.sum(t * t, axis=-1, keepdims=True) * (1.0 / NSA_DH) + EPS) * gain
        return _rope(t, c, sa, sb, NSA_ROT // 2)

    for hd in range(NSA_HEADS):
        q_ref[hd] = norm_rot(r[:, hd * LANE:(hd + 1) * LANE], qg_ref[...]).astype(BF16)
    base = NSA_HEADS * LANE
    for st in range(N_KV_STREAMS):
        for gi in range(NSA_GROUPS):
            off = base + (st * NSA_GROUPS + gi) * LANE
            t = r[:, off:off + LANE]
            if st == 2:
                t = norm_rot(t, kg_ref[1:2, :])
                ks_ref[gi] = jnp.concatenate([t.astype(BF16), blk_ref[...]], axis=1)
            elif st == 4:
                t = norm_rot(t, kg_ref[2:3, :])
            elif st in (3, 5):
                t = t + _one_lane()
            kv_ref[st * NSA_GROUPS + gi] = t.astype(BF16)
    gl_ref[...] = r[:, base + N_KV_STREAMS * NSA_GROUPS * LANE:]


def _nsa_compress_body(x_ref, pea_ref, peb_ref, w1a_ref, w1b_ref, w2_ref, kg_ref, c_ref, sa_ref, sb_ref,
                       o_ref, *, is_key, n_cmp):
    x = x_ref[0].astype(F32)
    xa = (x + pea_ref[...]).astype(BF16)
    xb = (x + peb_ref[...]).astype(BF16)
    za = jnp.dot(xa, w1a_ref[...], preferred_element_type=F32)
    zb = jnp.dot(xb, w1b_ref[...], preferred_element_type=F32)
    rows = za.shape[0]
    z = _silu(za + pltpu.roll(zb, rows - 1, 0))
    t = jnp.dot(z.astype(BF16), w2_ref[...], preferred_element_type=F32)
    if is_key:
        t = t * lax.rsqrt(jnp.sum(t * t, axis=-1, keepdims=True) * (1.0 / NSA_DH) + EPS) * kg_ref[...]
        t = _rope(t, c_ref[...], sa_ref[...], sb_ref[...], NSA_ROT // 2)
    valid = lax.broadcasted_iota(I32, t.shape, 0) < n_cmp
    o_ref[0] = jnp.where(valid, t, 0.0).astype(BF16)


def _nsa_cmp_select_body(q_ref, kc_ref, vc_ref, oc_ref, sel_ref, p_sc, *, tq, ncp, n_cmp, n_top):
    i = pl.program_id(2)
    t0 = i * tq
    kc = kc_ref[0]
    vc = vc_ref[0]
    rep = q_ref.shape[0]
    q = q_ref[...].reshape(rep * tq, LANE)
    sc = lax.dot_general(q, kc, _NT, preferred_element_type=F32)
    qpos = t0 + lax.broadcasted_iota(I32, (tq, ncp), 0)
    blk = lax.broadcasted_iota(I32, (tq, ncp), 1)
    mask = ((blk * CMP_STRIDE + (CMP_BLOCK - 1)) <= qpos) & (blk < n_cmp)
    sc = jnp.where(mask[None], sc.reshape(rep, tq, ncp), NEG)
    p = jnp.exp2(sc - jnp.max(sc, axis=-1, keepdims=True))
    p = jnp.where(mask[None], p / jnp.sum(p, axis=-1, keepdims=True), 0.0)
    oc = jnp.dot(p.reshape(rep * tq, ncp).astype(BF16), vc, preferred_element_type=F32)
    oc_ref[...] = oc.reshape(rep, tq, LANE).astype(BF16)
    qpos_t = t0 + lax.broadcasted_iota(I32, (ncp, tq), 1)
    blk_t = lax.broadcasted_iota(I32, (ncp, tq), 0)
    mask_t = ((blk_t * CMP_STRIDE + (CMP_BLOCK - 1)) <= qpos_t) & (blk_t < n_cmp)
    psum = jnp.zeros((ncp, tq), F32)
    for r in range(rep):
        st = lax.dot_general(kc, q_ref[r], _NT, preferred_element_type=F32)
        st = jnp.where(mask_t, st, NEG)
        pt = jnp.exp2(st - jnp.max(st, axis=0, keepdims=True))
        psum = psum + jnp.where(mask_t, pt / jnp.sum(pt, axis=0, keepdims=True), 0.0)
    p_sc[...] = jnp.zeros(p_sc.shape, F32)
    per = SLC_BLOCK // CMP_STRIDE
    slabs = []
    for sb in range(tq // LANE):
        p_sc[sb, 8:8 + ncp, :] = psum[:, sb * LANE:(sb + 1) * LANE]
        part = p_sc[sb, pl.ds(7, MAX_SLC, stride=per), :]
        for k in range(1, per + 1):
            part = part + p_sc[sb, pl.ds(7 + k, MAX_SLC, stride=per), :]
        slabs.append(part)
    imp = jnp.concatenate(slabs, axis=1)
    jb = lax.broadcasted_iota(I32, (MAX_SLC, tq), 0)
    qp = t0 + lax.broadcasted_iota(I32, (MAX_SLC, tq), 1)
    imp = jnp.where(jb * SLC_BLOCK <= qp, imp, -BIG)
    imp = jnp.where((jb == 0) | (jb == jnp.right_shift(qp, 6)), BIG, imp)
    jbf = jb.astype(F32)
    sel_t = jnp.full((MAX_SLC, tq), SEL_OFF, F32)
    for _ in range(n_top):
        top = jnp.max(imp, axis=0, keepdims=True)
        first = jnp.min(jnp.where(imp == top, jbf, float(MAX_SLC)), axis=0, keepdims=True)
        hit = jbf == first
        sel_t = jnp.where(hit, 0.0, sel_t)
        imp = jnp.where(hit, -jnp.inf, imp)
    sel_ref[0] = sel_t.T.astype(BF16)


def _nsa_mixer(x2, a, b, g, w_in, w_cmp1, w_cmp2, cmp_pos, q_gain, k_gain, w_out, batch, seq):
    T, Dm = x2.shape
    H, G, dh = NSA_HEADS, NSA_GROUPS, NSA_DH
    scale = dh ** -0.5 * LOG2E
    n_cmp = seq // CMP_STRIDE - 1
    ncp = seq // CMP_STRIDE
    n_slc = seq // SLC_BLOCK
    assert n_slc <= MAX_SLC and ncp <= MAX_SLC * (SLC_BLOCK // CMP_STRIDE)
    n_top = min(SLC_TOP_N, n_slc)
    tm = 256
    nb = seq // tm
    q_cols = _pad_heads_cols(w_in[:, :H * dh], H, dh)
    kv_cols = _pad_heads_cols(w_in[:, H * dh:H * dh + N_KV_STREAMS * G * dh], N_KV_STREAMS * G, dh)
    gl_cols = jnp.pad(w_in[:, H * dh + N_KV_STREAMS * G * dh:], ((0, 0), (0, LANE - 3 * H)))
    w_p = jnp.concatenate([q_cols, kv_cols, gl_cols], axis=1).astype(BF16)
    pad_gain = lambda v: jnp.pad(v, ((0, 0), (0, LANE - dh)))
    c, sa, sb = _rope_tables(jnp.arange(seq), NSA_ROT, 0)
    row = lambda i: (i, 0)
    full = lambda i: (0, 0)
    per_b = lambda i: (i // nb, 0, 0)
    pos = lambda i: (i % nb, 0)
    blk_onehot = (jnp.arange(seq)[:, None] // SLC_BLOCK == jnp.arange(MAX_SLC)[None, :]).astype(BF16)
    q, kv, ks, gl = pl.pallas_call(
        _nsa_proj_body,
        grid=(T // tm,),
        in_specs=[pl.BlockSpec((tm, Dm), row),
                  pl.BlockSpec((1, 1, Dm), per_b),
                  pl.BlockSpec((1, 1, Dm), per_b),
                  pl.BlockSpec(w_p.shape, full),
                  pl.BlockSpec((1, LANE), full),
                  pl.BlockSpec((3, LANE), full),
                  pl.BlockSpec((tm, LANE), pos),
                  pl.BlockSpec((tm, LANE), pos),
                  pl.BlockSpec((tm, LANE), pos),
                  pl.BlockSpec((tm, MAX_SLC), pos)],
        out_specs=[pl.BlockSpec((H, tm, LANE), lambda i: (0, i, 0)),
                   pl.BlockSpec((N_KV_STREAMS * G, tm, LANE), lambda i: (0, i, 0)),
                   pl.BlockSpec((G, tm, LANE + MAX_SLC), lambda i: (0, i, 0)),
                   pl.BlockSpec((tm, LANE), row)],
        out_shape=[jax.ShapeDtypeStruct((H, T, LANE), BF16),
                   jax.ShapeDtypeStruct((N_KV_STREAMS * G, T, LANE), BF16),
                   jax.ShapeDtypeStruct((G, T, LANE + MAX_SLC), BF16),
                   jax.ShapeDtypeStruct((T, LANE), F32)],
        compiler_params=_cparams("parallel"),
        name="nsa_proj",
    )(x2, a, b, w_p, pad_gain(q_gain.reshape(1, dh) * scale), pad_gain(k_gain), c, sa, sb, blk_onehot)
    kc_raw, vc_raw, _, vs, kw, vw = [kv[s * G:(s + 1) * G] for s in range(N_KV_STREAMS)]

    cmp_end = jnp.arange(ncp) * CMP_STRIDE + (CMP_BLOCK - 1)
    cc, csa, csb = _rope_tables(cmp_end, NSA_ROT, 0)
    kdim = CMP_STRIDE * LANE

    def compress(raw, w1, w2, pe, is_key):
        x16 = raw.reshape(G, T // CMP_STRIDE, kdim)
        w1p = jnp.pad(w1, ((0, 0), (0, LANE - dh), (0, LANE - dh)))
        w1a = w1p[:CMP_STRIDE].reshape(kdim, LANE).astype(BF16)
        w1b = w1p[CMP_STRIDE:].reshape(kdim, LANE).astype(BF16)
        pep = jnp.pad(pe, ((0, 0), (0, LANE - dh)))
        pea = pep[:CMP_STRIDE].reshape(1, kdim)
        peb = pep[CMP_STRIDE:].reshape(1, kdim)
        w2p = jnp.pad(w2, ((0, LANE - dh), (0, LANE - dh))).astype(BF16)
        const = lambda gi, bi: (0, 0)
        return pl.pallas_call(
            functools.partial(_nsa_compress_body, is_key=is_key, n_cmp=n_cmp),
            grid=(G, batch),
            in_specs=[pl.BlockSpec((1, ncp, kdim), lambda gi, bi: (gi, bi, 0)),
                      pl.BlockSpec((1, kdim), const),
                      pl.BlockSpec((1, kdim), const),
                      pl.BlockSpec((kdim, LANE), const),
                      pl.BlockSpec((kdim, LANE), const),
                      pl.BlockSpec((LANE, LANE), const),
                      pl.BlockSpec((1, LANE), const),
                      pl.BlockSpec((ncp, LANE), const),
                      pl.BlockSpec((ncp, LANE), const),
                      pl.BlockSpec((ncp, LANE), const)],
            out_specs=pl.BlockSpec((1, ncp, LANE), lambda gi, bi: (gi, bi, 0)),
            out_shape=jax.ShapeDtypeStruct((G, batch * ncp, LANE), BF16),
            compiler_params=_cparams("parallel", "parallel"),
            name="nsa_compress_" + ("k" if is_key else "v"),
        )(x16, pea, peb, w1a, w1b, w2p, pad_gain(k_gain)[0:1], cc, csa, csb)

    kc = compress(kc_raw, w_cmp1[0], w_cmp2[0], cmp_pos[0], True)
    vc = compress(vc_raw, w_cmp1[1], w_cmp2[1], cmp_pos[1], False)

    tq = 256
    nq = seq // tq
    q_map = lambda gi, bi, i: (gi, bi * nq + i, 0)
    c_map = lambda gi, bi, i: (gi, bi, 0)
    o_c, sel = pl.pallas_call(
        functools.partial(_nsa_cmp_select_body, tq=tq, ncp=ncp, n_cmp=n_cmp, n_top=n_top),
        grid=(G, batch, nq),
        in_specs=[pl.BlockSpec((NSA_REP, tq, LANE), q_map),
                  pl.BlockSpec((1, ncp, LANE), c_map),
                  pl.BlockSpec((1, ncp, LANE), c_map)],
        out_specs=[pl.BlockSpec((NSA_REP, tq, LANE), q_map),
                   pl.BlockSpec((1, tq, LANE), q_map)],
        out_shape=[jax.ShapeDtypeStruct((H, T, LANE), BF16),
                   jax.ShapeDtypeStruct((G, T, LANE), BF16)],
        scratch_shapes=[pltpu.VMEM((tq // LANE, 8 + MAX_SLC * (SLC_BLOCK // CMP_STRIDE), LANE), F32)],
        compiler_params=_cparams("parallel", "parallel", "parallel"),
        name="nsa_cmp_select",
    )(q, kc, vc)

    o_s = _flash(q, ks, vs, sel, batch=batch, seq=seq, mode="select", tq=256, tk=min(1024, seq))
    o_w = _flash(q, kw, vw, None, batch=batch, seq=seq, mode="window", tq=256, tk=WINDOW)
    return _attn_out([o_c, o_s, o_w], gl, _pad_heads_rows(w_out, H, dh), x2, g, seq)


def _router_body(x_ref, a_ref, b_ref, rwt_ref, rb_ref, tri_ref, e_ref, w_ref, rank_ref, cnt_ref,
                 carry_sc, *, tm):
    i = pl.program_id(0)

    @pl.when(i == 0)
    def _():
        carry_sc[...] = jnp.zeros(carry_sc.shape, F32)

    h = _normmod(x_ref[...], a_ref[0], b_ref[0])
    logits = lax.dot_general(rwt_ref[...], h, _NT, preferred_element_type=F32, precision=HIGHEST)
    scores = _sigmoid(logits)
    biased = scores + rb_ref[...]
    ng, per = N_GROUPS, EXPERTS_PER_GROUP
    row = lambda arr, r: arr[r:r + 1, :]
    gsel = jnp.zeros((1, tm), I32)
    best = None
    for gi in range(ng):
        v = [row(biased, gi * per + k) for k in range(per)]
        top2 = None
        for p in range(per):
            for q in range(p + 1, per):
                s = v[p] + v[q]
                top2 = s if top2 is None else jnp.maximum(top2, s)
        if best is None:
            best = top2
        else:
            better = top2 > best
            gsel = jnp.where(better, gi, gsel)
            best = jnp.where(better, top2, best)
    cb, cs = [], []
    for k in range(per):
        b_k = row(biased, k)
        s_k = row(scores, k)
        for gi in range(1, ng):
            hit = gsel == gi
            b_k = jnp.where(hit, row(biased, gi * per + k), b_k)
            s_k = jnp.where(hit, row(scores, gi * per + k), s_k)
        cb.append(b_k)
        cs.append(s_k)

    def argmax_first(vals):
        idx = jnp.zeros((1, tm), I32)
        top = vals[0]
        for k in range(1, per):
            better = vals[k] > top
            idx = jnp.where(better, k, idx)
            top = jnp.where(better, vals[k], top)
        return idx

    def pick(vals, idx):
        out = vals[0]
        for k in range(1, per):
            out = jnp.where(idx == k, vals[k], out)
        return out

    i1 = argmax_first(cb)
    i2 = argmax_first([jnp.where(i1 == k, -jnp.inf, cb[k]) for k in range(per)])
    w1 = pick(cs, i1)
    w2 = pick(cs, i2)
    tot = w1 + w2
    e1 = gsel * per + i1
    e2 = gsel * per + i2
    eid = lax.broadcasted_iota(I32, (N_EXPERTS, tm), 0)
    hot = (eid == e1) | (eid == e2)
    onehot = jnp.where(hot, 1.0, 0.0)
    before = carry_sc[...] + jnp.dot(onehot.astype(BF16), tri_ref[...], preferred_element_type=F32)
    r1 = jnp.sum(jnp.where(eid == e1, before, 0.0), axis=0, keepdims=True)
    r2 = jnp.sum(jnp.where(eid == e2, before, 0.0), axis=0, keepdims=True)
    carry = carry_sc[...] + jnp.sum(onehot, axis=1, keepdims=True)
    carry_sc[...] = carry
    cnt_ref[...] = jnp.broadcast_to(carry, cnt_ref.shape)
    zi = jnp.zeros((6, tm), I32)
    e_ref[...] = jnp.concatenate([e1, e2, zi], axis=0)
    rank_ref[...] = jnp.concatenate([r1.astype(I32), r2.astype(I32), zi], axis=0)
    w_ref[...] = jnp.concatenate([w1 / tot, w2 / tot, jnp.zeros((6, tm), F32)], axis=0)


ROW_SUB = D_MODEL // LANE


def _row_tile(r):
    return pl.ds(pl.multiple_of(r * ROW_SUB, ROW_SUB), ROW_SUB)


def _to_row_tiles(ref, val):
    n = val.shape[0]
    for s in range(ROW_SUB):
        ref[pl.ds(s, n, stride=ROW_SUB), :] = val[:, s * LANE:(s + 1) * LANE]


def _from_row_tiles(ref, n, s):
    return ref[pl.ds(s, n, stride=ROW_SUB), :]


def _moe_dispatch_body(starts_ref, counts_ref, padded_ref, nv_ref, d0_ref, d1_ref,
                       x_ref, a_ref, b_ref, xs_hbm, hbuf0, hbuf1, zbuf, sems, zsem,
                       *, tm, tg, n_tiles, nt):
    i = pl.program_id(0)

    @pl.when(i == 0)
    def _():
        zbuf[...] = jnp.zeros(zbuf.shape, F32)
        zrow = zbuf.at[pl.ds(0, ROW_SUB)]
        for ex in range(N_EXPERTS):
            lo = starts_ref[ex] + counts_ref[ex]
            hi = starts_ref[ex] + padded_ref[ex]

            def fill(r, carry):
                pltpu.make_async_copy(zrow, xs_hbm.at[_row_tile(r)], zsem).start()
                return carry

            lax.fori_loop(lo, hi, fill, 0)

            def drain_fill(r, carry):
                pltpu.make_async_copy(zrow, xs_hbm.at[_row_tile(0)], zsem).wait()
                return carry

            lax.fori_loop(lo, hi, drain_fill, 0)

        rows_per_tile = tg * ROW_SUB

        def fill_tile(t, carry):
            dst = xs_hbm.at[pl.ds(pl.multiple_of(t * rows_per_tile, rows_per_tile), rows_per_tile)]
            pltpu.make_async_copy(zbuf, dst, zsem).start()
            return carry

        lax.fori_loop(nv_ref[0], n_tiles, fill_tile, 0)

        def drain_tile(t, carry):
            pltpu.make_async_copy(zbuf, xs_hbm.at[pl.ds(0, rows_per_tile)], zsem).wait()
            return carry

        lax.fori_loop(nv_ref[0], n_tiles, drain_tile, 0)

    h = _normmod(x_ref[...], a_ref[0], b_ref[0])

    def scatter_from(hbuf, sem, other_buf, other_sem):
        _to_row_tiles(hbuf, h)

        def row_copy(buf, sm, r, slot):
            return pltpu.make_async_copy(buf.at[_row_tile(r)], xs_hbm.at[_row_tile(slot)], sm)

        def issue(r, carry):
            row_copy(hbuf, sem, r, d0_ref[r]).start()
            row_copy(hbuf, sem, r, d1_ref[r]).start()
            return carry

        lax.fori_loop(0, tm, issue, 0, unroll=8)

        def drain(buf, sm):
            def body(r, carry):
                row_copy(buf, sm, r, 0).wait()
                row_copy(buf, sm, r, 0).wait()
                return carry

            lax.fori_loop(0, tm, body, 0, unroll=8)

        pl.when(i > 0)(lambda: drain(other_buf, other_sem))
        pl.when(i == nt - 1)(lambda: drain(hbuf, sem))

    pl.when(i % 2 == 0)(lambda: scatter_from(hbuf0, sems.at[0], hbuf1, sems.at[1]))
    pl.when(i % 2 == 1)(lambda: scatter_from(hbuf1, sems.at[1], hbuf0, sems.at[0]))


def _moe_expert_body(te_ref, nv_ref, xs_ref, win_ref, wout_ref, y_ref, *, tg):
    i = pl.program_id(0)

    @pl.when(i < nv_ref[0])
    def _():
        x = jnp.concatenate([_from_row_tiles(xs_ref, tg, s) for s in range(ROW_SUB)], axis=1).astype(BF16)
        gu = jnp.dot(x, win_ref[0], preferred_element_type=F32)
        act = (_silu(gu[:, :D_EXPERT]) * gu[:, D_EXPERT:]).astype(BF16)
        _to_row_tiles(y_ref, jnp.dot(act, wout_ref[0], preferred_element_type=F32))

    @pl.when(i >= nv_ref[0])
    def _():
        y_ref[...] = jnp.zeros(y_ref.shape, F32)


def _moe_combine_body(d0c_ref, d1c_ref, d0n_ref, d1n_ref, y_hbm, x_ref, w_ref, g_ref, o_ref,
                      ya0, ya1, yb0, yb1, sems, *, tm, nt):
    i = pl.program_id(0)

    def row_copy(slot, buf, sem, r):
        return pltpu.make_async_copy(y_hbm.at[_row_tile(slot)], buf.at[_row_tile(r)], sem)

    def issue(d0_ref, d1_ref, bufs, sem):
        def body(r, carry):
            row_copy(d0_ref[r], bufs[0], sem, r).start()
            row_copy(d1_ref[r], bufs[1], sem, r).start()
            return carry

        lax.fori_loop(0, tm, body, 0, unroll=8)

    def finish(bufs, sem):
        def body(r, carry):
            row_copy(0, bufs[0], sem, r).wait()
            row_copy(0, bufs[1], sem, r).wait()
            return carry

        lax.fori_loop(0, tm, body, 0, unroll=8)
        w = w_ref[...]
        w0, w1 = w[:, 0:1], w[:, 1:2]
        gate = g_ref[0]
        for s in range(ROW_SUB):
            sl = slice(s * LANE, (s + 1) * LANE)
            y = w0 * _from_row_tiles(bufs[0], tm, s) + w1 * _from_row_tiles(bufs[1], tm, s)
            o_ref[:, sl] = x_ref[:, sl] + gate[:, sl] * y

    set_a, set_b = (ya0, ya1), (yb0, yb1)
    pl.when(i == 0)(lambda: issue(d0c_ref, d1c_ref, set_a, sems.at[0]))

    def even():
        pl.when(i + 1 < nt)(lambda: issue(d0n_ref, d1n_ref, set_b, sems.at[1]))
        finish(set_a, sems.at[0])

    def odd():
        pl.when(i + 1 < nt)(lambda: issue(d0n_ref, d1n_ref, set_a, sems.at[0]))
        finish(set_b, sems.at[1])

    pl.when(i % 2 == 0)(even)
    pl.when(i % 2 == 1)(odd)


def _grouped_moe(x2, a, b, g, router_w, router_bias, w_in, w_out, seq):
    T, Dm = x2.shape
    E = N_EXPERTS
    tm = 512
    nb = seq // tm
    nt = T // tm
    row = lambda i: (i, 0)
    full = lambda i: (0, 0)
    per_b = lambda i: (i // nb, 0, 0)
    lanes = lambda i: (0, i)
    tri = jnp.asarray(np.triu(np.ones((tm, tm), np.float32), 1)).astype(BF16)
    e, w, rank, cnt = pl.pallas_call(
        functools.partial(_router_body, tm=tm),
        grid=(nt,),
        in_specs=[pl.BlockSpec((tm, Dm), row),
                  pl.BlockSpec((1, 1, Dm), per_b),
                  pl.BlockSpec((1, 1, Dm), per_b),
                  pl.BlockSpec((E, Dm), full),
                  pl.BlockSpec((E, 1), full),
                  pl.BlockSpec((tm, tm), full)],
        out_specs=[pl.BlockSpec((8, tm), lanes),
                   pl.BlockSpec((8, tm), lanes),
                   pl.BlockSpec((8, tm), lanes),
                   pl.BlockSpec((E, LANE), full)],
        out_shape=[jax.ShapeDtypeStruct((8, T), I32),
                   jax.ShapeDtypeStruct((8, T), F32),
                   jax.ShapeDtypeStruct((8, T), I32),
                   jax.ShapeDtypeStruct((E, LANE), F32)],
        scratch_shapes=[pltpu.VMEM((E, 1), F32)],
        compiler_params=_cparams("arbitrary"),
        name="moe_router",
    )(x2, a, b, router_w.T, router_bias.reshape(E, 1), tri)

    tg = MOE_TILE
    n_tiles = (2 * T) // tg + E
    n_slots = n_tiles * tg
    counts = cnt[:, 0].astype(I32)
    padded = ((counts + tg - 1) // tg) * tg
    ends = jnp.cumsum(padded)
    starts = ends - padded
    tile_start = jnp.arange(n_tiles, dtype=I32) * tg
    tile_expert = jnp.minimum(jnp.sum(tile_start[:, None] >= ends[None, :], axis=1), E - 1).astype(I32)
    n_valid = (ends[-1] // tg).astype(I32).reshape(1)

    tc = 256
    ntc = T // tc
    nbc = seq // tc
    smem_cur = pl.BlockSpec((tc,), lambda i, *_: (i,), memory_space=pltpu.SMEM)
    smem_nxt = pl.BlockSpec((tc,), lambda i, *_: (jnp.minimum(i + 1, ntc - 1),), memory_space=pltpu.SMEM)
    tile_rows = tc * ROW_SUB
    seg_start = functools.reduce(lambda acc, k: jnp.where(e[:2] == k, starts[k], acc), range(E),
                                 jnp.zeros_like(e[:2]))
    dest = seg_start + rank[:2]
    d0, d1 = dest[0], dest[1]
    xs = pl.pallas_call(
        functools.partial(_moe_dispatch_body, tm=tc, tg=tg, n_tiles=n_tiles, nt=ntc),
        grid_spec=pltpu.PrefetchScalarGridSpec(
            num_scalar_prefetch=4,
            grid=(ntc,),
            in_specs=[smem_cur, smem_cur,
                      pl.BlockSpec((tc, Dm), lambda i, *_: (i, 0)),
                      pl.BlockSpec((1, 1, Dm), lambda i, *_: (i // nbc, 0, 0)),
                      pl.BlockSpec((1, 1, Dm), lambda i, *_: (i // nbc, 0, 0))],
            out_specs=pl.BlockSpec(memory_space=pl.ANY),
            scratch_shapes=[pltpu.VMEM((tile_rows, LANE), F32), pltpu.VMEM((tile_rows, LANE), F32),
                            pltpu.VMEM((tg * ROW_SUB, LANE), F32),
                            pltpu.SemaphoreType.DMA((2,)), pltpu.SemaphoreType.DMA(())]),
        out_shape=jax.ShapeDtypeStruct((n_slots * ROW_SUB, LANE), F32),
        compiler_params=_cparams("arbitrary"),
        name="moe_dispatch",
    )(starts.astype(I32), counts, padded, n_valid, d0, d1, x2, a, b)

    last_tile = lambda i, te, nv: (jnp.minimum(i, nv[0] - 1), 0)
    y = pl.pallas_call(
        functools.partial(_moe_expert_body, tg=tg),
        grid_spec=pltpu.PrefetchScalarGridSpec(
            num_scalar_prefetch=2,
            grid=(n_tiles,),
            in_specs=[pl.BlockSpec((tg * ROW_SUB, LANE), last_tile),
                      pl.BlockSpec((1, Dm, 2 * D_EXPERT), lambda i, te, nv: (te[i], 0, 0)),
                      pl.BlockSpec((1, D_EXPERT, Dm), lambda i, te, nv: (te[i], 0, 0))],
            out_specs=pl.BlockSpec((tg * ROW_SUB, LANE), lambda i, te, nv: (i, 0))),
        out_shape=jax.ShapeDtypeStruct((n_slots * ROW_SUB, LANE), F32),
        compiler_params=_cparams("arbitrary"),
        name="moe_experts",
    )(tile_expert, n_valid, xs, w_in.astype(BF16), w_out.astype(BF16))

    return pl.pallas_call(
        functools.partial(_moe_combine_body, tm=tc, nt=ntc),
        grid=(ntc,),
        in_specs=[smem_cur, smem_cur, smem_nxt, smem_nxt,
                  pl.BlockSpec(memory_space=pl.ANY),
                  pl.BlockSpec((tc, Dm), row),
                  pl.BlockSpec((tc, 2), row),
                  pl.BlockSpec((1, 1, Dm), lambda i: (i // nbc, 0, 0))],
        out_specs=pl.BlockSpec((tc, Dm), row),
        out_shape=jax.ShapeDtypeStruct((T, Dm), F32),
        scratch_shapes=[pltpu.VMEM((tile_rows, LANE), F32)] * 4 + [pltpu.SemaphoreType.DMA((2,))],
        compiler_params=_cparams("arbitrary"),
        name="moe_combine",
    )(d0, d1, d0, d1, y, x2, w[:2].T, g)


def kernel(x, c, norm_mix_g, norm_ffn_g, w_ada, b_ada, conv_w_pw1, conv_b_pw1, conv_w_dw, conv_b_dw, conv_ln_g, conv_ln_b, conv_w_pw2, conv_b_pw2, nsa_w_in, nsa_w_cmp1, nsa_w_cmp2, nsa_cmp_pos, nsa_q_gain, nsa_k_gain, nsa_w_out, mla_w_in, mla_q_lat_g, mla_kv_lat_g, mla_w_uq, mla_w_ukv, mla_q_gain, mla_k_gain, mla_w_out, router_w, router_bias, moe_w_in, moe_w_out):
    B, S, Dm = x.shape
    depth = w_ada.shape[0]
    mods = _ada(c, w_ada, b_ada)
    x2 = x.reshape(B * S, Dm)
    for i in range(depth):
        sh1, sc1, g1, sh2, sc2, g2 = [m.reshape(B, 1, Dm) for m in jnp.split(mods[i], 6, axis=-1)]
        a1 = norm_mix_g[i] * (1.0 + sc1)
        kind, j = i % 3, i // 3
        if kind == 0:
            x2 = _conv_mixer(x2, a1, sh1, g1, conv_w_pw1[j], conv_b_pw1[j], conv_w_dw[j], conv_b_dw[j],
                             conv_ln_g[j], conv_ln_b[j], conv_w_pw2[j], conv_b_pw2[j], S)
        elif kind == 1:
            x2 = _nsa_mixer(x2, a1, sh1, g1, nsa_w_in[j], nsa_w_cmp1[j], nsa_w_cmp2[j], nsa_cmp_pos[j],
                            nsa_q_gain[j], nsa_k_gain[j], nsa_w_out[j], B, S)
        else:
            x2 = _mla_mixer(x2, a1, sh1, g1, mla_w_in[j], mla_q_lat_g[j], mla_kv_lat_g[j], mla_w_uq[j],
                            mla_w_ukv[j], mla_q_gain[j], mla_k_gain[j], mla_w_out[j], B, S)
        a2 = norm_ffn_g[i] * (1.0 + sc2)
        x2 = _grouped_moe(x2, a2, sh2, g2, router_w, router_bias, moe_w_in[i], moe_w_out[i], S)
    return x2.reshape(B, S, Dm)
```

```python
import functools
import math

import numpy as np
import jax
import jax.numpy as jnp
from jax import lax
from jax.experimental import pallas as pl
from jax.experimental.pallas import tpu as pltpu

F32 = jnp.float32
BF16 = jnp.bfloat16
I32 = jnp.int32
HIGHEST = lax.Precision.HIGHEST

EPS = 1e-6
NEG = -1e30
BIG = 1e30
ROPE_THETA = 500000.0
LANE = 128
VMEM_LIMIT = 56 * 1024 * 1024

D_MODEL = 1024
CONV_WIDTH = 31
CONV_HALO = 32

NSA_HEADS = 16
NSA_GROUPS = 4
NSA_REP = NSA_HEADS // NSA_GROUPS
NSA_DH = 64
NSA_ROT = 16
CMP_BLOCK = 32
CMP_STRIDE = 16
SLC_BLOCK = 64
SLC_TOP_N = 16
WINDOW = 512
MAX_SLC = 128

MLA_HEADS = 16
MLA_Q_LORA = 384
MLA_KV_LORA = 256
MLA_NOPE = 64
MLA_ROPE = 32
MLA_V = 64
MLA_QK = MLA_NOPE + MLA_ROPE

N_EXPERTS = 16
N_GROUPS = 4
EXPERTS_PER_GROUP = 4
D_EXPERT = 512
MOE_TILE = 512

_NT = (((1,), (1,)), ((), ()))


def _cparams(*sem):
    return pltpu.CompilerParams(dimension_semantics=sem, vmem_limit_bytes=VMEM_LIMIT)


def _sigmoid(x):
    return 1.0 / (1.0 + jnp.exp(-x))


def _silu(x):
    return x * _sigmoid(x)


def _normmod(x, a, b):
    ms = jnp.mean(x * x, axis=-1, keepdims=True)
    return x * lax.rsqrt(ms + EPS) * a + b


def _rope(x, c, sa, sb, half):
    n = x.shape[-1]
    return x * c + pltpu.roll(x, n - half, 1) * sa + pltpu.roll(x, half, 1) * sb


def _ada_body(c_ref, w_ref, b_ref, o_ref):
    c = c_ref[...]
    o_ref[0] = jnp.dot(_silu(c), w_ref[0], preferred_element_type=F32, precision=HIGHEST) + b_ref[0]


def _ada(c, w_ada, b_ada):
    B = c.shape[0]
    L, Dm, N = w_ada.shape
    Bp = -(-B // 8) * 8
    cp = jnp.pad(c, ((0, Bp - B), (0, 0)))
    tn = 1536
    out = pl.pallas_call(
        _ada_body,
        grid=(L, N // tn),
        in_specs=[pl.BlockSpec((Bp, Dm), lambda l, j: (0, 0)),
                  pl.BlockSpec((1, Dm, tn), lambda l, j: (l, 0, j)),
                  pl.BlockSpec((1, 1, tn), lambda l, j: (l, 0, j))],
        out_specs=pl.BlockSpec((1, Bp, tn), lambda l, j: (l, 0, j)),
        out_shape=jax.ShapeDtypeStruct((L, Bp, N), F32),
        compiler_params=_cparams("arbitrary", "arbitrary"),
        name="adaln",
    )(cp, w_ada, b_ada.reshape(L, 1, N))
    return out[:, :B]


def _conv_pw1_body(x_ref, a_ref, b_ref, w_ref, bias_ref, u_ref):
    h = _normmod(x_ref[...], a_ref[0], b_ref[0]).astype(BF16)
    r = jnp.dot(h, w_ref[...], preferred_element_type=F32) + bias_ref[...]
    d = u_ref.shape[-1]
    u_ref[...] = r[:, :d] * _sigmoid(r[:, d:])


def _conv_dw_body(u_ref, uh_ref, wdw_ref, bdw_ref, lng_ref, lnb_ref, w2_ref, b2_ref, x_ref, g_ref,
                  o_ref, ext_ref, acc_ref, *, tm, seq):
    i = pl.program_id(0)
    at_seq_start = (i * tm) % seq == 0
    ext_ref[0:CONV_HALO, :] = jnp.where(at_seq_start, 0.0, uh_ref[...])
    ext_ref[CONV_HALO:, :] = u_ref[...]
    base = CONV_HALO - (CONV_WIDTH - 1)
    rc = 64
    for lc in range(u_ref.shape[-1] // LANE):
        ls = slice(lc * LANE, (lc + 1) * LANE)
        wl = wdw_ref[:, ls]
        bl = bdw_ref[:, ls]
        for r0 in range(0, tm, rc):
            e = ext_ref[r0:r0 + rc + CONV_HALO, ls]
            acc = bl
            for s in range(8):
                rows = rc if s == 0 else rc + 8
                p = None
                for a in range((base + CONV_WIDTH - 1) // 8 + 1):
                    k = 8 * a + s - base
                    if 0 <= k < CONV_WIDTH and 8 * a + rows <= rc + CONV_HALO:
                        term = wl[k:k + 1, :] * e[8 * a:8 * a + rows, :]
                        p = term if p is None else p + term
                acc = acc + (p if s == 0 else p[s:s + rc, :])
            acc_ref[r0:r0 + rc, ls] = acc
    acc = acc_ref[...]
    mu = jnp.mean(acc, axis=-1, keepdims=True)
    dlt = acc - mu
    var = jnp.mean(dlt * dlt, axis=-1, keepdims=True)
    y = dlt * lax.rsqrt(var + EPS) * lng_ref[...] + lnb_ref[...]
    z = _silu(y).astype(BF16)
    out = jnp.dot(z, w2_ref[...], preferred_element_type=F32) + b2_ref[...]
    o_ref[...] = x_ref[...] + g_ref[0] * out


def _conv_mixer(x2, a, b, g, w_pw1, b_pw1, w_dw, b_dw, ln_g, ln_b, w_pw2, b_pw2, seq):
    T, Dm = x2.shape
    tm = 512
    nb = seq // tm
    row = lambda i: (i, 0)
    per_b = lambda i: (i // nb, 0, 0)
    full = lambda i: (0, 0)
    u = pl.pallas_call(
        _conv_pw1_body,
        grid=(T // tm,),
        in_specs=[pl.BlockSpec((tm, Dm), row),
                  pl.BlockSpec((1, 1, Dm), per_b),
                  pl.BlockSpec((1, 1, Dm), per_b),
                  pl.BlockSpec((Dm, 2 * Dm), full),
                  pl.BlockSpec((1, 2 * Dm), full)],
        out_specs=pl.BlockSpec((tm, Dm), row),
        out_shape=jax.ShapeDtypeStruct((T, Dm), F32),
        compiler_params=_cparams("parallel"),
        name="conv_pw1_glu",
    )(x2, a, b, w_pw1.astype(BF16), b_pw1.reshape(1, -1))
    hb = tm // CONV_HALO
    wdw = jnp.pad(w_dw, ((0, CONV_HALO - CONV_WIDTH), (0, 0)))
    vec = lambda v: v.reshape(1, -1)
    return pl.pallas_call(
        functools.partial(_conv_dw_body, tm=tm, seq=seq),
        grid=(T // tm,),
        in_specs=[pl.BlockSpec((tm, Dm), row),
                  pl.BlockSpec((CONV_HALO, Dm), lambda i: (jnp.maximum(i * hb - 1, 0), 0)),
                  pl.BlockSpec((CONV_HALO, Dm), full),
                  pl.BlockSpec((1, Dm), full),
                  pl.BlockSpec((1, Dm), full),
                  pl.BlockSpec((1, Dm), full),
                  pl.BlockSpec((Dm, Dm), full),
                  pl.BlockSpec((1, Dm), full),
                  pl.BlockSpec((tm, Dm), row),
                  pl.BlockSpec((1, 1, Dm), per_b)],
        out_specs=pl.BlockSpec((tm, Dm), row),
        out_shape=jax.ShapeDtypeStruct((T, Dm), F32),
        scratch_shapes=[pltpu.VMEM((tm + CONV_HALO, Dm), F32), pltpu.VMEM((tm, Dm), F32)],
        compiler_params=_cparams("parallel"),
        name="conv_dw_ln_pw2",
    )(u, u, wdw, vec(b_dw), vec(ln_g), vec(ln_b), w_pw2.astype(BF16), vec(b_pw2), x2, g)


FLASH_FIRST, FLASH_LAST, FLASH_MASKED = 1, 2, 4
FLASH_CHUNK = 16
SEL_OFF = -(2.0 ** 100)
ONE_LANE = 64
LOG2E = 1.4426950408889634
FLASH_NPROB = 4


def _one_lane():
    return jnp.where(lax.broadcasted_iota(I32, (1, LANE), 1) == ONE_LANE, 1.0, 0.0)


def _flash_body(qi_ref, kj_ref, flag_ref, *refs, mode, rep, tq, tk, nprob):
    if mode == "select":
        q_ref, k_ref, v_ref, selb_ref, o_ref, qa_sc, s_sc, p_sc, mb_sc, al_sc, acc_sc = refs
    else:
        q_ref, k_ref, v_ref, o_ref, s_sc, p_sc, mb_sc, al_sc, acc_sc = refs
    step = pl.program_id(2)
    qi = qi_ref[step]
    kj = kj_ref[step]
    flag = flag_ref[step]
    rows = rep * tq
    ch = FLASH_CHUNK
    nl = tk // LANE
    lane_fold = lambda t, op: functools.reduce(op, [t[:, i * LANE:(i + 1) * LANE] for i in range(nl)])

    @pl.when((flag & FLASH_FIRST) != 0)
    def _():
        mb_sc[...] = jnp.full(mb_sc.shape, NEG, F32)
        acc_sc[...] = jnp.zeros(acc_sc.shape, F32)
        if mode == "select":
            for pr in range(nprob):
                qa_sc[pr, :, :LANE] = q_ref[pr * rep:(pr + 1) * rep].reshape(rows, LANE)
                qa_sc[pr, :, LANE:] = jnp.concatenate([selb_ref[pr]] * rep, axis=0)

    def process(masked):
        thr = qi * tq - kj * tk
        for pr in range(nprob):
            q = qa_sc[pr] if mode == "select" else q_ref[pr * rep:(pr + 1) * rep].reshape(rows, LANE)
            s_sc[pr] = lax.dot_general(q, k_ref[pr], _NT, preferred_element_type=F32)
        if masked:
            diff = lax.broadcasted_iota(I32, (ch, tk), 1) - lax.broadcasted_iota(I32, (ch, tk), 0)
        for pr in range(nprob):
            for r0 in range(0, rows, ch):
                sc = s_sc[pr, r0:r0 + ch, :]
                if masked:
                    lim = thr + (r0 % tq)
                    ok = diff <= lim
                    if mode == "window":
                        ok = ok & (diff > lim - WINDOW)
                    sc = jnp.where(ok, sc, NEG)
                    s_sc[pr, r0:r0 + ch, :] = sc
                m_prev = mb_sc[pr, r0:r0 + ch, :]
                m_new = jnp.maximum(m_prev, jnp.max(lane_fold(sc, jnp.maximum), axis=1, keepdims=True))
                al_sc[pr, r0:r0 + ch, :] = jnp.exp2(m_prev - m_new)
                mb_sc[pr, r0:r0 + ch, :] = m_new
            for r0 in range(0, rows, ch):
                mb = mb_sc[pr, r0:r0 + ch, :]
                p = jnp.exp2(s_sc[pr, r0:r0 + ch, :] - jnp.concatenate([mb] * nl, axis=1))
                p_sc[pr, r0:r0 + ch, :] = p.astype(BF16)
            acc_sc[pr] = al_sc[pr] * acc_sc[pr] + jnp.dot(p_sc[pr], v_ref[pr], preferred_element_type=F32)

    if mode == "window":
        pl.when(kj >= 0)(lambda: process(True))
    else:
        pl.when((kj >= 0) & ((flag & FLASH_MASKED) != 0))(lambda: process(True))
        pl.when((kj >= 0) & ((flag & FLASH_MASKED) == 0))(lambda: process(False))

    @pl.when((flag & FLASH_LAST) != 0)
    def _():
        for pr in range(nprob):
            acc = acc_sc[pr]
            o = acc / acc[:, ONE_LANE:ONE_LANE + 1]
            o_ref[pr * rep:(pr + 1) * rep] = o.reshape(rep, tq, LANE).astype(o_ref.dtype)


def _flash_schedule(seq, tq, tk, mode):
    qi, kj, flags = [], [], []
    for i in range(seq // tq):
        hi = (i * tq + tq - 1) // tk
        lo = 0 if mode != "window" else (i * tq - (WINDOW - 1)) // tk
        js = list(range(lo, hi + 1))
        for n, j in enumerate(js):
            crosses_diagonal = (j + 1) * tk - 1 > i * tq
            qi.append(i)
            kj.append(j if j >= 0 else -1)
            flags.append((FLASH_FIRST if n == 0 else 0) | (FLASH_LAST if n == len(js) - 1 else 0)
                         | (FLASH_MASKED if crosses_diagonal else 0))
    as_i32 = lambda v: jnp.asarray(np.asarray(v, np.int32))
    return as_i32(qi), as_i32(kj), as_i32(flags), len(qi)


def _flash(q, k, v, selb, *, batch, seq, mode, tq, tk):
    hq, T, _ = q.shape
    hkv = k.shape[0]
    kw = k.shape[-1]
    rep = hq // hkv
    npb = FLASH_NPROB
    assert tq & (tq - 1) == 0 and tq % FLASH_CHUNK == 0 and hkv % npb == 0
    qi, kj, flags, nsteps = _flash_schedule(seq, tq, tk, mode)
    nq, nk = seq // tq, seq // tk
    q_map = lambda g, b, s, qi, kj, fl: (g, b * nq + qi[s], 0)
    k_map = lambda g, b, s, qi, kj, fl: (g, b * nk + jnp.maximum(kj[s], 0), 0)
    in_specs = [pl.BlockSpec((npb * rep, tq, LANE), q_map),
                pl.BlockSpec((npb, tk, kw), k_map),
                pl.BlockSpec((npb, tk, LANE), k_map)]
    args = [q, k, v]
    rows = rep * tq
    scratch = []
    if mode == "select":
        in_specs.append(pl.BlockSpec((npb, tq, LANE), q_map))
        args.append(selb)
        scratch.append(pltpu.VMEM((npb, rows, kw), BF16))
    scratch += [pltpu.VMEM((npb, rows, tk), F32), pltpu.VMEM((npb, rows, tk), BF16),
                pltpu.VMEM((npb, rows, LANE), F32), pltpu.VMEM((npb, rows, LANE), F32),
                pltpu.VMEM((npb, rows, LANE), F32)]
    return pl.pallas_call(
        functools.partial(_flash_body, mode=mode, rep=rep, tq=tq, tk=tk, nprob=npb),
        grid_spec=pltpu.PrefetchScalarGridSpec(
            num_scalar_prefetch=3,
            grid=(hkv // npb, batch, nsteps),
            in_specs=in_specs,
            out_specs=pl.BlockSpec((npb * rep, tq, LANE), q_map),
            scratch_shapes=scratch),
        out_shape=jax.ShapeDtypeStruct((hq, T, LANE), BF16),
        compiler_params=_cparams("parallel", "parallel", "arbitrary"),
        name="flash_" + mode,
    )(qi, kj, flags, *args)


def _attn_out_body(*refs, n_branch, heads):
    o_refs = refs[:n_branch]
    if n_branch > 1:
        gl_ref, w_ref, x_ref, g_ref, out_ref = refs[n_branch:]
        gates = _sigmoid(gl_ref[...])
    else:
        w_ref, x_ref, g_ref, out_ref = refs[n_branch:]
    per_head = []
    for h in range(heads):
        if n_branch > 1:
            o = jnp.zeros(o_refs[0].shape[1:], F32)
            for c in range(n_branch):
                col = n_branch * h + c
                o = o + gates[:, col:col + 1] * o_refs[c][h].astype(F32)
            o = o.astype(BF16)
        else:
            o = o_refs[0][h]
        per_head.append(o)
    acc = jnp.dot(jnp.concatenate(per_head, axis=1), w_ref[...], preferred_element_type=F32)
    out_ref[...] = x_ref[...] + g_ref[0] * acc


def _attn_out(os, gl, w_heads, x2, g, seq):
    T, Dm = x2.shape
    heads = w_heads.shape[0]
    tm = 256
    nb = seq // tm
    row = lambda i: (i, 0)
    o_spec = pl.BlockSpec((heads, tm, LANE), lambda i: (0, i, 0))
    in_specs = [o_spec] * len(os)
    args = list(os)
    if len(os) > 1:
        in_specs.append(pl.BlockSpec((tm, LANE), row))
        args.append(gl)
    in_specs += [pl.BlockSpec((heads * LANE, Dm), lambda i: (0, 0)),
                 pl.BlockSpec((tm, Dm), row),
                 pl.BlockSpec((1, 1, Dm), lambda i: (i // nb, 0, 0))]
    args += [w_heads.reshape(heads * LANE, Dm), x2, g]
    return pl.pallas_call(
        functools.partial(_attn_out_body, n_branch=len(os), heads=heads),
        grid=(T // tm,),
        in_specs=in_specs,
        out_specs=pl.BlockSpec((tm, Dm), row),
        out_shape=jax.ShapeDtypeStruct((T, Dm), F32),
        compiler_params=_cparams("parallel"),
        name="attn_out_%d" % len(os),
    )(*args)


def _pad_heads_rows(w, heads, dh):
    w = w.reshape(heads, dh, -1)
    return jnp.pad(w, ((0, 0), (0, LANE - dh), (0, 0))).astype(BF16)


def _pad_heads_cols(w, heads, dh):
    k = w.shape[0]
    w = w.reshape(k, heads, dh)
    return jnp.pad(w, ((0, 0), (0, 0), (0, LANE - dh))).reshape(k, heads * LANE)


def _rope_tables(pos, rot, offset):
    half = rot // 2
    inv_freq = ROPE_THETA ** (-jnp.arange(0, rot, 2, dtype=F32) / rot)
    ang = pos.astype(F32)[:, None] * inv_freq[None, :]
    cos, sin = jnp.cos(ang), jnp.sin(ang)
    n = pos.shape[0]
    c = jnp.ones((n, LANE), F32).at[:, offset:offset + rot].set(jnp.concatenate([cos, cos], axis=1))
    sa = jnp.zeros((n, LANE), F32).at[:, offset:offset + half].set(-sin)
    sb = jnp.zeros((n, LANE), F32).at[:, offset + half:offset + rot].set(sin)
    return c, sa, sb


def _mla_proj_body(x_ref, a_ref, b_ref, win_ref, qlg_ref, kvlg_ref, wuq_ref, wuqs_ref, wuk_ref, wuv_ref,
                   ones_ref, q1_ref, q2_ref, k1_ref, k2_ref, q_ref, k_ref, v_ref):
    h = _normmod(x_ref[...], a_ref[0], b_ref[0]).astype(BF16)
    r = jnp.dot(h, win_ref[...], preferred_element_type=F32)
    lat = MLA_Q_LORA + MLA_KV_LORA
    q_lat = r[:, :MLA_Q_LORA]
    kv_lat = r[:, MLA_Q_LORA:lat]
    kpe = r[:, lat:lat + LANE]
    kpe_swap = r[:, lat + LANE:]
    ql = q_lat * lax.rsqrt(jnp.mean(q_lat * q_lat, axis=-1, keepdims=True) + EPS) * qlg_ref[...]
    kvl = kv_lat * lax.rsqrt(jnp.mean(kv_lat * kv_lat, axis=-1, keepdims=True) + EPS) * kvlg_ref[...]
    ql = ql.astype(BF16)
    kvl = kvl.astype(BF16)
    q = jnp.dot(ql, wuq_ref[...], preferred_element_type=F32)
    q_swap = jnp.dot(ql, wuqs_ref[...], preferred_element_type=F32)
    kn = jnp.dot(kvl, wuk_ref[...], preferred_element_type=F32)
    v = jnp.dot(kvl, wuv_ref[...], preferred_element_type=F32)
    ones = ones_ref[...]
    q1, q2, k1, k2 = q1_ref[...], q2_ref[...], k1_ref[...], k2_ref[...]
    k_rot = kpe_swap * k2

    def inv_rms(t):
        ss = jnp.dot((t * t).astype(BF16), ones, preferred_element_type=F32)
        return lax.rsqrt(ss * (1.0 / MLA_QK) + EPS)

    for hd in range(MLA_HEADS):
        sl = slice(hd * LANE, (hd + 1) * LANE)
        xq = q[:, sl]
        q_ref[hd] = ((xq * q1 + q_swap[:, sl] * q2) * inv_rms(xq)).astype(BF16)
        xk = kn[:, sl] + kpe
        k_ref[hd] = ((xk * k1 + k_rot) * inv_rms(xk)).astype(BF16)
        v_ref[hd] = (v[:, sl] + _one_lane()).astype(BF16)


def _mla_mixer(x2, a, b, g, w_in, q_lat_g, kv_lat_g, w_uq, w_ukv, q_gain, k_gain, w_out, batch, seq):
    T, Dm = x2.shape
    H = MLA_HEADS
    tm = 256
    nb = seq // tm
    scale = MLA_QK ** -0.5 * LOG2E
    lat = MLA_Q_LORA + MLA_KV_LORA
    half = MLA_ROPE // 2

    def swap_rope(t):
        lo, hi = t[..., MLA_NOPE:MLA_NOPE + half], t[..., MLA_NOPE + half:MLA_QK]
        return jnp.concatenate([jnp.zeros_like(t[..., :MLA_NOPE]), hi, lo], axis=-1)

    to_slot = lambda t: jnp.pad(t, [(0, 0)] * (t.ndim - 1) + [(0, LANE - MLA_QK)])
    kpe_w = jnp.concatenate([jnp.zeros((Dm, MLA_NOPE), F32), w_in[:, lat:]], axis=1)
    win_p = jnp.concatenate([w_in[:, :lat], to_slot(kpe_w), to_slot(swap_rope(kpe_w))], axis=1).astype(BF16)
    wuq3 = w_uq.reshape(MLA_Q_LORA, H, MLA_QK)
    wuq_p = to_slot(wuq3).reshape(MLA_Q_LORA, H * LANE).astype(BF16)
    wuqs_p = to_slot(swap_rope(wuq3)).reshape(MLA_Q_LORA, H * LANE).astype(BF16)
    wukv = w_ukv.reshape(MLA_KV_LORA, H, MLA_NOPE + MLA_V)
    wuk_p = _pad_heads_cols(wukv[:, :, :MLA_NOPE].reshape(MLA_KV_LORA, -1), H, MLA_NOPE).astype(BF16)
    wuv_p = _pad_heads_cols(wukv[:, :, MLA_NOPE:].reshape(MLA_KV_LORA, -1), H, MLA_V).astype(BF16)
    c, sa, sb = _rope_tables(jnp.arange(seq), MLA_ROPE, MLA_NOPE)
    qg = q_gain * scale
    q1, q2 = c * to_slot(qg)[None], (sa + sb) * to_slot(swap_rope(qg))[None]
    k1, k2 = c * to_slot(k_gain)[None], (sa + sb) * to_slot(swap_rope(k_gain))[None]
    ones = jnp.ones((LANE, LANE), BF16)
    row = lambda i: (i, 0)
    full = lambda i: (0, 0)
    per_b = lambda i: (i // nb, 0, 0)
    pos = lambda i: (i % nb, 0)
    head_out = pl.BlockSpec((H, tm, LANE), lambda i: (0, i, 0))
    hshape = jax.ShapeDtypeStruct((H, T, LANE), BF16)
    q, k, v = pl.pallas_call(
        _mla_proj_body,
        grid=(T // tm,),
        in_specs=[pl.BlockSpec((tm, Dm), row),
                  pl.BlockSpec((1, 1, Dm), per_b),
                  pl.BlockSpec((1, 1, Dm), per_b),
                  pl.BlockSpec(win_p.shape, full),
                  pl.BlockSpec((1, MLA_Q_LORA), full),
                  pl.BlockSpec((1, MLA_KV_LORA), full),
                  pl.BlockSpec(wuq_p.shape, full),
                  pl.BlockSpec(wuqs_p.shape, full),
                  pl.BlockSpec(wuk_p.shape, full),
                  pl.BlockSpec(wuv_p.shape, full),
                  pl.BlockSpec((LANE, LANE), full),
                  pl.BlockSpec((tm, LANE), pos),
                  pl.BlockSpec((tm, LANE), pos),
                  pl.BlockSpec((tm, LANE), pos),
                  pl.BlockSpec((tm, LANE), pos)],
        out_specs=[head_out, head_out, head_out],
        out_shape=[hshape, hshape, hshape],
        compiler_params=_cparams("parallel"),
        name="mla_proj",
    )(x2, a, b, win_p, q_lat_g.reshape(1, -1), kv_lat_g.reshape(1, -1), wuq_p, wuqs_p, wuk_p, wuv_p,
      ones, q1, q2, k1, k2)
    blk = min(1024, seq)
    o = _flash(q, k, v, None, batch=batch, seq=seq, mode="causal", tq=blk, tk=blk)
    return _attn_out([o], None, _pad_heads_rows(w_out, H, MLA_V), x2, g, seq)


N_KV_STREAMS = 6


def _nsa_proj_body(x_ref, a_ref, b_ref, w_ref, qg_ref, kg_ref, c_ref, sa_ref, sb_ref, blk_ref,
                   q_ref, kv_ref, ks_ref, gl_ref):
    h = _normmod(x_ref[...], a_ref[0], b_ref[0]).astype(BF16)
    r = jnp.dot(h, w_ref[...], preferred_element_type=F32)
    c, sa, sb = c_ref[...], sa_ref[...], sb_ref[...]

    def norm_rot(t, gain):
        t = t * lax.rsqrt(jnp.sum(t * t, axis=-1, keepdims=True) * (1.0 / NSA_DH) + EPS) * gain
        return _rope(t, c, sa, sb, NSA_ROT // 2)

    for hd in range(NSA_HEADS):
        q_ref[hd] = norm_rot(r[:, hd * LANE:(hd + 1) * LANE], qg_ref[...]).astype(BF16)
    base = NSA_HEADS * LANE
    for st in range(N_KV_STREAMS):
        for gi in range(NSA_GROUPS):
            off = base + (st * NSA_GROUPS + gi) * LANE
            t = r[:, off:off + LANE]
            if st == 2:
                t = norm_rot(t, kg_ref[1:2, :])
                ks_ref[gi] = jnp.concatenate([t.astype(BF16), blk_ref[...]], axis=1)
            elif st == 4:
                t = norm_rot(t, kg_ref[2:3, :])
            elif st in (3, 5):
                t = t + _one_lane()
            kv_ref[st * NSA_GROUPS + gi] = t.astype(BF16)
    gl_ref[...] = r[:, base + N_KV_STREAMS * NSA_GROUPS * LANE:]


def _nsa_compress_body(x_ref, pea_ref, peb_ref, w1a_ref, w1b_ref, w2_ref, kg_ref, c_ref, sa_ref, sb_ref,
                       o_ref, *, is_key, n_cmp):
    x = x_ref[0].astype(F32)
    xa = (x + pea_ref[...]).astype(BF16)
    xb = (x + peb_ref[...]).astype(BF16)
    za = jnp.dot(xa, w1a_ref[...], preferred_element_type=F32)
    zb = jnp.dot(xb, w1b_ref[...], preferred_element_type=F32)
    rows = za.shape[0]
    z = _silu(za + pltpu.roll(zb, rows - 1, 0))
    t = jnp.dot(z.astype(BF16), w2_ref[...], preferred_element_type=F32)
    if is_key:
        t = t * lax.rsqrt(jnp.sum(t * t, axis=-1, keepdims=True) * (1.0 / NSA_DH) + EPS) * kg_ref[...]
        t = _rope(t, c_ref[...], sa_ref[...], sb_ref[...], NSA_ROT // 2)
    valid = lax.broadcasted_iota(I32, t.shape, 0) < n_cmp
    o_ref[0] = jnp.where(valid, t, 0.0).astype(BF16)


def _nsa_cmp_select_body(q_ref, kc_ref, vc_ref, oc_ref, sel_ref, p_sc, *, tq, limits, n_cmp, n_top):
    i = pl.program_id(2)
    t0 = i * tq
    rep = q_ref.shape[0]
    per = SLC_BLOCK // CMP_STRIDE

    def compute(limit):
        nc = limit // CMP_STRIDE
        ns = limit // SLC_BLOCK
        kc = kc_ref[0, :nc, :]
        vc = vc_ref[0, :nc, :]
        q = q_ref[...].reshape(rep * tq, LANE)
        sc = lax.dot_general(q, kc, _NT, preferred_element_type=F32)
        qpos = t0 + lax.broadcasted_iota(I32, (tq, nc), 0)
        blk = lax.broadcasted_iota(I32, (tq, nc), 1)
        mask = ((blk * CMP_STRIDE + (CMP_BLOCK - 1)) <= qpos) & (blk < n_cmp)
        sc = jnp.where(mask[None], sc.reshape(rep, tq, nc), NEG)
        p = jnp.exp2(sc - jnp.max(sc, axis=-1, keepdims=True))
        p = jnp.where(mask[None], p / jnp.sum(p, axis=-1, keepdims=True), 0.0)
        oc = jnp.dot(p.reshape(rep * tq, nc).astype(BF16), vc, preferred_element_type=F32)
        oc_ref[...] = oc.reshape(rep, tq, LANE).astype(BF16)
        qpos_t = t0 + lax.broadcasted_iota(I32, (nc, tq), 1)
        blk_t = lax.broadcasted_iota(I32, (nc, tq), 0)
        mask_t = ((blk_t * CMP_STRIDE + (CMP_BLOCK - 1)) <= qpos_t) & (blk_t < n_cmp)
        psum = jnp.zeros((nc, tq), F32)
        for r in range(rep):
            st = lax.dot_general(kc, q_ref[r], _NT, preferred_element_type=F32)
            st = jnp.where(mask_t, st, NEG)
            pt = jnp.exp2(st - jnp.max(st, axis=0, keepdims=True))
            psum = psum + jnp.where(mask_t, pt / jnp.sum(pt, axis=0, keepdims=True), 0.0)
        slabs = []
        for sb in range(tq // LANE):
            p_sc[sb, 0:8, :] = jnp.zeros((8, LANE), F32)
            p_sc[sb, 8:8 + nc, :] = psum[:, sb * LANE:(sb + 1) * LANE]
            part = p_sc[sb, pl.ds(7, ns, stride=per), :]
            for k in range(1, per + 1):
                part = part + p_sc[sb, pl.ds(7 + k, ns, stride=per), :]
            slabs.append(part)
        imp = jnp.concatenate(slabs, axis=1)
        jb = lax.broadcasted_iota(I32, (ns, tq), 0)
        qp = t0 + lax.broadcasted_iota(I32, (ns, tq), 1)
        imp = jnp.where(jb * SLC_BLOCK <= qp, imp, -BIG)
        imp = jnp.where((jb == 0) | (jb == jnp.right_shift(qp, 6)), BIG, imp)
        jbf = jb.astype(F32)
        sel_t = jnp.full((ns, tq), SEL_OFF, F32)
        for _ in range(n_top):
            top = jnp.max(imp, axis=0, keepdims=True)
            first = jnp.min(jnp.where(imp == top, jbf, float(MAX_SLC)), axis=0, keepdims=True)
            hit = jbf == first
            sel_t = jnp.where(hit, 0.0, sel_t)
            imp = jnp.where(hit, -jnp.inf, imp)
        if ns < MAX_SLC:
            sel_t = jnp.concatenate([sel_t, jnp.full((MAX_SLC - ns, tq), SEL_OFF, F32)], axis=0)
        sel_ref[0] = sel_t.T.astype(BF16)

    end = t0 + tq
    lower = 0
    for limit in limits:
        pl.when((end > lower) & (end <= limit))(functools.partial(compute, limit))
        lower = limit


def _nsa_mixer(x2, a, b, g, w_in, w_cmp1, w_cmp2, cmp_pos, q_gain, k_gain, w_out, batch, seq):
    T, Dm = x2.shape
    H, G, dh = NSA_HEADS, NSA_GROUPS, NSA_DH
    scale = dh ** -0.5 * LOG2E
    n_cmp = seq // CMP_STRIDE - 1
    ncp = seq // CMP_STRIDE
    n_slc = seq // SLC_BLOCK
    assert n_slc <= MAX_SLC and ncp <= MAX_SLC * (SLC_BLOCK // CMP_STRIDE)
    n_top = min(SLC_TOP_N, n_slc)
    tm = 256
    nb = seq // tm
    q_cols = _pad_heads_cols(w_in[:, :H * dh], H, dh)
    kv_cols = _pad_heads_cols(w_in[:, H * dh:H * dh + N_KV_STREAMS * G * dh], N_KV_STREAMS * G, dh)
    gl_cols = jnp.pad(w_in[:, H * dh + N_KV_STREAMS * G * dh:], ((0, 0), (0, LANE - 3 * H)))
    w_p = jnp.concatenate([q_cols, kv_cols, gl_cols], axis=1).astype(BF16)
    pad_gain = lambda v: jnp.pad(v, ((0, 0), (0, LANE - dh)))
    c, sa, sb = _rope_tables(jnp.arange(seq), NSA_ROT, 0)
    row = lambda i: (i, 0)
    full = lambda i: (0, 0)
    per_b = lambda i: (i // nb, 0, 0)
    pos = lambda i: (i % nb, 0)
    blk_onehot = (jnp.arange(seq)[:, None] // SLC_BLOCK == jnp.arange(MAX_SLC)[None, :]).astype(BF16)
    q, kv, ks, gl = pl.pallas_call(
        _nsa_proj_body,
        grid=(T // tm,),
        in_specs=[pl.BlockSpec((tm, Dm), row),
                  pl.BlockSpec((1, 1, Dm), per_b),
                  pl.BlockSpec((1, 1, Dm), per_b),
                  pl.BlockSpec(w_p.shape, full),
                  pl.BlockSpec((1, LANE), full),
                  pl.BlockSpec((3, LANE), full),
                  pl.BlockSpec((tm, LANE), pos),
                  pl.BlockSpec((tm, LANE), pos),
                  pl.BlockSpec((tm, LANE), pos),
                  pl.BlockSpec((tm, MAX_SLC), pos)],
        out_specs=[pl.BlockSpec((H, tm, LANE), lambda i: (0, i, 0)),
                   pl.BlockSpec((N_KV_STREAMS * G, tm, LANE), lambda i: (0, i, 0)),
                   pl.BlockSpec((G, tm, LANE + MAX_SLC), lambda i: (0, i, 0)),
                   pl.BlockSpec((tm, LANE), row)],
        out_shape=[jax.ShapeDtypeStruct((H, T, LANE), BF16),
                   jax.ShapeDtypeStruct((N_KV_STREAMS * G, T, LANE), BF16),
                   jax.ShapeDtypeStruct((G, T, LANE + MAX_SLC), BF16),
                   jax.ShapeDtypeStruct((T, LANE), F32)],
        compiler_params=_cparams("parallel"),
        name="nsa_proj",
    )(x2, a, b, w_p, pad_gain(q_gain.reshape(1, dh) * scale), pad_gain(k_gain), c, sa, sb, blk_onehot)
    kc_raw, vc_raw, _, vs, kw, vw = [kv[s * G:(s + 1) * G] for s in range(N_KV_STREAMS)]

    cmp_end = jnp.arange(ncp) * CMP_STRIDE + (CMP_BLOCK - 1)
    cc, csa, csb = _rope_tables(cmp_end, NSA_ROT, 0)
    kdim = CMP_STRIDE * LANE

    def compress(raw, w1, w2, pe, is_key):
        x16 = raw.reshape(G, T // CMP_STRIDE, kdim)
        w1p = jnp.pad(w1, ((0, 0), (0, LANE - dh), (0, LANE - dh)))
        w1a = w1p[:CMP_STRIDE].reshape(kdim, LANE).astype(BF16)
        w1b = w1p[CMP_STRIDE:].reshape(kdim, LANE).astype(BF16)
        pep = jnp.pad(pe, ((0, 0), (0, LANE - dh)))
        pea = pep[:CMP_STRIDE].reshape(1, kdim)
        peb = pep[CMP_STRIDE:].reshape(1, kdim)
        w2p = jnp.pad(w2, ((0, LANE - dh), (0, LANE - dh))).astype(BF16)
        const = lambda gi, bi: (0, 0)
        return pl.pallas_call(
            functools.partial(_nsa_compress_body, is_key=is_key, n_cmp=n_cmp),
            grid=(G, batch),
            in_specs=[pl.BlockSpec((1, ncp, kdim), lambda gi, bi: (gi, bi, 0)),
                      pl.BlockSpec((1, kdim), const),
                      pl.BlockSpec((1, kdim), const),
                      pl.BlockSpec((kdim, LANE), const),
                      pl.BlockSpec((kdim, LANE), const),
                      pl.BlockSpec((LANE, LANE), const),
                      pl.BlockSpec((1, LANE), const),
                      pl.BlockSpec((ncp, LANE), const),
                      pl.BlockSpec((ncp, LANE), const),
                      pl.BlockSpec((ncp, LANE), const)],
            out_specs=pl.BlockSpec((1, ncp, LANE), lambda gi, bi: (gi, bi, 0)),
            out_shape=jax.ShapeDtypeStruct((G, batch * ncp, LANE), BF16),
            compiler_params=_cparams("parallel", "parallel"),
            name="nsa_compress_" + ("k" if is_key else "v"),
        )(x16, pea, peb, w1a, w1b, w2p, pad_gain(k_gain)[0:1], cc, csa, csb)

    kc = compress(kc_raw, w_cmp1[0], w_cmp2[0], cmp_pos[0], True)
    vc = compress(vc_raw, w_cmp1[1], w_cmp2[1], cmp_pos[1], False)

    tq = 256
    nq = seq // tq
    limits = tuple(sorted({max(seq // d, min(seq, LANE * CMP_STRIDE)) for d in (4, 2, 1)}))
    q_map = lambda gi, bi, i: (gi, bi * nq + i, 0)
    c_map = lambda gi, bi, i: (gi, bi, 0)
    o_c, sel = pl.pallas_call(
        functools.partial(_nsa_cmp_select_body, tq=tq, limits=limits, n_cmp=n_cmp, n_top=n_top),
        grid=(G, batch, nq),
        in_specs=[pl.BlockSpec((NSA_REP, tq, LANE), q_map),
                  pl.BlockSpec((1, ncp, LANE), c_map),
                  pl.BlockSpec((1, ncp, LANE), c_map)],
        out_specs=[pl.BlockSpec((NSA_REP, tq, LANE), q_map),
                   pl.BlockSpec((1, tq, LANE), q_map)],
        out_shape=[jax.ShapeDtypeStruct((H, T, LANE), BF16),
                   jax.ShapeDtypeStruct((G, T, LANE), BF16)],
        scratch_shapes=[pltpu.VMEM((tq // LANE, 8 + MAX_SLC * (SLC_BLOCK // CMP_STRIDE), LANE), F32)],
        compiler_params=_cparams("parallel", "parallel", "parallel"),
        name="nsa_cmp_select",
    )(q, kc, vc)

    o_s = _flash(q, ks, vs, sel, batch=batch, seq=seq, mode="select", tq=256, tk=min(1024, seq))
    o_w = _flash(q, kw, vw, None, batch=batch, seq=seq, mode="window", tq=256, tk=WINDOW)
    return _attn_out([o_c, o_s, o_w], gl, _pad_heads_rows(w_out, H, dh), x2, g, seq)


def _router_body(x_ref, a_ref, b_ref, rwt_ref, rb_ref, tri_ref, e_ref, w_ref, rank_ref, cnt_ref,
                 carry_sc, *, tm):
    i = pl.program_id(0)

    @pl.when(i == 0)
    def _():
        carry_sc[...] = jnp.zeros(carry_sc.shape, F32)

    h = _normmod(x_ref[...], a_ref[0], b_ref[0])
    logits = lax.dot_general(rwt_ref[...], h, _NT, preferred_element_type=F32, precision=HIGHEST)
    scores = _sigmoid(logits)
    biased = scores + rb_ref[...]
    ng, per = N_GROUPS, EXPERTS_PER_GROUP
    row = lambda arr, r: arr[r:r + 1, :]
    gsel = jnp.zeros((1, tm), I32)
    best = None
    for gi in range(ng):
        v = [row(biased, gi * per + k) for k in range(per)]
        top2 = None
        for p in range(per):
            for q in range(p + 1, per):
                s = v[p] + v[q]
                top2 = s if top2 is None else jnp.maximum(top2, s)
        if best is None:
            best = top2
        else:
            better = top2 > best
            gsel = jnp.where(better, gi, gsel)
            best = jnp.where(better, top2, best)
    cb, cs = [], []
    for k in range(per):
        b_k = row(biased, k)
        s_k = row(scores, k)
        for gi in range(1, ng):
            hit = gsel == gi
            b_k = jnp.where(hit, row(biased, gi * per + k), b_k)
            s_k = jnp.where(hit, row(scores, gi * per + k), s_k)
        cb.append(b_k)
        cs.append(s_k)

    def argmax_first(vals):
        idx = jnp.zeros((1, tm), I32)
        top = vals[0]
        for k in range(1, per):
            better = vals[k] > top
            idx = jnp.where(better, k, idx)
            top = jnp.where(better, vals[k], top)
        return idx

    def pick(vals, idx):
        out = vals[0]
        for k in range(1, per):
            out = jnp.where(idx == k, vals[k], out)
        return out

    i1 = argmax_first(cb)
    i2 = argmax_first([jnp.where(i1 == k, -jnp.inf, cb[k]) for k in range(per)])
    w1 = pick(cs, i1)
    w2 = pick(cs, i2)
    tot = w1 + w2
    e1 = gsel * per + i1
    e2 = gsel * per + i2
    eid = lax.broadcasted_iota(I32, (N_EXPERTS, tm), 0)
    hot = (eid == e1) | (eid == e2)
    onehot = jnp.where(hot, 1.0, 0.0)
    before = carry_sc[...] + jnp.dot(onehot.astype(BF16), tri_ref[...], preferred_element_type=F32)
    r1 = jnp.sum(jnp.where(eid == e1, before, 0.0), axis=0, keepdims=True)
    r2 = jnp.sum(jnp.where(eid == e2, before, 0.0), axis=0, keepdims=True)
    carry = carry_sc[...] + jnp.sum(onehot, axis=1, keepdims=True)
    carry_sc[...] = carry
    cnt_ref[...] = jnp.broadcast_to(carry, cnt_ref.shape)
    zi = jnp.zeros((6, tm), I32)
    e_ref[...] = jnp.concatenate([e1, e2, zi], axis=0)
    rank_ref[...] = jnp.concatenate([r1.astype(I32), r2.astype(I32), zi], axis=0)
    w_ref[...] = jnp.concatenate([w1 / tot, w2 / tot, jnp.zeros((6, tm), F32)], axis=0)


ROW_SUB = D_MODEL // LANE


def _row_tile(r):
    return pl.ds(pl.multiple_of(r * ROW_SUB, ROW_SUB), ROW_SUB)


def _to_row_tiles(ref, val):
    n = val.shape[0]
    for s in range(ROW_SUB):
        ref[pl.ds(s, n, stride=ROW_SUB), :] = val[:, s * LANE:(s + 1) * LANE]


def _from_row_tiles(ref, n, s):
    return ref[pl.ds(s, n, stride=ROW_SUB), :]


def _moe_dispatch_body(starts_ref, counts_ref, padded_ref, nv_ref, d0_ref, d1_ref,
                       x_ref, a_ref, b_ref, xs_hbm, hbuf0, hbuf1, zbuf, sems, zsem,
                       *, tm, tg, n_tiles, nt):
    i = pl.program_id(0)

    @pl.when(i == 0)
    def _():
        zbuf[...] = jnp.zeros(zbuf.shape, F32)
        zrow = zbuf.at[pl.ds(0, ROW_SUB)]
        for ex in range(N_EXPERTS):
            lo = starts_ref[ex] + counts_ref[ex]
            hi = starts_ref[ex] + padded_ref[ex]

            def fill(r, carry):
                pltpu.make_async_copy(zrow, xs_hbm.at[_row_tile(r)], zsem).start()
                return carry

            lax.fori_loop(lo, hi, fill, 0)

            def drain_fill(r, carry):
                pltpu.make_async_copy(zrow, xs_hbm.at[_row_tile(0)], zsem).wait()
                return carry

            lax.fori_loop(lo, hi, drain_fill, 0)

        rows_per_tile = tg * ROW_SUB

        def fill_tile(t, carry):
            dst = xs_hbm.at[pl.ds(pl.multiple_of(t * rows_per_tile, rows_per_tile), rows_per_tile)]
            pltpu.make_async_copy(zbuf, dst, zsem).start()
            return carry

        lax.fori_loop(nv_ref[0], n_tiles, fill_tile, 0)

        def drain_tile(t, carry):
            pltpu.make_async_copy(zbuf, xs_hbm.at[pl.ds(0, rows_per_tile)], zsem).wait()
            return carry

        lax.fori_loop(nv_ref[0], n_tiles, drain_tile, 0)

    h = _normmod(x_ref[...], a_ref[0], b_ref[0])

    def scatter_from(hbuf, sem, other_buf, other_sem):
        _to_row_tiles(hbuf, h)

        def row_copy(buf, sm, r, slot):
            return pltpu.make_async_copy(buf.at[_row_tile(r)], xs_hbm.at[_row_tile(slot)], sm)

        def issue(r, carry):
            row_copy(hbuf, sem, r, d0_ref[r]).start()
            row_copy(hbuf, sem, r, d1_ref[r]).start()
            return carry

        lax.fori_loop(0, tm, issue, 0, unroll=8)

        def drain(buf, sm):
            def body(r, carry):
                row_copy(buf, sm, r, 0).wait()
                row_copy(buf, sm, r, 0).wait()
                return carry

            lax.fori_loop(0, tm, body, 0, unroll=8)

        pl.when(i > 0)(lambda: drain(other_buf, other_sem))
        pl.when(i == nt - 1)(lambda: drain(hbuf, sem))

    pl.when(i % 2 == 0)(lambda: scatter_from(hbuf0, sems.at[0], hbuf1, sems.at[1]))
    pl.when(i % 2 == 1)(lambda: scatter_from(hbuf1, sems.at[1], hbuf0, sems.at[0]))


def _moe_expert_body(te_ref, nv_ref, xs_ref, win_ref, wout_ref, y_ref, *, tg):
    i = pl.program_id(0)

    @pl.when(i < nv_ref[0])
    def _():
        x = jnp.concatenate([_from_row_tiles(xs_ref, tg, s) for s in range(ROW_SUB)], axis=1).astype(BF16)
        gu = jnp.dot(x, win_ref[0], preferred_element_type=F32)
        act = (_silu(gu[:, :D_EXPERT]) * gu[:, D_EXPERT:]).astype(BF16)
        _to_row_tiles(y_ref, jnp.dot(act, wout_ref[0], preferred_element_type=F32))

    @pl.when(i >= nv_ref[0])
    def _():
        y_ref[...] = jnp.zeros(y_ref.shape, F32)


def _moe_combine_body(d0c_ref, d1c_ref, d0n_ref, d1n_ref, y_hbm, x_ref, w_ref, g_ref, o_ref,
                      ya0, ya1, yb0, yb1, sems, *, tm, nt):
    i = pl.program_id(0)

    def row_copy(slot, buf, sem, r):
        return pltpu.make_async_copy(y_hbm.at[_row_tile(slot)], buf.at[_row_tile(r)], sem)

    def issue(d0_ref, d1_ref, bufs, sem):
        def body(r, carry):
            row_copy(d0_ref[r], bufs[0], sem, r).start()
            row_copy(d1_ref[r], bufs[1], sem, r).start()
            return carry

        lax.fori_loop(0, tm, body, 0, unroll=8)

    def finish(bufs, sem):
        def body(r, carry):
            row_copy(0, bufs[0], sem, r).wait()
            row_copy(0, bufs[1], sem, r).wait()
            return carry

        lax.fori_loop(0, tm, body, 0, unroll=8)
        w = w_ref[...]
        w0, w1 = w[:, 0:1], w[:, 1:2]
        gate = g_ref[0]
        for s in range(ROW_SUB):
            sl = slice(s * LANE, (s + 1) * LANE)
            y = w0 * _from_row_tiles(bufs[0], tm, s) + w1 * _from_row_tiles(bufs[1], tm, s)
            o_ref[:, sl] = x_ref[:, sl] + gate[:, sl] * y

    set_a, set_b = (ya0, ya1), (yb0, yb1)
    pl.when(i == 0)(lambda: issue(d0c_ref, d1c_ref, set_a, sems.at[0]))

    def even():
        pl.when(i + 1 < nt)(lambda: issue(d0n_ref, d1n_ref, set_b, sems.at[1]))
        finish(set_a, sems.at[0])

    def odd():
        pl.when(i + 1 < nt)(lambda: issue(d0n_ref, d1n_ref, set_a, sems.at[0]))
        finish(set_b, sems.at[1])

    pl.when(i % 2 == 0)(even)
    pl.when(i % 2 == 1)(odd)


def _grouped_moe(x2, a, b, g, router_w, router_bias, w_in, w_out, seq):
    T, Dm = x2.shape
    E = N_EXPERTS
    tm = 512
    nb = seq // tm
    nt = T // tm
    row = lambda i: (i, 0)
    full = lambda i: (0, 0)
    per_b = lambda i: (i // nb, 0, 0)
    lanes = lambda i: (0, i)
    tri = jnp.asarray(np.triu(np.ones((tm, tm), np.float32), 1)).astype(BF16)
    e, w, rank, cnt = pl.pallas_call(
        functools.partial(_router_body, tm=tm),
        grid=(nt,),
        in_specs=[pl.BlockSpec((tm, Dm), row),
                  pl.BlockSpec((1, 1, Dm), per_b),
                  pl.BlockSpec((1, 1, Dm), per_b),
                  pl.BlockSpec((E, Dm), full),
                  pl.BlockSpec((E, 1), full),
                  pl.BlockSpec((tm, tm), full)],
        out_specs=[pl.BlockSpec((8, tm), lanes),
                   pl.BlockSpec((8, tm), lanes),
                   pl.BlockSpec((8, tm), lanes),
                   pl.BlockSpec((E, LANE), full)],
        out_shape=[jax.ShapeDtypeStruct((8, T), I32),
                   jax.ShapeDtypeStruct((8, T), F32),
                   jax.ShapeDtypeStruct((8, T), I32),
                   jax.ShapeDtypeStruct((E, LANE), F32)],
        scratch_shapes=[pltpu.VMEM((E, 1), F32)],
        compiler_params=_cparams("arbitrary"),
        name="moe_router",
    )(x2, a, b, router_w.T, router_bias.reshape(E, 1), tri)

    tg = MOE_TILE
    n_tiles = (2 * T) // tg + E
    n_slots = n_tiles * tg
    counts = cnt[:, 0].astype(I32)
    padded = ((counts + tg - 1) // tg) * tg
    ends = jnp.cumsum(padded)
    starts = ends - padded
    tile_start = jnp.arange(n_tiles, dtype=I32) * tg
    tile_expert = jnp.minimum(jnp.sum(tile_start[:, None] >= ends[None, :], axis=1), E - 1).astype(I32)
    n_valid = (ends[-1] // tg).astype(I32).reshape(1)

    tc = 256
    ntc = T // tc
    nbc = seq // tc
    smem_cur = pl.BlockSpec((tc,), lambda i, *_: (i,), memory_space=pltpu.SMEM)
    smem_nxt = pl.BlockSpec((tc,), lambda i, *_: (jnp.minimum(i + 1, ntc - 1),), memory_space=pltpu.SMEM)
    tile_rows = tc * ROW_SUB
    seg_start = functools.reduce(lambda acc, k: jnp.where(e[:2] == k, starts[k], acc), range(E),
                                 jnp.zeros_like(e[:2]))
    dest = seg_start + rank[:2]
    d0, d1 = dest[0], dest[1]
    xs = pl.pallas_call(
        functools.partial(_moe_dispatch_body, tm=tc, tg=tg, n_tiles=n_tiles, nt=ntc),
        grid_spec=pltpu.PrefetchScalarGridSpec(
            num_scalar_prefetch=4,
            grid=(ntc,),
            in_specs=[smem_cur, smem_cur,
                      pl.BlockSpec((tc, Dm), lambda i, *_: (i, 0)),
                      pl.BlockSpec((1, 1, Dm), lambda i, *_: (i // nbc, 0, 0)),
                      pl.BlockSpec((1, 1, Dm), lambda i, *_: (i // nbc, 0, 0))],
            out_specs=pl.BlockSpec(memory_space=pl.ANY),
            scratch_shapes=[pltpu.VMEM((tile_rows, LANE), F32), pltpu.VMEM((tile_rows, LANE), F32),
                            pltpu.VMEM((tg * ROW_SUB, LANE), F32),
                            pltpu.SemaphoreType.DMA((2,)), pltpu.SemaphoreType.DMA(())]),
        out_shape=jax.ShapeDtypeStruct((n_slots * ROW_SUB, LANE), F32),
        compiler_params=_cparams("arbitrary"),
        name="moe_dispatch",
    )(starts.astype(I32), counts, padded, n_valid, d0, d1, x2, a, b)

    last_tile = lambda i, te, nv: (jnp.minimum(i, nv[0] - 1), 0)
    y = pl.pallas_call(
        functools.partial(_moe_expert_body, tg=tg),
        grid_spec=pltpu.PrefetchScalarGridSpec(
            num_scalar_prefetch=2,
            grid=(n_tiles,),
            in_specs=[pl.BlockSpec((tg * ROW_SUB, LANE), last_tile),
                      pl.BlockSpec((1, Dm, 2 * D_EXPERT), lambda i, te, nv: (te[i], 0, 0)),
                      pl.BlockSpec((1, D_EXPERT, Dm), lambda i, te, nv: (te[i], 0, 0))],
            out_specs=pl.BlockSpec((tg * ROW_SUB, LANE), lambda i, te, nv: (i, 0))),
        out_shape=jax.ShapeDtypeStruct((n_slots * ROW_SUB, LANE), F32),
        compiler_params=_cparams("arbitrary"),
        name="moe_experts",
    )(tile_expert, n_valid, xs, w_in.astype(BF16), w_out.astype(BF16))

    return pl.pallas_call(
        functools.partial(_moe_combine_body, tm=tc, nt=ntc),
        grid=(ntc,),
        in_specs=[smem_cur, smem_cur, smem_nxt, smem_nxt,
                  pl.BlockSpec(memory_space=pl.ANY),
                  pl.BlockSpec((tc, Dm), row),
                  pl.BlockSpec((tc, 2), row),
                  pl.BlockSpec((1, 1, Dm), lambda i: (i // nbc, 0, 0))],
        out_specs=pl.BlockSpec((tc, Dm), row),
        out_shape=jax.ShapeDtypeStruct((T, Dm), F32),
        scratch_shapes=[pltpu.VMEM((tile_rows, LANE), F32)] * 4 + [pltpu.SemaphoreType.DMA((2,))],
        compiler_params=_cparams("arbitrary"),
        name="moe_combine",
    )(d0, d1, d0, d1, y, x2, w[:2].T, g)


def kernel(x, c, norm_mix_g, norm_ffn_g, w_ada, b_ada, conv_w_pw1, conv_b_pw1, conv_w_dw, conv_b_dw, conv_ln_g, conv_ln_b, conv_w_pw2, conv_b_pw2, nsa_w_in, nsa_w_cmp1, nsa_w_cmp2, nsa_cmp_pos, nsa_q_gain, nsa_k_gain, nsa_w_out, mla_w_in, mla_q_lat_g, mla_kv_lat_g, mla_w_uq, mla_w_ukv, mla_q_gain, mla_k_gain, mla_w_out, router_w, router_bias, moe_w_in, moe_w_out):
    B, S, Dm = x.shape
    depth = w_ada.shape[0]
    mods = _ada(c, w_ada, b_ada)
    x2 = x.reshape(B * S, Dm)
    for i in range(depth):
        sh1, sc1, g1, sh2, sc2, g2 = [m.reshape(B, 1, Dm) for m in jnp.split(mods[i], 6, axis=-1)]
        a1 = norm_mix_g[i] * (1.0 + sc1)
        kind, j = i % 3, i // 3
        if kind == 0:
            x2 = _conv_mixer(x2, a1, sh1, g1, conv_w_pw1[j], conv_b_pw1[j], conv_w_dw[j], conv_b_dw[j],
                             conv_ln_g[j], conv_ln_b[j], conv_w_pw2[j], conv_b_pw2[j], S)
        elif kind == 1:
            x2 = _nsa_mixer(x2, a1, sh1, g1, nsa_w_in[j], nsa_w_cmp1[j], nsa_w_cmp2[j], nsa_cmp_pos[j],
                            nsa_q_gain[j], nsa_k_gain[j], nsa_w_out[j], B, S)
        else:
            x2 = _mla_mixer(x2, a1, sh1, g1, mla_w_in[j], mla_q_lat_g[j], mla_kv_lat_g[j], mla_w_uq[j],
                            mla_w_ukv[j], mla_q_gain[j], mla_k_gain[j], mla_w_out[j], B, S)
        a2 = norm_ffn_g[i] * (1.0 + sc2)
        x2 = _grouped_moe(x2, a2, sh2, g2, router_w, router_bias, moe_w_in[i], moe_w_out[i], S)
    return x2.reshape(B, S, Dm)
```

```python
import functools
import math

import numpy as np
import jax
import jax.numpy as jnp
from jax import lax
from jax.experimental import pallas as pl
from jax.experimental.pallas import tpu as pltpu

F32 = jnp.float32
BF16 = jnp.bfloat16
I32 = jnp.int32
HIGHEST = lax.Precision.HIGHEST

EPS = 1e-6
NEG = -1e30
BIG = 1e30
ROPE_THETA = 500000.0
LANE = 128
VMEM_LIMIT = 56 * 1024 * 1024

D_MODEL = 1024
CONV_WIDTH = 31
CONV_HALO = 32

NSA_HEADS = 16
NSA_GROUPS = 4
NSA_REP = NSA_HEADS // NSA_GROUPS
NSA_DH = 64
NSA_ROT = 16
CMP_BLOCK = 32
CMP_STRIDE = 16
SLC_BLOCK = 64
SLC_TOP_N = 16
WINDOW = 512
MAX_SLC = 128

MLA_HEADS = 16
MLA_Q_LORA = 384
MLA_KV_LORA = 256
MLA_NOPE = 64
MLA_ROPE = 32
MLA_V = 64
MLA_QK = MLA_NOPE + MLA_ROPE

N_EXPERTS = 16
N_GROUPS = 4
EXPERTS_PER_GROUP = 4
D_EXPERT = 512
MOE_TILE = 512

_NT = (((1,), (1,)), ((), ()))


def _cparams(*sem):
    return pltpu.CompilerParams(dimension_semantics=sem, vmem_limit_bytes=VMEM_LIMIT)


def _sigmoid(x):
    return 1.0 / (1.0 + jnp.exp(-x))


def _silu(x):
    return x * _sigmoid(x)


def _normmod(x, a, b):
    ms = jnp.mean(x * x, axis=-1, keepdims=True)
    return x * lax.rsqrt(ms + EPS) * a + b


def _rope(x, c, sa, sb, half):
    n = x.shape[-1]
    return x * c + pltpu.roll(x, n - half, 1) * sa + pltpu.roll(x, half, 1) * sb


def _ada_body(c_ref, w_ref, b_ref, o_ref):
    c = c_ref[...]
    o_ref[0] = jnp.dot(_silu(c), w_ref[0], preferred_element_type=F32, precision=HIGHEST) + b_ref[0]


def _ada(c, w_ada, b_ada):
    B = c.shape[0]
    L, Dm, N = w_ada.shape
    Bp = -(-B // 8) * 8
    cp = jnp.pad(c, ((0, Bp - B), (0, 0)))
    tn = 1536
    out = pl.pallas_call(
        _ada_body,
        grid=(L, N // tn),
        in_specs=[pl.BlockSpec((Bp, Dm), lambda l, j: (0, 0)),
                  pl.BlockSpec((1, Dm, tn), lambda l, j: (l, 0, j)),
                  pl.BlockSpec((1, 1, tn), lambda l, j: (l, 0, j))],
        out_specs=pl.BlockSpec((1, Bp, tn), lambda l, j: (l, 0, j)),
        out_shape=jax.ShapeDtypeStruct((L, Bp, N), F32),
        compiler_params=_cparams("arbitrary", "arbitrary"),
        name="adaln",
    )(cp, w_ada, b_ada.reshape(L, 1, N))
    return out[:, :B]


def _conv_pw1_body(x_ref, a_ref, b_ref, w_ref, bias_ref, u_ref):
    h = _normmod(x_ref[...], a_ref[0], b_ref[0]).astype(BF16)
    r = jnp.dot(h, w_ref[...], preferred_element_type=F32) + bias_ref[...]
    d = u_ref.shape[-1]
    u_ref[...] = r[:, :d] * _sigmoid(r[:, d:])


def _conv_dw_body(u_ref, uh_ref, wdw_ref, bdw_ref, lng_ref, lnb_ref, w2_ref, b2_ref, x_ref, g_ref,
                  o_ref, ext_ref, acc_ref, *, tm, seq):
    i = pl.program_id(0)
    at_seq_start = (i * tm) % seq == 0
    ext_ref[0:CONV_HALO, :] = jnp.where(at_seq_start, 0.0, uh_ref[...])
    ext_ref[CONV_HALO:, :] = u_ref[...]
    base = CONV_HALO - (CONV_WIDTH - 1)
    rc = 64
    for lc in range(u_ref.shape[-1] // LANE):
        ls = slice(lc * LANE, (lc + 1) * LANE)
        wl = wdw_ref[:, ls]
        bl = bdw_ref[:, ls]
        for r0 in range(0, tm, rc):
            e = ext_ref[r0:r0 + rc + CONV_HALO, ls]
            acc = bl
            for s in range(8):
                rows = rc if s == 0 else rc + 8
                p = None
                for a in range((base + CONV_WIDTH - 1) // 8 + 1):
                    k = 8 * a + s - base
                    if 0 <= k < CONV_WIDTH and 8 * a + rows <= rc + CONV_HALO:
                        term = wl[k:k + 1, :] * e[8 * a:8 * a + rows, :]
                        p = term if p is None else p + term
                acc = acc + (p if s == 0 else p[s:s + rc, :])
            acc_ref[r0:r0 + rc, ls] = acc
    acc = acc_ref[...]
    mu = jnp.mean(acc, axis=-1, keepdims=True)
    dlt = acc - mu
    var = jnp.mean(dlt * dlt, axis=-1, keepdims=True)
    y = dlt * lax.rsqrt(var + EPS) * lng_ref[...] + lnb_ref[...]
    z = _silu(y).astype(BF16)
    out = jnp.dot(z, w2_ref[...], preferred_element_type=F32) + b2_ref[...]
    o_ref[...] = x_ref[...] + g_ref[0] * out


def _conv_mixer(x2, a, b, g, w_pw1, b_pw1, w_dw, b_dw, ln_g, ln_b, w_pw2, b_pw2, seq):
    T, Dm = x2.shape
    tm = 512
    nb = seq // tm
    row = lambda i: (i, 0)
    per_b = lambda i: (i // nb, 0, 0)
    full = lambda i: (0, 0)
    u = pl.pallas_call(
        _conv_pw1_body,
        grid=(T // tm,),
        in_specs=[pl.BlockSpec((tm, Dm), row),
                  pl.BlockSpec((1, 1, Dm), per_b),
                  pl.BlockSpec((1, 1, Dm), per_b),
                  pl.BlockSpec((Dm, 2 * Dm), full),
                  pl.BlockSpec((1, 2 * Dm), full)],
        out_specs=pl.BlockSpec((tm, Dm), row),
        out_shape=jax.ShapeDtypeStruct((T, Dm), F32),
        compiler_params=_cparams("parallel"),
        name="conv_pw1_glu",
    )(x2, a, b, w_pw1.astype(BF16), b_pw1.reshape(1, -1))
    hb = tm // CONV_HALO
    wdw = jnp.pad(w_dw, ((0, CONV_HALO - CONV_WIDTH), (0, 0)))
    vec = lambda v: v.reshape(1, -1)
    return pl.pallas_call(
        functools.partial(_conv_dw_body, tm=tm, seq=seq),
        grid=(T // tm,),
        in_specs=[pl.BlockSpec((tm, Dm), row),
                  pl.BlockSpec((CONV_HALO, Dm), lambda i: (jnp.maximum(i * hb - 1, 0), 0)),
                  pl.BlockSpec((CONV_HALO, Dm), full),
                  pl.BlockSpec((1, Dm), full),
                  pl.BlockSpec((1, Dm), full),
                  pl.BlockSpec((1, Dm), full),
                  pl.BlockSpec((Dm, Dm), full),
                  pl.BlockSpec((1, Dm), full),
                  pl.BlockSpec((tm, Dm), row),
                  pl.BlockSpec((1, 1, Dm), per_b)],
        out_specs=pl.BlockSpec((tm, Dm), row),
        out_shape=jax.ShapeDtypeStruct((T, Dm), F32),
        scratch_shapes=[pltpu.VMEM((tm + CONV_HALO, Dm), F32), pltpu.VMEM((tm, Dm), F32)],
        compiler_params=_cparams("parallel"),
        name="conv_dw_ln_pw2",
    )(u, u, wdw, vec(b_dw), vec(ln_g), vec(ln_b), w_pw2.astype(BF16), vec(b_pw2), x2, g)


FLASH_FIRST, FLASH_LAST, FLASH_MASKED = 1, 2, 4
FLASH_CHUNK = 16
SEL_OFF = -(2.0 ** 100)
ONE_LANE = 64
LOG2E = 1.4426950408889634
FLASH_NPROB = 4


def _one_lane():
    return jnp.where(lax.broadcasted_iota(I32, (1, LANE), 1) == ONE_LANE, 1.0, 0.0)


def _flash_body(qi_ref, kj_ref, flag_ref, *refs, mode, rep, tq, tk, nprob):
    if mode == "select":
        q_ref, k_ref, v_ref, selb_ref, o_ref, qa_sc, s_sc, p_sc, mb_sc, al_sc, acc_sc = refs
    else:
        q_ref, k_ref, v_ref, o_ref, s_sc, p_sc, mb_sc, al_sc, acc_sc = refs
    step = pl.program_id(2)
    qi = qi_ref[step]
    kj = kj_ref[step]
    flag = flag_ref[step]
    rows = rep * tq
    ch = FLASH_CHUNK
    nl = tk // LANE
    lane_fold = lambda t, op: functools.reduce(op, [t[:, i * LANE:(i + 1) * LANE] for i in range(nl)])

    @pl.when((flag & FLASH_FIRST) != 0)
    def _():
        mb_sc[...] = jnp.full(mb_sc.shape, NEG, F32)
        acc_sc[...] = jnp.zeros(acc_sc.shape, F32)
        if mode == "select":
            for pr in range(nprob):
                qa_sc[pr, :, :LANE] = q_ref[pr * rep:(pr + 1) * rep].reshape(rows, LANE)
                qa_sc[pr, :, LANE:] = jnp.concatenate([selb_ref[pr]] * rep, axis=0)

    def process(masked):
        thr = qi * tq - kj * tk
        for pr in range(nprob):
            q = qa_sc[pr] if mode == "select" else q_ref[pr * rep:(pr + 1) * rep].reshape(rows, LANE)
            s_sc[pr] = lax.dot_general(q, k_ref[pr], _NT, preferred_element_type=F32)
        if masked:
            diff = lax.broadcasted_iota(I32, (ch, tk), 1) - lax.broadcasted_iota(I32, (ch, tk), 0)
        for pr in range(nprob):
            for r0 in range(0, rows, ch):
                sc = s_sc[pr, r0:r0 + ch, :]
                if masked:
                    lim = thr + (r0 % tq)
                    ok = diff <= lim
                    if mode == "window":
                        ok = ok & (diff > lim - WINDOW)
                    sc = jnp.where(ok, sc, NEG)
                    s_sc[pr, r0:r0 + ch, :] = sc
                m_prev = mb_sc[pr, r0:r0 + ch, :]
                m_new = jnp.maximum(m_prev, jnp.max(lane_fold(sc, jnp.maximum), axis=1, keepdims=True))
                al_sc[pr, r0:r0 + ch, :] = jnp.exp2(m_prev - m_new)
                mb_sc[pr, r0:r0 + ch, :] = m_new
            for r0 in range(0, rows, ch):
                mb = mb_sc[pr, r0:r0 + ch, :]
                p = jnp.exp2(s_sc[pr, r0:r0 + ch, :] - jnp.concatenate([mb] * nl, axis=1))
                p_sc[pr, r0:r0 + ch, :] = p.astype(BF16)
            acc_sc[pr] = al_sc[pr] * acc_sc[pr] + jnp.dot(p_sc[pr], v_ref[pr], preferred_element_type=F32)

    if mode == "window":
        pl.when(kj >= 0)(lambda: process(True))
    else:
        pl.when((kj >= 0) & ((flag & FLASH_MASKED) != 0))(lambda: process(True))
        pl.when((kj >= 0) & ((flag & FLASH_MASKED) == 0))(lambda: process(False))

    @pl.when((flag & FLASH_LAST) != 0)
    def _():
        for pr in range(nprob):
            acc = acc_sc[pr]
            o = acc / acc[:, ONE_LANE:ONE_LANE + 1]
            o_ref[pr * rep:(pr + 1) * rep] = o.reshape(rep, tq, LANE).astype(o_ref.dtype)


def _flash_schedule(seq, tq, tk, mode):
    qi, kj, flags = [], [], []
    for i in range(seq // tq):
        hi = (i * tq + tq - 1) // tk
        lo = 0 if mode != "window" else (i * tq - (WINDOW - 1)) // tk
        js = list(range(lo, hi + 1))
        for n, j in enumerate(js):
            crosses_diagonal = (j + 1) * tk - 1 > i * tq
            qi.append(i)
            kj.append(j if j >= 0 else -1)
            flags.append((FLASH_FIRST if n == 0 else 0) | (FLASH_LAST if n == len(js) - 1 else 0)
                         | (FLASH_MASKED if crosses_diagonal else 0))
    as_i32 = lambda v: jnp.asarray(np.asarray(v, np.int32))
    return as_i32(qi), as_i32(kj), as_i32(flags), len(qi)


def _flash(q, k, v, selb, *, batch, seq, mode, tq, tk):
    hq, T, _ = q.shape
    hkv = k.shape[0]
    kw = k.shape[-1]
    rep = hq // hkv
    npb = FLASH_NPROB
    assert tq & (tq - 1) == 0 and tq % FLASH_CHUNK == 0 and hkv % npb == 0
    qi, kj, flags, nsteps = _flash_schedule(seq, tq, tk, mode)
    nq, nk = seq // tq, seq // tk
    q_map = lambda g, b, s, qi, kj, fl: (g, b * nq + qi[s], 0)
    k_map = lambda g, b, s, qi, kj, fl: (g, b * nk + jnp.maximum(kj[s], 0), 0)
    in_specs = [pl.BlockSpec((npb * rep, tq, LANE), q_map),
                pl.BlockSpec((npb, tk, kw), k_map),
                pl.BlockSpec((npb, tk, LANE), k_map)]
    args = [q, k, v]
    rows = rep * tq
    scratch = []
    if mode == "select":
        in_specs.append(pl.BlockSpec((npb, tq, LANE), q_map))
        args.append(selb)
        scratch.append(pltpu.VMEM((npb, rows, kw), BF16))
    scratch += [pltpu.VMEM((npb, rows, tk), F32), pltpu.VMEM((npb, rows, tk), BF16),
                pltpu.VMEM((npb, rows, LANE), F32), pltpu.VMEM((npb, rows, LANE), F32),
                pltpu.VMEM((npb, rows, LANE), F32)]
    return pl.pallas_call(
        functools.partial(_flash_body, mode=mode, rep=rep, tq=tq, tk=tk, nprob=npb),
        grid_spec=pltpu.PrefetchScalarGridSpec(
            num_scalar_prefetch=3,
            grid=(hkv // npb, batch, nsteps),
            in_specs=in_specs,
            out_specs=pl.BlockSpec((npb * rep, tq, LANE), q_map),
            scratch_shapes=scratch),
        out_shape=jax.ShapeDtypeStruct((hq, T, LANE), BF16),
        compiler_params=_cparams("parallel", "parallel", "arbitrary"),
        name="flash_" + mode,
    )(qi, kj, flags, *args)


def _attn_out_body(*refs, n_branch, heads):
    o_refs = refs[:n_branch]
    if n_branch > 1:
        gl_ref, ex_ref, w_ref, x_ref, g_ref, out_ref = refs[n_branch:]
        spread = jnp.dot(_sigmoid(gl_ref[...]).astype(BF16), ex_ref[...], preferred_element_type=F32)
    else:
        w_ref, x_ref, g_ref, out_ref = refs[n_branch:]
    per_head = []
    for h in range(heads):
        if n_branch > 1:
            o = jnp.zeros(o_refs[0].shape[1:], F32)
            for c in range(n_branch):
                col = n_branch * h + c
                o = o + spread[:, col * LANE:(col + 1) * LANE] * o_refs[c][h].astype(F32)
            o = o.astype(BF16)
        else:
            o = o_refs[0][h]
        per_head.append(o)
    acc = jnp.dot(jnp.concatenate(per_head, axis=1), w_ref[...], preferred_element_type=F32)
    out_ref[...] = x_ref[...] + g_ref[0] * acc


def _attn_out(os, gl, w_heads, x2, g, seq):
    T, Dm = x2.shape
    heads = w_heads.shape[0]
    tm = 256
    nb = seq // tm
    row = lambda i: (i, 0)
    o_spec = pl.BlockSpec((heads, tm, LANE), lambda i: (0, i, 0))
    in_specs = [o_spec] * len(os)
    args = list(os)
    if len(os) > 1:
        ncol = len(os) * heads
        spread = (jnp.arange(LANE)[:, None] == jnp.arange(ncol * LANE)[None, :] // LANE).astype(BF16)
        in_specs += [pl.BlockSpec((tm, LANE), row), pl.BlockSpec((LANE, ncol * LANE), lambda i: (0, 0))]
        args += [gl, spread]
    in_specs += [pl.BlockSpec((heads * LANE, Dm), lambda i: (0, 0)),
                 pl.BlockSpec((tm, Dm), row),
                 pl.BlockSpec((1, 1, Dm), lambda i: (i // nb, 0, 0))]
    args += [w_heads.reshape(heads * LANE, Dm), x2, g]
    return pl.pallas_call(
        functools.partial(_attn_out_body, n_branch=len(os), heads=heads),
        grid=(T // tm,),
        in_specs=in_specs,
        out_specs=pl.BlockSpec((tm, Dm), row),
        out_shape=jax.ShapeDtypeStruct((T, Dm), F32),
        compiler_params=_cparams("parallel"),
        name="attn_out_%d" % len(os),
    )(*args)


def _pad_heads_rows(w, heads, dh):
    w = w.reshape(heads, dh, -1)
    return jnp.pad(w, ((0, 0), (0, LANE - dh), (0, 0))).astype(BF16)


def _pad_heads_cols(w, heads, dh):
    k = w.shape[0]
    w = w.reshape(k, heads, dh)
    return jnp.pad(w, ((0, 0), (0, 0), (0, LANE - dh))).reshape(k, heads * LANE)


def _rope_tables(pos, rot, offset):
    half = rot // 2
    inv_freq = ROPE_THETA ** (-jnp.arange(0, rot, 2, dtype=F32) / rot)
    ang = pos.astype(F32)[:, None] * inv_freq[None, :]
    cos, sin = jnp.cos(ang), jnp.sin(ang)
    n = pos.shape[0]
    c = jnp.ones((n, LANE), F32).at[:, offset:offset + rot].set(jnp.concatenate([cos, cos], axis=1))
    sa = jnp.zeros((n, LANE), F32).at[:, offset:offset + half].set(-sin)
    sb = jnp.zeros((n, LANE), F32).at[:, offset + half:offset + rot].set(sin)
    return c, sa, sb


def _mla_proj_body(x_ref, a_ref, b_ref, win_ref, qlg_ref, kvlg_ref, wuq_ref, wuqs_ref, wuk_ref, wuv_ref,
                   ones_ref, q1_ref, q2_ref, k1_ref, k2_ref, q_ref, k_ref, v_ref):
    h = _normmod(x_ref[...], a_ref[0], b_ref[0]).astype(BF16)
    r = jnp.dot(h, win_ref[...], preferred_element_type=F32)
    lat = MLA_Q_LORA + MLA_KV_LORA
    q_lat = r[:, :MLA_Q_LORA]
    kv_lat = r[:, MLA_Q_LORA:lat]
    kpe = r[:, lat:lat + LANE]
    kpe_swap = r[:, lat + LANE:]
    ql = q_lat * lax.rsqrt(jnp.mean(q_lat * q_lat, axis=-1, keepdims=True) + EPS) * qlg_ref[...]
    kvl = kv_lat * lax.rsqrt(jnp.mean(kv_lat * kv_lat, axis=-1, keepdims=True) + EPS) * kvlg_ref[...]
    ql = ql.astype(BF16)
    kvl = kvl.astype(BF16)
    q = jnp.dot(ql, wuq_ref[...], preferred_element_type=F32)
    q_swap = jnp.dot(ql, wuqs_ref[...], preferred_element_type=F32)
    kn = jnp.dot(kvl, wuk_ref[...], preferred_element_type=F32)
    v = jnp.dot(kvl, wuv_ref[...], preferred_element_type=F32)
    ones = ones_ref[...]
    q1, q2, k1, k2 = q1_ref[...], q2_ref[...], k1_ref[...], k2_ref[...]
    k_rot = kpe_swap * k2

    def inv_rms(t):
        ss = jnp.dot((t * t).astype(BF16), ones, preferred_element_type=F32)
        return lax.rsqrt(ss * (1.0 / MLA_QK) + EPS)

    for hd in range(MLA_HEADS):
        sl = slice(hd * LANE, (hd + 1) * LANE)
        xq = q[:, sl]
        q_ref[hd] = ((xq * q1 + q_swap[:, sl] * q2) * inv_rms(xq)).astype(BF16)
        xk = kn[:, sl] + kpe
        k_ref[hd] = ((xk * k1 + k_rot) * inv_rms(xk)).astype(BF16)
        v_ref[hd] = (v[:, sl] + _one_lane()).astype(BF16)


def _mla_mixer(x2, a, b, g, w_in, q_lat_g, kv_lat_g, w_uq, w_ukv, q_gain, k_gain, w_out, batch, seq):
    T, Dm = x2.shape
    H = MLA_HEADS
    tm = 256
    nb = seq // tm
    scale = MLA_QK ** -0.5 * LOG2E
    lat = MLA_Q_LORA + MLA_KV_LORA
    half = MLA_ROPE // 2

    def swap_rope(t):
        lo, hi = t[..., MLA_NOPE:MLA_NOPE + half], t[..., MLA_NOPE + half:MLA_QK]
        return jnp.concatenate([jnp.zeros_like(t[..., :MLA_NOPE]), hi, lo], axis=-1)

    to_slot = lambda t: jnp.pad(t, [(0, 0)] * (t.ndim - 1) + [(0, LANE - MLA_QK)])
    kpe_w = jnp.concatenate([jnp.zeros((Dm, MLA_NOPE), F32), w_in[:, lat:]], axis=1)
    win_p = jnp.concatenate([w_in[:, :lat], to_slot(kpe_w), to_slot(swap_rope(kpe_w))], axis=1).astype(BF16)
    wuq3 = w_uq.reshape(MLA_Q_LORA, H, MLA_QK)
    wuq_p = to_slot(wuq3).reshape(MLA_Q_LORA, H * LANE).astype(BF16)
    wuqs_p = to_slot(swap_rope(wuq3)).reshape(MLA_Q_LORA, H * LANE).astype(BF16)
    wukv = w_ukv.reshape(MLA_KV_LORA, H, MLA_NOPE + MLA_V)
    wuk_p = _pad_heads_cols(wukv[:, :, :MLA_NOPE].reshape(MLA_KV_LORA, -1), H, MLA_NOPE).astype(BF16)
    wuv_p = _pad_heads_cols(wukv[:, :, MLA_NOPE:].reshape(MLA_KV_LORA, -1), H, MLA_V).astype(BF16)
    c, sa, sb = _rope_tables(jnp.arange(seq), MLA_ROPE, MLA_NOPE)
    qg = q_gain * scale
    q1, q2 = c * to_slot(qg)[None], (sa + sb) * to_slot(swap_rope(qg))[None]
    k1, k2 = c * to_slot(k_gain)[None], (sa + sb) * to_slot(swap_rope(k_gain))[None]
    ones = jnp.ones((LANE, LANE), BF16)
    row = lambda i: (i, 0)
    full = lambda i: (0, 0)
    per_b = lambda i: (i // nb, 0, 0)
    pos = lambda i: (i % nb, 0)
    head_out = pl.BlockSpec((H, tm, LANE), lambda i: (0, i, 0))
    hshape = jax.ShapeDtypeStruct((H, T, LANE), BF16)
    q, k, v = pl.pallas_call(
        _mla_proj_body,
        grid=(T // tm,),
        in_specs=[pl.BlockSpec((tm, Dm), row),
                  pl.BlockSpec((1, 1, Dm), per_b),
                  pl.BlockSpec((1, 1, Dm), per_b),
                  pl.BlockSpec(win_p.shape, full),
                  pl.BlockSpec((1, MLA_Q_LORA), full),
                  pl.BlockSpec((1, MLA_KV_LORA), full),
                  pl.BlockSpec(wuq_p.shape, full),
                  pl.BlockSpec(wuqs_p.shape, full),
                  pl.BlockSpec(wuk_p.shape, full),
                  pl.BlockSpec(wuv_p.shape, full),
                  pl.BlockSpec((LANE, LANE), full),
                  pl.BlockSpec((tm, LANE), pos),
                  pl.BlockSpec((tm, LANE), pos),
                  pl.BlockSpec((tm, LANE), pos),
                  pl.BlockSpec((tm, LANE), pos)],
        out_specs=[head_out, head_out, head_out],
        out_shape=[hshape, hshape, hshape],
        compiler_params=_cparams("parallel"),
        name="mla_proj",
    )(x2, a, b, win_p, q_lat_g.reshape(1, -1), kv_lat_g.reshape(1, -1), wuq_p, wuqs_p, wuk_p, wuv_p,
      ones, q1, q2, k1, k2)
    blk = min(1024, seq)
    o = _flash(q, k, v, None, batch=batch, seq=seq, mode="causal", tq=blk, tk=blk)
    return _attn_out([o], None, _pad_heads_rows(w_out, H, MLA_V), x2, g, seq)


N_KV_STREAMS = 6


def _nsa_proj_body(x_ref, a_ref, b_ref, w_ref, qg_ref, kg_ref, c_ref, sa_ref, sb_ref, blk_ref,
                   q_ref, craw_ref, ks_ref, vs_ref, kw_ref, vw_ref, gl_ref):
    h = _normmod(x_ref[...], a_ref[0], b_ref[0]).astype(BF16)
    r = jnp.dot(h, w_ref[...], preferred_element_type=F32)
    c, sa, sb = c_ref[...], sa_ref[...], sb_ref[...]

    def norm_rot(t, gain):
        t = t * lax.rsqrt(jnp.sum(t * t, axis=-1, keepdims=True) * (1.0 / NSA_DH) + EPS) * gain
        return _rope(t, c, sa, sb, NSA_ROT // 2)

    for hd in range(NSA_HEADS):
        q_ref[hd] = norm_rot(r[:, hd * LANE:(hd + 1) * LANE], qg_ref[...]).astype(BF16)
    base = NSA_HEADS * LANE
    for st in range(N_KV_STREAMS):
        for gi in range(NSA_GROUPS):
            off = base + (st * NSA_GROUPS + gi) * LANE
            t = r[:, off:off + LANE]
            if st < 2:
                craw_ref[st * NSA_GROUPS + gi] = t
            elif st == 2:
                t = norm_rot(t, kg_ref[1:2, :])
                ks_ref[gi] = jnp.concatenate([t.astype(BF16), blk_ref[...]], axis=1)
            elif st == 4:
                kw_ref[gi] = norm_rot(t, kg_ref[2:3, :]).astype(BF16)
            else:
                (vs_ref if st == 3 else vw_ref)[gi] = (t + _one_lane()).astype(BF16)
    gl_ref[...] = r[:, base + N_KV_STREAMS * NSA_GROUPS * LANE:]


def _nsa_compress_body(x_ref, pea_ref, peb_ref, w1a_ref, w1b_ref, w2_ref, kg_ref, c_ref, sa_ref, sb_ref,
                       o_ref, *, is_key, n_cmp):
    ncp = o_ref.shape[1]
    x = jnp.concatenate([x_ref[0, pl.ds(l, ncp, stride=CMP_STRIDE), :] for l in range(CMP_STRIDE)], axis=1)
    xa = (x + pea_ref[...]).astype(BF16)
    xb = (x + peb_ref[...]).astype(BF16)
    za = jnp.dot(xa, w1a_ref[...], preferred_element_type=F32)
    zb = jnp.dot(xb, w1b_ref[...], preferred_element_type=F32)
    rows = za.shape[0]
    z = _silu(za + pltpu.roll(zb, rows - 1, 0))
    t = jnp.dot(z.astype(BF16), w2_ref[...], preferred_element_type=F32)
    if is_key:
        t = t * lax.rsqrt(jnp.sum(t * t, axis=-1, keepdims=True) * (1.0 / NSA_DH) + EPS) * kg_ref[...]
        t = _rope(t, c_ref[...], sa_ref[...], sb_ref[...], NSA_ROT // 2)
    valid = lax.broadcasted_iota(I32, t.shape, 0) < n_cmp
    o_ref[0] = jnp.where(valid, t, 0.0).astype(BF16)


def _nsa_cmp_select_body(q_ref, kc_ref, vc_ref, oc_ref, sel_ref, p_sc, *, tq, limits, n_cmp, n_top):
    i = pl.program_id(2)
    t0 = i * tq
    rep = q_ref.shape[0]
    per = SLC_BLOCK // CMP_STRIDE

    def compute(limit):
        nc = limit // CMP_STRIDE
        ns = limit // SLC_BLOCK
        kc = kc_ref[0, :nc, :]
        vc = vc_ref[0, :nc, :]
        q = q_ref[...].reshape(rep * tq, LANE)
        sc = lax.dot_general(q, kc, _NT, preferred_element_type=F32)
        qpos = t0 + lax.broadcasted_iota(I32, (tq, nc), 0)
        blk = lax.broadcasted_iota(I32, (tq, nc), 1)
        mask = ((blk * CMP_STRIDE + (CMP_BLOCK - 1)) <= qpos) & (blk < n_cmp)
        sc = jnp.where(mask[None], sc.reshape(rep, tq, nc), NEG)
        p = jnp.exp2(sc - jnp.max(sc, axis=-1, keepdims=True))
        p = jnp.where(mask[None], p / jnp.sum(p, axis=-1, keepdims=True), 0.0)
        oc = jnp.dot(p.reshape(rep * tq, nc).astype(BF16), vc, preferred_element_type=F32)
        oc_ref[...] = oc.reshape(rep, tq, LANE).astype(BF16)
        qpos_t = t0 + lax.broadcasted_iota(I32, (nc, tq), 1)
        blk_t = lax.broadcasted_iota(I32, (nc, tq), 0)
        mask_t = ((blk_t * CMP_STRIDE + (CMP_BLOCK - 1)) <= qpos_t) & (blk_t < n_cmp)
        psum = jnp.zeros((nc, tq), F32)
        for r in range(rep):
            st = lax.dot_general(kc, q_ref[r], _NT, preferred_element_type=F32)
            st = jnp.where(mask_t, st, NEG)
            pt = jnp.exp2(st - jnp.max(st, axis=0, keepdims=True))
            psum = psum + jnp.where(mask_t, pt / jnp.sum(pt, axis=0, keepdims=True), 0.0)
        slabs = []
        for sb in range(tq // LANE):
            p_sc[sb, 0:8, :] = jnp.zeros((8, LANE), F32)
            p_sc[sb, 8:8 + nc, :] = psum[:, sb * LANE:(sb + 1) * LANE]
            part = p_sc[sb, pl.ds(7, ns, stride=per), :]
            for k in range(1, per + 1):
                part = part + p_sc[sb, pl.ds(7 + k, ns, stride=per), :]
            slabs.append(part)
        imp = jnp.concatenate(slabs, axis=1)
        jb = lax.broadcasted_iota(I32, (ns, tq), 0)
        qp = t0 + lax.broadcasted_iota(I32, (ns, tq), 1)
        imp = jnp.where(jb * SLC_BLOCK <= qp, imp, -BIG)
        imp = jnp.where((jb == 0) | (jb == jnp.right_shift(qp, 6)), BIG, imp)
        jbf = jb.astype(F32)
        sel_t = jnp.full((ns, tq), SEL_OFF, F32)
        for _ in range(n_top):
            top = jnp.max(imp, axis=0, keepdims=True)
            first = jnp.min(jnp.where(imp == top, jbf, float(MAX_SLC)), axis=0, keepdims=True)
            hit = jbf == first
            sel_t = jnp.where(hit, 0.0, sel_t)
            imp = jnp.where(hit, -jnp.inf, imp)
        if ns < MAX_SLC:
            sel_t = jnp.concatenate([sel_t, jnp.full((MAX_SLC - ns, tq), SEL_OFF, F32)], axis=0)
        sel_ref[0] = sel_t.T.astype(BF16)

    end = t0 + tq
    lower = 0
    for limit in limits:
        pl.when((end > lower) & (end <= limit))(functools.partial(compute, limit))
        lower = limit


def _nsa_mixer(x2, a, b, g, w_in, w_cmp1, w_cmp2, cmp_pos, q_gain, k_gain, w_out, batch, seq):
    T, Dm = x2.shape
    H, G, dh = NSA_HEADS, NSA_GROUPS, NSA_DH
    scale = dh ** -0.5 * LOG2E
    n_cmp = seq // CMP_STRIDE - 1
    ncp = seq // CMP_STRIDE
    n_slc = seq // SLC_BLOCK
    assert n_slc <= MAX_SLC and ncp <= MAX_SLC * (SLC_BLOCK // CMP_STRIDE)
    n_top = min(SLC_TOP_N, n_slc)
    tm = 256
    nb = seq // tm
    q_cols = _pad_heads_cols(w_in[:, :H * dh], H, dh)
    kv_cols = _pad_heads_cols(w_in[:, H * dh:H * dh + N_KV_STREAMS * G * dh], N_KV_STREAMS * G, dh)
    gl_cols = jnp.pad(w_in[:, H * dh + N_KV_STREAMS * G * dh:], ((0, 0), (0, LANE - 3 * H)))
    w_p = jnp.concatenate([q_cols, kv_cols, gl_cols], axis=1).astype(BF16)
    pad_gain = lambda v: jnp.pad(v, ((0, 0), (0, LANE - dh)))
    c, sa, sb = _rope_tables(jnp.arange(seq), NSA_ROT, 0)
    row = lambda i: (i, 0)
    full = lambda i: (0, 0)
    per_b = lambda i: (i // nb, 0, 0)
    pos = lambda i: (i % nb, 0)
    blk_onehot = (jnp.arange(seq)[:, None] // SLC_BLOCK == jnp.arange(MAX_SLC)[None, :]).astype(BF16)
    group_out = pl.BlockSpec((G, tm, LANE), lambda i: (0, i, 0))
    group_shape = jax.ShapeDtypeStruct((G, T, LANE), BF16)
    q, craw, ks, vs, kw, vw, gl = pl.pallas_call(
        _nsa_proj_body,
        grid=(T // tm,),
        in_specs=[pl.BlockSpec((tm, Dm), row),
                  pl.BlockSpec((1, 1, Dm), per_b),
                  pl.BlockSpec((1, 1, Dm), per_b),
                  pl.BlockSpec(w_p.shape, full),
                  pl.BlockSpec((1, LANE), full),
                  pl.BlockSpec((3, LANE), full),
                  pl.BlockSpec((tm, LANE), pos),
                  pl.BlockSpec((tm, LANE), pos),
                  pl.BlockSpec((tm, LANE), pos),
                  pl.BlockSpec((tm, MAX_SLC), pos)],
        out_specs=[pl.BlockSpec((H, tm, LANE), lambda i: (0, i, 0)),
                   pl.BlockSpec((2 * G, tm, LANE), lambda i: (0, i, 0)),
                   pl.BlockSpec((G, tm, LANE + MAX_SLC), lambda i: (0, i, 0)),
                   group_out, group_out, group_out,
                   pl.BlockSpec((tm, LANE), row)],
        out_shape=[jax.ShapeDtypeStruct((H, T, LANE), BF16),
                   jax.ShapeDtypeStruct((2 * G, T, LANE), F32),
                   jax.ShapeDtypeStruct((G, T, LANE + MAX_SLC), BF16),
                   group_shape, group_shape, group_shape,
                   jax.ShapeDtypeStruct((T, LANE), F32)],
        compiler_params=_cparams("parallel"),
        name="nsa_proj",
    )(x2, a, b, w_p, pad_gain(q_gain.reshape(1, dh) * scale), pad_gain(k_gain), c, sa, sb, blk_onehot)

    cmp_end = jnp.arange(ncp) * CMP_STRIDE + (CMP_BLOCK - 1)
    cc, csa, csb = _rope_tables(cmp_end, NSA_ROT, 0)
    kdim = CMP_STRIDE * LANE

    def compress(stream, w1, w2, pe, is_key):
        w1p = jnp.pad(w1, ((0, 0), (0, LANE - dh), (0, LANE - dh)))
        w1a = w1p[:CMP_STRIDE].reshape(kdim, LANE).astype(BF16)
        w1b = w1p[CMP_STRIDE:].reshape(kdim, LANE).astype(BF16)
        pep = jnp.pad(pe, ((0, 0), (0, LANE - dh)))
        pea = pep[:CMP_STRIDE].reshape(1, kdim)
        peb = pep[CMP_STRIDE:].reshape(1, kdim)
        w2p = jnp.pad(w2, ((0, LANE - dh), (0, LANE - dh))).astype(BF16)
        const = lambda gi, bi: (0, 0)
        return pl.pallas_call(
            functools.partial(_nsa_compress_body, is_key=is_key, n_cmp=n_cmp),
            grid=(G, batch),
            in_specs=[pl.BlockSpec((1, seq, LANE), lambda gi, bi: (stream * G + gi, bi, 0)),
                      pl.BlockSpec((1, kdim), const),
                      pl.BlockSpec((1, kdim), const),
                      pl.BlockSpec((kdim, LANE), const),
                      pl.BlockSpec((kdim, LANE), const),
                      pl.BlockSpec((LANE, LANE), const),
                      pl.BlockSpec((1, LANE), const),
                      pl.BlockSpec((ncp, LANE), const),
                      pl.BlockSpec((ncp, LANE), const),
                      pl.BlockSpec((ncp, LANE), const)],
            out_specs=pl.BlockSpec((1, ncp, LANE), lambda gi, bi: (gi, bi, 0)),
            out_shape=jax.ShapeDtypeStruct((G, batch * ncp, LANE), BF16),
            compiler_params=_cparams("parallel", "parallel"),
            name="nsa_compress_" + ("k" if is_key else "v"),
        )(craw, pea, peb, w1a, w1b, w2p, pad_gain(k_gain)[0:1], cc, csa, csb)

    kc = compress(0, w_cmp1[0], w_cmp2[0], cmp_pos[0], True)
    vc = compress(1, w_cmp1[1], w_cmp2[1], cmp_pos[1], False)

    tq = 256
    nq = seq // tq
    limits = tuple(sorted({max(seq // d, min(seq, LANE * CMP_STRIDE)) for d in (4, 2, 1)}))
    q_map = lambda gi, bi, i: (gi, bi * nq + i, 0)
    c_map = lambda gi, bi, i: (gi, bi, 0)
    o_c, sel = pl.pallas_call(
        functools.partial(_nsa_cmp_select_body, tq=tq, limits=limits, n_cmp=n_cmp, n_top=n_top),
        grid=(G, batch, nq),
        in_specs=[pl.BlockSpec((NSA_REP, tq, LANE), q_map),
                  pl.BlockSpec((1, ncp, LANE), c_map),
                  pl.BlockSpec((1, ncp, LANE), c_map)],
        out_specs=[pl.BlockSpec((NSA_REP, tq, LANE), q_map),
                   pl.BlockSpec((1, tq, LANE), q_map)],
        out_shape=[jax.ShapeDtypeStruct((H, T, LANE), BF16),
                   jax.ShapeDtypeStruct((G, T, LANE), BF16)],
        scratch_shapes=[pltpu.VMEM((tq // LANE, 8 + MAX_SLC * (SLC_BLOCK // CMP_STRIDE), LANE), F32)],
        compiler_params=_cparams("parallel", "parallel", "parallel"),
        name="nsa_cmp_select",
    )(q, kc, vc)

    o_s = _flash(q, ks, vs, sel, batch=batch, seq=seq, mode="select", tq=256, tk=min(1024, seq))
    o_w = _flash(q, kw, vw, None, batch=batch, seq=seq, mode="window", tq=256, tk=WINDOW)
    return _attn_out([o_c, o_s, o_w], gl, _pad_heads_rows(w_out, H, dh), x2, g, seq)


def _router_body(x_ref, a_ref, b_ref, rwt_ref, rb_ref, tri_ref, e_ref, w_ref, rank_ref, cnt_ref,
                 carry_sc, *, tm):
    i = pl.program_id(0)

    @pl.when(i == 0)
    def _():
        carry_sc[...] = jnp.zeros(carry_sc.shape, F32)

    h = _normmod(x_ref[...], a_ref[0], b_ref[0])
    h_hi = h.astype(BF16)
    h_lo = (h - h_hi.astype(F32)).astype(BF16)
    w_hi, w_lo = rwt_ref[0], rwt_ref[1]
    dot_nt = lambda p, q: lax.dot_general(p, q, _NT, preferred_element_type=F32)
    logits = dot_nt(w_hi, h_hi) + (dot_nt(w_hi, h_lo) + dot_nt(w_lo, h_hi))
    scores = _sigmoid(logits)
    biased = scores + rb_ref[...]
    ng, per = N_GROUPS, EXPERTS_PER_GROUP
    row = lambda arr, r: arr[r:r + 1, :]
    gsel = jnp.zeros((1, tm), I32)
    best = None
    for gi in range(ng):
        v = [row(biased, gi * per + k) for k in range(per)]
        top2 = None
        for p in range(per):
            for q in range(p + 1, per):
                s = v[p] + v[q]
                top2 = s if top2 is None else jnp.maximum(top2, s)
        if best is None:
            best = top2
        else:
            better = top2 > best
            gsel = jnp.where(better, gi, gsel)
            best = jnp.where(better, top2, best)
    cb, cs = [], []
    for k in range(per):
        b_k = row(biased, k)
        s_k = row(scores, k)
        for gi in range(1, ng):
            hit = gsel == gi
            b_k = jnp.where(hit, row(biased, gi * per + k), b_k)
            s_k = jnp.where(hit, row(scores, gi * per + k), s_k)
        cb.append(b_k)
        cs.append(s_k)

    def argmax_first(vals):
        idx = jnp.zeros((1, tm), I32)
        top = vals[0]
        for k in range(1, per):
            better = vals[k] > top
            idx = jnp.where(better, k, idx)
            top = jnp.where(better, vals[k], top)
        return idx

    def pick(vals, idx):
        out = vals[0]
        for k in range(1, per):
            out = jnp.where(idx == k, vals[k], out)
        return out

    i1 = argmax_first(cb)
    i2 = argmax_first([jnp.where(i1 == k, -jnp.inf, cb[k]) for k in range(per)])
    w1 = pick(cs, i1)
    w2 = pick(cs, i2)
    tot = w1 + w2
    e1 = gsel * per + i1
    e2 = gsel * per + i2
    eid = lax.broadcasted_iota(I32, (N_EXPERTS, tm), 0)
    hot = (eid == e1) | (eid == e2)
    onehot = jnp.where(hot, 1.0, 0.0)
    before = carry_sc[...] + jnp.dot(onehot.astype(BF16), tri_ref[...], preferred_element_type=F32)
    r1 = jnp.sum(jnp.where(eid == e1, before, 0.0), axis=0, keepdims=True)
    r2 = jnp.sum(jnp.where(eid == e2, before, 0.0), axis=0, keepdims=True)
    carry = carry_sc[...] + jnp.sum(onehot, axis=1, keepdims=True)
    carry_sc[...] = carry
    cnt_ref[...] = jnp.broadcast_to(carry, cnt_ref.shape)
    zi = jnp.zeros((6, tm), I32)
    e_ref[...] = jnp.concatenate([e1, e2, zi], axis=0)
    rank_ref[...] = jnp.concatenate([r1.astype(I32), r2.astype(I32), zi], axis=0)
    w_ref[...] = jnp.concatenate([w1 / tot, w2 / tot, jnp.zeros((6, tm), F32)], axis=0)


ROW_SUB = D_MODEL // LANE


def _row_tile(r):
    return pl.ds(pl.multiple_of(r * ROW_SUB, ROW_SUB), ROW_SUB)


def _to_row_tiles(ref, val):
    n = val.shape[0]
    for s in range(ROW_SUB):
        ref[pl.ds(s, n, stride=ROW_SUB), :] = val[:, s * LANE:(s + 1) * LANE]


def _from_row_tiles(ref, n, s):
    return ref[pl.ds(s, n, stride=ROW_SUB), :]


def _moe_dispatch_body(starts_ref, counts_ref, padded_ref, nv_ref, d0_ref, d1_ref,
                       x_ref, a_ref, b_ref, xs_hbm, hbuf0, hbuf1, zbuf, sems, zsem,
                       *, tm, tg, n_tiles, nt):
    i = pl.program_id(0)

    @pl.when(i == 0)
    def _():
        zbuf[...] = jnp.zeros(zbuf.shape, F32)
        zrow = zbuf.at[pl.ds(0, ROW_SUB)]
        for ex in range(N_EXPERTS):
            lo = starts_ref[ex] + counts_ref[ex]
            hi = starts_ref[ex] + padded_ref[ex]

            def fill(r, carry):
                pltpu.make_async_copy(zrow, xs_hbm.at[_row_tile(r)], zsem).start()
                return carry

            lax.fori_loop(lo, hi, fill, 0)

            def drain_fill(r, carry):
                pltpu.make_async_copy(zrow, xs_hbm.at[_row_tile(0)], zsem).wait()
                return carry

            lax.fori_loop(lo, hi, drain_fill, 0)

        rows_per_tile = tg * ROW_SUB

        def fill_tile(t, carry):
            dst = xs_hbm.at[pl.ds(pl.multiple_of(t * rows_per_tile, rows_per_tile), rows_per_tile)]
            pltpu.make_async_copy(zbuf, dst, zsem).start()
            return carry

        lax.fori_loop(nv_ref[0], n_tiles, fill_tile, 0)

        def drain_tile(t, carry):
            pltpu.make_async_copy(zbuf, xs_hbm.at[pl.ds(0, rows_per_tile)], zsem).wait()
            return carry

        lax.fori_loop(nv_ref[0], n_tiles, drain_tile, 0)

    h = _normmod(x_ref[...], a_ref[0], b_ref[0])

    def scatter_from(hbuf, sem, other_buf, other_sem):
        _to_row_tiles(hbuf, h)

        def row_copy(buf, sm, r, slot):
            return pltpu.make_async_copy(buf.at[_row_tile(r)], xs_hbm.at[_row_tile(slot)], sm)

        def issue(r, carry):
            row_copy(hbuf, sem, r, d0_ref[r]).start()
            row_copy(hbuf, sem, r, d1_ref[r]).start()
            return carry

        lax.fori_loop(0, tm, issue, 0, unroll=8)

        def drain(buf, sm):
            def body(r, carry):
                row_copy(buf, sm, r, 0).wait()
                row_copy(buf, sm, r, 0).wait()
                return carry

            lax.fori_loop(0, tm, body, 0, unroll=8)

        pl.when(i > 0)(lambda: drain(other_buf, other_sem))
        pl.when(i == nt - 1)(lambda: drain(hbuf, sem))

    pl.when(i % 2 == 0)(lambda: scatter_from(hbuf0, sems.at[0], hbuf1, sems.at[1]))
    pl.when(i % 2 == 1)(lambda: scatter_from(hbuf1, sems.at[1], hbuf0, sems.at[0]))


def _moe_expert_body(te_ref, nv_ref, xs_ref, win_ref, wout_ref, y_ref, *, tg):
    i = pl.program_id(0)

    @pl.when(i < nv_ref[0])
    def _():
        x = jnp.concatenate([_from_row_tiles(xs_ref, tg, s) for s in range(ROW_SUB)], axis=1).astype(BF16)
        gu = jnp.dot(x, win_ref[0], preferred_element_type=F32)
        act = (_silu(gu[:, :D_EXPERT]) * gu[:, D_EXPERT:]).astype(BF16)
        _to_row_tiles(y_ref, jnp.dot(act, wout_ref[0], preferred_element_type=F32))

    @pl.when(i >= nv_ref[0])
    def _():
        y_ref[...] = jnp.zeros(y_ref.shape, F32)


def _moe_combine_body(d0c_ref, d1c_ref, d0n_ref, d1n_ref, y_hbm, x_ref, w_ref, g_ref, o_ref,
                      ya0, ya1, yb0, yb1, sems, *, tm, nt):
    i = pl.program_id(0)

    def row_copy(slot, buf, sem, r):
        return pltpu.make_async_copy(y_hbm.at[_row_tile(slot)], buf.at[_row_tile(r)], sem)

    def issue(d0_ref, d1_ref, bufs, sem):
        def body(r, carry):
            row_copy(d0_ref[r], bufs[0], sem, r).start()
            row_copy(d1_ref[r], bufs[1], sem, r).start()
            return carry

        lax.fori_loop(0, tm, body, 0, unroll=8)

    def finish(bufs, sem):
        def body(r, carry):
            row_copy(0, bufs[0], sem, r).wait()
            row_copy(0, bufs[1], sem, r).wait()
            return carry

        lax.fori_loop(0, tm, body, 0, unroll=8)
        w = w_ref[...]
        w0, w1 = w[:, 0:1], w[:, 1:2]
        gate = g_ref[0]
        for s in range(ROW_SUB):
            sl = slice(s * LANE, (s + 1) * LANE)
            y = w0 * _from_row_tiles(bufs[0], tm, s) + w1 * _from_row_tiles(bufs[1], tm, s)
            o_ref[:, sl] = x_ref[:, sl] + gate[:, sl] * y

    set_a, set_b = (ya0, ya1), (yb0, yb1)
    pl.when(i == 0)(lambda: issue(d0c_ref, d1c_ref, set_a, sems.at[0]))

    def even():
        pl.when(i + 1 < nt)(lambda: issue(d0n_ref, d1n_ref, set_b, sems.at[1]))
        finish(set_a, sems.at[0])

    def odd():
        pl.when(i + 1 < nt)(lambda: issue(d0n_ref, d1n_ref, set_a, sems.at[0]))
        finish(set_b, sems.at[1])

    pl.when(i % 2 == 0)(even)
    pl.when(i % 2 == 1)(odd)


def _grouped_moe(x2, a, b, g, router_w, router_bias, w_in, w_out, seq):
    T, Dm = x2.shape
    E = N_EXPERTS
    tm = 512
    nb = seq // tm
    nt = T // tm
    row = lambda i: (i, 0)
    full = lambda i: (0, 0)
    per_b = lambda i: (i // nb, 0, 0)
    lanes = lambda i: (0, i)
    tri = jnp.asarray(np.triu(np.ones((tm, tm), np.float32), 1)).astype(BF16)
    rw_hi = router_w.T.astype(BF16)
    rw_split = jnp.stack([rw_hi, (router_w.T - rw_hi.astype(F32)).astype(BF16)])
    e, w, rank, cnt = pl.pallas_call(
        functools.partial(_router_body, tm=tm),
        grid=(nt,),
        in_specs=[pl.BlockSpec((tm, Dm), row),
                  pl.BlockSpec((1, 1, Dm), per_b),
                  pl.BlockSpec((1, 1, Dm), per_b),
                  pl.BlockSpec((2, E, Dm), lambda i: (0, 0, 0)),
                  pl.BlockSpec((E, 1), full),
                  pl.BlockSpec((tm, tm), full)],
        out_specs=[pl.BlockSpec((8, tm), lanes),
                   pl.BlockSpec((8, tm), lanes),
                   pl.BlockSpec((8, tm), lanes),
                   pl.BlockSpec((E, LANE), full)],
        out_shape=[jax.ShapeDtypeStruct((8, T), I32),
                   jax.ShapeDtypeStruct((8, T), F32),
                   jax.ShapeDtypeStruct((8, T), I32),
                   jax.ShapeDtypeStruct((E, LANE), F32)],
        scratch_shapes=[pltpu.VMEM((E, 1), F32)],
        compiler_params=_cparams("arbitrary"),
        name="moe_router",
    )(x2, a, b, rw_split, router_bias.reshape(E, 1), tri)

    tg = MOE_TILE
    n_tiles = (2 * T) // tg + E
    n_slots = n_tiles * tg
    counts = cnt[:, 0].astype(I32)
    padded = ((counts + tg - 1) // tg) * tg
    ends = jnp.cumsum(padded)
    starts = ends - padded
    tile_start = jnp.arange(n_tiles, dtype=I32) * tg
    tile_expert = jnp.minimum(jnp.sum(tile_start[:, None] >= ends[None, :], axis=1), E - 1).astype(I32)
    n_valid = (ends[-1] // tg).astype(I32).reshape(1)

    tc = 256
    ntc = T // tc
    nbc = seq // tc
    smem_cur = pl.BlockSpec((tc,), lambda i, *_: (i,), memory_space=pltpu.SMEM)
    smem_nxt = pl.BlockSpec((tc,), lambda i, *_: (jnp.minimum(i + 1, ntc - 1),), memory_space=pltpu.SMEM)
    tile_rows = tc * ROW_SUB
    seg_start = functools.reduce(lambda acc, k: jnp.where(e[:2] == k, starts[k], acc), range(E),
                                 jnp.zeros_like(e[:2]))
    dest = seg_start + rank[:2]
    d0, d1 = dest[0], dest[1]
    xs = pl.pallas_call(
        functools.partial(_moe_dispatch_body, tm=tc, tg=tg, n_tiles=n_tiles, nt=ntc),
        grid_spec=pltpu.PrefetchScalarGridSpec(
            num_scalar_prefetch=4,
            grid=(ntc,),
            in_specs=[smem_cur, smem_cur,
                      pl.BlockSpec((tc, Dm), lambda i, *_: (i, 0)),
                      pl.BlockSpec((1, 1, Dm), lambda i, *_: (i // nbc, 0, 0)),
                      pl.BlockSpec((1, 1, Dm), lambda i, *_: (i // nbc, 0, 0))],
            out_specs=pl.BlockSpec(memory_space=pl.ANY),
            scratch_shapes=[pltpu.VMEM((tile_rows, LANE), F32), pltpu.VMEM((tile_rows, LANE), F32),
                            pltpu.VMEM((tg * ROW_SUB, LANE), F32),
                            pltpu.SemaphoreType.DMA((2,)), pltpu.SemaphoreType.DMA(())]),
        out_shape=jax.ShapeDtypeStruct((n_slots * ROW_SUB, LANE), F32),
        compiler_params=_cparams("arbitrary"),
        name="moe_dispatch",
    )(starts.astype(I32), counts, padded, n_valid, d0, d1, x2, a, b)

    last_tile = lambda i, te, nv: (jnp.minimum(i, nv[0] - 1), 0)
    y = pl.pallas_call(
        functools.partial(_moe_expert_body, tg=tg),
        grid_spec=pltpu.PrefetchScalarGridSpec(
            num_scalar_prefetch=2,
            grid=(n_tiles,),
            in_specs=[pl.BlockSpec((tg * ROW_SUB, LANE), last_tile),
                      pl.BlockSpec((1, Dm, 2 * D_EXPERT), lambda i, te, nv: (te[i], 0, 0)),
                      pl.BlockSpec((1, D_EXPERT, Dm), lambda i, te, nv: (te[i], 0, 0))],
            out_specs=pl.BlockSpec((tg * ROW_SUB, LANE), lambda i, te, nv: (i, 0))),
        out_shape=jax.ShapeDtypeStruct((n_slots * ROW_SUB, LANE), F32),
        compiler_params=_cparams("arbitrary"),
        name="moe_experts",
    )(tile_expert, n_valid, xs, w_in.astype(BF16), w_out.astype(BF16))

    return pl.pallas_call(
        functools.partial(_moe_combine_body, tm=tc, nt=ntc),
        grid=(ntc,),
        in_specs=[smem_cur, smem_cur, smem_nxt, smem_nxt,
                  pl.BlockSpec(memory_space=pl.ANY),
                  pl.BlockSpec((tc, Dm), row),
                  pl.BlockSpec((tc, 2), row),
                  pl.BlockSpec((1, 1, Dm), lambda i: (i // nbc, 0, 0))],
        out_specs=pl.BlockSpec((tc, Dm), row),
        out_shape=jax.ShapeDtypeStruct((T, Dm), F32),
        scratch_shapes=[pltpu.VMEM((tile_rows, LANE), F32)] * 4 + [pltpu.SemaphoreType.DMA((2,))],
        compiler_params=_cparams("arbitrary"),
        name="moe_combine",
    )(d0, d1, d0, d1, y, x2, w[:2].T, g)


def kernel(x, c, norm_mix_g, norm_ffn_g, w_ada, b_ada, conv_w_pw1, conv_b_pw1, conv_w_dw, conv_b_dw, conv_ln_g, conv_ln_b, conv_w_pw2, conv_b_pw2, nsa_w_in, nsa_w_cmp1, nsa_w_cmp2, nsa_cmp_pos, nsa_q_gain, nsa_k_gain, nsa_w_out, mla_w_in, mla_q_lat_g, mla_kv_lat_g, mla_w_uq, mla_w_ukv, mla_q_gain, mla_k_gain, mla_w_out, router_w, router_bias, moe_w_in, moe_w_out):
    B, S, Dm = x.shape
    depth = w_ada.shape[0]
    mods = _ada(c, w_ada, b_ada)
    x2 = x.reshape(B * S, Dm)
    for i in range(depth):
        sh1, sc1, g1, sh2, sc2, g2 = [m.reshape(B, 1, Dm) for m in jnp.split(mods[i], 6, axis=-1)]
        a1 = norm_mix_g[i] * (1.0 + sc1)
        kind, j = i % 3, i // 3
        if kind == 0:
            x2 = _conv_mixer(x2, a1, sh1, g1, conv_w_pw1[j], conv_b_pw1[j], conv_w_dw[j], conv_b_dw[j],
                             conv_ln_g[j], conv_ln_b[j], conv_w_pw2[j], conv_b_pw2[j], S)
        elif kind == 1:
            x2 = _nsa_mixer(x2, a1, sh1, g1, nsa_w_in[j], nsa_w_cmp1[j], nsa_w_cmp2[j], nsa_cmp_pos[j],
                            nsa_q_gain[j], nsa_k_gain[j], nsa_w_out[j], B, S)
        else:
            x2 = _mla_mixer(x2, a1, sh1, g1, mla_w_in[j], mla_q_lat_g[j], mla_kv_lat_g[j], mla_w_uq[j],
                            mla_w_ukv[j], mla_q_gain[j], mla_k_gain[j], mla_w_out[j], B, S)
        a2 = norm_ffn_g[i] * (1.0 + sc2)
        x2 = _grouped_moe(x2, a2, sh2, g2, router_w, router_bias, moe_w_in[i], moe_w_out[i], S)
    return x2.reshape(B, S, Dm)
```

```python
import functools
import math

import numpy as np
import jax
import jax.numpy as jnp
from jax import lax
from jax.experimental import pallas as pl
from jax.experimental.pallas import tpu as pltpu

F32 = jnp.float32
BF16 = jnp.bfloat16
I32 = jnp.int32
HIGHEST = lax.Precision.HIGHEST

EPS = 1e-6
NEG = -1e30
BIG = 1e30
ROPE_THETA = 500000.0
LANE = 128
VMEM_LIMIT = 56 * 1024 * 1024

D_MODEL = 1024
CONV_WIDTH = 31
CONV_HALO = 32

NSA_HEADS = 16
NSA_GROUPS = 4
NSA_REP = NSA_HEADS // NSA_GROUPS
NSA_DH = 64
NSA_ROT = 16
CMP_BLOCK = 32
CMP_STRIDE = 16
SLC_BLOCK = 64
SLC_TOP_N = 16
WINDOW = 512
MAX_SLC = 128

MLA_HEADS = 16
MLA_Q_LORA = 384
MLA_KV_LORA = 256
MLA_NOPE = 64
MLA_ROPE = 32
MLA_V = 64
MLA_QK = MLA_NOPE + MLA_ROPE

N_EXPERTS = 16
N_GROUPS = 4
EXPERTS_PER_GROUP = 4
D_EXPERT = 512
MOE_TILE = 512

_NT = (((1,), (1,)), ((), ()))


def _cparams(*sem):
    return pltpu.CompilerParams(dimension_semantics=sem, vmem_limit_bytes=VMEM_LIMIT)


def _sigmoid(x):
    return 1.0 / (1.0 + jnp.exp(-x))


def _silu(x):
    return x * _sigmoid(x)


def _normmod(x, a, b):
    ms = jnp.mean(x * x, axis=-1, keepdims=True)
    return x * lax.rsqrt(ms + EPS) * a + b


def _rope(x, c, sa, sb, half):
    n = x.shape[-1]
    return x * c + pltpu.roll(x, n - half, 1) * sa + pltpu.roll(x, half, 1) * sb


def _ada_body(c_ref, w_ref, b_ref, o_ref):
    c = c_ref[...]
    o_ref[0] = jnp.dot(_silu(c), w_ref[0], preferred_element_type=F32, precision=HIGHEST) + b_ref[0]


def _ada(c, w_ada, b_ada):
    B = c.shape[0]
    L, Dm, N = w_ada.shape
    Bp = -(-B // 8) * 8
    cp = jnp.pad(c, ((0, Bp - B), (0, 0)))
    tn = 1536
    out = pl.pallas_call(
        _ada_body,
        grid=(L, N // tn),
        in_specs=[pl.BlockSpec((Bp, Dm), lambda l, j: (0, 0)),
                  pl.BlockSpec((1, Dm, tn), lambda l, j: (l, 0, j)),
                  pl.BlockSpec((1, 1, tn), lambda l, j: (l, 0, j))],
        out_specs=pl.BlockSpec((1, Bp, tn), lambda l, j: (l, 0, j)),
        out_shape=jax.ShapeDtypeStruct((L, Bp, N), F32),
        compiler_params=_cparams("arbitrary", "arbitrary"),
        name="adaln",
    )(cp, w_ada, b_ada.reshape(L, 1, N))
    return out[:, :B]


def _conv_pw1_body(x_ref, a_ref, b_ref, w_ref, bias_ref, u_ref):
    h = _normmod(x_ref[...], a_ref[0], b_ref[0]).astype(BF16)
    r = jnp.dot(h, w_ref[...], preferred_element_type=F32) + bias_ref[...]
    d = u_ref.shape[-1]
    u_ref[...] = r[:, :d] * _sigmoid(r[:, d:])


def _conv_dw_body(u_ref, uh_ref, wdw_ref, bdw_ref, lng_ref, lnb_ref, w2_ref, b2_ref, x_ref, g_ref,
                  o_ref, ext_ref, acc_ref, *, tm, seq):
    i = pl.program_id(0)
    at_seq_start = (i * tm) % seq == 0
    ext_ref[0:CONV_HALO, :] = jnp.where(at_seq_start, 0.0, uh_ref[...])
    ext_ref[CONV_HALO:, :] = u_ref[...]
    base = CONV_HALO - (CONV_WIDTH - 1)
    rc = 64
    for lc in range(u_ref.shape[-1] // LANE):
        ls = slice(lc * LANE, (lc + 1) * LANE)
        wl = wdw_ref[:, ls]
        bl = bdw_ref[:, ls]
        for r0 in range(0, tm, rc):
            e = ext_ref[r0:r0 + rc + CONV_HALO, ls]
            acc = bl
            for s in range(8):
                rows = rc if s == 0 else rc + 8
                p = None
                for a in range((base + CONV_WIDTH - 1) // 8 + 1):
                    k = 8 * a + s - base
                    if 0 <= k < CONV_WIDTH and 8 * a + rows <= rc + CONV_HALO:
                        term = wl[k:k + 1, :] * e[8 * a:8 * a + rows, :]
                        p = term if p is None else p + term
                acc = acc + (p if s == 0 else p[s:s + rc, :])
            acc_ref[r0:r0 + rc, ls] = acc
    acc = acc_ref[...]
    mu = jnp.mean(acc, axis=-1, keepdims=True)
    dlt = acc - mu
    var = jnp.mean(dlt * dlt, axis=-1, keepdims=True)
    y = dlt * lax.rsqrt(var + EPS) * lng_ref[...] + lnb_ref[...]
    z = _silu(y).astype(BF16)
    out = jnp.dot(z, w2_ref[...], preferred_element_type=F32) + b2_ref[...]
    o_ref[...] = x_ref[...] + g_ref[0] * out


def _conv_mixer(x2, a, b, g, w_pw1, b_pw1, w_dw, b_dw, ln_g, ln_b, w_pw2, b_pw2, seq):
    T, Dm = x2.shape
    tm = 512
    nb = seq // tm
    row = lambda i: (i, 0)
    per_b = lambda i: (i // nb, 0, 0)
    full = lambda i: (0, 0)
    u = pl.pallas_call(
        _conv_pw1_body,
        grid=(T // tm,),
        in_specs=[pl.BlockSpec((tm, Dm), row),
                  pl.BlockSpec((1, 1, Dm), per_b),
                  pl.BlockSpec((1, 1, Dm), per_b),
                  pl.BlockSpec((Dm, 2 * Dm), full),
                  pl.BlockSpec((1, 2 * Dm), full)],
        out_specs=pl.BlockSpec((tm, Dm), row),
        out_shape=jax.ShapeDtypeStruct((T, Dm), F32),
        compiler_params=_cparams("parallel"),
        name="conv_pw1_glu",
    )(x2, a, b, w_pw1.astype(BF16), b_pw1.reshape(1, -1))
    hb = tm // CONV_HALO
    wdw = jnp.pad(w_dw, ((0, CONV_HALO - CONV_WIDTH), (0, 0)))
    vec = lambda v: v.reshape(1, -1)
    return pl.pallas_call(
        functools.partial(_conv_dw_body, tm=tm, seq=seq),
        grid=(T // tm,),
        in_specs=[pl.BlockSpec((tm, Dm), row),
                  pl.BlockSpec((CONV_HALO, Dm), lambda i: (jnp.maximum(i * hb - 1, 0), 0)),
                  pl.BlockSpec((CONV_HALO, Dm), full),
                  pl.BlockSpec((1, Dm), full),
                  pl.BlockSpec((1, Dm), full),
                  pl.BlockSpec((1, Dm), full),
                  pl.BlockSpec((Dm, Dm), full),
                  pl.BlockSpec((1, Dm), full),
                  pl.BlockSpec((tm, Dm), row),
                  pl.BlockSpec((1, 1, Dm), per_b)],
        out_specs=pl.BlockSpec((tm, Dm), row),
        out_shape=jax.ShapeDtypeStruct((T, Dm), F32),
        scratch_shapes=[pltpu.VMEM((tm + CONV_HALO, Dm), F32), pltpu.VMEM((tm, Dm), F32)],
        compiler_params=_cparams("parallel"),
        name="conv_dw_ln_pw2",
    )(u, u, wdw, vec(b_dw), vec(ln_g), vec(ln_b), w_pw2.astype(BF16), vec(b_pw2), x2, g)


FLASH_FIRST, FLASH_LAST = 1, 2
FLASH_CLASS_SHIFT = 2
FLASH_COL_STEP = 256
FLASH_CHUNK = 16
SEL_OFF = -(2.0 ** 100)
ONE_LANE = 64
LOG2E = 1.4426950408889634
FLASH_NPROB = 4


def _one_lane():
    return jnp.where(lax.broadcasted_iota(I32, (1, LANE), 1) == ONE_LANE, 1.0, 0.0)


def _flash_body(qi_ref, kj_ref, flag_ref, *refs, mode, rep, tq, tk, nprob, classes):
    if mode == "select":
        q_ref, k_ref, v_ref, selb_ref, o_ref, qa_sc, s_sc, p_sc, mb_sc, al_sc, acc_sc = refs
    else:
        q_ref, k_ref, v_ref, o_ref, s_sc, p_sc, mb_sc, al_sc, acc_sc = refs
    step = pl.program_id(2)
    qi = qi_ref[step]
    kj = kj_ref[step]
    flag = flag_ref[step]
    rows = rep * tq
    ch = FLASH_CHUNK

    @pl.when((flag & FLASH_FIRST) != 0)
    def _():
        mb_sc[...] = jnp.full(mb_sc.shape, NEG, F32)
        acc_sc[...] = jnp.zeros(acc_sc.shape, F32)
        if mode == "select":
            for pr in range(nprob):
                qa_sc[pr, :, :LANE] = q_ref[pr * rep:(pr + 1) * rep].reshape(rows, LANE)
                qa_sc[pr, :, LANE:] = jnp.concatenate([selb_ref[pr]] * rep, axis=0)

    def process(masked, c0, c1):
        w = c1 - c0
        nl = w // LANE
        lane_fold = lambda t, op: functools.reduce(op, [t[:, i * LANE:(i + 1) * LANE] for i in range(nl)])
        thr = qi * tq - kj * tk
        for pr in range(nprob):
            q = qa_sc[pr] if mode == "select" else q_ref[pr * rep:(pr + 1) * rep].reshape(rows, LANE)
            s_sc[pr, :, :w] = lax.dot_general(q, k_ref[pr, c0:c1, :], _NT, preferred_element_type=F32)
        if masked:
            diff = c0 + lax.broadcasted_iota(I32, (ch, w), 1) - lax.broadcasted_iota(I32, (ch, w), 0)
        for pr in range(nprob):
            for r0 in range(0, rows, ch):
                sc = s_sc[pr, r0:r0 + ch, :w]
                if masked:
                    lim = thr + (r0 % tq)
                    ok = diff <= lim
                    if mode == "window":
                        ok = ok & (diff > lim - WINDOW)
                    sc = jnp.where(ok, sc, NEG)
                    s_sc[pr, r0:r0 + ch, :w] = sc
                m_prev = mb_sc[pr, r0:r0 + ch, :]
                m_new = jnp.maximum(m_prev, jnp.max(lane_fold(sc, jnp.maximum), axis=1, keepdims=True))
                al_sc[pr, r0:r0 + ch, :] = jnp.exp2(m_prev - m_new)
                mb_sc[pr, r0:r0 + ch, :] = m_new
            for r0 in range(0, rows, ch):
                mb = mb_sc[pr, r0:r0 + ch, :]
                p = jnp.exp2(s_sc[pr, r0:r0 + ch, :w] - jnp.concatenate([mb] * nl, axis=1))
                p_sc[pr, r0:r0 + ch, :w] = p.astype(BF16)
            acc_sc[pr] = al_sc[pr] * acc_sc[pr] + jnp.dot(p_sc[pr, :, :w], v_ref[pr, c0:c1, :],
                                                           preferred_element_type=F32)

    for n, (masked, c0, c1) in enumerate(classes):
        pl.when((kj >= 0) & ((flag >> FLASH_CLASS_SHIFT) == n))(functools.partial(process, masked, c0, c1))

    @pl.when((flag & FLASH_LAST) != 0)
    def _():
        for pr in range(nprob):
            acc = acc_sc[pr]
            o = acc / acc[:, ONE_LANE:ONE_LANE + 1]
            o_ref[pr * rep:(pr + 1) * rep] = o.reshape(rep, tq, LANE).astype(o_ref.dtype)


def _flash_schedule(seq, tq, tk, mode):
    qi, kj, flags, classes = [], [], [], []
    cw = min(FLASH_COL_STEP, tk)
    for i in range(seq // tq):
        q_lo, q_hi = i * tq, i * tq + tq - 1
        hi = q_hi // tk
        lo = 0 if mode != "window" else (q_lo - (WINDOW - 1)) // tk
        js = list(range(lo, hi + 1))
        for n, j in enumerate(js):
            c1 = min(tk, -(-(q_hi - j * tk + 1) // cw) * cw)
            c0 = 0 if mode != "window" else max(0, (q_lo - (WINDOW - 1) - j * tk) // cw * cw)
            masked = mode == "window" or j * tk + c1 - 1 > q_lo
            cls = (masked, c0, c1)
            if cls not in classes:
                classes.append(cls)
            qi.append(i)
            kj.append(j if j >= 0 else -1)
            flags.append((FLASH_FIRST if n == 0 else 0) | (FLASH_LAST if n == len(js) - 1 else 0)
                         | (classes.index(cls) << FLASH_CLASS_SHIFT))
    as_i32 = lambda v: jnp.asarray(np.asarray(v, np.int32))
    return as_i32(qi), as_i32(kj), as_i32(flags), len(qi), tuple(classes)


def _flash(q, k, v, selb, *, batch, seq, mode, tq, tk):
    hq, T, _ = q.shape
    hkv = k.shape[0]
    kw = k.shape[-1]
    rep = hq // hkv
    npb = FLASH_NPROB
    assert tq & (tq - 1) == 0 and tq % FLASH_CHUNK == 0 and hkv % npb == 0
    qi, kj, flags, nsteps, classes = _flash_schedule(seq, tq, tk, mode)
    nq, nk = seq // tq, seq // tk
    q_map = lambda g, b, s, qi, kj, fl: (g, b * nq + qi[s], 0)
    k_map = lambda g, b, s, qi, kj, fl: (g, b * nk + jnp.maximum(kj[s], 0), 0)
    in_specs = [pl.BlockSpec((npb * rep, tq, LANE), q_map),
                pl.BlockSpec((npb, tk, kw), k_map),
                pl.BlockSpec((npb, tk, LANE), k_map)]
    args = [q, k, v]
    rows = rep * tq
    scratch = []
    if mode == "select":
        in_specs.append(pl.BlockSpec((npb, tq, LANE), q_map))
        args.append(selb)
        scratch.append(pltpu.VMEM((npb, rows, kw), BF16))
    scratch += [pltpu.VMEM((npb, rows, tk), F32), pltpu.VMEM((npb, rows, tk), BF16),
                pltpu.VMEM((npb, rows, LANE), F32), pltpu.VMEM((npb, rows, LANE), F32),
                pltpu.VMEM((npb, rows, LANE), F32)]
    return pl.pallas_call(
        functools.partial(_flash_body, mode=mode, rep=rep, tq=tq, tk=tk, nprob=npb, classes=classes),
        grid_spec=pltpu.PrefetchScalarGridSpec(
            num_scalar_prefetch=3,
            grid=(hkv // npb, batch, nsteps),
            in_specs=in_specs,
            out_specs=pl.BlockSpec((npb * rep, tq, LANE), q_map),
            scratch_shapes=scratch),
        out_shape=jax.ShapeDtypeStruct((hq, T, LANE), BF16),
        compiler_params=_cparams("parallel", "parallel", "arbitrary"),
        name="flash_" + mode,
    )(qi, kj, flags, *args)


def _attn_out_body(*refs, n_branch, heads):
    o_refs = refs[:n_branch]
    if n_branch > 1:
        gl_ref, ex_ref, w_ref, x_ref, g_ref, out_ref = refs[n_branch:]
        spread = jnp.dot(_sigmoid(gl_ref[...]).astype(BF16), ex_ref[...], preferred_element_type=F32)
    else:
        w_ref, x_ref, g_ref, out_ref = refs[n_branch:]
    per_head = []
    for h in range(heads):
        if n_branch > 1:
            o = jnp.zeros(o_refs[0].shape[1:], F32)
            for c in range(n_branch):
                col = n_branch * h + c
                o = o + spread[:, col * LANE:(col + 1) * LANE] * o_refs[c][h].astype(F32)
            o = o.astype(BF16)
        else:
            o = o_refs[0][h]
        per_head.append(o)
    acc = jnp.dot(jnp.concatenate(per_head, axis=1), w_ref[...], preferred_element_type=F32)
    out_ref[...] = x_ref[...] + g_ref[0] * acc


def _attn_out(os, gl, w_heads, x2, g, seq):
    T, Dm = x2.shape
    heads = w_heads.shape[0]
    tm = 256
    nb = seq // tm
    row = lambda i: (i, 0)
    o_spec = pl.BlockSpec((heads, tm, LANE), lambda i: (0, i, 0))
    in_specs = [o_spec] * len(os)
    args = list(os)
    if len(os) > 1:
        ncol = len(os) * heads
        spread = (jnp.arange(LANE)[:, None] == jnp.arange(ncol * LANE)[None, :] // LANE).astype(BF16)
        in_specs += [pl.BlockSpec((tm, LANE), row), pl.BlockSpec((LANE, ncol * LANE), lambda i: (0, 0))]
        args += [gl, spread]
    in_specs += [pl.BlockSpec((heads * LANE, Dm), lambda i: (0, 0)),
                 pl.BlockSpec((tm, Dm), row),
                 pl.BlockSpec((1, 1, Dm), lambda i: (i // nb, 0, 0))]
    args += [w_heads.reshape(heads * LANE, Dm), x2, g]
    return pl.pallas_call(
        functools.partial(_attn_out_body, n_branch=len(os), heads=heads),
        grid=(T // tm,),
        in_specs=in_specs,
        out_specs=pl.BlockSpec((tm, Dm), row),
        out_shape=jax.ShapeDtypeStruct((T, Dm), F32),
        compiler_params=_cparams("parallel"),
        name="attn_out_%d" % len(os),
    )(*args)


def _pad_heads_rows(w, heads, dh):
    w = w.reshape(heads, dh, -1)
    return jnp.pad(w, ((0, 0), (0, LANE - dh), (0, 0))).astype(BF16)


def _pad_heads_cols(w, heads, dh):
    k = w.shape[0]
    w = w.reshape(k, heads, dh)
    return jnp.pad(w, ((0, 0), (0, 0), (0, LANE - dh))).reshape(k, heads * LANE)


def _rope_tables(pos, rot, offset):
    half = rot // 2
    inv_freq = ROPE_THETA ** (-jnp.arange(0, rot, 2, dtype=F32) / rot)
    ang = pos.astype(F32)[:, None] * inv_freq[None, :]
    cos, sin = jnp.cos(ang), jnp.sin(ang)
    n = pos.shape[0]
    c = jnp.ones((n, LANE), F32).at[:, offset:offset + rot].set(jnp.concatenate([cos, cos], axis=1))
    sa = jnp.zeros((n, LANE), F32).at[:, offset:offset + half].set(-sin)
    sb = jnp.zeros((n, LANE), F32).at[:, offset + half:offset + rot].set(sin)
    return c, sa, sb


def _mla_proj_body(x_ref, a_ref, b_ref, win_ref, qlg_ref, kvlg_ref, wuq_ref, wuqs_ref, wuk_ref, wuv_ref,
                   ones_ref, q1_ref, q2_ref, k1_ref, k2_ref, q_ref, k_ref, v_ref):
    h = _normmod(x_ref[...], a_ref[0], b_ref[0]).astype(BF16)
    r = jnp.dot(h, win_ref[...], preferred_element_type=F32)
    lat = MLA_Q_LORA + MLA_KV_LORA
    q_lat = r[:, :MLA_Q_LORA]
    kv_lat = r[:, MLA_Q_LORA:lat]
    kpe = r[:, lat:lat + LANE]
    kpe_swap = r[:, lat + LANE:]
    ql = q_lat * lax.rsqrt(jnp.mean(q_lat * q_lat, axis=-1, keepdims=True) + EPS) * qlg_ref[...]
    kvl = kv_lat * lax.rsqrt(jnp.mean(kv_lat * kv_lat, axis=-1, keepdims=True) + EPS) * kvlg_ref[...]
    ql = ql.astype(BF16)
    kvl = kvl.astype(BF16)
    q = jnp.dot(ql, wuq_ref[...], preferred_element_type=F32)
    q_swap = jnp.dot(ql, wuqs_ref[...], preferred_element_type=F32)
    kn = jnp.dot(kvl, wuk_ref[...], preferred_element_type=F32)
    v = jnp.dot(kvl, wuv_ref[...], preferred_element_type=F32)
    ones = ones_ref[...]
    q1, q2, k1, k2 = q1_ref[...], q2_ref[...], k1_ref[...], k2_ref[...]
    k_rot = kpe_swap * k2

    def inv_rms(t):
        ss = jnp.dot((t * t).astype(BF16), ones, preferred_element_type=F32)
        return lax.rsqrt(ss * (1.0 / MLA_QK) + EPS)

    for hd in range(MLA_HEADS):
        sl = slice(hd * LANE, (hd + 1) * LANE)
        xq = q[:, sl]
        q_ref[hd] = ((xq * q1 + q_swap[:, sl] * q2) * inv_rms(xq)).astype(BF16)
        xk = kn[:, sl] + kpe
        k_ref[hd] = ((xk * k1 + k_rot) * inv_rms(xk)).astype(BF16)
        v_ref[hd] = (v[:, sl] + _one_lane()).astype(BF16)


def _mla_mixer(x2, a, b, g, w_in, q_lat_g, kv_lat_g, w_uq, w_ukv, q_gain, k_gain, w_out, batch, seq):
    T, Dm = x2.shape
    H = MLA_HEADS
    tm = 256
    nb = seq // tm
    scale = MLA_QK ** -0.5 * LOG2E
    lat = MLA_Q_LORA + MLA_KV_LORA
    half = MLA_ROPE // 2

    def swap_rope(t):
        lo, hi = t[..., MLA_NOPE:MLA_NOPE + half], t[..., MLA_NOPE + half:MLA_QK]
        return jnp.concatenate([jnp.zeros_like(t[..., :MLA_NOPE]), hi, lo], axis=-1)

    to_slot = lambda t: jnp.pad(t, [(0, 0)] * (t.ndim - 1) + [(0, LANE - MLA_QK)])
    kpe_w = jnp.concatenate([jnp.zeros((Dm, MLA_NOPE), F32), w_in[:, lat:]], axis=1)
    win_p = jnp.concatenate([w_in[:, :lat], to_slot(kpe_w), to_slot(swap_rope(kpe_w))], axis=1).astype(BF16)
    wuq3 = w_uq.reshape(MLA_Q_LORA, H, MLA_QK)
    wuq_p = to_slot(wuq3).reshape(MLA_Q_LORA, H * LANE).astype(BF16)
    wuqs_p = to_slot(swap_rope(wuq3)).reshape(MLA_Q_LORA, H * LANE).astype(BF16)
    wukv = w_ukv.reshape(MLA_KV_LORA, H, MLA_NOPE + MLA_V)
    wuk_p = _pad_heads_cols(wukv[:, :, :MLA_NOPE].reshape(MLA_KV_LORA, -1), H, MLA_NOPE).astype(BF16)
    wuv_p = _pad_heads_cols(wukv[:, :, MLA_NOPE:].reshape(MLA_KV_LORA, -1), H, MLA_V).astype(BF16)
    c, sa, sb = _rope_tables(jnp.arange(seq), MLA_ROPE, MLA_NOPE)
    qg = q_gain * scale
    q1, q2 = c * to_slot(qg)[None], (sa + sb) * to_slot(swap_rope(qg))[None]
    k1, k2 = c * to_slot(k_gain)[None], (sa + sb) * to_slot(swap_rope(k_gain))[None]
    ones = jnp.ones((LANE, LANE), BF16)
    row = lambda i: (i, 0)
    full = lambda i: (0, 0)
    per_b = lambda i: (i // nb, 0, 0)
    pos = lambda i: (i % nb, 0)
    head_out = pl.BlockSpec((H, tm, LANE), lambda i: (0, i, 0))
    hshape = jax.ShapeDtypeStruct((H, T, LANE), BF16)
    q, k, v = pl.pallas_call(
        _mla_proj_body,
        grid=(T // tm,),
        in_specs=[pl.BlockSpec((tm, Dm), row),
                  pl.BlockSpec((1, 1, Dm), per_b),
                  pl.BlockSpec((1, 1, Dm), per_b),
                  pl.BlockSpec(win_p.shape, full),
                  pl.BlockSpec((1, MLA_Q_LORA), full),
                  pl.BlockSpec((1, MLA_KV_LORA), full),
                  pl.BlockSpec(wuq_p.shape, full),
                  pl.BlockSpec(wuqs_p.shape, full),
                  pl.BlockSpec(wuk_p.shape, full),
                  pl.BlockSpec(wuv_p.shape, full),
                  pl.BlockSpec((LANE, LANE), full),
                  pl.BlockSpec((tm, LANE), pos),
                  pl.BlockSpec((tm, LANE), pos),
                  pl.BlockSpec((tm, LANE), pos),
                  pl.BlockSpec((tm, LANE), pos)],
        out_specs=[head_out, head_out, head_out],
        out_shape=[hshape, hshape, hshape],
        compiler_params=_cparams("parallel"),
        name="mla_proj",
    )(x2, a, b, win_p, q_lat_g.reshape(1, -1), kv_lat_g.reshape(1, -1), wuq_p, wuqs_p, wuk_p, wuv_p,
      ones, q1, q2, k1, k2)
    blk = min(1024, seq)
    o = _flash(q, k, v, None, batch=batch, seq=seq, mode="causal", tq=blk, tk=blk)
    return _attn_out([o], None, _pad_heads_rows(w_out, H, MLA_V), x2, g, seq)


N_KV_STREAMS = 6


def _nsa_proj_body(x_ref, a_ref, b_ref, w_ref, qg_ref, kg_ref, c_ref, sa_ref, sb_ref, blk_ref,
                   q_ref, craw_ref, ks_ref, vs_ref, kw_ref, vw_ref, gl_ref):
    h = _normmod(x_ref[...], a_ref[0], b_ref[0]).astype(BF16)
    r = jnp.dot(h, w_ref[...], preferred_element_type=F32)
    c, sa, sb = c_ref[...], sa_ref[...], sb_ref[...]

    def norm_rot(t, gain):
        t = t * lax.rsqrt(jnp.sum(t * t, axis=-1, keepdims=True) * (1.0 / NSA_DH) + EPS) * gain
        return _rope(t, c, sa, sb, NSA_ROT // 2)

    for hd in range(NSA_HEADS):
        q_ref[hd] = norm_rot(r[:, hd * LANE:(hd + 1) * LANE], qg_ref[...]).astype(BF16)
    base = NSA_HEADS * LANE
    for st in range(N_KV_STREAMS):
        for gi in range(NSA_GROUPS):
            off = base + (st * NSA_GROUPS + gi) * LANE
            t = r[:, off:off + LANE]
            if st < 2:
                craw_ref[st * NSA_GROUPS + gi] = t
            elif st == 2:
                t = norm_rot(t, kg_ref[1:2, :])
                ks_ref[gi] = jnp.concatenate([t.astype(BF16), blk_ref[...]], axis=1)
            elif st == 4:
                kw_ref[gi] = norm_rot(t, kg_ref[2:3, :]).astype(BF16)
            else:
                (vs_ref if st == 3 else vw_ref)[gi] = (t + _one_lane()).astype(BF16)
    gl_ref[...] = r[:, base + N_KV_STREAMS * NSA_GROUPS * LANE:]


def _nsa_compress_body(x_ref, pea_ref, peb_ref, w1a_ref, w1b_ref, w2_ref, kg_ref, c_ref, sa_ref, sb_ref,
                       o_ref, *, is_key, n_cmp):
    ncp = o_ref.shape[1]
    x = jnp.concatenate([x_ref[0, pl.ds(l, ncp, stride=CMP_STRIDE), :] for l in range(CMP_STRIDE)], axis=1)
    xa = (x + pea_ref[...]).astype(BF16)
    xb = (x + peb_ref[...]).astype(BF16)
    za = jnp.dot(xa, w1a_ref[...], preferred_element_type=F32)
    zb = jnp.dot(xb, w1b_ref[...], preferred_element_type=F32)
    rows = za.shape[0]
    z = _silu(za + pltpu.roll(zb, rows - 1, 0))
    t = jnp.dot(z.astype(BF16), w2_ref[...], preferred_element_type=F32)
    if is_key:
        t = t * lax.rsqrt(jnp.sum(t * t, axis=-1, keepdims=True) * (1.0 / NSA_DH) + EPS) * kg_ref[...]
        t = _rope(t, c_ref[...], sa_ref[...], sb_ref[...], NSA_ROT // 2)
    valid = lax.broadcasted_iota(I32, t.shape, 0) < n_cmp
    o_ref[0] = jnp.where(valid, t, 0.0).astype(BF16)


def _nsa_cmp_select_body(q_ref, kc_ref, vc_ref, oc_ref, sel_ref, p_sc, *, tq, limits, n_cmp, n_top):
    i = pl.program_id(2)
    t0 = i * tq
    rep = q_ref.shape[0]
    per = SLC_BLOCK // CMP_STRIDE

    def compute(limit):
        nc = limit // CMP_STRIDE
        ns = limit // SLC_BLOCK
        kc = kc_ref[0, :nc, :]
        vc = vc_ref[0, :nc, :]
        q = q_ref[...].reshape(rep * tq, LANE)
        sc = lax.dot_general(q, kc, _NT, preferred_element_type=F32)
        qpos = t0 + lax.broadcasted_iota(I32, (tq, nc), 0)
        blk = lax.broadcasted_iota(I32, (tq, nc), 1)
        mask = ((blk * CMP_STRIDE + (CMP_BLOCK - 1)) <= qpos) & (blk < n_cmp)
        sc = jnp.where(mask[None], sc.reshape(rep, tq, nc), NEG)
        p = jnp.exp2(sc - jnp.max(sc, axis=-1, keepdims=True))
        p = jnp.where(mask[None], p / jnp.sum(p, axis=-1, keepdims=True), 0.0)
        oc = jnp.dot(p.reshape(rep * tq, nc).astype(BF16), vc, preferred_element_type=F32)
        oc_ref[...] = oc.reshape(rep, tq, LANE).astype(BF16)
        qpos_t = t0 + lax.broadcasted_iota(I32, (nc, tq), 1)
        blk_t = lax.broadcasted_iota(I32, (nc, tq), 0)
        mask_t = ((blk_t * CMP_STRIDE + (CMP_BLOCK - 1)) <= qpos_t) & (blk_t < n_cmp)
        psum = jnp.zeros((nc, tq), F32)
        for r in range(rep):
            st = lax.dot_general(kc, q_ref[r], _NT, preferred_element_type=F32)
            st = jnp.where(mask_t, st, NEG)
            pt = jnp.exp2(st - jnp.max(st, axis=0, keepdims=True))
            psum = psum + jnp.where(mask_t, pt / jnp.sum(pt, axis=0, keepdims=True), 0.0)
        slabs = []
        for sb in range(tq // LANE):
            p_sc[sb, 0:8, :] = jnp.zeros((8, LANE), F32)
            p_sc[sb, 8:8 + nc, :] = psum[:, sb * LANE:(sb + 1) * LANE]
            part = p_sc[sb, pl.ds(7, ns, stride=per), :]
            for k in range(1, per + 1):
                part = part + p_sc[sb, pl.ds(7 + k, ns, stride=per), :]
            slabs.append(part)
        imp = jnp.concatenate(slabs, axis=1)
        jb = lax.broadcasted_iota(I32, (ns, tq), 0)
        qp = t0 + lax.broadcasted_iota(I32, (ns, tq), 1)
        imp = jnp.where(jb * SLC_BLOCK <= qp, imp, -BIG)
        imp = jnp.where((jb == 0) | (jb == jnp.right_shift(qp, 6)), BIG, imp)
        jbf = jb.astype(F32)
        sel_t = jnp.full((ns, tq), SEL_OFF, F32)
        for _ in range(n_top):
            top = jnp.max(imp, axis=0, keepdims=True)
            first = jnp.min(jnp.where(imp == top, jbf, float(MAX_SLC)), axis=0, keepdims=True)
            hit = jbf == first
            sel_t = jnp.where(hit, 0.0, sel_t)
            imp = jnp.where(hit, -jnp.inf, imp)
        if ns < MAX_SLC:
            sel_t = jnp.concatenate([sel_t, jnp.full((MAX_SLC - ns, tq), SEL_OFF, F32)], axis=0)
        sel_ref[0] = sel_t.T.astype(BF16)

    end = t0 + tq
    lower = 0
    for limit in limits:
        pl.when((end > lower) & (end <= limit))(functools.partial(compute, limit))
        lower = limit


def _nsa_mixer(x2, a, b, g, w_in, w_cmp1, w_cmp2, cmp_pos, q_gain, k_gain, w_out, batch, seq):
    T, Dm = x2.shape
    H, G, dh = NSA_HEADS, NSA_GROUPS, NSA_DH
    scale = dh ** -0.5 * LOG2E
    n_cmp = seq // CMP_STRIDE - 1
    ncp = seq // CMP_STRIDE
    n_slc = seq // SLC_BLOCK
    assert n_slc <= MAX_SLC and ncp <= MAX_SLC * (SLC_BLOCK // CMP_STRIDE)
    n_top = min(SLC_TOP_N, n_slc)
    tm = 256
    nb = seq // tm
    q_cols = _pad_heads_cols(w_in[:, :H * dh], H, dh)
    kv_cols = _pad_heads_cols(w_in[:, H * dh:H * dh + N_KV_STREAMS * G * dh], N_KV_STREAMS * G, dh)
    gl_cols = jnp.pad(w_in[:, H * dh + N_KV_STREAMS * G * dh:], ((0, 0), (0, LANE - 3 * H)))
    w_p = jnp.concatenate([q_cols, kv_cols, gl_cols], axis=1).astype(BF16)
    pad_gain = lambda v: jnp.pad(v, ((0, 0), (0, LANE - dh)))
    c, sa, sb = _rope_tables(jnp.arange(seq), NSA_ROT, 0)
    row = lambda i: (i, 0)
    full = lambda i: (0, 0)
    per_b = lambda i: (i // nb, 0, 0)
    pos = lambda i: (i % nb, 0)
    blk_onehot = (jnp.arange(seq)[:, None] // SLC_BLOCK == jnp.arange(MAX_SLC)[None, :]).astype(BF16)
    group_out = pl.BlockSpec((G, tm, LANE), lambda i: (0, i, 0))
    group_shape = jax.ShapeDtypeStruct((G, T, LANE), BF16)
    q, craw, ks, vs, kw, vw, gl = pl.pallas_call(
        _nsa_proj_body,
        grid=(T // tm,),
        in_specs=[pl.BlockSpec((tm, Dm), row),
                  pl.BlockSpec((1, 1, Dm), per_b),
                  pl.BlockSpec((1, 1, Dm), per_b),
                  pl.BlockSpec(w_p.shape, full),
                  pl.BlockSpec((1, LANE), full),
                  pl.BlockSpec((3, LANE), full),
                  pl.BlockSpec((tm, LANE), pos),
                  pl.BlockSpec((tm, LANE), pos),
                  pl.BlockSpec((tm, LANE), pos),
                  pl.BlockSpec((tm, MAX_SLC), pos)],
        out_specs=[pl.BlockSpec((H, tm, LANE), lambda i: (0, i, 0)),
                   pl.BlockSpec((2 * G, tm, LANE), lambda i: (0, i, 0)),
                   pl.BlockSpec((G, tm, LANE + MAX_SLC), lambda i: (0, i, 0)),
                   group_out, group_out, group_out,
                   pl.BlockSpec((tm, LANE), row)],
        out_shape=[jax.ShapeDtypeStruct((H, T, LANE), BF16),
                   jax.ShapeDtypeStruct((2 * G, T, LANE), F32),
                   jax.ShapeDtypeStruct((G, T, LANE + MAX_SLC), BF16),
                   group_shape, group_shape, group_shape,
                   jax.ShapeDtypeStruct((T, LANE), F32)],
        compiler_params=_cparams("parallel"),
        name="nsa_proj",
    )(x2, a, b, w_p, pad_gain(q_gain.reshape(1, dh) * scale), pad_gain(k_gain), c, sa, sb, blk_onehot)

    cmp_end = jnp.arange(ncp) * CMP_STRIDE + (CMP_BLOCK - 1)
    cc, csa, csb = _rope_tables(cmp_end, NSA_ROT, 0)
    kdim = CMP_STRIDE * LANE

    def compress(stream, w1, w2, pe, is_key):
        w1p = jnp.pad(w1, ((0, 0), (0, LANE - dh), (0, LANE - dh)))
        w1a = w1p[:CMP_STRIDE].reshape(kdim, LANE).astype(BF16)
        w1b = w1p[CMP_STRIDE:].reshape(kdim, LANE).astype(BF16)
        pep = jnp.pad(pe, ((0, 0), (0, LANE - dh)))
        pea = pep[:CMP_STRIDE].reshape(1, kdim)
        peb = pep[CMP_STRIDE:].reshape(1, kdim)
        w2p = jnp.pad(w2, ((0, LANE - dh), (0, LANE - dh))).astype(BF16)
        const = lambda gi, bi: (0, 0)
        return pl.pallas_call(
            functools.partial(_nsa_compress_body, is_key=is_key, n_cmp=n_cmp),
            grid=(G, batch),
            in_specs=[pl.BlockSpec((1, seq, LANE), lambda gi, bi: (stream * G + gi, bi, 0)),
                      pl.BlockSpec((1, kdim), const),
                      pl.BlockSpec((1, kdim), const),
                      pl.BlockSpec((kdim, LANE), const),
                      pl.BlockSpec((kdim, LANE), const),
                      pl.BlockSpec((LANE, LANE), const),
                      pl.BlockSpec((1, LANE), const),
                      pl.BlockSpec((ncp, LANE), const),
                      pl.BlockSpec((ncp, LANE), const),
                      pl.BlockSpec((ncp, LANE), const)],
            out_specs=pl.BlockSpec((1, ncp, LANE), lambda gi, bi: (gi, bi, 0)),
            out_shape=jax.ShapeDtypeStruct((G, batch * ncp, LANE), BF16),
            compiler_params=_cparams("parallel", "parallel"),
            name="nsa_compress_" + ("k" if is_key else "v"),
        )(craw, pea, peb, w1a, w1b, w2p, pad_gain(k_gain)[0:1], cc, csa, csb)

    kc = compress(0, w_cmp1[0], w_cmp2[0], cmp_pos[0], True)
    vc = compress(1, w_cmp1[1], w_cmp2[1], cmp_pos[1], False)

    tq = 256
    nq = seq // tq
    limits = tuple(sorted({max(seq // d, min(seq, LANE * CMP_STRIDE)) for d in (4, 2, 1)}))
    q_map = lambda gi, bi, i: (gi, bi * nq + i, 0)
    c_map = lambda gi, bi, i: (gi, bi, 0)
    o_c, sel = pl.pallas_call(
        functools.partial(_nsa_cmp_select_body, tq=tq, limits=limits, n_cmp=n_cmp, n_top=n_top),
        grid=(G, batch, nq),
        in_specs=[pl.BlockSpec((NSA_REP, tq, LANE), q_map),
                  pl.BlockSpec((1, ncp, LANE), c_map),
                  pl.BlockSpec((1, ncp, LANE), c_map)],
        out_specs=[pl.BlockSpec((NSA_REP, tq, LANE), q_map),
                   pl.BlockSpec((1, tq, LANE), q_map)],
        out_shape=[jax.ShapeDtypeStruct((H, T, LANE), BF16),
                   jax.ShapeDtypeStruct((G, T, LANE), BF16)],
        scratch_shapes=[pltpu.VMEM((tq // LANE, 8 + MAX_SLC * (SLC_BLOCK // CMP_STRIDE), LANE), F32)],
        compiler_params=_cparams("parallel", "parallel", "parallel"),
        name="nsa_cmp_select",
    )(q, kc, vc)

    o_s = _flash(q, ks, vs, sel, batch=batch, seq=seq, mode="select", tq=256, tk=min(1024, seq))
    o_w = _flash(q, kw, vw, None, batch=batch, seq=seq, mode="window", tq=256, tk=WINDOW)
    return _attn_out([o_c, o_s, o_w], gl, _pad_heads_rows(w_out, H, dh), x2, g, seq)


def _router_body(x_ref, a_ref, b_ref, rwt_ref, rb_ref, tri_ref, e_ref, w_ref, rank_ref, cnt_ref,
                 carry_sc, *, tm):
    i = pl.program_id(0)

    @pl.when(i == 0)
    def _():
        carry_sc[...] = jnp.zeros(carry_sc.shape, F32)

    h = _normmod(x_ref[...], a_ref[0], b_ref[0])
    h_hi = h.astype(BF16)
    h_lo = (h - h_hi.astype(F32)).astype(BF16)
    w_hi, w_lo = rwt_ref[0], rwt_ref[1]
    dot_nt = lambda p, q: lax.dot_general(p, q, _NT, preferred_element_type=F32)
    logits = dot_nt(w_hi, h_hi) + (dot_nt(w_hi, h_lo) + dot_nt(w_lo, h_hi))
    scores = _sigmoid(logits)
    biased = scores + rb_ref[...]
    ng, per = N_GROUPS, EXPERTS_PER_GROUP
    row = lambda arr, r: arr[r:r + 1, :]
    gsel = jnp.zeros((1, tm), I32)
    best = None
    for gi in range(ng):
        v = [row(biased, gi * per + k) for k in range(per)]
        top2 = None
        for p in range(per):
            for q in range(p + 1, per):
                s = v[p] + v[q]
                top2 = s if top2 is None else jnp.maximum(top2, s)
        if best is None:
            best = top2
        else:
            better = top2 > best
            gsel = jnp.where(better, gi, gsel)
            best = jnp.where(better, top2, best)
    cb, cs = [], []
    for k in range(per):
        b_k = row(biased, k)
        s_k = row(scores, k)
        for gi in range(1, ng):
            hit = gsel == gi
            b_k = jnp.where(hit, row(biased, gi * per + k), b_k)
            s_k = jnp.where(hit, row(scores, gi * per + k), s_k)
        cb.append(b_k)
        cs.append(s_k)

    def argmax_first(vals):
        idx = jnp.zeros((1, tm), I32)
        top = vals[0]
        for k in range(1, per):
            better = vals[k] > top
            idx = jnp.where(better, k, idx)
            top = jnp.where(better, vals[k], top)
        return idx

    def pick(vals, idx):
        out = vals[0]
        for k in range(1, per):
            out = jnp.where(idx == k, vals[k], out)
        return out

    i1 = argmax_first(cb)
    i2 = argmax_first([jnp.where(i1 == k, -jnp.inf, cb[k]) for k in range(per)])
    w1 = pick(cs, i1)
    w2 = pick(cs, i2)
    tot = w1 + w2
    e1 = gsel * per + i1
    e2 = gsel * per + i2
    eid = lax.broadcasted_iota(I32, (N_EXPERTS, tm), 0)
    hot = (eid == e1) | (eid == e2)
    onehot = jnp.where(hot, 1.0, 0.0)
    before = carry_sc[...] + jnp.dot(onehot.astype(BF16), tri_ref[...], preferred_element_type=F32)
    r1 = jnp.sum(jnp.where(eid == e1, before, 0.0), axis=0, keepdims=True)
    r2 = jnp.sum(jnp.where(eid == e2, before, 0.0), axis=0, keepdims=True)
    carry = carry_sc[...] + jnp.sum(onehot, axis=1, keepdims=True)
    carry_sc[...] = carry
    cnt_ref[...] = jnp.broadcast_to(carry, cnt_ref.shape)
    zi = jnp.zeros((6, tm), I32)
    e_ref[...] = jnp.concatenate([e1, e2, zi], axis=0)
    rank_ref[...] = jnp.concatenate([r1.astype(I32), r2.astype(I32), zi], axis=0)
    w_ref[...] = jnp.concatenate([w1 / tot, w2 / tot, jnp.zeros((6, tm), F32)], axis=0)


ROW_SUB = D_MODEL // LANE


def _row_tile(r):
    return pl.ds(pl.multiple_of(r * ROW_SUB, ROW_SUB), ROW_SUB)


def _to_row_tiles(ref, val):
    n = val.shape[0]
    for s in range(ROW_SUB):
        ref[pl.ds(s, n, stride=ROW_SUB), :] = val[:, s * LANE:(s + 1) * LANE]


def _from_row_tiles(ref, n, s):
    return ref[pl.ds(s, n, stride=ROW_SUB), :]


def _moe_dispatch_body(starts_ref, counts_ref, padded_ref, nv_ref, d0_ref, d1_ref,
                       x_ref, a_ref, b_ref, xs_hbm, hbuf0, hbuf1, zbuf, sems, zsem,
                       *, tm, tg, n_tiles, nt):
    i = pl.program_id(0)

    @pl.when(i == 0)
    def _():
        zbuf[...] = jnp.zeros(zbuf.shape, F32)
        zrow = zbuf.at[pl.ds(0, ROW_SUB)]
        for ex in range(N_EXPERTS):
            lo = starts_ref[ex] + counts_ref[ex]
            hi = starts_ref[ex] + padded_ref[ex]

            def fill(r, carry):
                pltpu.make_async_copy(zrow, xs_hbm.at[_row_tile(r)], zsem).start()
                return carry

            lax.fori_loop(lo, hi, fill, 0)

            def drain_fill(r, carry):
                pltpu.make_async_copy(zrow, xs_hbm.at[_row_tile(0)], zsem).wait()
                return carry

            lax.fori_loop(lo, hi, drain_fill, 0)

        rows_per_tile = tg * ROW_SUB

        def fill_tile(t, carry):
            dst = xs_hbm.at[pl.ds(pl.multiple_of(t * rows_per_tile, rows_per_tile), rows_per_tile)]
            pltpu.make_async_copy(zbuf, dst, zsem).start()
            return carry

        lax.fori_loop(nv_ref[0], n_tiles, fill_tile, 0)

        def drain_tile(t, carry):
            pltpu.make_async_copy(zbuf, xs_hbm.at[pl.ds(0, rows_per_tile)], zsem).wait()
            return carry

        lax.fori_loop(nv_ref[0], n_tiles, drain_tile, 0)

    h = _normmod(x_ref[...], a_ref[0], b_ref[0])

    def scatter_from(hbuf, sem, other_buf, other_sem):
        _to_row_tiles(hbuf, h)

        def row_copy(buf, sm, r, slot):
            return pltpu.make_async_copy(buf.at[_row_tile(r)], xs_hbm.at[_row_tile(slot)], sm)

        def issue(r, carry):
            row_copy(hbuf, sem, r, d0_ref[r]).start()
            row_copy(hbuf, sem, r, d1_ref[r]).start()
            return carry

        lax.fori_loop(0, tm, issue, 0, unroll=8)

        def drain(buf, sm):
            def body(r, carry):
                row_copy(buf, sm, r, 0).wait()
                row_copy(buf, sm, r, 0).wait()
                return carry

            lax.fori_loop(0, tm, body, 0, unroll=8)

        pl.when(i > 0)(lambda: drain(other_buf, other_sem))
        pl.when(i == nt - 1)(lambda: drain(hbuf, sem))

    pl.when(i % 2 == 0)(lambda: scatter_from(hbuf0, sems.at[0], hbuf1, sems.at[1]))
    pl.when(i % 2 == 1)(lambda: scatter_from(hbuf1, sems.at[1], hbuf0, sems.at[0]))


def _moe_expert_body(te_ref, nv_ref, xs_ref, win_ref, wout_ref, y_ref, *, tg):
    i = pl.program_id(0)

    @pl.when(i < nv_ref[0])
    def _():
        x = jnp.concatenate([_from_row_tiles(xs_ref, tg, s) for s in range(ROW_SUB)], axis=1).astype(BF16)
        gu = jnp.dot(x, win_ref[0], preferred_element_type=F32)
        act = (_silu(gu[:, :D_EXPERT]) * gu[:, D_EXPERT:]).astype(BF16)
        _to_row_tiles(y_ref, jnp.dot(act, wout_ref[0], preferred_element_type=F32))

    @pl.when(i >= nv_ref[0])
    def _():
        y_ref[...] = jnp.zeros(y_ref.shape, F32)


def _moe_combine_body(d0c_ref, d1c_ref, d0n_ref, d1n_ref, y_hbm, x_ref, w_ref, g_ref, o_ref,
                      ya0, ya1, yb0, yb1, sems, *, tm, nt):
    i = pl.program_id(0)

    def row_copy(slot, buf, sem, r):
        return pltpu.make_async_copy(y_hbm.at[_row_tile(slot)], buf.at[_row_tile(r)], sem)

    def issue(d0_ref, d1_ref, bufs, sem):
        def body(r, carry):
            row_copy(d0_ref[r], bufs[0], sem, r).start()
            row_copy(d1_ref[r], bufs[1], sem, r).start()
            return carry

        lax.fori_loop(0, tm, body, 0, unroll=8)

    def finish(bufs, sem):
        def body(r, carry):
            row_copy(0, bufs[0], sem, r).wait()
            row_copy(0, bufs[1], sem, r).wait()
            return carry

        lax.fori_loop(0, tm, body, 0, unroll=8)
        w = w_ref[...]
        w0, w1 = w[:, 0:1], w[:, 1:2]
        gate = g_ref[0]
        for s in range(ROW_SUB):
            sl = slice(s * LANE, (s + 1) * LANE)
            y = w0 * _from_row_tiles(bufs[0], tm, s) + w1 * _from_row_tiles(bufs[1], tm, s)
            o_ref[:, sl] = x_ref[:, sl] + gate[:, sl] * y

    set_a, set_b = (ya0, ya1), (yb0, yb1)
    pl.when(i == 0)(lambda: issue(d0c_ref, d1c_ref, set_a, sems.at[0]))

    def even():
        pl.when(i + 1 < nt)(lambda: issue(d0n_ref, d1n_ref, set_b, sems.at[1]))
        finish(set_a, sems.at[0])

    def odd():
        pl.when(i + 1 < nt)(lambda: issue(d0n_ref, d1n_ref, set_a, sems.at[0]))
        finish(set_b, sems.at[1])

    pl.when(i % 2 == 0)(even)
    pl.when(i % 2 == 1)(odd)


def _grouped_moe(x2, a, b, g, router_w, router_bias, w_in, w_out, seq):
    T, Dm = x2.shape
    E = N_EXPERTS
    tm = 512
    nb = seq // tm
    nt = T // tm
    row = lambda i: (i, 0)
    full = lambda i: (0, 0)
    per_b = lambda i: (i // nb, 0, 0)
    lanes = lambda i: (0, i)
    tri = jnp.asarray(np.triu(np.ones((tm, tm), np.float32), 1)).astype(BF16)
    rw_hi = router_w.T.astype(BF16)
    rw_split = jnp.stack([rw_hi, (router_w.T - rw_hi.astype(F32)).astype(BF16)])
    e, w, rank, cnt = pl.pallas_call(
        functools.partial(_router_body, tm=tm),
        grid=(nt,),
        in_specs=[pl.BlockSpec((tm, Dm), row),
                  pl.BlockSpec((1, 1, Dm), per_b),
                  pl.BlockSpec((1, 1, Dm), per_b),
                  pl.BlockSpec((2, E, Dm), lambda i: (0, 0, 0)),
                  pl.BlockSpec((E, 1), full),
                  pl.BlockSpec((tm, tm), full)],
        out_specs=[pl.BlockSpec((8, tm), lanes),
                   pl.BlockSpec((8, tm), lanes),
                   pl.BlockSpec((8, tm), lanes),
                   pl.BlockSpec((E, LANE), full)],
        out_shape=[jax.ShapeDtypeStruct((8, T), I32),
                   jax.ShapeDtypeStruct((8, T), F32),
                   jax.ShapeDtypeStruct((8, T), I32),
                   jax.ShapeDtypeStruct((E, LANE), F32)],
        scratch_shapes=[pltpu.VMEM((E, 1), F32)],
        compiler_params=_cparams("arbitrary"),
        name="moe_router",
    )(x2, a, b, rw_split, router_bias.reshape(E, 1), tri)

    tg = MOE_TILE
    n_tiles = (2 * T) // tg + E
    n_slots = n_tiles * tg
    counts = cnt[:, 0].astype(I32)
    padded = ((counts + tg - 1) // tg) * tg
    ends = jnp.cumsum(padded)
    starts = ends - padded
    tile_start = jnp.arange(n_tiles, dtype=I32) * tg
    tile_expert = jnp.minimum(jnp.sum(tile_start[:, None] >= ends[None, :], axis=1), E - 1).astype(I32)
    n_valid = (ends[-1] // tg).astype(I32).reshape(1)

    tc = 256
    ntc = T // tc
    nbc = seq // tc
    smem_cur = pl.BlockSpec((tc,), lambda i, *_: (i,), memory_space=pltpu.SMEM)
    smem_nxt = pl.BlockSpec((tc,), lambda i, *_: (jnp.minimum(i + 1, ntc - 1),), memory_space=pltpu.SMEM)
    tile_rows = tc * ROW_SUB
    seg_start = functools.reduce(lambda acc, k: jnp.where(e[:2] == k, starts[k], acc), range(E),
                                 jnp.zeros_like(e[:2]))
    dest = seg_start + rank[:2]
    d0, d1 = dest[0], dest[1]
    xs = pl.pallas_call(
        functools.partial(_moe_dispatch_body, tm=tc, tg=tg, n_tiles=n_tiles, nt=ntc),
        grid_spec=pltpu.PrefetchScalarGridSpec(
            num_scalar_prefetch=4,
            grid=(ntc,),
            in_specs=[smem_cur, smem_cur,
                      pl.BlockSpec((tc, Dm), lambda i, *_: (i, 0)),
                      pl.BlockSpec((1, 1, Dm), lambda i, *_: (i // nbc, 0, 0)),
                      pl.BlockSpec((1, 1, Dm), lambda i, *_: (i // nbc, 0, 0))],
            out_specs=pl.BlockSpec(memory_space=pl.ANY),
            scratch_shapes=[pltpu.VMEM((tile_rows, LANE), F32), pltpu.VMEM((tile_rows, LANE), F32),
                            pltpu.VMEM((tg * ROW_SUB, LANE), F32),
                            pltpu.SemaphoreType.DMA((2,)), pltpu.SemaphoreType.DMA(())]),
        out_shape=jax.ShapeDtypeStruct((n_slots * ROW_SUB, LANE), F32),
        compiler_params=_cparams("arbitrary"),
        name="moe_dispatch",
    )(starts.astype(I32), counts, padded, n_valid, d0, d1, x2, a, b)

    last_tile = lambda i, te, nv: (jnp.minimum(i, nv[0] - 1), 0)
    y = pl.pallas_call(
        functools.partial(_moe_expert_body, tg=tg),
        grid_spec=pltpu.PrefetchScalarGridSpec(
            num_scalar_prefetch=2,
            grid=(n_tiles,),
            in_specs=[pl.BlockSpec((tg * ROW_SUB, LANE), last_tile),
                      pl.BlockSpec((1, Dm, 2 * D_EXPERT), lambda i, te, nv: (te[i], 0, 0)),
                      pl.BlockSpec((1, D_EXPERT, Dm), lambda i, te, nv: (te[i], 0, 0))],
            out_specs=pl.BlockSpec((tg * ROW_SUB, LANE), lambda i, te, nv: (i, 0))),
        out_shape=jax.ShapeDtypeStruct((n_slots * ROW_SUB, LANE), F32),
        compiler_params=_cparams("arbitrary"),
        name="moe_experts",
    )(tile_expert, n_valid, xs, w_in.astype(BF16), w_out.astype(BF16))

    return pl.pallas_call(
        functools.partial(_moe_combine_body, tm=tc, nt=ntc),
        grid=(ntc,),
        in_specs=[smem_cur, smem_cur, smem_nxt, smem_nxt,
                  pl.BlockSpec(memory_space=pl.ANY),
                  pl.BlockSpec((tc, Dm), row),
                  pl.BlockSpec((tc, 2), row),
                  pl.BlockSpec((1, 1, Dm), lambda i: (i // nbc, 0, 0))],
        out_specs=pl.BlockSpec((tc, Dm), row),
        out_shape=jax.ShapeDtypeStruct((T, Dm), F32),
        scratch_shapes=[pltpu.VMEM((tile_rows, LANE), F32)] * 4 + [pltpu.SemaphoreType.DMA((2,))],
        compiler_params=_cparams("arbitrary"),
        name="moe_combine",
    )(d0, d1, d0, d1, y, x2, w[:2].T, g)


def kernel(x, c, norm_mix_g, norm_ffn_g, w_ada, b_ada, conv_w_pw1, conv_b_pw1, conv_w_dw, conv_b_dw, conv_ln_g, conv_ln_b, conv_w_pw2, conv_b_pw2, nsa_w_in, nsa_w_cmp1, nsa_w_cmp2, nsa_cmp_pos, nsa_q_gain, nsa_k_gain, nsa_w_out, mla_w_in, mla_q_lat_g, mla_kv_lat_g, mla_w_uq, mla_w_ukv, mla_q_gain, mla_k_gain, mla_w_out, router_w, router_bias, moe_w_in, moe_w_out):
    B, S, Dm = x.shape
    depth = w_ada.shape[0]
    mods = _ada(c, w_ada, b_ada)
    x2 = x.reshape(B * S, Dm)
    for i in range(depth):
        sh1, sc1, g1, sh2, sc2, g2 = [m.reshape(B, 1, Dm) for m in jnp.split(mods[i], 6, axis=-1)]
        a1 = norm_mix_g[i] * (1.0 + sc1)
        kind, j = i % 3, i // 3
        if kind == 0:
            x2 = _conv_mixer(x2, a1, sh1, g1, conv_w_pw1[j], conv_b_pw1[j], conv_w_dw[j], conv_b_dw[j],
                             conv_ln_g[j], conv_ln_b[j], conv_w_pw2[j], conv_b_pw2[j], S)
        elif kind == 1:
            x2 = _nsa_mixer(x2, a1, sh1, g1, nsa_w_in[j], nsa_w_cmp1[j], nsa_w_cmp2[j], nsa_cmp_pos[j],
                            nsa_q_gain[j], nsa_k_gain[j], nsa_w_out[j], B, S)
        else:
            x2 = _mla_mixer(x2, a1, sh1, g1, mla_w_in[j], mla_q_lat_g[j], mla_kv_lat_g[j], mla_w_uq[j],
                            mla_w_ukv[j], mla_q_gain[j], mla_k_gain[j], mla_w_out[j], B, S)
        a2 = norm_ffn_g[i] * (1.0 + sc2)
        x2 = _grouped_moe(x2, a2, sh2, g2, router_w, router_bias, moe_w_in[i], moe_w_out[i], S)
    return x2.reshape(B, S, Dm)
```

```python
import functools
import math

import numpy as np
import jax
import jax.numpy as jnp
from jax import lax
from jax.experimental import pallas as pl
from jax.experimental.pallas import tpu as pltpu

F32 = jnp.float32
BF16 = jnp.bfloat16
I32 = jnp.int32
HIGHEST = lax.Precision.HIGHEST

EPS = 1e-6
NEG = -1e30
BIG = 1e30
ROPE_THETA = 500000.0
LANE = 128
VMEM_LIMIT = 56 * 1024 * 1024

D_MODEL = 1024
CONV_WIDTH = 31
CONV_HALO = 32

NSA_HEADS = 16
NSA_GROUPS = 4
NSA_REP = NSA_HEADS // NSA_GROUPS
NSA_DH = 64
NSA_ROT = 16
CMP_BLOCK = 32
CMP_STRIDE = 16
SLC_BLOCK = 64
SLC_TOP_N = 16
WINDOW = 512
MAX_SLC = 128

MLA_HEADS = 16
MLA_Q_LORA = 384
MLA_KV_LORA = 256
MLA_NOPE = 64
MLA_ROPE = 32
MLA_V = 64
MLA_QK = MLA_NOPE + MLA_ROPE

N_EXPERTS = 16
N_GROUPS = 4
EXPERTS_PER_GROUP = 4
D_EXPERT = 512
MOE_TILE = 512

_NT = (((1,), (1,)), ((), ()))


def _cparams(*sem):
    return pltpu.CompilerParams(dimension_semantics=sem, vmem_limit_bytes=VMEM_LIMIT)


def _sigmoid(x):
    return 1.0 / (1.0 + jnp.exp(-x))


def _silu(x):
    return x * _sigmoid(x)


def _normmod(x, a, b):
    ms = jnp.mean(x * x, axis=-1, keepdims=True)
    return x * lax.rsqrt(ms + EPS) * a + b


def _rope(x, c, sa, sb, half):
    n = x.shape[-1]
    return x * c + pltpu.roll(x, n - half, 1) * sa + pltpu.roll(x, half, 1) * sb


def _ada_body(c_ref, w_ref, b_ref, o_ref):
    c = c_ref[...]
    o_ref[0] = jnp.dot(_silu(c), w_ref[0], preferred_element_type=F32, precision=HIGHEST) + b_ref[0]


def _ada(c, w_ada, b_ada):
    B = c.shape[0]
    L, Dm, N = w_ada.shape
    Bp = -(-B // 8) * 8
    cp = jnp.pad(c, ((0, Bp - B), (0, 0)))
    tn = 1536
    out = pl.pallas_call(
        _ada_body,
        grid=(L, N // tn),
        in_specs=[pl.BlockSpec((Bp, Dm), lambda l, j: (0, 0)),
                  pl.BlockSpec((1, Dm, tn), lambda l, j: (l, 0, j)),
                  pl.BlockSpec((1, 1, tn), lambda l, j: (l, 0, j))],
        out_specs=pl.BlockSpec((1, Bp, tn), lambda l, j: (l, 0, j)),
        out_shape=jax.ShapeDtypeStruct((L, Bp, N), F32),
        compiler_params=_cparams("arbitrary", "arbitrary"),
        name="adaln",
    )(cp, w_ada, b_ada.reshape(L, 1, N))
    return out[:, :B]


def _conv_pw1_body(x_ref, a_ref, b_ref, w_ref, bias_ref, u_ref):
    h = _normmod(x_ref[...], a_ref[0], b_ref[0]).astype(BF16)
    r = jnp.dot(h, w_ref[...], preferred_element_type=F32) + bias_ref[...]
    d = u_ref.shape[-1]
    u_ref[...] = r[:, :d] * _sigmoid(r[:, d:])


def _conv_dw_body(u_ref, uh_ref, wdw_ref, bdw_ref, lng_ref, lnb_ref, w2_ref, b2_ref, x_ref, g_ref,
                  o_ref, ext_ref, acc_ref, *, tm, seq):
    i = pl.program_id(0)
    at_seq_start = (i * tm) % seq == 0
    ext_ref[0:CONV_HALO, :] = jnp.where(at_seq_start, 0.0, uh_ref[...])
    ext_ref[CONV_HALO:, :] = u_ref[...]
    base = CONV_HALO - (CONV_WIDTH - 1)
    rc = 64
    for lc in range(u_ref.shape[-1] // LANE):
        ls = slice(lc * LANE, (lc + 1) * LANE)
        wl = wdw_ref[:, ls]
        bl = bdw_ref[:, ls]
        for r0 in range(0, tm, rc):
            e = ext_ref[r0:r0 + rc + CONV_HALO, ls]
            acc = bl
            for s in range(8):
                rows = rc if s == 0 else rc + 8
                p = None
                for a in range((base + CONV_WIDTH - 1) // 8 + 1):
                    k = 8 * a + s - base
                    if 0 <= k < CONV_WIDTH and 8 * a + rows <= rc + CONV_HALO:
                        term = wl[k:k + 1, :] * e[8 * a:8 * a + rows, :]
                        p = term if p is None else p + term
                acc = acc + (p if s == 0 else p[s:s + rc, :])
            acc_ref[r0:r0 + rc, ls] = acc
    acc = acc_ref[...]
    mu = jnp.mean(acc, axis=-1, keepdims=True)
    dlt = acc - mu
    var = jnp.mean(dlt * dlt, axis=-1, keepdims=True)
    y = dlt * lax.rsqrt(var + EPS) * lng_ref[...] + lnb_ref[...]
    z = _silu(y).astype(BF16)
    out = jnp.dot(z, w2_ref[...], preferred_element_type=F32) + b2_ref[...]
    o_ref[...] = x_ref[...] + g_ref[0] * out


def _conv_mixer(x2, a, b, g, w_pw1, b_pw1, w_dw, b_dw, ln_g, ln_b, w_pw2, b_pw2, seq):
    T, Dm = x2.shape
    tm = 512
    nb = seq // tm
    row = lambda i: (i, 0)
    per_b = lambda i: (i // nb, 0, 0)
    full = lambda i: (0, 0)
    u = pl.pallas_call(
        _conv_pw1_body,
        grid=(T // tm,),
        in_specs=[pl.BlockSpec((tm, Dm), row),
                  pl.BlockSpec((1, 1, Dm), per_b),
                  pl.BlockSpec((1, 1, Dm), per_b),
                  pl.BlockSpec((Dm, 2 * Dm), full),
                  pl.BlockSpec((1, 2 * Dm), full)],
        out_specs=pl.BlockSpec((tm, Dm), row),
        out_shape=jax.ShapeDtypeStruct((T, Dm), F32),
        compiler_params=_cparams("parallel"),
        name="conv_pw1_glu",
    )(x2, a, b, w_pw1.astype(BF16), b_pw1.reshape(1, -1))
    hb = tm // CONV_HALO
    wdw = jnp.pad(w_dw, ((0, CONV_HALO - CONV_WIDTH), (0, 0)))
    vec = lambda v: v.reshape(1, -1)
    return pl.pallas_call(
        functools.partial(_conv_dw_body, tm=tm, seq=seq),
        grid=(T // tm,),
        in_specs=[pl.BlockSpec((tm, Dm), row),
                  pl.BlockSpec((CONV_HALO, Dm), lambda i: (jnp.maximum(i * hb - 1, 0), 0)),
                  pl.BlockSpec((CONV_HALO, Dm), full),
                  pl.BlockSpec((1, Dm), full),
                  pl.BlockSpec((1, Dm), full),
                  pl.BlockSpec((1, Dm), full),
                  pl.BlockSpec((Dm, Dm), full),
                  pl.BlockSpec((1, Dm), full),
                  pl.BlockSpec((tm, Dm), row),
                  pl.BlockSpec((1, 1, Dm), per_b)],
        out_specs=pl.BlockSpec((tm, Dm), row),
        out_shape=jax.ShapeDtypeStruct((T, Dm), F32),
        scratch_shapes=[pltpu.VMEM((tm + CONV_HALO, Dm), F32), pltpu.VMEM((tm, Dm), F32)],
        compiler_params=_cparams("parallel"),
        name="conv_dw_ln_pw2",
    )(u, u, wdw, vec(b_dw), vec(ln_g), vec(ln_b), w_pw2.astype(BF16), vec(b_pw2), x2, g)


FLASH_FIRST, FLASH_LAST = 1, 2
FLASH_CLASS_SHIFT = 2
FLASH_COL_STEP = 256
FLASH_CHUNK = 16
SEL_OFF = -(2.0 ** 100)
ONE_LANE = 64
LOG2E = 1.4426950408889634
FLASH_NPROB = 4


def _one_lane():
    return jnp.where(lax.broadcasted_iota(I32, (1, LANE), 1) == ONE_LANE, 1.0, 0.0)


def _flash_body(qi_ref, kj_ref, flag_ref, *refs, mode, rep, tq, tk, nprob, classes):
    if mode == "select":
        q_ref, k_ref, v_ref, selb_ref, o_ref, qa_sc, s_sc, p_sc, mb_sc, al_sc, acc_sc = refs
    else:
        q_ref, k_ref, v_ref, o_ref, s_sc, p_sc, mb_sc, al_sc, acc_sc = refs
    step = pl.program_id(2)
    qi = qi_ref[step]
    kj = kj_ref[step]
    flag = flag_ref[step]
    rows = rep * tq
    ch = FLASH_CHUNK

    @pl.when((flag & FLASH_FIRST) != 0)
    def _():
        mb_sc[...] = jnp.full(mb_sc.shape, NEG, F32)
        acc_sc[...] = jnp.zeros(acc_sc.shape, F32)
        if mode == "select":
            for pr in range(nprob):
                qa_sc[pr, :, :LANE] = q_ref[pr * rep:(pr + 1) * rep].reshape(rows, LANE)
                qa_sc[pr, :, LANE:] = jnp.concatenate([selb_ref[pr]] * rep, axis=0)

    def process(masked, c0, c1):
        w = c1 - c0
        nl = w // LANE
        lane_fold = lambda t, op: functools.reduce(op, [t[:, i * LANE:(i + 1) * LANE] for i in range(nl)])
        thr = qi * tq - kj * tk
        for pr in range(nprob):
            q = qa_sc[pr] if mode == "select" else q_ref[pr * rep:(pr + 1) * rep].reshape(rows, LANE)
            s_sc[pr, :, :w] = lax.dot_general(q, k_ref[pr, c0:c1, :], _NT, preferred_element_type=F32)
        if masked:
            diff = c0 + lax.broadcasted_iota(I32, (ch, w), 1) - lax.broadcasted_iota(I32, (ch, w), 0)
        for pr in range(nprob):
            for r0 in range(0, rows, ch):
                sc = s_sc[pr, r0:r0 + ch, :w]
                if masked:
                    lim = thr + (r0 % tq)
                    ok = diff <= lim
                    if mode == "window":
                        ok = ok & (diff > lim - WINDOW)
                    sc = jnp.where(ok, sc, NEG)
                    s_sc[pr, r0:r0 + ch, :w] = sc
                m_prev = mb_sc[pr, r0:r0 + ch, :]
                m_new = jnp.maximum(m_prev, jnp.max(lane_fold(sc, jnp.maximum), axis=1, keepdims=True))
                al_sc[pr, r0:r0 + ch, :] = jnp.exp2(m_prev - m_new)
                mb_sc[pr, r0:r0 + ch, :] = m_new
            for r0 in range(0, rows, ch):
                mb = mb_sc[pr, r0:r0 + ch, :]
                p = jnp.exp2(s_sc[pr, r0:r0 + ch, :w] - jnp.concatenate([mb] * nl, axis=1))
                p_sc[pr, r0:r0 + ch, :w] = p.astype(BF16)
            acc_sc[pr] = al_sc[pr] * acc_sc[pr] + jnp.dot(p_sc[pr, :, :w], v_ref[pr, c0:c1, :],
                                                           preferred_element_type=F32)

    for n, (masked, c0, c1) in enumerate(classes):
        pl.when((kj >= 0) & ((flag >> FLASH_CLASS_SHIFT) == n))(functools.partial(process, masked, c0, c1))

    @pl.when((flag & FLASH_LAST) != 0)
    def _():
        for pr in range(nprob):
            acc = acc_sc[pr]
            o = acc / acc[:, ONE_LANE:ONE_LANE + 1]
            o_ref[pr * rep:(pr + 1) * rep] = o.reshape(rep, tq, LANE).astype(o_ref.dtype)


def _flash_schedule(seq, tq, tk, mode):
    qi, kj, flags, classes = [], [], [], []
    cw = min(FLASH_COL_STEP, tk)
    for i in range(seq // tq):
        q_lo, q_hi = i * tq, i * tq + tq - 1
        hi = q_hi // tk
        lo = 0 if mode != "window" else (q_lo - (WINDOW - 1)) // tk
        js = list(range(lo, hi + 1))
        for n, j in enumerate(js):
            c1 = min(tk, -(-(q_hi - j * tk + 1) // cw) * cw)
            c0 = 0 if mode != "window" else max(0, (q_lo - (WINDOW - 1) - j * tk) // cw * cw)
            masked = mode == "window" or j * tk + c1 - 1 > q_lo
            cls = (masked, c0, c1)
            if cls not in classes:
                classes.append(cls)
            qi.append(i)
            kj.append(j if j >= 0 else -1)
            flags.append((FLASH_FIRST if n == 0 else 0) | (FLASH_LAST if n == len(js) - 1 else 0)
                         | (classes.index(cls) << FLASH_CLASS_SHIFT))
    as_i32 = lambda v: jnp.asarray(np.asarray(v, np.int32))
    return as_i32(qi), as_i32(kj), as_i32(flags), len(qi), tuple(classes)


def _flash(q, k, v, selb, *, batch, seq, mode, tq, tk):
    hq, T, _ = q.shape
    hkv = k.shape[0]
    kw = k.shape[-1]
    rep = hq // hkv
    npb = FLASH_NPROB
    assert tq & (tq - 1) == 0 and tq % FLASH_CHUNK == 0 and hkv % npb == 0
    qi, kj, flags, nsteps, classes = _flash_schedule(seq, tq, tk, mode)
    nq, nk = seq // tq, seq // tk
    q_map = lambda g, b, s, qi, kj, fl: (g, b * nq + qi[s], 0)
    k_map = lambda g, b, s, qi, kj, fl: (g, b * nk + jnp.maximum(kj[s], 0), 0)
    in_specs = [pl.BlockSpec((npb * rep, tq, LANE), q_map),
                pl.BlockSpec((npb, tk, kw), k_map),
                pl.BlockSpec((npb, tk, LANE), k_map)]
    args = [q, k, v]
    rows = rep * tq
    scratch = []
    if mode == "select":
        in_specs.append(pl.BlockSpec((npb, tq, LANE), q_map))
        args.append(selb)
        scratch.append(pltpu.VMEM((npb, rows, kw), BF16))
    scratch += [pltpu.VMEM((npb, rows, tk), F32), pltpu.VMEM((npb, rows, tk), BF16),
                pltpu.VMEM((npb, rows, LANE), F32), pltpu.VMEM((npb, rows, LANE), F32),
                pltpu.VMEM((npb, rows, LANE), F32)]
    return pl.pallas_call(
        functools.partial(_flash_body, mode=mode, rep=rep, tq=tq, tk=tk, nprob=npb, classes=classes),
        grid_spec=pltpu.PrefetchScalarGridSpec(
            num_scalar_prefetch=3,
            grid=(hkv // npb, batch, nsteps),
            in_specs=in_specs,
            out_specs=pl.BlockSpec((npb * rep, tq, LANE), q_map),
            scratch_shapes=scratch),
        out_shape=jax.ShapeDtypeStruct((hq, T, LANE), BF16),
        compiler_params=_cparams("parallel", "parallel", "arbitrary"),
        name="flash_" + mode,
    )(qi, kj, flags, *args)


def _attn_out_body(*refs, n_branch, heads):
    o_refs = refs[:n_branch]
    if n_branch > 1:
        gl_ref, ex_ref, w_ref, x_ref, g_ref, out_ref = refs[n_branch:]
        spread = jnp.dot(_sigmoid(gl_ref[...]).astype(BF16), ex_ref[...], preferred_element_type=F32)
    else:
        w_ref, x_ref, g_ref, out_ref = refs[n_branch:]
    per_head = []
    for h in range(heads):
        if n_branch > 1:
            o = jnp.zeros(o_refs[0].shape[1:], F32)
            for c in range(n_branch):
                col = n_branch * h + c
                o = o + spread[:, col * LANE:(col + 1) * LANE] * o_refs[c][h].astype(F32)
            o = o.astype(BF16)
        else:
            o = o_refs[0][h]
        per_head.append(o)
    acc = jnp.dot(jnp.concatenate(per_head, axis=1), w_ref[...], preferred_element_type=F32)
    out_ref[...] = x_ref[...] + g_ref[0] * acc


def _attn_out(os, gl, w_heads, x2, g, seq):
    T, Dm = x2.shape
    heads = w_heads.shape[0]
    tm = 256
    nb = seq // tm
    row = lambda i: (i, 0)
    o_spec = pl.BlockSpec((heads, tm, LANE), lambda i: (0, i, 0))
    in_specs = [o_spec] * len(os)
    args = list(os)
    if len(os) > 1:
        ncol = len(os) * heads
        spread = (jnp.arange(LANE)[:, None] == jnp.arange(ncol * LANE)[None, :] // LANE).astype(BF16)
        in_specs += [pl.BlockSpec((tm, LANE), row), pl.BlockSpec((LANE, ncol * LANE), lambda i: (0, 0))]
        args += [gl, spread]
    in_specs += [pl.BlockSpec((heads * LANE, Dm), lambda i: (0, 0)),
                 pl.BlockSpec((tm, Dm), row),
                 pl.BlockSpec((1, 1, Dm), lambda i: (i // nb, 0, 0))]
    args += [w_heads.reshape(heads * LANE, Dm), x2, g]
    return pl.pallas_call(
        functools.partial(_attn_out_body, n_branch=len(os), heads=heads),
        grid=(T // tm,),
        in_specs=in_specs,
        out_specs=pl.BlockSpec((tm, Dm), row),
        out_shape=jax.ShapeDtypeStruct((T, Dm), F32),
        compiler_params=_cparams("parallel"),
        name="attn_out_%d" % len(os),
    )(*args)


def _pad_heads_rows(w, heads, dh):
    w = w.reshape(heads, dh, -1)
    return jnp.pad(w, ((0, 0), (0, LANE - dh), (0, 0))).astype(BF16)


def _pad_heads_cols(w, heads, dh):
    k = w.shape[0]
    w = w.reshape(k, heads, dh)
    return jnp.pad(w, ((0, 0), (0, 0), (0, LANE - dh))).reshape(k, heads * LANE)


def _rope_tables(pos, rot, offset):
    half = rot // 2
    inv_freq = ROPE_THETA ** (-jnp.arange(0, rot, 2, dtype=F32) / rot)
    ang = pos.astype(F32)[:, None] * inv_freq[None, :]
    cos, sin = jnp.cos(ang), jnp.sin(ang)
    n = pos.shape[0]
    c = jnp.ones((n, LANE), F32).at[:, offset:offset + rot].set(jnp.concatenate([cos, cos], axis=1))
    sa = jnp.zeros((n, LANE), F32).at[:, offset:offset + half].set(-sin)
    sb = jnp.zeros((n, LANE), F32).at[:, offset + half:offset + rot].set(sin)
    return c, sa, sb


def _mla_proj_body(x_ref, a_ref, b_ref, win_ref, qlg_ref, kvlg_ref, wuq_ref, wuqs_ref, wuk_ref, wuv_ref,
                   ones_ref, q1_ref, q2_ref, k1_ref, k2_ref, q_ref, k_ref, v_ref):
    h = _normmod(x_ref[...], a_ref[0], b_ref[0]).astype(BF16)
    r = jnp.dot(h, win_ref[...], preferred_element_type=F32)
    lat = MLA_Q_LORA + MLA_KV_LORA
    q_lat = r[:, :MLA_Q_LORA]
    kv_lat = r[:, MLA_Q_LORA:lat]
    kpe = r[:, lat:lat + LANE]
    kpe_swap = r[:, lat + LANE:]
    ql = q_lat * lax.rsqrt(jnp.mean(q_lat * q_lat, axis=-1, keepdims=True) + EPS) * qlg_ref[...]
    kvl = kv_lat * lax.rsqrt(jnp.mean(kv_lat * kv_lat, axis=-1, keepdims=True) + EPS) * kvlg_ref[...]
    ql = ql.astype(BF16)
    kvl = kvl.astype(BF16)
    q = jnp.dot(ql, wuq_ref[...], preferred_element_type=F32)
    q_swap = jnp.dot(ql, wuqs_ref[...], preferred_element_type=F32)
    kn = jnp.dot(kvl, wuk_ref[...], preferred_element_type=F32)
    v = jnp.dot(kvl, wuv_ref[...], preferred_element_type=F32)
    ones = ones_ref[...]
    q1, q2, k1, k2 = q1_ref[...], q2_ref[...], k1_ref[...], k2_ref[...]
    k_rot = kpe_swap * k2

    def inv_rms(t):
        ss = jnp.dot((t * t).astype(BF16), ones, preferred_element_type=F32)
        return lax.rsqrt(ss * (1.0 / MLA_QK) + EPS)

    for hd in range(MLA_HEADS):
        sl = slice(hd * LANE, (hd + 1) * LANE)
        xq = q[:, sl]
        q_ref[hd] = ((xq * q1 + q_swap[:, sl] * q2) * inv_rms(xq)).astype(BF16)
        xk = kn[:, sl] + kpe
        k_ref[hd] = ((xk * k1 + k_rot) * inv_rms(xk)).astype(BF16)
        v_ref[hd] = (v[:, sl] + _one_lane()).astype(BF16)


def _mla_mixer(x2, a, b, g, w_in, q_lat_g, kv_lat_g, w_uq, w_ukv, q_gain, k_gain, w_out, batch, seq):
    T, Dm = x2.shape
    H = MLA_HEADS
    tm = 256
    nb = seq // tm
    scale = MLA_QK ** -0.5 * LOG2E
    lat = MLA_Q_LORA + MLA_KV_LORA
    half = MLA_ROPE // 2

    def swap_rope(t):
        lo, hi = t[..., MLA_NOPE:MLA_NOPE + half], t[..., MLA_NOPE + half:MLA_QK]
        return jnp.concatenate([jnp.zeros_like(t[..., :MLA_NOPE]), hi, lo], axis=-1)

    to_slot = lambda t: jnp.pad(t, [(0, 0)] * (t.ndim - 1) + [(0, LANE - MLA_QK)])
    kpe_w = jnp.concatenate([jnp.zeros((Dm, MLA_NOPE), F32), w_in[:, lat:]], axis=1)
    win_p = jnp.concatenate([w_in[:, :lat], to_slot(kpe_w), to_slot(swap_rope(kpe_w))], axis=1).astype(BF16)
    wuq3 = w_uq.reshape(MLA_Q_LORA, H, MLA_QK)
    wuq_p = to_slot(wuq3).reshape(MLA_Q_LORA, H * LANE).astype(BF16)
    wuqs_p = to_slot(swap_rope(wuq3)).reshape(MLA_Q_LORA, H * LANE).astype(BF16)
    wukv = w_ukv.reshape(MLA_KV_LORA, H, MLA_NOPE + MLA_V)
    wuk_p = _pad_heads_cols(wukv[:, :, :MLA_NOPE].reshape(MLA_KV_LORA, -1), H, MLA_NOPE).astype(BF16)
    wuv_p = _pad_heads_cols(wukv[:, :, MLA_NOPE:].reshape(MLA_KV_LORA, -1), H, MLA_V).astype(BF16)
    c, sa, sb = _rope_tables(jnp.arange(seq), MLA_ROPE, MLA_NOPE)
    qg = q_gain * scale
    q1, q2 = c * to_slot(qg)[None], (sa + sb) * to_slot(swap_rope(qg))[None]
    k1, k2 = c * to_slot(k_gain)[None], (sa + sb) * to_slot(swap_rope(k_gain))[None]
    ones = jnp.ones((LANE, LANE), BF16)
    row = lambda i: (i, 0)
    full = lambda i: (0, 0)
    per_b = lambda i: (i // nb, 0, 0)
    pos = lambda i: (i % nb, 0)
    head_out = pl.BlockSpec((H, tm, LANE), lambda i: (0, i, 0))
    hshape = jax.ShapeDtypeStruct((H, T, LANE), BF16)
    q, k, v = pl.pallas_call(
        _mla_proj_body,
        grid=(T // tm,),
        in_specs=[pl.BlockSpec((tm, Dm), row),
                  pl.BlockSpec((1, 1, Dm), per_b),
                  pl.BlockSpec((1, 1, Dm), per_b),
                  pl.BlockSpec(win_p.shape, full),
                  pl.BlockSpec((1, MLA_Q_LORA), full),
                  pl.BlockSpec((1, MLA_KV_LORA), full),
                  pl.BlockSpec(wuq_p.shape, full),
                  pl.BlockSpec(wuqs_p.shape, full),
                  pl.BlockSpec(wuk_p.shape, full),
                  pl.BlockSpec(wuv_p.shape, full),
                  pl.BlockSpec((LANE, LANE), full),
                  pl.BlockSpec((tm, LANE), pos),
                  pl.BlockSpec((tm, LANE), pos),
                  pl.BlockSpec((tm, LANE), pos),
                  pl.BlockSpec((tm, LANE), pos)],
        out_specs=[head_out, head_out, head_out],
        out_shape=[hshape, hshape, hshape],
        compiler_params=_cparams("parallel"),
        name="mla_proj",
    )(x2, a, b, win_p, q_lat_g.reshape(1, -1), kv_lat_g.reshape(1, -1), wuq_p, wuqs_p, wuk_p, wuv_p,
      ones, q1, q2, k1, k2)
    blk = min(1024, seq)
    o = _flash(q, k, v, None, batch=batch, seq=seq, mode="causal", tq=blk, tk=blk)
    return _attn_out([o], None, _pad_heads_rows(w_out, H, MLA_V), x2, g, seq)


N_KV_STREAMS = 6


def _nsa_proj_body(x_ref, a_ref, b_ref, w_ref, qg_ref, kg_ref, c_ref, sa_ref, sb_ref, blk_ref,
                   q_ref, craw_ref, ks_ref, vs_ref, kw_ref, vw_ref, gl_ref):
    h = _normmod(x_ref[...], a_ref[0], b_ref[0]).astype(BF16)
    r = jnp.dot(h, w_ref[...], preferred_element_type=F32)
    c, sa, sb = c_ref[...], sa_ref[...], sb_ref[...]

    def norm_rot(t, gain):
        t = t * lax.rsqrt(jnp.sum(t * t, axis=-1, keepdims=True) * (1.0 / NSA_DH) + EPS) * gain
        return _rope(t, c, sa, sb, NSA_ROT // 2)

    for hd in range(NSA_HEADS):
        q_ref[hd] = norm_rot(r[:, hd * LANE:(hd + 1) * LANE], qg_ref[...]).astype(BF16)
    base = NSA_HEADS * LANE
    for st in range(N_KV_STREAMS):
        for gi in range(NSA_GROUPS):
            off = base + (st * NSA_GROUPS + gi) * LANE
            t = r[:, off:off + LANE]
            if st < 2:
                craw_ref[st * NSA_GROUPS + gi] = t
            elif st == 2:
                t = norm_rot(t, kg_ref[1:2, :])
                ks_ref[gi] = jnp.concatenate([t.astype(BF16), blk_ref[...]], axis=1)
            elif st == 4:
                kw_ref[gi] = norm_rot(t, kg_ref[2:3, :]).astype(BF16)
            else:
                (vs_ref if st == 3 else vw_ref)[gi] = (t + _one_lane()).astype(BF16)
    gl_ref[...] = r[:, base + N_KV_STREAMS * NSA_GROUPS * LANE:]


def _nsa_compress_body(x_ref, pea_ref, peb_ref, w1a_ref, w1b_ref, w2_ref, kg_ref, c_ref, sa_ref, sb_ref,
                       o_ref, *, is_key, n_cmp):
    ncp = o_ref.shape[1]
    x = jnp.concatenate([x_ref[0, pl.ds(l, ncp, stride=CMP_STRIDE), :] for l in range(CMP_STRIDE)], axis=1)
    xa = (x + pea_ref[...]).astype(BF16)
    xb = (x + peb_ref[...]).astype(BF16)
    za = jnp.dot(xa, w1a_ref[...], preferred_element_type=F32)
    zb = jnp.dot(xb, w1b_ref[...], preferred_element_type=F32)
    rows = za.shape[0]
    z = _silu(za + pltpu.roll(zb, rows - 1, 0))
    t = jnp.dot(z.astype(BF16), w2_ref[...], preferred_element_type=F32)
    if is_key:
        t = t * lax.rsqrt(jnp.sum(t * t, axis=-1, keepdims=True) * (1.0 / NSA_DH) + EPS) * kg_ref[...]
        t = _rope(t, c_ref[...], sa_ref[...], sb_ref[...], NSA_ROT // 2)
    valid = lax.broadcasted_iota(I32, t.shape, 0) < n_cmp
    o_ref[0] = jnp.where(valid, t, 0.0).astype(BF16)


def _nsa_cmp_select_body(q_ref, kc_ref, vc_ref, oc_ref, sel_ref, p_sc, *, tq, limits, n_cmp, n_top):
    i = pl.program_id(2)
    t0 = i * tq
    rep = q_ref.shape[0]
    per = SLC_BLOCK // CMP_STRIDE

    def compute(limit):
        nc = limit // CMP_STRIDE
        ns = limit // SLC_BLOCK
        kc = kc_ref[0, :nc, :]
        vc = vc_ref[0, :nc, :]
        q = q_ref[...].reshape(rep * tq, LANE)
        sc = lax.dot_general(q, kc, _NT, preferred_element_type=F32)
        qpos = t0 + lax.broadcasted_iota(I32, (tq, nc), 0)
        blk = lax.broadcasted_iota(I32, (tq, nc), 1)
        mask = ((blk * CMP_STRIDE + (CMP_BLOCK - 1)) <= qpos) & (blk < n_cmp)
        sc = jnp.where(mask[None], sc.reshape(rep, tq, nc), NEG)
        p = jnp.exp2(sc - jnp.max(sc, axis=-1, keepdims=True))
        p = jnp.where(mask[None], p / jnp.sum(p, axis=-1, keepdims=True), 0.0)
        oc = jnp.dot(p.reshape(rep * tq, nc).astype(BF16), vc, preferred_element_type=F32)
        oc_ref[...] = oc.reshape(rep, tq, LANE).astype(BF16)
        qpos_t = t0 + lax.broadcasted_iota(I32, (nc, tq), 1)
        blk_t = lax.broadcasted_iota(I32, (nc, tq), 0)
        mask_t = ((blk_t * CMP_STRIDE + (CMP_BLOCK - 1)) <= qpos_t) & (blk_t < n_cmp)
        psum = jnp.zeros((nc, tq), F32)
        for r in range(rep):
            st = lax.dot_general(kc, q_ref[r], _NT, preferred_element_type=F32)
            st = jnp.where(mask_t, st, NEG)
            pt = jnp.exp2(st - jnp.max(st, axis=0, keepdims=True))
            psum = psum + jnp.where(mask_t, pt / jnp.sum(pt, axis=0, keepdims=True), 0.0)
        slabs = []
        for sb in range(tq // LANE):
            p_sc[sb, 0:8, :] = jnp.zeros((8, LANE), F32)
            p_sc[sb, 8:8 + nc, :] = psum[:, sb * LANE:(sb + 1) * LANE]
            part = p_sc[sb, pl.ds(7, ns, stride=per), :]
            for k in range(1, per + 1):
                part = part + p_sc[sb, pl.ds(7 + k, ns, stride=per), :]
            slabs.append(part)
        imp = jnp.concatenate(slabs, axis=1)
        jb = lax.broadcasted_iota(I32, (ns, tq), 0)
        qp = t0 + lax.broadcasted_iota(I32, (ns, tq), 1)
        imp = jnp.where(jb * SLC_BLOCK <= qp, imp, -BIG)
        imp = jnp.where((jb == 0) | (jb == jnp.right_shift(qp, 6)), BIG, imp)
        jbf = jb.astype(F32)
        sel_t = jnp.full((ns, tq), SEL_OFF, F32)
        for _ in range(n_top):
            top = jnp.max(imp, axis=0, keepdims=True)
            first = jnp.min(jnp.where(imp == top, jbf, float(MAX_SLC)), axis=0, keepdims=True)
            hit = jbf == first
            sel_t = jnp.where(hit, 0.0, sel_t)
            imp = jnp.where(hit, -jnp.inf, imp)
        if ns < MAX_SLC:
            sel_t = jnp.concatenate([sel_t, jnp.full((MAX_SLC - ns, tq), SEL_OFF, F32)], axis=0)
        sel_ref[0] = sel_t.T.astype(BF16)

    end = t0 + tq
    lower = 0
    for limit in limits:
        pl.when((end > lower) & (end <= limit))(functools.partial(compute, limit))
        lower = limit


def _nsa_mixer(x2, a, b, g, w_in, w_cmp1, w_cmp2, cmp_pos, q_gain, k_gain, w_out, batch, seq):
    T, Dm = x2.shape
    H, G, dh = NSA_HEADS, NSA_GROUPS, NSA_DH
    scale = dh ** -0.5 * LOG2E
    n_cmp = seq // CMP_STRIDE - 1
    ncp = seq // CMP_STRIDE
    n_slc = seq // SLC_BLOCK
    assert n_slc <= MAX_SLC and ncp <= MAX_SLC * (SLC_BLOCK // CMP_STRIDE)
    n_top = min(SLC_TOP_N, n_slc)
    tm = 256
    nb = seq // tm
    q_cols = _pad_heads_cols(w_in[:, :H * dh], H, dh)
    kv_cols = _pad_heads_cols(w_in[:, H * dh:H * dh + N_KV_STREAMS * G * dh], N_KV_STREAMS * G, dh)
    gl_cols = jnp.pad(w_in[:, H * dh + N_KV_STREAMS * G * dh:], ((0, 0), (0, LANE - 3 * H)))
    w_p = jnp.concatenate([q_cols, kv_cols, gl_cols], axis=1).astype(BF16)
    pad_gain = lambda v: jnp.pad(v, ((0, 0), (0, LANE - dh)))
    c, sa, sb = _rope_tables(jnp.arange(seq), NSA_ROT, 0)
    row = lambda i: (i, 0)
    full = lambda i: (0, 0)
    per_b = lambda i: (i // nb, 0, 0)
    pos = lambda i: (i % nb, 0)
    blk_onehot = (jnp.arange(seq)[:, None] // SLC_BLOCK == jnp.arange(MAX_SLC)[None, :]).astype(BF16)
    group_out = pl.BlockSpec((G, tm, LANE), lambda i: (0, i, 0))
    group_shape = jax.ShapeDtypeStruct((G, T, LANE), BF16)
    q, craw, ks, vs, kw, vw, gl = pl.pallas_call(
        _nsa_proj_body,
        grid=(T // tm,),
        in_specs=[pl.BlockSpec((tm, Dm), row),
                  pl.BlockSpec((1, 1, Dm), per_b),
                  pl.BlockSpec((1, 1, Dm), per_b),
                  pl.BlockSpec(w_p.shape, full),
                  pl.BlockSpec((1, LANE), full),
                  pl.BlockSpec((3, LANE), full),
                  pl.BlockSpec((tm, LANE), pos),
                  pl.BlockSpec((tm, LANE), pos),
                  pl.BlockSpec((tm, LANE), pos),
                  pl.BlockSpec((tm, MAX_SLC), pos)],
        out_specs=[pl.BlockSpec((H, tm, LANE), lambda i: (0, i, 0)),
                   pl.BlockSpec((2 * G, tm, LANE), lambda i: (0, i, 0)),
                   pl.BlockSpec((G, tm, LANE + MAX_SLC), lambda i: (0, i, 0)),
                   group_out, group_out, group_out,
                   pl.BlockSpec((tm, LANE), row)],
        out_shape=[jax.ShapeDtypeStruct((H, T, LANE), BF16),
                   jax.ShapeDtypeStruct((2 * G, T, LANE), F32),
                   jax.ShapeDtypeStruct((G, T, LANE + MAX_SLC), BF16),
                   group_shape, group_shape, group_shape,
                   jax.ShapeDtypeStruct((T, LANE), F32)],
        compiler_params=_cparams("parallel"),
        name="nsa_proj",
    )(x2, a, b, w_p, pad_gain(q_gain.reshape(1, dh) * scale), pad_gain(k_gain), c, sa, sb, blk_onehot)

    cmp_end = jnp.arange(ncp) * CMP_STRIDE + (CMP_BLOCK - 1)
    cc, csa, csb = _rope_tables(cmp_end, NSA_ROT, 0)
    kdim = CMP_STRIDE * LANE

    def compress(stream, w1, w2, pe, is_key):
        w1p = jnp.pad(w1, ((0, 0), (0, LANE - dh), (0, LANE - dh)))
        w1a = w1p[:CMP_STRIDE].reshape(kdim, LANE).astype(BF16)
        w1b = w1p[CMP_STRIDE:].reshape(kdim, LANE).astype(BF16)
        pep = jnp.pad(pe, ((0, 0), (0, LANE - dh)))
        pea = pep[:CMP_STRIDE].reshape(1, kdim)
        peb = pep[CMP_STRIDE:].reshape(1, kdim)
        w2p = jnp.pad(w2, ((0, LANE - dh), (0, LANE - dh))).astype(BF16)
        const = lambda gi, bi: (0, 0)
        return pl.pallas_call(
            functools.partial(_nsa_compress_body, is_key=is_key, n_cmp=n_cmp),
            grid=(G, batch),
            in_specs=[pl.BlockSpec((1, seq, LANE), lambda gi, bi: (stream * G + gi, bi, 0)),
                      pl.BlockSpec((1, kdim), const),
                      pl.BlockSpec((1, kdim), const),
                      pl.BlockSpec((kdim, LANE), const),
                      pl.BlockSpec((kdim, LANE), const),
                      pl.BlockSpec((LANE, LANE), const),
                      pl.BlockSpec((1, LANE), const),
                      pl.BlockSpec((ncp, LANE), const),
                      pl.BlockSpec((ncp, LANE), const),
                      pl.BlockSpec((ncp, LANE), const)],
            out_specs=pl.BlockSpec((1, ncp, LANE), lambda gi, bi: (gi, bi, 0)),
            out_shape=jax.ShapeDtypeStruct((G, batch * ncp, LANE), BF16),
            compiler_params=_cparams("parallel", "parallel"),
            name="nsa_compress_" + ("k" if is_key else "v"),
        )(craw, pea, peb, w1a, w1b, w2p, pad_gain(k_gain)[0:1], cc, csa, csb)

    kc = compress(0, w_cmp1[0], w_cmp2[0], cmp_pos[0], True)
    vc = compress(1, w_cmp1[1], w_cmp2[1], cmp_pos[1], False)

    tq = 256
    nq = seq // tq
    limits = tuple(sorted({max(seq // d, min(seq, LANE * CMP_STRIDE)) for d in (4, 2, 1)}))
    q_map = lambda gi, bi, i: (gi, bi * nq + i, 0)
    c_map = lambda gi, bi, i: (gi, bi, 0)
    o_c, sel = pl.pallas_call(
        functools.partial(_nsa_cmp_select_body, tq=tq, limits=limits, n_cmp=n_cmp, n_top=n_top),
        grid=(G, batch, nq),
        in_specs=[pl.BlockSpec((NSA_REP, tq, LANE), q_map),
                  pl.BlockSpec((1, ncp, LANE), c_map),
                  pl.BlockSpec((1, ncp, LANE), c_map)],
        out_specs=[pl.BlockSpec((NSA_REP, tq, LANE), q_map),
                   pl.BlockSpec((1, tq, LANE), q_map)],
        out_shape=[jax.ShapeDtypeStruct((H, T, LANE), BF16),
                   jax.ShapeDtypeStruct((G, T, LANE), BF16)],
        scratch_shapes=[pltpu.VMEM((tq // LANE, 8 + MAX_SLC * (SLC_BLOCK // CMP_STRIDE), LANE), F32)],
        compiler_params=_cparams("parallel", "parallel", "parallel"),
        name="nsa_cmp_select",
    )(q, kc, vc)

    o_s = _flash(q, ks, vs, sel, batch=batch, seq=seq, mode="select", tq=256, tk=min(1024, seq))
    o_w = _flash(q, kw, vw, None, batch=batch, seq=seq, mode="window", tq=256, tk=WINDOW)
    return _attn_out([o_c, o_s, o_w], gl, _pad_heads_rows(w_out, H, dh), x2, g, seq)


def _router_body(x_ref, a_ref, b_ref, rwt_ref, rb_ref, tri_ref, e_ref, w_ref, rank_ref, cnt_ref,
                 carry_sc, *, tm):
    i = pl.program_id(0)

    @pl.when(i == 0)
    def _():
        carry_sc[...] = jnp.zeros(carry_sc.shape, F32)

    h = _normmod(x_ref[...], a_ref[0], b_ref[0])
    h_hi = h.astype(BF16)
    h_lo = (h - h_hi.astype(F32)).astype(BF16)
    w_hi, w_lo = rwt_ref[0], rwt_ref[1]
    dot_nt = lambda p, q: lax.dot_general(p, q, _NT, preferred_element_type=F32)
    logits = dot_nt(w_hi, h_hi) + (dot_nt(w_hi, h_lo) + dot_nt(w_lo, h_hi))
    scores = _sigmoid(logits)
    biased = scores + rb_ref[...]
    ng, per = N_GROUPS, EXPERTS_PER_GROUP
    row = lambda arr, r: arr[r:r + 1, :]
    gsel = jnp.zeros((1, tm), I32)
    best = None
    for gi in range(ng):
        v = [row(biased, gi * per + k) for k in range(per)]
        top2 = None
        for p in range(per):
            for q in range(p + 1, per):
                s = v[p] + v[q]
                top2 = s if top2 is None else jnp.maximum(top2, s)
        if best is None:
            best = top2
        else:
            better = top2 > best
            gsel = jnp.where(better, gi, gsel)
            best = jnp.where(better, top2, best)
    cb, cs = [], []
    for k in range(per):
        b_k = row(biased, k)
        s_k = row(scores, k)
        for gi in range(1, ng):
            hit = gsel == gi
            b_k = jnp.where(hit, row(biased, gi * per + k), b_k)
            s_k = jnp.where(hit, row(scores, gi * per + k), s_k)
        cb.append(b_k)
        cs.append(s_k)

    def argmax_first(vals):
        idx = jnp.zeros((1, tm), I32)
        top = vals[0]
        for k in range(1, per):
            better = vals[k] > top
            idx = jnp.where(better, k, idx)
            top = jnp.where(better, vals[k], top)
        return idx

    def pick(vals, idx):
        out = vals[0]
        for k in range(1, per):
            out = jnp.where(idx == k, vals[k], out)
        return out

    i1 = argmax_first(cb)
    i2 = argmax_first([jnp.where(i1 == k, -jnp.inf, cb[k]) for k in range(per)])
    w1 = pick(cs, i1)
    w2 = pick(cs, i2)
    tot = w1 + w2
    e1 = gsel * per + i1
    e2 = gsel * per + i2
    eid = lax.broadcasted_iota(I32, (N_EXPERTS, tm), 0)
    hot = (eid == e1) | (eid == e2)
    onehot = jnp.where(hot, 1.0, 0.0)
    before = carry_sc[...] + jnp.dot(onehot.astype(BF16), tri_ref[...], preferred_element_type=F32)
    r1 = jnp.sum(jnp.where(eid == e1, before, 0.0), axis=0, keepdims=True)
    r2 = jnp.sum(jnp.where(eid == e2, before, 0.0), axis=0, keepdims=True)
    carry = carry_sc[...] + jnp.sum(onehot, axis=1, keepdims=True)
    carry_sc[...] = carry
    cnt_ref[...] = jnp.broadcast_to(carry, cnt_ref.shape)
    zi = jnp.zeros((6, tm), I32)
    e_ref[...] = jnp.concatenate([e1, e2, zi], axis=0)
    rank_ref[...] = jnp.concatenate([r1.astype(I32), r2.astype(I32), zi], axis=0)
    w_ref[...] = jnp.concatenate([w1 / tot, w2 / tot, jnp.zeros((6, tm), F32)], axis=0)


ROW_SUB = D_MODEL // LANE


def _row_tile(r):
    return pl.ds(pl.multiple_of(r * ROW_SUB, ROW_SUB), ROW_SUB)


def _to_row_tiles(ref, val):
    n = val.shape[0]
    for s in range(ROW_SUB):
        ref[pl.ds(s, n, stride=ROW_SUB), :] = val[:, s * LANE:(s + 1) * LANE]


def _from_row_tiles(ref, n, s):
    return ref[pl.ds(s, n, stride=ROW_SUB), :]


def _moe_dispatch_body(starts_ref, counts_ref, padded_ref, nv_ref, d0_ref, d1_ref,
                       x_ref, a_ref, b_ref, xs_hbm, hbuf0, hbuf1, zbuf, sems, zsem,
                       *, tm, tg, n_tiles, nt):
    i = pl.program_id(0)

    @pl.when(i == 0)
    def _():
        zbuf[...] = jnp.zeros(zbuf.shape, F32)
        zrow = zbuf.at[pl.ds(0, ROW_SUB)]
        for ex in range(N_EXPERTS):
            lo = starts_ref[ex] + counts_ref[ex]
            hi = starts_ref[ex] + padded_ref[ex]

            def fill(r, carry):
                pltpu.make_async_copy(zrow, xs_hbm.at[_row_tile(r)], zsem).start()
                return carry

            lax.fori_loop(lo, hi, fill, 0)

            def drain_fill(r, carry):
                pltpu.make_async_copy(zrow, xs_hbm.at[_row_tile(0)], zsem).wait()
                return carry

            lax.fori_loop(lo, hi, drain_fill, 0)

        rows_per_tile = tg * ROW_SUB

        def fill_tile(t, carry):
            dst = xs_hbm.at[pl.ds(pl.multiple_of(t * rows_per_tile, rows_per_tile), rows_per_tile)]
            pltpu.make_async_copy(zbuf, dst, zsem).start()
            return carry

        lax.fori_loop(nv_ref[0], n_tiles, fill_tile, 0)

        def drain_tile(t, carry):
            pltpu.make_async_copy(zbuf, xs_hbm.at[pl.ds(0, rows_per_tile)], zsem).wait()
            return carry

        lax.fori_loop(nv_ref[0], n_tiles, drain_tile, 0)

    h = _normmod(x_ref[...], a_ref[0], b_ref[0])

    def scatter_from(hbuf, sem, other_buf, other_sem):
        _to_row_tiles(hbuf, h)

        def row_copy(buf, sm, r, slot):
            return pltpu.make_async_copy(buf.at[_row_tile(r)], xs_hbm.at[_row_tile(slot)], sm)

        def issue(r, carry):
            row_copy(hbuf, sem, r, d0_ref[r]).start()
            row_copy(hbuf, sem, r, d1_ref[r]).start()
            return carry

        lax.fori_loop(0, tm, issue, 0, unroll=8)

        def drain(buf, sm):
            def body(r, carry):
                row_copy(buf, sm, r, 0).wait()
                row_copy(buf, sm, r, 0).wait()
                return carry

            lax.fori_loop(0, tm, body, 0, unroll=8)

        pl.when(i > 0)(lambda: drain(other_buf, other_sem))
        pl.when(i == nt - 1)(lambda: drain(hbuf, sem))

    pl.when(i % 2 == 0)(lambda: scatter_from(hbuf0, sems.at[0], hbuf1, sems.at[1]))
    pl.when(i % 2 == 1)(lambda: scatter_from(hbuf1, sems.at[1], hbuf0, sems.at[0]))


def _moe_expert_body(te_ref, nv_ref, xs_ref, win_ref, wout_ref, y_ref, win_sc, wout_sc, *, tg):
    i = pl.program_id(0)

    @pl.when((i < nv_ref[0]) & ((i == 0) | (te_ref[i] != te_ref[jnp.maximum(i - 1, 0)])))
    def _():
        win_sc[...] = win_ref[0].astype(BF16)
        wout_sc[...] = wout_ref[0].astype(BF16)

    @pl.when(i < nv_ref[0])
    def _():
        x = jnp.concatenate([_from_row_tiles(xs_ref, tg, s) for s in range(ROW_SUB)], axis=1).astype(BF16)
        gu = jnp.dot(x, win_sc[...], preferred_element_type=F32)
        act = (_silu(gu[:, :D_EXPERT]) * gu[:, D_EXPERT:]).astype(BF16)
        _to_row_tiles(y_ref, jnp.dot(act, wout_sc[...], preferred_element_type=F32))

    @pl.when(i >= nv_ref[0])
    def _():
        y_ref[...] = jnp.zeros(y_ref.shape, F32)


def _moe_combine_body(d0c_ref, d1c_ref, d0n_ref, d1n_ref, y_hbm, x_ref, w_ref, g_ref, o_ref,
                      ya0, ya1, yb0, yb1, sems, *, tm, nt):
    i = pl.program_id(0)

    def row_copy(slot, buf, sem, r):
        return pltpu.make_async_copy(y_hbm.at[_row_tile(slot)], buf.at[_row_tile(r)], sem)

    def issue(d0_ref, d1_ref, bufs, sem):
        def body(r, carry):
            row_copy(d0_ref[r], bufs[0], sem, r).start()
            row_copy(d1_ref[r], bufs[1], sem, r).start()
            return carry

        lax.fori_loop(0, tm, body, 0, unroll=8)

    def finish(bufs, sem):
        def body(r, carry):
            row_copy(0, bufs[0], sem, r).wait()
            row_copy(0, bufs[1], sem, r).wait()
            return carry

        lax.fori_loop(0, tm, body, 0, unroll=8)
        w = w_ref[...]
        w0, w1 = w[:, 0:1], w[:, 1:2]
        gate = g_ref[0]
        for s in range(ROW_SUB):
            sl = slice(s * LANE, (s + 1) * LANE)
            y = w0 * _from_row_tiles(bufs[0], tm, s) + w1 * _from_row_tiles(bufs[1], tm, s)
            o_ref[:, sl] = x_ref[:, sl] + gate[:, sl] * y

    set_a, set_b = (ya0, ya1), (yb0, yb1)
    pl.when(i == 0)(lambda: issue(d0c_ref, d1c_ref, set_a, sems.at[0]))

    def even():
        pl.when(i + 1 < nt)(lambda: issue(d0n_ref, d1n_ref, set_b, sems.at[1]))
        finish(set_a, sems.at[0])

    def odd():
        pl.when(i + 1 < nt)(lambda: issue(d0n_ref, d1n_ref, set_a, sems.at[0]))
        finish(set_b, sems.at[1])

    pl.when(i % 2 == 0)(even)
    pl.when(i % 2 == 1)(odd)


def _grouped_moe(x2, a, b, g, router_w, router_bias, w_in, w_out, seq):
    T, Dm = x2.shape
    E = N_EXPERTS
    tm = 512
    nb = seq // tm
    nt = T // tm
    row = lambda i: (i, 0)
    full = lambda i: (0, 0)
    per_b = lambda i: (i // nb, 0, 0)
    lanes = lambda i: (0, i)
    tri = jnp.asarray(np.triu(np.ones((tm, tm), np.float32), 1)).astype(BF16)
    rw_hi = router_w.T.astype(BF16)
    rw_split = jnp.stack([rw_hi, (router_w.T - rw_hi.astype(F32)).astype(BF16)])
    e, w, rank, cnt = pl.pallas_call(
        functools.partial(_router_body, tm=tm),
        grid=(nt,),
        in_specs=[pl.BlockSpec((tm, Dm), row),
                  pl.BlockSpec((1, 1, Dm), per_b),
                  pl.BlockSpec((1, 1, Dm), per_b),
                  pl.BlockSpec((2, E, Dm), lambda i: (0, 0, 0)),
                  pl.BlockSpec((E, 1), full),
                  pl.BlockSpec((tm, tm), full)],
        out_specs=[pl.BlockSpec((8, tm), lanes),
                   pl.BlockSpec((8, tm), lanes),
                   pl.BlockSpec((8, tm), lanes),
                   pl.BlockSpec((E, LANE), full)],
        out_shape=[jax.ShapeDtypeStruct((8, T), I32),
                   jax.ShapeDtypeStruct((8, T), F32),
                   jax.ShapeDtypeStruct((8, T), I32),
                   jax.ShapeDtypeStruct((E, LANE), F32)],
        scratch_shapes=[pltpu.VMEM((E, 1), F32)],
        compiler_params=_cparams("arbitrary"),
        name="moe_router",
    )(x2, a, b, rw_split, router_bias.reshape(E, 1), tri)

    tg = MOE_TILE
    n_tiles = (2 * T) // tg + E
    n_slots = n_tiles * tg
    counts = cnt[:, 0].astype(I32)
    padded = ((counts + tg - 1) // tg) * tg
    ends = jnp.cumsum(padded)
    starts = ends - padded
    tile_start = jnp.arange(n_tiles, dtype=I32) * tg
    tile_expert = jnp.minimum(jnp.sum(tile_start[:, None] >= ends[None, :], axis=1), E - 1).astype(I32)
    n_valid = (ends[-1] // tg).astype(I32).reshape(1)

    tc = 256
    ntc = T // tc
    nbc = seq // tc
    smem_cur = pl.BlockSpec((tc,), lambda i, *_: (i,), memory_space=pltpu.SMEM)
    smem_nxt = pl.BlockSpec((tc,), lambda i, *_: (jnp.minimum(i + 1, ntc - 1),), memory_space=pltpu.SMEM)
    tile_rows = tc * ROW_SUB
    seg_start = functools.reduce(lambda acc, k: jnp.where(e[:2] == k, starts[k], acc), range(E),
                                 jnp.zeros_like(e[:2]))
    dest = seg_start + rank[:2]
    d0, d1 = dest[0], dest[1]
    xs = pl.pallas_call(
        functools.partial(_moe_dispatch_body, tm=tc, tg=tg, n_tiles=n_tiles, nt=ntc),
        grid_spec=pltpu.PrefetchScalarGridSpec(
            num_scalar_prefetch=4,
            grid=(ntc,),
            in_specs=[smem_cur, smem_cur,
                      pl.BlockSpec((tc, Dm), lambda i, *_: (i, 0)),
                      pl.BlockSpec((1, 1, Dm), lambda i, *_: (i // nbc, 0, 0)),
                      pl.BlockSpec((1, 1, Dm), lambda i, *_: (i // nbc, 0, 0))],
            out_specs=pl.BlockSpec(memory_space=pl.ANY),
            scratch_shapes=[pltpu.VMEM((tile_rows, LANE), F32), pltpu.VMEM((tile_rows, LANE), F32),
                            pltpu.VMEM((tg * ROW_SUB, LANE), F32),
                            pltpu.SemaphoreType.DMA((2,)), pltpu.SemaphoreType.DMA(())]),
        out_shape=jax.ShapeDtypeStruct((n_slots * ROW_SUB, LANE), F32),
        compiler_params=_cparams("arbitrary"),
        name="moe_dispatch",
    )(starts.astype(I32), counts, padded, n_valid, d0, d1, x2, a, b)

    last_tile = lambda i, te, nv: (jnp.minimum(i, nv[0] - 1), 0)
    y = pl.pallas_call(
        functools.partial(_moe_expert_body, tg=tg),
        grid_spec=pltpu.PrefetchScalarGridSpec(
            num_scalar_prefetch=2,
            grid=(n_tiles,),
            in_specs=[pl.BlockSpec((tg * ROW_SUB, LANE), last_tile),
                      pl.BlockSpec((1, Dm, 2 * D_EXPERT), lambda i, te, nv: (te[i], 0, 0)),
                      pl.BlockSpec((1, D_EXPERT, Dm), lambda i, te, nv: (te[i], 0, 0))],
            out_specs=pl.BlockSpec((tg * ROW_SUB, LANE), lambda i, te, nv: (i, 0)),
            scratch_shapes=[pltpu.VMEM((Dm, 2 * D_EXPERT), BF16), pltpu.VMEM((D_EXPERT, Dm), BF16)]),
        out_shape=jax.ShapeDtypeStruct((n_slots * ROW_SUB, LANE), F32),
        compiler_params=_cparams("arbitrary"),
        name="moe_experts",
    )(tile_expert, n_valid, xs, w_in, w_out)

    return pl.pallas_call(
        functools.partial(_moe_combine_body, tm=tc, nt=ntc),
        grid=(ntc,),
        in_specs=[smem_cur, smem_cur, smem_nxt, smem_nxt,
                  pl.BlockSpec(memory_space=pl.ANY),
                  pl.BlockSpec((tc, Dm), row),
                  pl.BlockSpec((tc, 2), row),
                  pl.BlockSpec((1, 1, Dm), lambda i: (i // nbc, 0, 0))],
        out_specs=pl.BlockSpec((tc, Dm), row),
        out_shape=jax.ShapeDtypeStruct((T, Dm), F32),
        scratch_shapes=[pltpu.VMEM((tile_rows, LANE), F32)] * 4 + [pltpu.SemaphoreType.DMA((2,))],
        compiler_params=_cparams("arbitrary"),
        name="moe_combine",
    )(d0, d1, d0, d1, y, x2, w[:2].T, g)


def kernel(x, c, norm_mix_g, norm_ffn_g, w_ada, b_ada, conv_w_pw1, conv_b_pw1, conv_w_dw, conv_b_dw, conv_ln_g, conv_ln_b, conv_w_pw2, conv_b_pw2, nsa_w_in, nsa_w_cmp1, nsa_w_cmp2, nsa_cmp_pos, nsa_q_gain, nsa_k_gain, nsa_w_out, mla_w_in, mla_q_lat_g, mla_kv_lat_g, mla_w_uq, mla_w_ukv, mla_q_gain, mla_k_gain, mla_w_out, router_w, router_bias, moe_w_in, moe_w_out):
    B, S, Dm = x.shape
    depth = w_ada.shape[0]
    mods = _ada(c, w_ada, b_ada)
    x2 = x.reshape(B * S, Dm)
    for i in range(depth):
        sh1, sc1, g1, sh2, sc2, g2 = [m.reshape(B, 1, Dm) for m in jnp.split(mods[i], 6, axis=-1)]
        a1 = norm_mix_g[i] * (1.0 + sc1)
        kind, j = i % 3, i // 3
        if kind == 0:
            x2 = _conv_mixer(x2, a1, sh1, g1, conv_w_pw1[j], conv_b_pw1[j], conv_w_dw[j], conv_b_dw[j],
                             conv_ln_g[j], conv_ln_b[j], conv_w_pw2[j], conv_b_pw2[j], S)
        elif kind == 1:
            x2 = _nsa_mixer(x2, a1, sh1, g1, nsa_w_in[j], nsa_w_cmp1[j], nsa_w_cmp2[j], nsa_cmp_pos[j],
                            nsa_q_gain[j], nsa_k_gain[j], nsa_w_out[j], B, S)
        else:
            x2 = _mla_mixer(x2, a1, sh1, g1, mla_w_in[j], mla_q_lat_g[j], mla_kv_lat_g[j], mla_w_uq[j],
                            mla_w_ukv[j], mla_q_gain[j], mla_k_gain[j], mla_w_out[j], B, S)
        a2 = norm_ffn_g[i] * (1.0 + sc2)
        x2 = _grouped_moe(x2, a2, sh2, g2, router_w, router_bias, moe_w_in[i], moe_w_out[i], S)
    return x2.reshape(B, S, Dm)
```

```python
import functools
import math

import numpy as np
import jax
import jax.numpy as jnp
from jax import lax
from jax.experimental import pallas as pl
from jax.experimental.pallas import tpu as pltpu

F32 = jnp.float32
BF16 = jnp.bfloat16
I32 = jnp.int32
HIGHEST = lax.Precision.HIGHEST

EPS = 1e-6
NEG = -1e30
BIG = 1e30
ROPE_THETA = 500000.0
LANE = 128
VMEM_LIMIT = 56 * 1024 * 1024

D_MODEL = 1024
CONV_WIDTH = 31
CONV_HALO = 32

NSA_HEADS = 16
NSA_GROUPS = 4
NSA_REP = NSA_HEADS // NSA_GROUPS
NSA_DH = 64
NSA_ROT = 16
CMP_BLOCK = 32
CMP_STRIDE = 16
SLC_BLOCK = 64
SLC_TOP_N = 16
WINDOW = 512
MAX_SLC = 128

MLA_HEADS = 16
MLA_Q_LORA = 384
MLA_KV_LORA = 256
MLA_NOPE = 64
MLA_ROPE = 32
MLA_V = 64
MLA_QK = MLA_NOPE + MLA_ROPE

N_EXPERTS = 16
N_GROUPS = 4
EXPERTS_PER_GROUP = 4
D_EXPERT = 512
MOE_TILE = 512

_NT = (((1,), (1,)), ((), ()))


def _cparams(*sem):
    return pltpu.CompilerParams(dimension_semantics=sem, vmem_limit_bytes=VMEM_LIMIT)


def _sigmoid(x):
    return 1.0 / (1.0 + jnp.exp(-x))


def _silu(x):
    return x * _sigmoid(x)


def _normmod(x, a, b):
    ms = jnp.mean(x * x, axis=-1, keepdims=True)
    return x * lax.rsqrt(ms + EPS) * a + b


def _rope(x, c, sa, sb, half):
    n = x.shape[-1]
    return x * c + pltpu.roll(x, n - half, 1) * sa + pltpu.roll(x, half, 1) * sb


def _ada_body(c_ref, w_ref, b_ref, o_ref):
    c = c_ref[...]
    o_ref[0] = jnp.dot(_silu(c), w_ref[0], preferred_element_type=F32, precision=HIGHEST) + b_ref[0]


def _ada(c, w_ada, b_ada):
    B = c.shape[0]
    L, Dm, N = w_ada.shape
    Bp = -(-B // 8) * 8
    cp = jnp.pad(c, ((0, Bp - B), (0, 0)))
    tn = 1536
    out = pl.pallas_call(
        _ada_body,
        grid=(L, N // tn),
        in_specs=[pl.BlockSpec((Bp, Dm), lambda l, j: (0, 0)),
                  pl.BlockSpec((1, Dm, tn), lambda l, j: (l, 0, j)),
                  pl.BlockSpec((1, 1, tn), lambda l, j: (l, 0, j))],
        out_specs=pl.BlockSpec((1, Bp, tn), lambda l, j: (l, 0, j)),
        out_shape=jax.ShapeDtypeStruct((L, Bp, N), F32),
        compiler_params=_cparams("arbitrary", "arbitrary"),
        name="adaln",
    )(cp, w_ada, b_ada.reshape(L, 1, N))
    return out[:, :B]


def _conv_pw1_body(x_ref, a_ref, b_ref, w_ref, bias_ref, u_ref):
    h = _normmod(x_ref[...], a_ref[0], b_ref[0]).astype(BF16)
    r = jnp.dot(h, w_ref[...], preferred_element_type=F32) + bias_ref[...]
    d = u_ref.shape[-1]
    u_ref[...] = r[:, :d] * _sigmoid(r[:, d:])


def _conv_dw_body(u_ref, uh_ref, wdw_ref, bdw_ref, lng_ref, lnb_ref, w2_ref, b2_ref, x_ref, g_ref,
                  o_ref, ext_ref, acc_ref, *, tm, seq):
    i = pl.program_id(0)
    at_seq_start = (i * tm) % seq == 0
    ext_ref[0:CONV_HALO, :] = jnp.where(at_seq_start, 0.0, uh_ref[...])
    ext_ref[CONV_HALO:, :] = u_ref[...]
    base = CONV_HALO - (CONV_WIDTH - 1)
    rc = 64
    for lc in range(u_ref.shape[-1] // LANE):
        ls = slice(lc * LANE, (lc + 1) * LANE)
        wl = wdw_ref[:, ls]
        bl = bdw_ref[:, ls]
        for r0 in range(0, tm, rc):
            e = ext_ref[r0:r0 + rc + CONV_HALO, ls]
            acc = bl
            for s in range(8):
                rows = rc if s == 0 else rc + 8
                p = None
                for a in range((base + CONV_WIDTH - 1) // 8 + 1):
                    k = 8 * a + s - base
                    if 0 <= k < CONV_WIDTH and 8 * a + rows <= rc + CONV_HALO:
                        term = wl[k:k + 1, :] * e[8 * a:8 * a + rows, :]
                        p = term if p is None else p + term
                acc = acc + (p if s == 0 else p[s:s + rc, :])
            acc_ref[r0:r0 + rc, ls] = acc
    acc = acc_ref[...]
    mu = jnp.mean(acc, axis=-1, keepdims=True)
    dlt = acc - mu
    var = jnp.mean(dlt * dlt, axis=-1, keepdims=True)
    y = dlt * lax.rsqrt(var + EPS) * lng_ref[...] + lnb_ref[...]
    z = _silu(y).astype(BF16)
    out = jnp.dot(z, w2_ref[...], preferred_element_type=F32) + b2_ref[...]
    o_ref[...] = x_ref[...] + g_ref[0] * out


def _conv_mixer(x2, a, b, g, w_pw1, b_pw1, w_dw, b_dw, ln_g, ln_b, w_pw2, b_pw2, seq):
    T, Dm = x2.shape
    tm = 512
    nb = seq // tm
    row = lambda i: (i, 0)
    per_b = lambda i: (i // nb, 0, 0)
    full = lambda i: (0, 0)
    u = pl.pallas_call(
        _conv_pw1_body,
        grid=(T // tm,),
        in_specs=[pl.BlockSpec((tm, Dm), row),
                  pl.BlockSpec((1, 1, Dm), per_b),
                  pl.BlockSpec((1, 1, Dm), per_b),
                  pl.BlockSpec((Dm, 2 * Dm), full),
                  pl.BlockSpec((1, 2 * Dm), full)],
        out_specs=pl.BlockSpec((tm, Dm), row),
        out_shape=jax.ShapeDtypeStruct((T, Dm), F32),
        compiler_params=_cparams("parallel"),
        name="conv_pw1_glu",
    )(x2, a, b, w_pw1.astype(BF16), b_pw1.reshape(1, -1))
    hb = tm // CONV_HALO
    wdw = jnp.pad(w_dw, ((0, CONV_HALO - CONV_WIDTH), (0, 0)))
    vec = lambda v: v.reshape(1, -1)
    return pl.pallas_call(
        functools.partial(_conv_dw_body, tm=tm, seq=seq),
        grid=(T // tm,),
        in_specs=[pl.BlockSpec((tm, Dm), row),
                  pl.BlockSpec((CONV_HALO, Dm), lambda i: (jnp.maximum(i * hb - 1, 0), 0)),
                  pl.BlockSpec((CONV_HALO, Dm), full),
                  pl.BlockSpec((1, Dm), full),
                  pl.BlockSpec((1, Dm), full),
                  pl.BlockSpec((1, Dm), full),
                  pl.BlockSpec((Dm, Dm), full),
                  pl.BlockSpec((1, Dm), full),
                  pl.BlockSpec((tm, Dm), row),
                  pl.BlockSpec((1, 1, Dm), per_b)],
        out_specs=pl.BlockSpec((tm, Dm), row),
        out_shape=jax.ShapeDtypeStruct((T, Dm), F32),
        scratch_shapes=[pltpu.VMEM((tm + CONV_HALO, Dm), F32), pltpu.VMEM((tm, Dm), F32)],
        compiler_params=_cparams("parallel"),
        name="conv_dw_ln_pw2",
    )(u, u, wdw, vec(b_dw), vec(ln_g), vec(ln_b), w_pw2.astype(BF16), vec(b_pw2), x2, g)


FLASH_FIRST, FLASH_LAST = 1, 2
FLASH_CLASS_SHIFT = 2
FLASH_COL_STEP = 256
FLASH_CHUNK = 16
SEL_OFF = -(2.0 ** 100)
ONE_LANE = 64
LOG2E = 1.4426950408889634
FLASH_NPROB = 4


def _one_lane():
    return jnp.where(lax.broadcasted_iota(I32, (1, LANE), 1) == ONE_LANE, 1.0, 0.0)


def _flash_body(qi_ref, kj_ref, flag_ref, *refs, mode, rep, tq, tk, nprob, classes):
    if mode == "select":
        q_ref, k_ref, v_ref, selb_ref, o_ref, qa_sc, s_sc, p_sc, mb_sc, al_sc, acc_sc = refs
    else:
        q_ref, k_ref, v_ref, o_ref, s_sc, p_sc, mb_sc, al_sc, acc_sc = refs
    step = pl.program_id(2)
    qi = qi_ref[step]
    kj = kj_ref[step]
    flag = flag_ref[step]
    rows = rep * tq
    ch = FLASH_CHUNK

    @pl.when((flag & FLASH_FIRST) != 0)
    def _():
        mb_sc[...] = jnp.full(mb_sc.shape, NEG, F32)
        acc_sc[...] = jnp.zeros(acc_sc.shape, F32)
        if mode == "select":
            for pr in range(nprob):
                qa_sc[pr, :, :LANE] = q_ref[pr * rep:(pr + 1) * rep].reshape(rows, LANE)
                qa_sc[pr, :, LANE:] = jnp.concatenate([selb_ref[pr]] * rep, axis=0)

    def process(masked, c0, c1):
        w = c1 - c0
        nl = w // LANE
        lane_fold = lambda t, op: functools.reduce(op, [t[:, i * LANE:(i + 1) * LANE] for i in range(nl)])
        thr = qi * tq - kj * tk
        for pr in range(nprob):
            q = qa_sc[pr] if mode == "select" else q_ref[pr * rep:(pr + 1) * rep].reshape(rows, LANE)
            s_sc[pr, :, :w] = lax.dot_general(q, k_ref[pr, c0:c1, :], _NT, preferred_element_type=F32)
        if masked:
            diff = c0 + lax.broadcasted_iota(I32, (ch, w), 1) - lax.broadcasted_iota(I32, (ch, w), 0)
        for pr in range(nprob):
            for r0 in range(0, rows, ch):
                sc = s_sc[pr, r0:r0 + ch, :w]
                if masked:
                    lim = thr + (r0 % tq)
                    ok = diff <= lim
                    if mode == "window":
                        ok = ok & (diff > lim - WINDOW)
                    sc = jnp.where(ok, sc, NEG)
                    s_sc[pr, r0:r0 + ch, :w] = sc
                m_prev = mb_sc[pr, r0:r0 + ch, :]
                m_new = jnp.maximum(m_prev, jnp.max(lane_fold(sc, jnp.maximum), axis=1, keepdims=True))
                al_sc[pr, r0:r0 + ch, :] = jnp.exp2(m_prev - m_new)
                mb_sc[pr, r0:r0 + ch, :] = m_new
            for r0 in range(0, rows, ch):
                mb = mb_sc[pr, r0:r0 + ch, :]
                p = jnp.exp2(s_sc[pr, r0:r0 + ch, :w] - jnp.concatenate([mb] * nl, axis=1))
                p_sc[pr, r0:r0 + ch, :w] = p.astype(BF16)
            acc_sc[pr] = al_sc[pr] * acc_sc[pr] + jnp.dot(p_sc[pr, :, :w], v_ref[pr, c0:c1, :],
                                                           preferred_element_type=F32)

    for n, (masked, c0, c1) in enumerate(classes):
        pl.when((kj >= 0) & ((flag >> FLASH_CLASS_SHIFT) == n))(functools.partial(process, masked, c0, c1))

    @pl.when((flag & FLASH_LAST) != 0)
    def _():
        for pr in range(nprob):
            acc = acc_sc[pr]
            o = acc / acc[:, ONE_LANE:ONE_LANE + 1]
            o_ref[pr * rep:(pr + 1) * rep] = o.reshape(rep, tq, LANE).astype(o_ref.dtype)


def _flash_schedule(seq, tq, tk, mode):
    qi, kj, flags, classes = [], [], [], []
    cw = min(FLASH_COL_STEP, tk)
    for i in range(seq // tq):
        q_lo, q_hi = i * tq, i * tq + tq - 1
        hi = q_hi // tk
        lo = 0 if mode != "window" else (q_lo - (WINDOW - 1)) // tk
        js = list(range(lo, hi + 1))
        for n, j in enumerate(js):
            c1 = min(tk, -(-(q_hi - j * tk + 1) // cw) * cw)
            c0 = 0 if mode != "window" else max(0, (q_lo - (WINDOW - 1) - j * tk) // cw * cw)
            masked = mode == "window" or j * tk + c1 - 1 > q_lo
            cls = (masked, c0, c1)
            if cls not in classes:
                classes.append(cls)
            qi.append(i)
            kj.append(j if j >= 0 else -1)
            flags.append((FLASH_FIRST if n == 0 else 0) | (FLASH_LAST if n == len(js) - 1 else 0)
                         | (classes.index(cls) << FLASH_CLASS_SHIFT))
    as_i32 = lambda v: jnp.asarray(np.asarray(v, np.int32))
    return as_i32(qi), as_i32(kj), as_i32(flags), len(qi), tuple(classes)


def _flash(q, k, v, selb, *, batch, seq, mode, tq, tk):
    hq, T, _ = q.shape
    hkv = k.shape[0]
    kw = k.shape[-1]
    rep = hq // hkv
    npb = FLASH_NPROB
    assert tq & (tq - 1) == 0 and tq % FLASH_CHUNK == 0 and hkv % npb == 0
    qi, kj, flags, nsteps, classes = _flash_schedule(seq, tq, tk, mode)
    nq, nk = seq // tq, seq // tk
    q_map = lambda g, b, s, qi, kj, fl: (g, b * nq + qi[s], 0)
    k_map = lambda g, b, s, qi, kj, fl: (g, b * nk + jnp.maximum(kj[s], 0), 0)
    in_specs = [pl.BlockSpec((npb * rep, tq, LANE), q_map),
                pl.BlockSpec((npb, tk, kw), k_map),
                pl.BlockSpec((npb, tk, LANE), k_map)]
    args = [q, k, v]
    rows = rep * tq
    scratch = []
    if mode == "select":
        in_specs.append(pl.BlockSpec((npb, tq, LANE), q_map))
        args.append(selb)
        scratch.append(pltpu.VMEM((npb, rows, kw), BF16))
    scratch += [pltpu.VMEM((npb, rows, tk), F32), pltpu.VMEM((npb, rows, tk), BF16),
                pltpu.VMEM((npb, rows, LANE), F32), pltpu.VMEM((npb, rows, LANE), F32),
                pltpu.VMEM((npb, rows, LANE), F32)]
    return pl.pallas_call(
        functools.partial(_flash_body, mode=mode, rep=rep, tq=tq, tk=tk, nprob=npb, classes=classes),
        grid_spec=pltpu.PrefetchScalarGridSpec(
            num_scalar_prefetch=3,
            grid=(hkv // npb, batch, nsteps),
            in_specs=in_specs,
            out_specs=pl.BlockSpec((npb * rep, tq, LANE), q_map),
            scratch_shapes=scratch),
        out_shape=jax.ShapeDtypeStruct((hq, T, LANE), BF16),
        compiler_params=_cparams("parallel", "parallel", "arbitrary"),
        name="flash_" + mode,
    )(qi, kj, flags, *args)


def _attn_out_body(*refs, n_branch, heads):
    o_refs = refs[:n_branch]
    if n_branch > 1:
        gl_ref, ex_ref, w_ref, x_ref, g_ref, out_ref = refs[n_branch:]
        spread = jnp.dot(_sigmoid(gl_ref[...]).astype(BF16), ex_ref[...], preferred_element_type=F32)
    else:
        w_ref, x_ref, g_ref, out_ref = refs[n_branch:]
    per_head = []
    for h in range(heads):
        if n_branch > 1:
            o = jnp.zeros(o_refs[0].shape[1:], F32)
            for c in range(n_branch):
                col = n_branch * h + c
                o = o + spread[:, col * LANE:(col + 1) * LANE] * o_refs[c][h].astype(F32)
            o = o.astype(BF16)
        else:
            o = o_refs[0][h]
        per_head.append(o)
    acc = jnp.dot(jnp.concatenate(per_head, axis=1), w_ref[...], preferred_element_type=F32)
    out_ref[...] = x_ref[...] + g_ref[0] * acc


def _attn_out(os, gl, w_heads, x2, g, seq):
    T, Dm = x2.shape
    heads = w_heads.shape[0]
    tm = 256
    nb = seq // tm
    row = lambda i: (i, 0)
    o_spec = pl.BlockSpec((heads, tm, LANE), lambda i: (0, i, 0))
    in_specs = [o_spec] * len(os)
    args = list(os)
    if len(os) > 1:
        ncol = len(os) * heads
        spread = (jnp.arange(LANE)[:, None] == jnp.arange(ncol * LANE)[None, :] // LANE).astype(BF16)
        in_specs += [pl.BlockSpec((tm, LANE), row), pl.BlockSpec((LANE, ncol * LANE), lambda i: (0, 0))]
        args += [gl, spread]
    in_specs += [pl.BlockSpec((heads * LANE, Dm), lambda i: (0, 0)),
                 pl.BlockSpec((tm, Dm), row),
                 pl.BlockSpec((1, 1, Dm), lambda i: (i // nb, 0, 0))]
    args += [w_heads.reshape(heads * LANE, Dm), x2, g]
    return pl.pallas_call(
        functools.partial(_attn_out_body, n_branch=len(os), heads=heads),
        grid=(T // tm,),
        in_specs=in_specs,
        out_specs=pl.BlockSpec((tm, Dm), row),
        out_shape=jax.ShapeDtypeStruct((T, Dm), F32),
        compiler_params=_cparams("parallel"),
        name="attn_out_%d" % len(os),
    )(*args)


def _pad_heads_rows(w, heads, dh):
    w = w.reshape(heads, dh, -1)
    return jnp.pad(w, ((0, 0), (0, LANE - dh), (0, 0))).astype(BF16)


def _pad_heads_cols(w, heads, dh):
    k = w.shape[0]
    w = w.reshape(k, heads, dh)
    return jnp.pad(w, ((0, 0), (0, 0), (0, LANE - dh))).reshape(k, heads * LANE)


def _rope_tables(pos, rot, offset):
    half = rot // 2
    inv_freq = ROPE_THETA ** (-jnp.arange(0, rot, 2, dtype=F32) / rot)
    ang = pos.astype(F32)[:, None] * inv_freq[None, :]
    cos, sin = jnp.cos(ang), jnp.sin(ang)
    n = pos.shape[0]
    c = jnp.ones((n, LANE), F32).at[:, offset:offset + rot].set(jnp.concatenate([cos, cos], axis=1))
    sa = jnp.zeros((n, LANE), F32).at[:, offset:offset + half].set(-sin)
    sb = jnp.zeros((n, LANE), F32).at[:, offset + half:offset + rot].set(sin)
    return c, sa, sb


def _mla_proj_body(x_ref, a_ref, b_ref, win_ref, qlg_ref, kvlg_ref, wuq_ref, wuqs_ref, wuk_ref, wuv_ref,
                   ones_ref, q1_ref, q2_ref, k1_ref, k2_ref, q_ref, k_ref, v_ref):
    h = _normmod(x_ref[...], a_ref[0], b_ref[0]).astype(BF16)
    r = jnp.dot(h, win_ref[...], preferred_element_type=F32)
    lat = MLA_Q_LORA + MLA_KV_LORA
    q_lat = r[:, :MLA_Q_LORA]
    kv_lat = r[:, MLA_Q_LORA:lat]
    kpe = r[:, lat:lat + LANE]
    kpe_swap = r[:, lat + LANE:]
    ql = q_lat * lax.rsqrt(jnp.mean(q_lat * q_lat, axis=-1, keepdims=True) + EPS) * qlg_ref[...]
    kvl = kv_lat * lax.rsqrt(jnp.mean(kv_lat * kv_lat, axis=-1, keepdims=True) + EPS) * kvlg_ref[...]
    ql = ql.astype(BF16)
    kvl = kvl.astype(BF16)
    q = jnp.dot(ql, wuq_ref[...], preferred_element_type=F32)
    q_swap = jnp.dot(ql, wuqs_ref[...], preferred_element_type=F32)
    kn = jnp.dot(kvl, wuk_ref[...], preferred_element_type=F32)
    v = jnp.dot(kvl, wuv_ref[...], preferred_element_type=F32)
    ones = ones_ref[...]
    q1, q2, k1, k2 = q1_ref[...], q2_ref[...], k1_ref[...], k2_ref[...]
    k_rot = kpe_swap * k2

    def inv_rms(t):
        ss = jnp.dot((t * t).astype(BF16), ones, preferred_element_type=F32)
        return lax.rsqrt(ss * (1.0 / MLA_QK) + EPS)

    for hd in range(MLA_HEADS):
        sl = slice(hd * LANE, (hd + 1) * LANE)
        xq = q[:, sl]
        q_ref[hd] = ((xq * q1 + q_swap[:, sl] * q2) * inv_rms(xq)).astype(BF16)
        xk = kn[:, sl] + kpe
        k_ref[hd] = ((xk * k1 + k_rot) * inv_rms(xk)).astype(BF16)
        v_ref[hd] = (v[:, sl] + _one_lane()).astype(BF16)


def _mla_mixer(x2, a, b, g, w_in, q_lat_g, kv_lat_g, w_uq, w_ukv, q_gain, k_gain, w_out, batch, seq):
    T, Dm = x2.shape
    H = MLA_HEADS
    tm = 256
    nb = seq // tm
    scale = MLA_QK ** -0.5 * LOG2E
    lat = MLA_Q_LORA + MLA_KV_LORA
    half = MLA_ROPE // 2

    def swap_rope(t):
        lo, hi = t[..., MLA_NOPE:MLA_NOPE + half], t[..., MLA_NOPE + half:MLA_QK]
        return jnp.concatenate([jnp.zeros_like(t[..., :MLA_NOPE]), hi, lo], axis=-1)

    to_slot = lambda t: jnp.pad(t, [(0, 0)] * (t.ndim - 1) + [(0, LANE - MLA_QK)])
    kpe_w = jnp.concatenate([jnp.zeros((Dm, MLA_NOPE), F32), w_in[:, lat:]], axis=1)
    win_p = jnp.concatenate([w_in[:, :lat], to_slot(kpe_w), to_slot(swap_rope(kpe_w))], axis=1).astype(BF16)
    wuq3 = w_uq.reshape(MLA_Q_LORA, H, MLA_QK)
    wuq_p = to_slot(wuq3).reshape(MLA_Q_LORA, H * LANE).astype(BF16)
    wuqs_p = to_slot(swap_rope(wuq3)).reshape(MLA_Q_LORA, H * LANE).astype(BF16)
    wukv = w_ukv.reshape(MLA_KV_LORA, H, MLA_NOPE + MLA_V)
    wuk_p = _pad_heads_cols(wukv[:, :, :MLA_NOPE].reshape(MLA_KV_LORA, -1), H, MLA_NOPE).astype(BF16)
    wuv_p = _pad_heads_cols(wukv[:, :, MLA_NOPE:].reshape(MLA_KV_LORA, -1), H, MLA_V).astype(BF16)
    c, sa, sb = _rope_tables(jnp.arange(seq), MLA_ROPE, MLA_NOPE)
    qg = q_gain * scale
    q1, q2 = c * to_slot(qg)[None], (sa + sb) * to_slot(swap_rope(qg))[None]
    k1, k2 = c * to_slot(k_gain)[None], (sa + sb) * to_slot(swap_rope(k_gain))[None]
    ones = jnp.ones((LANE, LANE), BF16)
    row = lambda i: (i, 0)
    full = lambda i: (0, 0)
    per_b = lambda i: (i // nb, 0, 0)
    pos = lambda i: (i % nb, 0)
    head_out = pl.BlockSpec((H, tm, LANE), lambda i: (0, i, 0))
    hshape = jax.ShapeDtypeStruct((H, T, LANE), BF16)
    q, k, v = pl.pallas_call(
        _mla_proj_body,
        grid=(T // tm,),
        in_specs=[pl.BlockSpec((tm, Dm), row),
                  pl.BlockSpec((1, 1, Dm), per_b),
                  pl.BlockSpec((1, 1, Dm), per_b),
                  pl.BlockSpec(win_p.shape, full),
                  pl.BlockSpec((1, MLA_Q_LORA), full),
                  pl.BlockSpec((1, MLA_KV_LORA), full),
                  pl.BlockSpec(wuq_p.shape, full),
                  pl.BlockSpec(wuqs_p.shape, full),
                  pl.BlockSpec(wuk_p.shape, full),
                  pl.BlockSpec(wuv_p.shape, full),
                  pl.BlockSpec((LANE, LANE), full),
                  pl.BlockSpec((tm, LANE), pos),
                  pl.BlockSpec((tm, LANE), pos),
                  pl.BlockSpec((tm, LANE), pos),
                  pl.BlockSpec((tm, LANE), pos)],
        out_specs=[head_out, head_out, head_out],
        out_shape=[hshape, hshape, hshape],
        compiler_params=_cparams("parallel"),
        name="mla_proj",
    )(x2, a, b, win_p, q_lat_g.reshape(1, -1), kv_lat_g.reshape(1, -1), wuq_p, wuqs_p, wuk_p, wuv_p,
      ones, q1, q2, k1, k2)
    blk = min(1024, seq)
    o = _flash(q, k, v, None, batch=batch, seq=seq, mode="causal", tq=blk, tk=blk)
    return _attn_out([o], None, _pad_heads_rows(w_out, H, MLA_V), x2, g, seq)


N_KV_STREAMS = 6


def _nsa_proj_body(x_ref, a_ref, b_ref, w_ref, qg_ref, kg_ref, c_ref, sa_ref, sb_ref, blk_ref,
                   q_ref, craw_ref, ks_ref, vs_ref, kw_ref, vw_ref, gl_ref):
    h = _normmod(x_ref[...], a_ref[0], b_ref[0]).astype(BF16)
    r = jnp.dot(h, w_ref[...], preferred_element_type=F32)
    c, sa, sb = c_ref[...], sa_ref[...], sb_ref[...]

    def norm_rot(t, gain):
        t = t * lax.rsqrt(jnp.sum(t * t, axis=-1, keepdims=True) * (1.0 / NSA_DH) + EPS) * gain
        return _rope(t, c, sa, sb, NSA_ROT // 2)

    for hd in range(NSA_HEADS):
        q_ref[hd] = norm_rot(r[:, hd * LANE:(hd + 1) * LANE], qg_ref[...]).astype(BF16)
    base = NSA_HEADS * LANE
    for st in range(N_KV_STREAMS):
        for gi in range(NSA_GROUPS):
            off = base + (st * NSA_GROUPS + gi) * LANE
            t = r[:, off:off + LANE]
            if st < 2:
                craw_ref[st * NSA_GROUPS + gi] = t
            elif st == 2:
                t = norm_rot(t, kg_ref[1:2, :])
                ks_ref[gi] = jnp.concatenate([t.astype(BF16), blk_ref[...]], axis=1)
            elif st == 4:
                kw_ref[gi] = norm_rot(t, kg_ref[2:3, :]).astype(BF16)
            else:
                (vs_ref if st == 3 else vw_ref)[gi] = (t + _one_lane()).astype(BF16)
    gl_ref[...] = r[:, base + N_KV_STREAMS * NSA_GROUPS * LANE:]


def _nsa_compress_body(x_ref, pea_ref, peb_ref, w1a_ref, w1b_ref, w2_ref, kg_ref, c_ref, sa_ref, sb_ref,
                       o_ref, *, is_key, n_cmp):
    ncp = o_ref.shape[1]
    x = jnp.concatenate([x_ref[0, pl.ds(l, ncp, stride=CMP_STRIDE), :] for l in range(CMP_STRIDE)], axis=1)
    xa = (x + pea_ref[...]).astype(BF16)
    xb = (x + peb_ref[...]).astype(BF16)
    za = jnp.dot(xa, w1a_ref[...], preferred_element_type=F32)
    zb = jnp.dot(xb, w1b_ref[...], preferred_element_type=F32)
    rows = za.shape[0]
    z = _silu(za + pltpu.roll(zb, rows - 1, 0))
    t = jnp.dot(z.astype(BF16), w2_ref[...], preferred_element_type=F32)
    if is_key:
        t = t * lax.rsqrt(jnp.sum(t * t, axis=-1, keepdims=True) * (1.0 / NSA_DH) + EPS) * kg_ref[...]
        t = _rope(t, c_ref[...], sa_ref[...], sb_ref[...], NSA_ROT // 2)
    valid = lax.broadcasted_iota(I32, t.shape, 0) < n_cmp
    o_ref[0] = jnp.where(valid, t, 0.0).astype(BF16)


def _nsa_cmp_select_body(q_ref, kc_ref, vc_ref, oc_ref, sel_ref, p_sc, *, tq, limits, n_cmp, n_top):
    i = pl.program_id(2)
    t0 = i * tq
    rep = q_ref.shape[0]
    per = SLC_BLOCK // CMP_STRIDE

    def compute(limit):
        nc = limit // CMP_STRIDE
        ns = limit // SLC_BLOCK
        kc = kc_ref[0, :nc, :]
        vc = vc_ref[0, :nc, :]
        q = q_ref[...].reshape(rep * tq, LANE)
        sc = lax.dot_general(q, kc, _NT, preferred_element_type=F32)
        qpos = t0 + lax.broadcasted_iota(I32, (tq, nc), 0)
        blk = lax.broadcasted_iota(I32, (tq, nc), 1)
        mask = ((blk * CMP_STRIDE + (CMP_BLOCK - 1)) <= qpos) & (blk < n_cmp)
        sc = jnp.where(mask[None], sc.reshape(rep, tq, nc), NEG)
        p = jnp.exp2(sc - jnp.max(sc, axis=-1, keepdims=True))
        p = jnp.where(mask[None], p / jnp.sum(p, axis=-1, keepdims=True), 0.0)
        oc = jnp.dot(p.reshape(rep * tq, nc).astype(BF16), vc, preferred_element_type=F32)
        oc_ref[...] = oc.reshape(rep, tq, LANE).astype(BF16)
        qpos_t = t0 + lax.broadcasted_iota(I32, (nc, tq), 1)
        blk_t = lax.broadcasted_iota(I32, (nc, tq), 0)
        mask_t = ((blk_t * CMP_STRIDE + (CMP_BLOCK - 1)) <= qpos_t) & (blk_t < n_cmp)
        psum = jnp.zeros((nc, tq), F32)
        for r in range(rep):
            st = lax.dot_general(kc, q_ref[r], _NT, preferred_element_type=F32)
            st = jnp.where(mask_t, st, NEG)
            pt = jnp.exp2(st - jnp.max(st, axis=0, keepdims=True))
            psum = psum + jnp.where(mask_t, pt / jnp.sum(pt, axis=0, keepdims=True), 0.0)
        slabs = []
        for sb in range(tq // LANE):
            p_sc[sb, 0:8, :] = jnp.zeros((8, LANE), F32)
            p_sc[sb, 8:8 + nc, :] = psum[:, sb * LANE:(sb + 1) * LANE]
            part = p_sc[sb, pl.ds(7, ns, stride=per), :]
            for k in range(1, per + 1):
                part = part + p_sc[sb, pl.ds(7 + k, ns, stride=per), :]
            slabs.append(part)
        imp = jnp.concatenate(slabs, axis=1)
        jb = lax.broadcasted_iota(I32, (ns, tq), 0)
        qp = t0 + lax.broadcasted_iota(I32, (ns, tq), 1)
        imp = jnp.where(jb * SLC_BLOCK <= qp, imp, -BIG)
        imp = jnp.where((jb == 0) | (jb == jnp.right_shift(qp, 6)), BIG, imp)
        jbf = jb.astype(F32)
        sel_t = jnp.full((ns, tq), SEL_OFF, F32)
        for _ in range(n_top):
            top = jnp.max(imp, axis=0, keepdims=True)
            first = jnp.min(jnp.where(imp == top, jbf, float(MAX_SLC)), axis=0, keepdims=True)
            hit = jbf == first
            sel_t = jnp.where(hit, 0.0, sel_t)
            imp = jnp.where(hit, -jnp.inf, imp)
        if ns < MAX_SLC:
            sel_t = jnp.concatenate([sel_t, jnp.full((MAX_SLC - ns, tq), SEL_OFF, F32)], axis=0)
        sel_ref[0] = sel_t.T.astype(BF16)

    end = t0 + tq
    lower = 0
    for limit in limits:
        pl.when((end > lower) & (end <= limit))(functools.partial(compute, limit))
        lower = limit


def _nsa_mixer(x2, a, b, g, w_in, w_cmp1, w_cmp2, cmp_pos, q_gain, k_gain, w_out, batch, seq):
    T, Dm = x2.shape
    H, G, dh = NSA_HEADS, NSA_GROUPS, NSA_DH
    scale = dh ** -0.5 * LOG2E
    n_cmp = seq // CMP_STRIDE - 1
    ncp = seq // CMP_STRIDE
    n_slc = seq // SLC_BLOCK
    assert n_slc <= MAX_SLC and ncp <= MAX_SLC * (SLC_BLOCK // CMP_STRIDE)
    n_top = min(SLC_TOP_N, n_slc)
    tm = 256
    nb = seq // tm
    q_cols = _pad_heads_cols(w_in[:, :H * dh], H, dh)
    kv_cols = _pad_heads_cols(w_in[:, H * dh:H * dh + N_KV_STREAMS * G * dh], N_KV_STREAMS * G, dh)
    gl_cols = jnp.pad(w_in[:, H * dh + N_KV_STREAMS * G * dh:], ((0, 0), (0, LANE - 3 * H)))
    w_p = jnp.concatenate([q_cols, kv_cols, gl_cols], axis=1).astype(BF16)
    pad_gain = lambda v: jnp.pad(v, ((0, 0), (0, LANE - dh)))
    c, sa, sb = _rope_tables(jnp.arange(seq), NSA_ROT, 0)
    row = lambda i: (i, 0)
    full = lambda i: (0, 0)
    per_b = lambda i: (i // nb, 0, 0)
    pos = lambda i: (i % nb, 0)
    blk_onehot = (jnp.arange(seq)[:, None] // SLC_BLOCK == jnp.arange(MAX_SLC)[None, :]).astype(BF16)
    group_out = pl.BlockSpec((G, tm, LANE), lambda i: (0, i, 0))
    group_shape = jax.ShapeDtypeStruct((G, T, LANE), BF16)
    q, craw, ks, vs, kw, vw, gl = pl.pallas_call(
        _nsa_proj_body,
        grid=(T // tm,),
        in_specs=[pl.BlockSpec((tm, Dm), row),
                  pl.BlockSpec((1, 1, Dm), per_b),
                  pl.BlockSpec((1, 1, Dm), per_b),
                  pl.BlockSpec(w_p.shape, full),
                  pl.BlockSpec((1, LANE), full),
                  pl.BlockSpec((3, LANE), full),
                  pl.BlockSpec((tm, LANE), pos),
                  pl.BlockSpec((tm, LANE), pos),
                  pl.BlockSpec((tm, LANE), pos),
                  pl.BlockSpec((tm, MAX_SLC), pos)],
        out_specs=[pl.BlockSpec((H, tm, LANE), lambda i: (0, i, 0)),
                   pl.BlockSpec((2 * G, tm, LANE), lambda i: (0, i, 0)),
                   pl.BlockSpec((G, tm, LANE + MAX_SLC), lambda i: (0, i, 0)),
                   group_out, group_out, group_out,
                   pl.BlockSpec((tm, LANE), row)],
        out_shape=[jax.ShapeDtypeStruct((H, T, LANE), BF16),
                   jax.ShapeDtypeStruct((2 * G, T, LANE), F32),
                   jax.ShapeDtypeStruct((G, T, LANE + MAX_SLC), BF16),
                   group_shape, group_shape, group_shape,
                   jax.ShapeDtypeStruct((T, LANE), F32)],
        compiler_params=_cparams("parallel"),
        name="nsa_proj",
    )(x2, a, b, w_p, pad_gain(q_gain.reshape(1, dh) * scale), pad_gain(k_gain), c, sa, sb, blk_onehot)

    cmp_end = jnp.arange(ncp) * CMP_STRIDE + (CMP_BLOCK - 1)
    cc, csa, csb = _rope_tables(cmp_end, NSA_ROT, 0)
    kdim = CMP_STRIDE * LANE

    def compress(stream, w1, w2, pe, is_key):
        w1p = jnp.pad(w1, ((0, 0), (0, LANE - dh), (0, LANE - dh)))
        w1a = w1p[:CMP_STRIDE].reshape(kdim, LANE).astype(BF16)
        w1b = w1p[CMP_STRIDE:].reshape(kdim, LANE).astype(BF16)
        pep = jnp.pad(pe, ((0, 0), (0, LANE - dh)))
        pea = pep[:CMP_STRIDE].reshape(1, kdim)
        peb = pep[CMP_STRIDE:].reshape(1, kdim)
        w2p = jnp.pad(w2, ((0, LANE - dh), (0, LANE - dh))).astype(BF16)
        const = lambda gi, bi: (0, 0)
        return pl.pallas_call(
            functools.partial(_nsa_compress_body, is_key=is_key, n_cmp=n_cmp),
            grid=(G, batch),
            in_specs=[pl.BlockSpec((1, seq, LANE), lambda gi, bi: (stream * G + gi, bi, 0)),
                      pl.BlockSpec((1, kdim), const),
                      pl.BlockSpec((1, kdim), const),
                      pl.BlockSpec((kdim, LANE), const),
                      pl.BlockSpec((kdim, LANE), const),
                      pl.BlockSpec((LANE, LANE), const),
                      pl.BlockSpec((1, LANE), const),
                      pl.BlockSpec((ncp, LANE), const),
                      pl.BlockSpec((ncp, LANE), const),
                      pl.BlockSpec((ncp, LANE), const)],
            out_specs=pl.BlockSpec((1, ncp, LANE), lambda gi, bi: (gi, bi, 0)),
            out_shape=jax.ShapeDtypeStruct((G, batch * ncp, LANE), BF16),
            compiler_params=_cparams("parallel", "parallel"),
            name="nsa_compress_" + ("k" if is_key else "v"),
        )(craw, pea, peb, w1a, w1b, w2p, pad_gain(k_gain)[0:1], cc, csa, csb)

    kc = compress(0, w_cmp1[0], w_cmp2[0], cmp_pos[0], True)
    vc = compress(1, w_cmp1[1], w_cmp2[1], cmp_pos[1], False)

    tq = 256
    nq = seq // tq
    limits = tuple(sorted({max(seq // d, min(seq, LANE * CMP_STRIDE)) for d in (4, 2, 1)}))
    q_map = lambda gi, bi, i: (gi, bi * nq + i, 0)
    c_map = lambda gi, bi, i: (gi, bi, 0)
    o_c, sel = pl.pallas_call(
        functools.partial(_nsa_cmp_select_body, tq=tq, limits=limits, n_cmp=n_cmp, n_top=n_top),
        grid=(G, batch, nq),
        in_specs=[pl.BlockSpec((NSA_REP, tq, LANE), q_map),
                  pl.BlockSpec((1, ncp, LANE), c_map),
                  pl.BlockSpec((1, ncp, LANE), c_map)],
        out_specs=[pl.BlockSpec((NSA_REP, tq, LANE), q_map),
                   pl.BlockSpec((1, tq, LANE), q_map)],
        out_shape=[jax.ShapeDtypeStruct((H, T, LANE), BF16),
                   jax.ShapeDtypeStruct((G, T, LANE), BF16)],
        scratch_shapes=[pltpu.VMEM((tq // LANE, 8 + MAX_SLC * (SLC_BLOCK // CMP_STRIDE), LANE), F32)],
        compiler_params=_cparams("parallel", "parallel", "parallel"),
        name="nsa_cmp_select",
    )(q, kc, vc)

    o_s = _flash(q, ks, vs, sel, batch=batch, seq=seq, mode="select", tq=256, tk=min(1024, seq))
    o_w = _flash(q, kw, vw, None, batch=batch, seq=seq, mode="window", tq=256, tk=WINDOW)
    return _attn_out([o_c, o_s, o_w], gl, _pad_heads_rows(w_out, H, dh), x2, g, seq)


def _router_body(x_ref, a_ref, b_ref, rwt_ref, rb_ref, tri_ref, e_ref, w_ref, rank_ref, cnt_ref,
                 carry_sc, *, tm):
    i = pl.program_id(0)

    @pl.when(i == 0)
    def _():
        carry_sc[...] = jnp.zeros(carry_sc.shape, F32)

    h = _normmod(x_ref[...], a_ref[0], b_ref[0])
    h_hi = h.astype(BF16)
    h_lo = (h - h_hi.astype(F32)).astype(BF16)
    w_hi, w_lo = rwt_ref[0], rwt_ref[1]
    dot_nt = lambda p, q: lax.dot_general(p, q, _NT, preferred_element_type=F32)
    logits = dot_nt(w_hi, h_hi) + (dot_nt(w_hi, h_lo) + dot_nt(w_lo, h_hi))
    scores = _sigmoid(logits)
    biased = scores + rb_ref[...]
    ng, per = N_GROUPS, EXPERTS_PER_GROUP
    row = lambda arr, r: arr[r:r + 1, :]
    gsel = jnp.zeros((1, tm), I32)
    best = None
    for gi in range(ng):
        v = [row(biased, gi * per + k) for k in range(per)]
        top2 = None
        for p in range(per):
            for q in range(p + 1, per):
                s = v[p] + v[q]
                top2 = s if top2 is None else jnp.maximum(top2, s)
        if best is None:
            best = top2
        else:
            better = top2 > best
            gsel = jnp.where(better, gi, gsel)
            best = jnp.where(better, top2, best)
    cb, cs = [], []
    for k in range(per):
        b_k = row(biased, k)
        s_k = row(scores, k)
        for gi in range(1, ng):
            hit = gsel == gi
            b_k = jnp.where(hit, row(biased, gi * per + k), b_k)
            s_k = jnp.where(hit, row(scores, gi * per + k), s_k)
        cb.append(b_k)
        cs.append(s_k)

    def argmax_first(vals):
        idx = jnp.zeros((1, tm), I32)
        top = vals[0]
        for k in range(1, per):
            better = vals[k] > top
            idx = jnp.where(better, k, idx)
            top = jnp.where(better, vals[k], top)
        return idx

    def pick(vals, idx):
        out = vals[0]
        for k in range(1, per):
            out = jnp.where(idx == k, vals[k], out)
        return out

    i1 = argmax_first(cb)
    i2 = argmax_first([jnp.where(i1 == k, -jnp.inf, cb[k]) for k in range(per)])
    w1 = pick(cs, i1)
    w2 = pick(cs, i2)
    tot = w1 + w2
    e1 = gsel * per + i1
    e2 = gsel * per + i2
    eid = lax.broadcasted_iota(I32, (N_EXPERTS, tm), 0)
    hot = (eid == e1) | (eid == e2)
    onehot = jnp.where(hot, 1.0, 0.0)
    before = carry_sc[...] + jnp.dot(onehot.astype(BF16), tri_ref[...], preferred_element_type=F32)
    r1 = jnp.sum(jnp.where(eid == e1, before, 0.0), axis=0, keepdims=True)
    r2 = jnp.sum(jnp.where(eid == e2, before, 0.0), axis=0, keepdims=True)
    carry = carry_sc[...] + jnp.sum(onehot, axis=1, keepdims=True)
    carry_sc[...] = carry
    cnt_ref[...] = jnp.broadcast_to(carry, cnt_ref.shape)
    zi = jnp.zeros((6, tm), I32)
    e_ref[...] = jnp.concatenate([e1, e2, zi], axis=0)
    rank_ref[...] = jnp.concatenate([r1.astype(I32), r2.astype(I32), zi], axis=0)
    w_ref[...] = jnp.concatenate([w1 / tot, w2 / tot, jnp.zeros((6, tm), F32)], axis=0)


ROW_SUB = D_MODEL // LANE


def _row_tile(r):
    return pl.ds(pl.multiple_of(r * ROW_SUB, ROW_SUB), ROW_SUB)


def _to_row_tiles(ref, val):
    n = val.shape[0]
    for s in range(ROW_SUB):
        ref[pl.ds(s, n, stride=ROW_SUB), :] = val[:, s * LANE:(s + 1) * LANE]


def _from_row_tiles(ref, n, s):
    return ref[pl.ds(s, n, stride=ROW_SUB), :]


def _moe_dispatch_body(starts_ref, counts_ref, padded_ref, nv_ref, d0_ref, d1_ref,
                       x_ref, a_ref, b_ref, xs_hbm, hbuf0, hbuf1, zbuf, sems, zsem,
                       *, tm, tg, n_tiles, nt):
    i = pl.program_id(0)

    @pl.when(i == 0)
    def _():
        zbuf[...] = jnp.zeros(zbuf.shape, F32)
        zrow = zbuf.at[pl.ds(0, ROW_SUB)]
        for ex in range(N_EXPERTS):
            lo = starts_ref[ex] + counts_ref[ex]
            hi = starts_ref[ex] + padded_ref[ex]

            def fill(r, carry):
                pltpu.make_async_copy(zrow, xs_hbm.at[_row_tile(r)], zsem).start()
                return carry

            lax.fori_loop(lo, hi, fill, 0)

            def drain_fill(r, carry):
                pltpu.make_async_copy(zrow, xs_hbm.at[_row_tile(0)], zsem).wait()
                return carry

            lax.fori_loop(lo, hi, drain_fill, 0)

        rows_per_tile = tg * ROW_SUB

        def fill_tile(t, carry):
            dst = xs_hbm.at[pl.ds(pl.multiple_of(t * rows_per_tile, rows_per_tile), rows_per_tile)]
            pltpu.make_async_copy(zbuf, dst, zsem).start()
            return carry

        lax.fori_loop(nv_ref[0], n_tiles, fill_tile, 0)

        def drain_tile(t, carry):
            pltpu.make_async_copy(zbuf, xs_hbm.at[pl.ds(0, rows_per_tile)], zsem).wait()
            return carry

        lax.fori_loop(nv_ref[0], n_tiles, drain_tile, 0)

    h = _normmod(x_ref[...], a_ref[0], b_ref[0])

    def scatter_from(hbuf, sem, other_buf, other_sem):
        _to_row_tiles(hbuf, h)

        def row_copy(buf, sm, r, slot):
            return pltpu.make_async_copy(buf.at[_row_tile(r)], xs_hbm.at[_row_tile(slot)], sm)

        def issue(r, carry):
            row_copy(hbuf, sem, r, d0_ref[r]).start(priority=0)
            row_copy(hbuf, sem, r, d1_ref[r]).start(priority=1)
            return carry

        lax.fori_loop(0, tm, issue, 0, unroll=8)

        def drain(buf, sm):
            def body(r, carry):
                row_copy(buf, sm, r, 0).wait()
                row_copy(buf, sm, r, 0).wait()
                return carry

            lax.fori_loop(0, tm, body, 0, unroll=8)

        pl.when(i > 0)(lambda: drain(other_buf, other_sem))
        pl.when(i == nt - 1)(lambda: drain(hbuf, sem))

    pl.when(i % 2 == 0)(lambda: scatter_from(hbuf0, sems.at[0], hbuf1, sems.at[1]))
    pl.when(i % 2 == 1)(lambda: scatter_from(hbuf1, sems.at[1], hbuf0, sems.at[0]))


def _moe_expert_body(te_ref, nv_ref, xs_ref, win_ref, wout_ref, y_ref, *, tg):
    i = pl.program_id(0)

    @pl.when(i < nv_ref[0])
    def _():
        x = jnp.concatenate([_from_row_tiles(xs_ref, tg, s) for s in range(ROW_SUB)], axis=1).astype(BF16)
        gu = jnp.dot(x, win_ref[0], preferred_element_type=F32)
        act = (_silu(gu[:, :D_EXPERT]) * gu[:, D_EXPERT:]).astype(BF16)
        _to_row_tiles(y_ref, jnp.dot(act, wout_ref[0], preferred_element_type=F32))

    @pl.when(i >= nv_ref[0])
    def _():
        y_ref[...] = jnp.zeros(y_ref.shape, F32)


def _moe_combine_body(d0c_ref, d1c_ref, d0n_ref, d1n_ref, y_hbm, x_ref, w_ref, g_ref, o_ref,
                      ya0, ya1, yb0, yb1, sems, *, tm, nt):
    i = pl.program_id(0)

    def row_copy(slot, buf, sem, r):
        return pltpu.make_async_copy(y_hbm.at[_row_tile(slot)], buf.at[_row_tile(r)], sem)

    def issue(d0_ref, d1_ref, bufs, sem):
        def body(r, carry):
            row_copy(d0_ref[r], bufs[0], sem, r).start(priority=0)
            row_copy(d1_ref[r], bufs[1], sem, r).start(priority=1)
            return carry

        lax.fori_loop(0, tm, body, 0, unroll=8)

    def finish(bufs, sem):
        def body(r, carry):
            row_copy(0, bufs[0], sem, r).wait()
            row_copy(0, bufs[1], sem, r).wait()
            return carry

        lax.fori_loop(0, tm, body, 0, unroll=8)
        w = w_ref[...]
        w0, w1 = w[:, 0:1], w[:, 1:2]
        gate = g_ref[0]
        for s in range(ROW_SUB):
            sl = slice(s * LANE, (s + 1) * LANE)
            y = w0 * _from_row_tiles(bufs[0], tm, s) + w1 * _from_row_tiles(bufs[1], tm, s)
            o_ref[:, sl] = x_ref[:, sl] + gate[:, sl] * y

    set_a, set_b = (ya0, ya1), (yb0, yb1)
    pl.when(i == 0)(lambda: issue(d0c_ref, d1c_ref, set_a, sems.at[0]))

    def even():
        pl.when(i + 1 < nt)(lambda: issue(d0n_ref, d1n_ref, set_b, sems.at[1]))
        finish(set_a, sems.at[0])

    def odd():
        pl.when(i + 1 < nt)(lambda: issue(d0n_ref, d1n_ref, set_a, sems.at[0]))
        finish(set_b, sems.at[1])

    pl.when(i % 2 == 0)(even)
    pl.when(i % 2 == 1)(odd)


def _grouped_moe(x2, a, b, g, router_w, router_bias, w_in, w_out, seq):
    T, Dm = x2.shape
    E = N_EXPERTS
    tm = 512
    nb = seq // tm
    nt = T // tm
    row = lambda i: (i, 0)
    full = lambda i: (0, 0)
    per_b = lambda i: (i // nb, 0, 0)
    lanes = lambda i: (0, i)
    tri = jnp.asarray(np.triu(np.ones((tm, tm), np.float32), 1)).astype(BF16)
    rw_hi = router_w.T.astype(BF16)
    rw_split = jnp.stack([rw_hi, (router_w.T - rw_hi.astype(F32)).astype(BF16)])
    e, w, rank, cnt = pl.pallas_call(
        functools.partial(_router_body, tm=tm),
        grid=(nt,),
        in_specs=[pl.BlockSpec((tm, Dm), row),
                  pl.BlockSpec((1, 1, Dm), per_b),
                  pl.BlockSpec((1, 1, Dm), per_b),
                  pl.BlockSpec((2, E, Dm), lambda i: (0, 0, 0)),
                  pl.BlockSpec((E, 1), full),
                  pl.BlockSpec((tm, tm), full)],
        out_specs=[pl.BlockSpec((8, tm), lanes),
                   pl.BlockSpec((8, tm), lanes),
                   pl.BlockSpec((8, tm), lanes),
                   pl.BlockSpec((E, LANE), full)],
        out_shape=[jax.ShapeDtypeStruct((8, T), I32),
                   jax.ShapeDtypeStruct((8, T), F32),
                   jax.ShapeDtypeStruct((8, T), I32),
                   jax.ShapeDtypeStruct((E, LANE), F32)],
        scratch_shapes=[pltpu.VMEM((E, 1), F32)],
        compiler_params=_cparams("arbitrary"),
        name="moe_router",
    )(x2, a, b, rw_split, router_bias.reshape(E, 1), tri)

    tg = MOE_TILE
    n_tiles = (2 * T) // tg + E
    n_slots = n_tiles * tg
    counts = cnt[:, 0].astype(I32)
    padded = ((counts + tg - 1) // tg) * tg
    ends = jnp.cumsum(padded)
    starts = ends - padded
    tile_start = jnp.arange(n_tiles, dtype=I32) * tg
    tile_expert = jnp.minimum(jnp.sum(tile_start[:, None] >= ends[None, :], axis=1), E - 1).astype(I32)
    n_valid = (ends[-1] // tg).astype(I32).reshape(1)

    tc = 256
    ntc = T // tc
    nbc = seq // tc
    smem_cur = pl.BlockSpec((tc,), lambda i, *_: (i,), memory_space=pltpu.SMEM)
    smem_nxt = pl.BlockSpec((tc,), lambda i, *_: (jnp.minimum(i + 1, ntc - 1),), memory_space=pltpu.SMEM)
    tile_rows = tc * ROW_SUB
    seg_start = functools.reduce(lambda acc, k: jnp.where(e[:2] == k, starts[k], acc), range(E),
                                 jnp.zeros_like(e[:2]))
    dest = seg_start + rank[:2]
    d0, d1 = dest[0], dest[1]
    xs = pl.pallas_call(
        functools.partial(_moe_dispatch_body, tm=tc, tg=tg, n_tiles=n_tiles, nt=ntc),
        grid_spec=pltpu.PrefetchScalarGridSpec(
            num_scalar_prefetch=4,
            grid=(ntc,),
            in_specs=[smem_cur, smem_cur,
                      pl.BlockSpec((tc, Dm), lambda i, *_: (i, 0)),
                      pl.BlockSpec((1, 1, Dm), lambda i, *_: (i // nbc, 0, 0)),
                      pl.BlockSpec((1, 1, Dm), lambda i, *_: (i // nbc, 0, 0))],
            out_specs=pl.BlockSpec(memory_space=pl.ANY),
            scratch_shapes=[pltpu.VMEM((tile_rows, LANE), F32), pltpu.VMEM((tile_rows, LANE), F32),
                            pltpu.VMEM((tg * ROW_SUB, LANE), F32),
                            pltpu.SemaphoreType.DMA((2,)), pltpu.SemaphoreType.DMA(())]),
        out_shape=jax.ShapeDtypeStruct((n_slots * ROW_SUB, LANE), F32),
        compiler_params=_cparams("arbitrary"),
        name="moe_dispatch",
    )(starts.astype(I32), counts, padded, n_valid, d0, d1, x2, a, b)

    last_tile = lambda i, te, nv: (jnp.minimum(i, nv[0] - 1), 0)
    y = pl.pallas_call(
        functools.partial(_moe_expert_body, tg=tg),
        grid_spec=pltpu.PrefetchScalarGridSpec(
            num_scalar_prefetch=2,
            grid=(n_tiles,),
            in_specs=[pl.BlockSpec((tg * ROW_SUB, LANE), last_tile),
                      pl.BlockSpec((1, Dm, 2 * D_EXPERT), lambda i, te, nv: (te[i], 0, 0)),
                      pl.BlockSpec((1, D_EXPERT, Dm), lambda i, te, nv: (te[i], 0, 0))],
            out_specs=pl.BlockSpec((tg * ROW_SUB, LANE), lambda i, te, nv: (i, 0))),
        out_shape=jax.ShapeDtypeStruct((n_slots * ROW_SUB, LANE), F32),
        compiler_params=_cparams("arbitrary"),
        name="moe_experts",
    )(tile_expert, n_valid, xs, w_in.astype(BF16), w_out.astype(BF16))

    return pl.pallas_call(
        functools.partial(_moe_combine_body, tm=tc, nt=ntc),
        grid=(ntc,),
        in_specs=[smem_cur, smem_cur, smem_nxt, smem_nxt,
                  pl.BlockSpec(memory_space=pl.ANY),
                  pl.BlockSpec((tc, Dm), row),
                  pl.BlockSpec((tc, 2), row),
                  pl.BlockSpec((1, 1, Dm), lambda i: (i // nbc, 0, 0))],
        out_specs=pl.BlockSpec((tc, Dm), row),
        out_shape=jax.ShapeDtypeStruct((T, Dm), F32),
        scratch_shapes=[pltpu.VMEM((tile_rows, LANE), F32)] * 4 + [pltpu.SemaphoreType.DMA((2,))],
        compiler_params=_cparams("arbitrary"),
        name="moe_combine",
    )(d0, d1, d0, d1, y, x2, w[:2].T, g)


def kernel(x, c, norm_mix_g, norm_ffn_g, w_ada, b_ada, conv_w_pw1, conv_b_pw1, conv_w_dw, conv_b_dw, conv_ln_g, conv_ln_b, conv_w_pw2, conv_b_pw2, nsa_w_in, nsa_w_cmp1, nsa_w_cmp2, nsa_cmp_pos, nsa_q_gain, nsa_k_gain, nsa_w_out, mla_w_in, mla_q_lat_g, mla_kv_lat_g, mla_w_uq, mla_w_ukv, mla_q_gain, mla_k_gain, mla_w_out, router_w, router_bias, moe_w_in, moe_w_out):
    B, S, Dm = x.shape
    depth = w_ada.shape[0]
    mods = _ada(c, w_ada, b_ada)
    x2 = x.reshape(B * S, Dm)
    for i in range(depth):
        sh1, sc1, g1, sh2, sc2, g2 = [m.reshape(B, 1, Dm) for m in jnp.split(mods[i], 6, axis=-1)]
        a1 = norm_mix_g[i] * (1.0 + sc1)
        kind, j = i % 3, i // 3
        if kind == 0:
            x2 = _conv_mixer(x2, a1, sh1, g1, conv_w_pw1[j], conv_b_pw1[j], conv_w_dw[j], conv_b_dw[j],
                             conv_ln_g[j], conv_ln_b[j], conv_w_pw2[j], conv_b_pw2[j], S)
        elif kind == 1:
            x2 = _nsa_mixer(x2, a1, sh1, g1, nsa_w_in[j], nsa_w_cmp1[j], nsa_w_cmp2[j], nsa_cmp_pos[j],
                            nsa_q_gain[j], nsa_k_gain[j], nsa_w_out[j], B, S)
        else:
            x2 = _mla_mixer(x2, a1, sh1, g1, mla_w_in[j], mla_q_lat_g[j], mla_kv_lat_g[j], mla_w_uq[j],
                            mla_w_ukv[j], mla_q_gain[j], mla_k_gain[j], mla_w_out[j], B, S)
        a2 = norm_ffn_g[i] * (1.0 + sc2)
        x2 = _grouped_moe(x2, a2, sh2, g2, router_w, router_bias, moe_w_in[i], moe_w_out[i], S)
    return x2.reshape(B, S, Dm)
```
